```python
import jax, jax.numpy as jnp
from jax import lax
import numpy as np

D_MODEL = 1024
BATCH = 8
SEQ = 16384
DEPTH = 4

CHUNK = 64
N_MIXERS = 2
N_A = (DEPTH + 1) // 2
N_B = DEPTH // 2
CONV_WIDTH = 31
CONV_INNER = D_MODEL
POOL_INNER = D_MODEL
POOL_WINDOWS = (2, 4, 8, 16)
POOL_GROUPS = len(POOL_WINDOWS)
POOL_GC = POOL_INNER // POOL_GROUPS
RMS_EPS = 1e-6
LN_EPS = 1e-5

kernel_name = "hybrid_conformer_conv_multiscale_pool_trunk"


def rmsnorm(x, g):
    xf = x.astype(jnp.float32)
    y = xf * lax.rsqrt(jnp.mean(xf * xf, axis=-1, keepdims=True) + RMS_EPS)
    return (y * g.astype(jnp.float32)).astype(x.dtype)


def layernorm(x, g, b):
    xf = x.astype(jnp.float32)
    mu = jnp.mean(xf, axis=-1, keepdims=True)
    var = jnp.mean(jnp.square(xf - mu), axis=-1, keepdims=True)
    y = (xf - mu) * lax.rsqrt(var + LN_EPS)
    return (y * g.astype(jnp.float32) + b.astype(jnp.float32)).astype(x.dtype)


def conformer_conv_branch(h, w_in, dw, dw_b, ln_g, ln_b, w_out):
    p = jnp.einsum('bsd,de->bse', h, w_in)
    a, b, z = jnp.split(p, 3, axis=-1)
    u = a * jax.nn.sigmoid(b)
    u = lax.conv_general_dilated(
        u, dw[:, None, :].astype(u.dtype), window_strides=(1,),
        padding=[(CONV_WIDTH - 1, 0)],
        dimension_numbers=('NWC', 'WIO', 'NWC'),
        feature_group_count=CONV_INNER) + dw_b
    u = layernorm(u, ln_g, ln_b)
    u = jax.nn.silu(u) * jax.nn.silu(z)
    return jnp.einsum('bse,ed->bsd', u, w_out)


def multiscale_pool_branch(h, w_in, w_grp, b_grp, scale, w_out):
    p = jnp.einsum('bsd,de->bse', h, w_in)
    u, z = jnp.split(p, 2, axis=-1)
    S = u.shape[1]
    uf = u.astype(jnp.float32)
    cs = jnp.pad(jnp.cumsum(uf, axis=1), ((0, 0), (1, 0), (0, 0)))
    pos = jnp.arange(S, dtype=jnp.int32) + 1
    outs = []
    for g, w in enumerate(POOL_WINDOWS):
        sl = slice(g * POOL_GC, (g + 1) * POOL_GC)
        csg = cs[:, :, sl]
        upper = csg[:, 1:]
        lower = jnp.pad(csg[:, :S + 1 - w], ((0, 0), (w - 1, 0), (0, 0)))
        count = jnp.minimum(pos, w).astype(jnp.float32)[None, :, None]
        d = (upper - lower) / count - uf[:, :, sl]
        outs.append(jnp.einsum('bsc,cd->bsd', d.astype(u.dtype), w_grp[g]))
    y = (jnp.concatenate(outs, axis=-1) + b_grp) * scale
    y = y * jax.nn.silu(z)
    return jnp.einsum('bse,ed->bsd', y, w_out)


def _fwd_setup_inputs(seed: int = 0) -> dict:
    key = jax.random.key(seed)
    ks = jax.random.split(key, 16)
    D, EA, EB = D_MODEL, CONV_INNER, POOL_INNER
    nrm = jax.random.normal
    f32 = jnp.float32
    return {
        "x": nrm(ks[0], (BATCH, SEQ, D), f32),
        "norm_g": 1.0 + 0.05 * nrm(ks[1], (DEPTH, D), f32),
        "final_g": 1.0 + 0.05 * nrm(ks[2], (D,), f32),
        "conv_w_in": nrm(ks[3], (N_A, D, 3 * EA), f32) * D ** -0.5,
        "conv_dw": nrm(ks[4], (N_A, CONV_WIDTH, EA), f32) * CONV_WIDTH ** -0.5,
        "conv_dw_b": 0.02 * nrm(ks[5], (N_A, EA), f32),
        "conv_ln_g": 1.0 + 0.05 * nrm(ks[6], (N_A, EA), f32),
        "conv_ln_b": 0.02 * nrm(ks[7], (N_A, EA), f32),
        "conv_w_out": nrm(ks[8], (N_A, EA, D), f32) * EA ** -0.5,
        "pool_w_in": nrm(ks[9], (N_B, D, 2 * EB), f32) * D ** -0.5,
        "pool_w_grp": nrm(ks[10], (N_B, POOL_GROUPS, POOL_GC, POOL_GC), f32) * POOL_GC ** -0.5,
        "pool_b_grp": 0.02 * nrm(ks[11], (N_B, EB), f32),
        "pool_scale": 1.0 + 0.1 * nrm(ks[12], (N_B, EB), f32),
        "pool_w_out": nrm(ks[13], (N_B, EB, D), f32) * EB ** -0.5,
    }


def _fwd_reference(x, norm_g, final_g, conv_w_in, conv_dw, conv_dw_b, conv_ln_g,
              conv_ln_b, conv_w_out, pool_w_in, pool_w_grp, pool_b_grp,
              pool_scale, pool_w_out):
    h = x
    for i in range(DEPTH):
        hn = rmsnorm(h, norm_g[i])
        j = i // N_MIXERS
        if i % N_MIXERS == 0:
            y = conformer_conv_branch(hn, conv_w_in[j], conv_dw[j], conv_dw_b[j],
                                      conv_ln_g[j], conv_ln_b[j], conv_w_out[j])
        else:
            y = multiscale_pool_branch(hn, pool_w_in[j], pool_w_grp[j], pool_b_grp[j],
                                       pool_scale[j], pool_w_out[j])
        h = h + y
    return rmsnorm(h, final_g)


import jax as _jax
import jax.numpy as _jnp

TWIN_FORMAT = 'train_step'
FWD_PARAMS = ['x', 'norm_g', 'final_g', 'conv_w_in', 'conv_dw', 'conv_dw_b', 'conv_ln_g', 'conv_ln_b', 'conv_w_out', 'pool_w_in', 'pool_w_grp', 'pool_b_grp', 'pool_scale', 'pool_w_out']
TWIN_WEIGHTS = ['norm_g', 'final_g', 'conv_w_in', 'conv_dw', 'conv_dw_b', 'conv_ln_g', 'conv_ln_b', 'conv_w_out', 'pool_w_in', 'pool_w_grp', 'pool_b_grp', 'pool_scale', 'pool_w_out']
TWIN_DIFF_INPUT = 'x'
TWIN_INPUTS = ['x', 'norm_g', 'final_g', 'conv_w_in', 'conv_dw', 'conv_dw_b', 'conv_ln_g', 'conv_ln_b', 'conv_w_out', 'pool_w_in', 'pool_w_grp', 'pool_b_grp', 'pool_scale', 'pool_w_out', 'loss_target', 'm_norm_g', 'm_final_g', 'm_conv_w_in', 'm_conv_dw', 'm_conv_dw_b', 'm_conv_ln_g', 'm_conv_ln_b', 'm_conv_w_out', 'm_pool_w_in', 'm_pool_w_grp', 'm_pool_b_grp', 'm_pool_scale', 'm_pool_w_out', 'v_norm_g', 'v_final_g', 'v_conv_w_in', 'v_conv_dw', 'v_conv_dw_b', 'v_conv_ln_g', 'v_conv_ln_b', 'v_conv_w_out', 'v_pool_w_in', 'v_pool_w_grp', 'v_pool_b_grp', 'v_pool_scale', 'v_pool_w_out']
TWIN_OUTPUTS = ['loss', 'grad_x', 'grad_norm_g', 'grad_final_g', 'grad_conv_w_in', 'grad_conv_dw', 'grad_conv_dw_b', 'grad_conv_ln_g', 'grad_conv_ln_b', 'grad_conv_w_out', 'grad_pool_w_in', 'grad_pool_w_grp', 'grad_pool_b_grp', 'grad_pool_scale', 'grad_pool_w_out', 'delta_norm_g', 'delta_final_g', 'delta_conv_w_in', 'delta_conv_dw', 'delta_conv_dw_b', 'delta_conv_ln_g', 'delta_conv_ln_b', 'delta_conv_w_out', 'delta_pool_w_in', 'delta_pool_w_grp', 'delta_pool_b_grp', 'delta_pool_scale', 'delta_pool_w_out', 'new_m_norm_g', 'new_m_final_g', 'new_m_conv_w_in', 'new_m_conv_dw', 'new_m_conv_dw_b', 'new_m_conv_ln_g', 'new_m_conv_ln_b', 'new_m_conv_w_out', 'new_m_pool_w_in', 'new_m_pool_w_grp', 'new_m_pool_b_grp', 'new_m_pool_scale', 'new_m_pool_w_out', 'new_v_norm_g', 'new_v_final_g', 'new_v_conv_w_in', 'new_v_conv_dw', 'new_v_conv_dw_b', 'new_v_conv_ln_g', 'new_v_conv_ln_b', 'new_v_conv_w_out', 'new_v_pool_w_in', 'new_v_pool_w_grp', 'new_v_pool_b_grp', 'new_v_pool_scale', 'new_v_pool_w_out']
TWIN_LEAF_KINDS = {'loss': 'loss', 'grad_x': 'grad_x', 'grad_norm_g': 'grad_w', 'grad_final_g': 'grad_w', 'grad_conv_w_in': 'grad_w', 'grad_conv_dw': 'grad_w', 'grad_conv_dw_b': 'grad_w', 'grad_conv_ln_g': 'grad_w', 'grad_conv_ln_b': 'grad_w', 'grad_conv_w_out': 'grad_w', 'grad_pool_w_in': 'grad_w', 'grad_pool_w_grp': 'grad_w', 'grad_pool_b_grp': 'grad_w', 'grad_pool_scale': 'grad_w', 'grad_pool_w_out': 'grad_w', 'delta_norm_g': 'delta_w', 'delta_final_g': 'delta_w', 'delta_conv_w_in': 'delta_w', 'delta_conv_dw': 'delta_w', 'delta_conv_dw_b': 'delta_w', 'delta_conv_ln_g': 'delta_w', 'delta_conv_ln_b': 'delta_w', 'delta_conv_w_out': 'delta_w', 'delta_pool_w_in': 'delta_w', 'delta_pool_w_grp': 'delta_w', 'delta_pool_b_grp': 'delta_w', 'delta_pool_scale': 'delta_w', 'delta_pool_w_out': 'delta_w', 'new_m_norm_g': 'new_m', 'new_m_final_g': 'new_m', 'new_m_conv_w_in': 'new_m', 'new_m_conv_dw': 'new_m', 'new_m_conv_dw_b': 'new_m', 'new_m_conv_ln_g': 'new_m', 'new_m_conv_ln_b': 'new_m', 'new_m_conv_w_out': 'new_m', 'new_m_pool_w_in': 'new_m', 'new_m_pool_w_grp': 'new_m', 'new_m_pool_b_grp': 'new_m', 'new_m_pool_scale': 'new_m', 'new_m_pool_w_out': 'new_m', 'new_v_norm_g': 'new_v', 'new_v_final_g': 'new_v', 'new_v_conv_w_in': 'new_v', 'new_v_conv_dw': 'new_v', 'new_v_conv_dw_b': 'new_v', 'new_v_conv_ln_g': 'new_v', 'new_v_conv_ln_b': 'new_v', 'new_v_conv_w_out': 'new_v', 'new_v_pool_w_in': 'new_v', 'new_v_pool_w_grp': 'new_v', 'new_v_pool_b_grp': 'new_v', 'new_v_pool_scale': 'new_v', 'new_v_pool_w_out': 'new_v'}


def _forward(args):
    return _fwd_reference(*[args[k] for k in FWD_PARAMS])


def _output_shape():
    def fwd():
        inp = _fwd_setup_inputs(0)
        return _fwd_reference(*[inp[k] for k in FWD_PARAMS])
    out = _jax.eval_shape(fwd)
    return out.shape, out.dtype

N_MICROBATCH = 1
ADAM_LR = 0.001
ADAM_B1 = 0.9
ADAM_B2 = 0.999
ADAM_EPS = 1e-08
ADAM_WD = 0.01
ADAM_STEP = 10
PER_EXAMPLE_BATCH_AXIS = {'x': 0, 'loss_target': 0}
SHARED_INPUTS = []
_WEIGHT_DTYPES = {'norm_g': _jnp.float32, 'final_g': _jnp.float32, 'conv_w_in': _jnp.float32, 'conv_dw': _jnp.float32, 'conv_dw_b': _jnp.float32, 'conv_ln_g': _jnp.float32, 'conv_ln_b': _jnp.float32, 'conv_w_out': _jnp.float32, 'pool_w_in': _jnp.float32, 'pool_w_grp': _jnp.float32, 'pool_b_grp': _jnp.float32, 'pool_scale': _jnp.float32, 'pool_w_out': _jnp.float32}
MOMENT_SCALE = {'norm_g': 2.199919e-01, 'final_g': 1.279825e+02, 'conv_w_in': 1.158885e-01, 'conv_dw': 1.367257e-01, 'conv_dw_b': 2.874815e-01, 'conv_ln_g': 1.631779e-01, 'conv_ln_b': 1.365885e-01, 'conv_w_out': 1.327960e-01, 'pool_w_in': 1.643572e-01, 'pool_w_grp': 1.618405e-01, 'pool_b_grp': 2.006902e-01, 'pool_scale': 1.751634e-01, 'pool_w_out': 1.653354e-01}


def _to_microbatches(a, axis):
    t = _jnp.moveaxis(a, axis, 0)
    t = t.reshape((N_MICROBATCH, t.shape[0] // N_MICROBATCH) + t.shape[1:])
    return _jnp.moveaxis(t, 1, axis + 1)


def setup_inputs(seed: int = 0) -> dict:
    inp = _fwd_setup_inputs(seed)
    key = _jax.random.fold_in(_jax.random.key(seed), 7919)
    shape, _ = _output_shape()
    out = dict(inp)
    out["loss_target"] = _jax.random.normal(_jax.random.fold_in(key, 0), shape, _jnp.float32)
    for i, name in enumerate(TWIN_WEIGHTS):
        w = inp[name].astype(_jnp.float32)
        if MOMENT_SCALE is None:
            s = _jnp.sqrt(_jnp.mean(_jnp.square(w)) + 1e-30)
        else:
            s = MOMENT_SCALE[name]
        km, kv = _jax.random.split(_jax.random.fold_in(key, i + 1))
        out[name] = w
        out["m_" + name] = s * _jax.random.normal(km, w.shape, _jnp.float32)
        out["v_" + name] = (s * s) * _jax.random.uniform(kv, w.shape, _jnp.float32, 0.5, 1.5)
    if N_MICROBATCH > 1:
        for name, axis in PER_EXAMPLE_BATCH_AXIS.items():
            out[name] = _to_microbatches(out[name], axis)
    return {'x': out['x'], 'norm_g': out['norm_g'], 'final_g': out['final_g'], 'conv_w_in': out['conv_w_in'], 'conv_dw': out['conv_dw'], 'conv_dw_b': out['conv_dw_b'], 'conv_ln_g': out['conv_ln_g'], 'conv_ln_b': out['conv_ln_b'], 'conv_w_out': out['conv_w_out'], 'pool_w_in': out['pool_w_in'], 'pool_w_grp': out['pool_w_grp'], 'pool_b_grp': out['pool_b_grp'], 'pool_scale': out['pool_scale'], 'pool_w_out': out['pool_w_out'], 'loss_target': out['loss_target'], 'm_norm_g': out['m_norm_g'], 'm_final_g': out['m_final_g'], 'm_conv_w_in': out['m_conv_w_in'], 'm_conv_dw': out['m_conv_dw'], 'm_conv_dw_b': out['m_conv_dw_b'], 'm_conv_ln_g': out['m_conv_ln_g'], 'm_conv_ln_b': out['m_conv_ln_b'], 'm_conv_w_out': out['m_conv_w_out'], 'm_pool_w_in': out['m_pool_w_in'], 'm_pool_w_grp': out['m_pool_w_grp'], 'm_pool_b_grp': out['m_pool_b_grp'], 'm_pool_scale': out['m_pool_scale'], 'm_pool_w_out': out['m_pool_w_out'], 'v_norm_g': out['v_norm_g'], 'v_final_g': out['v_final_g'], 'v_conv_w_in': out['v_conv_w_in'], 'v_conv_dw': out['v_conv_dw'], 'v_conv_dw_b': out['v_conv_dw_b'], 'v_conv_ln_g': out['v_conv_ln_g'], 'v_conv_ln_b': out['v_conv_ln_b'], 'v_conv_w_out': out['v_conv_w_out'], 'v_pool_w_in': out['v_pool_w_in'], 'v_pool_w_grp': out['v_pool_w_grp'], 'v_pool_b_grp': out['v_pool_b_grp'], 'v_pool_scale': out['v_pool_scale'], 'v_pool_w_out': out['v_pool_w_out']}


def _loss(weights, diff, rest, loss_target):
    with _jax.named_scope("forward"):
        args = {**rest, TWIN_DIFF_INPUT: diff, **{k: w.astype(_WEIGHT_DTYPES[k]) for k, w in weights.items()}}
        y = _forward(args)
    with _jax.named_scope("loss_head"):
        err = _jnp.square(y.astype(_jnp.float32) - loss_target)
        return 0.5 * _jnp.sum(_jnp.mean(err, axis=-1)) if err.ndim else 0.5 * err


def _adamw(w, g, m, v):
    m = ADAM_B1 * m + (1.0 - ADAM_B1) * g
    v = ADAM_B2 * v + (1.0 - ADAM_B2) * _jnp.square(g)
    m_hat = m / (1.0 - ADAM_B1 ** ADAM_STEP)
    v_hat = v / (1.0 - ADAM_B2 ** ADAM_STEP)
    delta = -ADAM_LR * (m_hat / (_jnp.sqrt(v_hat) + ADAM_EPS) + ADAM_WD * w)
    return delta, m, v


def reference(x, norm_g, final_g, conv_w_in, conv_dw, conv_dw_b, conv_ln_g, conv_ln_b, conv_w_out, pool_w_in, pool_w_grp, pool_b_grp, pool_scale, pool_w_out, loss_target, m_norm_g, m_final_g, m_conv_w_in, m_conv_dw, m_conv_dw_b, m_conv_ln_g, m_conv_ln_b, m_conv_w_out, m_pool_w_in, m_pool_w_grp, m_pool_b_grp, m_pool_scale, m_pool_w_out, v_norm_g, v_final_g, v_conv_w_in, v_conv_dw, v_conv_dw_b, v_conv_ln_g, v_conv_ln_b, v_conv_w_out, v_pool_w_in, v_pool_w_grp, v_pool_b_grp, v_pool_scale, v_pool_w_out):
    given = dict(x=x, norm_g=norm_g, final_g=final_g, conv_w_in=conv_w_in, conv_dw=conv_dw, conv_dw_b=conv_dw_b, conv_ln_g=conv_ln_g, conv_ln_b=conv_ln_b, conv_w_out=conv_w_out, pool_w_in=pool_w_in, pool_w_grp=pool_w_grp, pool_b_grp=pool_b_grp, pool_scale=pool_scale, pool_w_out=pool_w_out, loss_target=loss_target, m_norm_g=m_norm_g, m_final_g=m_final_g, m_conv_w_in=m_conv_w_in, m_conv_dw=m_conv_dw, m_conv_dw_b=m_conv_dw_b, m_conv_ln_g=m_conv_ln_g, m_conv_ln_b=m_conv_ln_b, m_conv_w_out=m_conv_w_out, m_pool_w_in=m_pool_w_in, m_pool_w_grp=m_pool_w_grp, m_pool_b_grp=m_pool_b_grp, m_pool_scale=m_pool_scale, m_pool_w_out=m_pool_w_out, v_norm_g=v_norm_g, v_final_g=v_final_g, v_conv_w_in=v_conv_w_in, v_conv_dw=v_conv_dw, v_conv_dw_b=v_conv_dw_b, v_conv_ln_g=v_conv_ln_g, v_conv_ln_b=v_conv_ln_b, v_conv_w_out=v_conv_w_out, v_pool_w_in=v_pool_w_in, v_pool_w_grp=v_pool_w_grp, v_pool_b_grp=v_pool_b_grp, v_pool_scale=v_pool_scale, v_pool_w_out=v_pool_w_out)
    weights = {n: given[n] for n in TWIN_WEIGHTS}
    shared = {n: given[n] for n in SHARED_INPUTS}
    per_example = {n: given[n] for n in ['x']}
    grad_fn = _jax.value_and_grad(_loss, argnums=(0, 1))

    def one_microbatch(ex, loss_target):
        ex = dict(ex)
        diff = ex.pop(TWIN_DIFF_INPUT)
        return grad_fn(weights, diff, {**shared, **ex}, loss_target)

    if N_MICROBATCH == 1:
        loss, (grad_w, grad_x) = one_microbatch(per_example, given["loss_target"])
    else:
        def body(carry, xs):
            loss_sum, grad_sum = carry
            l_k, (gw_k, gx_k) = one_microbatch(xs[0], xs[1])
            with _jax.named_scope("update"):
                return (loss_sum + l_k, _jax.tree.map(_jnp.add, grad_sum, gw_k)), gx_k

        init = (_jnp.zeros((), _jnp.float32), _jax.tree.map(_jnp.zeros_like, weights))
        (loss, grad_w), grad_x = _jax.lax.scan(body, init, (per_example, given["loss_target"]))
    with _jax.named_scope("update"):
        delta_w, new_m, new_v = {}, {}, {}
        for n in TWIN_WEIGHTS:
            delta_w[n], new_m[n], new_v[n] = _adamw(weights[n], grad_w[n], given["m_" + n], given["v_" + n])
    return (loss, grad_x, *[grad_w[n] for n in TWIN_WEIGHTS], *[delta_w[n] for n in TWIN_WEIGHTS],
            *[new_m[n] for n in TWIN_WEIGHTS], *[new_v[n] for n in TWIN_WEIGHTS])
```

```python
import functools

import jax
import jax.numpy as jnp
from jax import lax
from jax.experimental import pallas as pl
from jax.experimental.pallas import tpu as pltpu

F32 = jnp.float32
BF16 = jnp.bfloat16

RMS_EPS = 1e-6
LN_EPS = 1e-5
CONV_TAPS = 31
CONV_HALO = 32
POOL_WINDOWS = (2, 4, 8, 16)
POOL_HALO = 16
SUBLANES = 8
N_DEV = 8
V7X_VMEM_LIMIT = 56 * 1024 * 1024

ADAM_LR = 0.001
ADAM_B1 = 0.9
ADAM_B2 = 0.999
ADAM_EPS = 1e-08
ADAM_WD = 0.01
ADAM_STEP = 10

MESH = pl.DeviceIdType.MESH


def _dot(a, b):
    return lax.dot_general(a, b, (((1,), (0,)), ((), ())), preferred_element_type=F32)


def _dot_nt(a, b):
    return lax.dot_general(a, b, (((1,), (1,)), ((), ())), preferred_element_type=F32)


def _dot_tn(a, b):
    return lax.dot_general(a, b, (((0,), (0,)), ((), ())), preferred_element_type=F32)


def _sigmoid(x):
    return jax.nn.sigmoid(x)


def _dsilu(x, s):
    return s * (1.0 + x * (1.0 - s))


def _rows8(x):
    r, c = x.shape
    return jnp.sum(x.reshape(r // SUBLANES, SUBLANES, c), axis=0)


def _tile(t, pref):
    return pref if t >= 2 * pref else t // 2


def _const(shape, index):
    return pl.BlockSpec(shape, lambda i: index, pipeline_mode=pl.Buffered(1))


def _params():
    return pltpu.CompilerParams(dimension_semantics=("arbitrary",), vmem_limit_bytes=V7X_VMEM_LIMIT)


def _chunks(n, rc, fn, reverse=False):
    def step(c, carry):
        c = (n - 1 - c) if reverse else c
        fn(pl.multiple_of(c * rc, rc))
        return carry
    lax.fori_loop(0, n, step, 0)


def _in_fwd(h, norm_g, layer, w_in, w_layer, name):
    t, d = h.shape
    n = w_in.shape[-1]
    tm = _tile(t, 512)
    rc = 32

    def body(h_ref, g_ref, w_ref, p_ref, hn_scr):
        def c1(base):
            rows = pl.ds(base, rc)
            x = h_ref[rows, :]
            r = lax.rsqrt(jnp.mean(x * x, axis=-1, keepdims=True) + RMS_EPS)
            hn_scr[rows, :] = (x * r * g_ref[...]).astype(BF16)
        _chunks(tm // rc, rc, c1)
        p_ref[...] = _dot(hn_scr[...], w_ref[...])

    return pl.pallas_call(
        body, name=name, grid=(t // tm,),
        in_specs=[pl.BlockSpec((tm, d), lambda i: (i, 0)),
                  _const((None, 1, d), (layer, 0, 0)),
                  _const((None, d, n), (w_layer, 0, 0))],
        out_specs=pl.BlockSpec((tm, n), lambda i: (i, 0)),
        out_shape=jax.ShapeDtypeStruct((t, n), F32),
        scratch_shapes=[pltpu.VMEM((tm, d), BF16)],
        compiler_params=_params(),
    )(h, norm_g, w_in)


def _layernorm_rows(uc, lng, lnb):
    mu = jnp.mean(uc, axis=-1, keepdims=True)
    xc = uc - mu
    rstd = lax.rsqrt(jnp.mean(xc * xc, axis=-1, keepdims=True) + LN_EPS)
    xhat = xc * rstd
    return xhat, rstd, xhat * lng + lnb


def _conv_fwd(p, h, small, w_out, layer, name):
    t, d = h.shape
    e = w_out.shape[1]
    tm = _tile(t, 256)
    rc = 32
    nsm = small.shape[1]

    def body(p_ref, h_ref, sm_ref, wo_ref, ho_ref, uc_ref, us_scr, v_scr):
        i = pl.program_id(0)

        @pl.when(i == 0)
        def _():
            us_scr[:, pl.ds(0, CONV_HALO), :] = jnp.zeros((SUBLANES, CONV_HALO, e), F32)

        @pl.when(i > 0)
        def _():
            us_scr[:, pl.ds(0, CONV_HALO), :] = us_scr[:, pl.ds(tm, CONV_HALO), :]

        def c_shift(base):
            a = p_ref[pl.ds(base, rc), pl.ds(0, e)]
            b = p_ref[pl.ds(base, rc), pl.ds(e, e)]
            u = a * _sigmoid(b)
            row = pl.multiple_of(CONV_HALO + base, SUBLANES)
            us_scr[0, pl.ds(row, rc), :] = u
            prev = us_scr[0, pl.ds(pl.multiple_of(row - SUBLANES, SUBLANES), SUBLANES), :]
            w = jnp.concatenate([prev, u], axis=0)
            for r in range(1, SUBLANES):
                us_scr[r, pl.ds(row, rc), :] = pltpu.roll(w, r, axis=0)[SUBLANES:, :]
        _chunks(tm // rc, rc, c_shift)

        def c_conv(base):
            rows = pl.ds(base, rc)
            acc = jnp.broadcast_to(sm_ref[pl.ds(32, 1), :], (rc, e))
            for o in range(CONV_TAPS):
                q, r = divmod(o, SUBLANES)
                row = pl.multiple_of(CONV_HALO + base - SUBLANES * q, SUBLANES)
                acc = acc + sm_ref[pl.ds(CONV_TAPS - 1 - o, 1), :] * us_scr[r, pl.ds(row, rc), :]
            uc_ref[rows, :] = acc
            _, _, ul = _layernorm_rows(acc, sm_ref[pl.ds(33, 1), :], sm_ref[pl.ds(34, 1), :])
            z = p_ref[rows, pl.ds(2 * e, e)]
            v_scr[rows, :] = ((ul * _sigmoid(ul)) * (z * _sigmoid(z))).astype(BF16)
        _chunks(tm // rc, rc, c_conv)

        ho_ref[...] = h_ref[...] + _dot(v_scr[...], wo_ref[...])

    return pl.pallas_call(
        body, name=name, grid=(t // tm,),
        in_specs=[pl.BlockSpec((tm, 3 * e), lambda i: (i, 0)),
                  pl.BlockSpec((tm, d), lambda i: (i, 0)),
                  _const((None, nsm, e), (layer, 0, 0)),
                  _const((None, e, d), (layer, 0, 0))],
        out_specs=[pl.BlockSpec((tm, d), lambda i: (i, 0)),
                   pl.BlockSpec((tm, e), lambda i: (i, 0))],
        out_shape=(jax.ShapeDtypeStruct((t, d), F32), jax.ShapeDtypeStruct((t, e), F32)),
        scratch_shapes=[pltpu.VMEM((SUBLANES, tm + CONV_HALO, e), F32), pltpu.VMEM((tm, e), BF16)],
        compiler_params=_params(),
    )(p, h, small, w_out)


def _conv_bwd(dho, p, uc, small, w_out, layer, name):
    t, d = dho.shape
    e = w_out.shape[1]
    tm = _tile(t, 256)
    nt = t // tm
    rc = 16
    nsm = small.shape[1]

    def body(dho_ref, p_ref, uc_ref, sm_ref, wo_ref, dp_ref, dwo_ref, dsm_ref,
             v_scr, dv_scr, ds_scr, acc_scr):
        i = pl.program_id(0)

        @pl.when(i == 0)
        def _():
            dwo_ref[...] = jnp.zeros_like(dwo_ref)
            acc_scr[...] = jnp.zeros_like(acc_scr)
            ds_scr[:, pl.ds(tm, CONV_HALO), :] = jnp.zeros((SUBLANES, CONV_HALO, e), F32)

        @pl.when(i > 0)
        def _():
            ds_scr[:, pl.ds(tm, CONV_HALO), :] = ds_scr[:, pl.ds(0, CONV_HALO), :]

        lng = sm_ref[pl.ds(33, 1), :]
        lnb = sm_ref[pl.ds(34, 1), :]

        def c_v(base):
            rows = pl.ds(base, rc)
            _, _, ul = _layernorm_rows(uc_ref[rows, :], lng, lnb)
            z = p_ref[rows, pl.ds(2 * e, e)]
            v_scr[rows, :] = ((ul * _sigmoid(ul)) * (z * _sigmoid(z))).astype(BF16)
        _chunks(tm // rc, rc, c_v)

        dy = dho_ref[...].astype(BF16)
        dv_scr[...] = _dot_nt(dy, wo_ref[...])
        dwo_ref[...] += _dot_tn(v_scr[...], dy)

        def c_ln(base):
            rows = pl.ds(base, rc)
            xhat, rstd, ul = _layernorm_rows(uc_ref[rows, :], lng, lnb)
            z = p_ref[rows, pl.ds(2 * e, e)]
            sg_u = _sigmoid(ul)
            sg_z = _sigmoid(z)
            dv = dv_scr[rows, :]
            dul = dv * (z * sg_z) * _dsilu(ul, sg_u)
            dz = dv * (ul * sg_u) * _dsilu(z, sg_z)
            dp_ref[rows, pl.ds(2 * e, e)] = dz.astype(BF16)
            acc_scr[pl.ds(33 * SUBLANES, SUBLANES), :] += _rows8(dul * xhat)
            acc_scr[pl.ds(34 * SUBLANES, SUBLANES), :] += _rows8(dul)
            dxh = dul * lng
            duc = rstd * (dxh - jnp.mean(dxh, axis=-1, keepdims=True)
                          - xhat * jnp.mean(dxh * xhat, axis=-1, keepdims=True))
            acc_scr[pl.ds(32 * SUBLANES, SUBLANES), :] += _rows8(duc)
            ds_scr[0, rows, :] = duc
            nxt = ds_scr[0, pl.ds(pl.multiple_of(base + rc, SUBLANES), SUBLANES), :]
            w = jnp.concatenate([duc, nxt], axis=0)
            for r in range(1, SUBLANES):
                ds_scr[r, rows, :] = pltpu.roll(w, rc + SUBLANES - r, axis=0)[:rc, :]
        _chunks(tm // rc, rc, c_ln, reverse=True)

        def c_conv(base):
            rows = pl.ds(base, rc)
            a = p_ref[rows, pl.ds(0, e)]
            b = p_ref[rows, pl.ds(e, e)]
            sb = _sigmoid(b)
            u = a * sb
            du = jnp.zeros((rc, e), F32)
            for o in range(CONV_TAPS):
                q, r = divmod(o, SUBLANES)
                k = CONV_TAPS - 1 - o
                sh = ds_scr[r, pl.ds(pl.multiple_of(base + SUBLANES * q, SUBLANES), rc), :]
                du = du + sm_ref[pl.ds(k, 1), :] * sh
                acc_scr[pl.ds(k * SUBLANES, SUBLANES), :] += _rows8(u * sh)
            dp_ref[rows, pl.ds(0, e)] = (du * sb).astype(BF16)
            dp_ref[rows, pl.ds(e, e)] = (du * u * (1.0 - sb)).astype(BF16)
        _chunks(tm // rc, rc, c_conv)

        @pl.when(i == nt - 1)
        def _():
            for k in range(nsm):
                dsm_ref[pl.ds(k, 1), :] = jnp.sum(acc_scr[pl.ds(k * SUBLANES, SUBLANES), :], axis=0, keepdims=True)

    rev = lambda i: (nt - 1 - i, 0)
    return pl.pallas_call(
        body, name=name, grid=(nt,),
        in_specs=[pl.BlockSpec((tm, d), rev),
                  pl.BlockSpec((tm, 3 * e), rev),
                  pl.BlockSpec((tm, e), rev),
                  _const((None, nsm, e), (layer, 0, 0)),
                  _const((None, e, d), (layer, 0, 0))],
        out_specs=[pl.BlockSpec((tm, 3 * e), rev),
                   _const((e, d), (0, 0)),
                   _const((nsm, e), (0, 0))],
        out_shape=(jax.ShapeDtypeStruct((t, 3 * e), BF16),
                   jax.ShapeDtypeStruct((e, d), F32),
                   jax.ShapeDtypeStruct((nsm, e), F32)),
        scratch_shapes=[pltpu.VMEM((tm, e), BF16), pltpu.VMEM((tm, e), F32),
                        pltpu.VMEM((SUBLANES, tm + CONV_HALO, e), F32),
                        pltpu.VMEM((nsm * SUBLANES, e), F32)],
        compiler_params=_params(),
    )(dho, p, uc, small, w_out)


def _inv_count(tile, tm, base, rc, extra, w):
    tpos = tile * tm + base + lax.broadcasted_iota(jnp.int32, (rc + extra, 1), 0)
    return 1.0 / jnp.minimum(tpos + 1, w).astype(F32)


def _pool_d(ue_scr, tile, tm, base, rc, gc):
    out = []
    for g, w in enumerate(POOL_WINDOWS):
        cols = pl.ds(g * gc, gc)
        win = ue_scr[pl.ds(base, rc + POOL_HALO), cols]
        s = win
        sh = 1
        while sh < w:
            s = s + pltpu.roll(s, sh, axis=0)
            sh *= 2
        s = s[POOL_HALO:, :]
        out.append(s * _inv_count(tile, tm, base, rc, 0, w) - win[POOL_HALO:, :])
    return out


def _pool_fwd(p, h, w_grp, small, w_out, layer, name):
    t, d = h.shape
    e = w_out.shape[1]
    gc = e // len(POOL_WINDOWS)
    tm = _tile(t, 256)
    rc = 32
    nsm = small.shape[1]

    def body(p_ref, h_ref, wg_ref, sm_ref, wo_ref, ho_ref, ue_scr, d_scr, o_scr, y_scr):
        i = pl.program_id(0)

        @pl.when(i == 0)
        def _():
            ue_scr[pl.ds(0, POOL_HALO), :] = jnp.zeros((POOL_HALO, e), F32)

        @pl.when(i > 0)
        def _():
            ue_scr[pl.ds(0, POOL_HALO), :] = ue_scr[pl.ds(tm, POOL_HALO), :]

        ue_scr[pl.ds(POOL_HALO, tm), :] = p_ref[:, pl.ds(0, e)]

        def c_d(base):
            for g, dg in enumerate(_pool_d(ue_scr, i, tm, base, rc, gc)):
                d_scr[pl.ds(base, rc), pl.ds(g * gc, gc)] = dg.astype(BF16)
        _chunks(tm // rc, rc, c_d)

        for g in range(len(POOL_WINDOWS)):
            cols = pl.ds(g * gc, gc)
            o_scr[:, cols] = _dot(d_scr[:, cols], wg_ref[g])

        def c_y(base):
            rows = pl.ds(base, rc)
            z = p_ref[rows, pl.ds(e, e)]
            y1 = (o_scr[rows, :] + sm_ref[pl.ds(32, 1), :]) * sm_ref[pl.ds(33, 1), :]
            y_scr[rows, :] = (y1 * (z * _sigmoid(z))).astype(BF16)
        _chunks(tm // rc, rc, c_y)

        ho_ref[...] = h_ref[...] + _dot(y_scr[...], wo_ref[...])

    return pl.pallas_call(
        body, name=name, grid=(t // tm,),
        in_specs=[pl.BlockSpec((tm, 2 * e), lambda i: (i, 0)),
                  pl.BlockSpec((tm, d), lambda i: (i, 0)),
                  _const((None, len(POOL_WINDOWS), gc, gc), (layer, 0, 0, 0)),
                  _const((None, nsm, e), (layer, 0, 0)),
                  _const((None, e, d), (layer, 0, 0))],
        out_specs=pl.BlockSpec((tm, d), lambda i: (i, 0)),
        out_shape=jax.ShapeDtypeStruct((t, d), F32),
        scratch_shapes=[pltpu.VMEM((tm + POOL_HALO, e), F32), pltpu.VMEM((tm, e), BF16),
                        pltpu.VMEM((tm, e), F32), pltpu.VMEM((tm, e), BF16)],
        compiler_params=_params(),
    )(p, h, w_grp, small, w_out)


def _pool_bwd(dho, p, w_grp, small, w_out, layer, name):
    t, d = dho.shape
    e = w_out.shape[1]
    ng = len(POOL_WINDOWS)
    gc = e // ng
    tm = _tile(t, 256)
    nt = t // tm
    rc = 32
    nsm = small.shape[1]
    hb = tm // POOL_HALO

    def body(dho_ref, p_ref, ph_ref, wg_ref, sm_ref, wo_ref, dp_ref, dwo_ref, dwg_ref, dsm_ref,
             ue_scr, d_scr, o_scr, y_scr, dy_scr, do_scr, dd_scr, ee_scr, acc_scr):
        i = pl.program_id(0)
        tile = nt - 1 - i

        @pl.when(i == 0)
        def _():
            dwo_ref[...] = jnp.zeros_like(dwo_ref)
            dwg_ref[...] = jnp.zeros_like(dwg_ref)
            acc_scr[...] = jnp.zeros_like(acc_scr)
            ee_scr[pl.ds(tm, POOL_HALO), :] = jnp.zeros((POOL_HALO, e), F32)

        @pl.when(i > 0)
        def _():
            ee_scr[pl.ds(tm, POOL_HALO), :] = ee_scr[pl.ds(0, POOL_HALO), :]

        @pl.when(tile == 0)
        def _():
            ue_scr[pl.ds(0, POOL_HALO), :] = jnp.zeros((POOL_HALO, e), F32)

        @pl.when(tile > 0)
        def _():
            ue_scr[pl.ds(0, POOL_HALO), :] = ph_ref[:, pl.ds(0, e)]

        ue_scr[pl.ds(POOL_HALO, tm), :] = p_ref[:, pl.ds(0, e)]

        def c_d(base):
            for g, dg in enumerate(_pool_d(ue_scr, tile, tm, base, rc, gc)):
                d_scr[pl.ds(base, rc), pl.ds(g * gc, gc)] = dg.astype(BF16)
        _chunks(tm // rc, rc, c_d)

        for g in range(ng):
            cols = pl.ds(g * gc, gc)
            o_scr[:, cols] = _dot(d_scr[:, cols], wg_ref[g])

        bg = sm_ref[pl.ds(32, 1), :]
        sc = sm_ref[pl.ds(33, 1), :]

        def c_y(base):
            rows = pl.ds(base, rc)
            z = p_ref[rows, pl.ds(e, e)]
            y1 = (o_scr[rows, :] + bg) * sc
            y_scr[rows, :] = (y1 * (z * _sigmoid(z))).astype(BF16)
        _chunks(tm // rc, rc, c_y)

        dy = dho_ref[...].astype(BF16)
        dy_scr[...] = _dot_nt(dy, wo_ref[...])
        dwo_ref[...] += _dot_tn(y_scr[...], dy)

        def c_g(base):
            rows = pl.ds(base, rc)
            z = p_ref[rows, pl.ds(e, e)]
            sg_z = _sigmoid(z)
            ob = o_scr[rows, :] + bg
            dy2 = dy_scr[rows, :]
            dy1 = dy2 * (z * sg_z)
            dp_ref[rows, pl.ds(e, e)] = (dy2 * (ob * sc) * _dsilu(z, sg_z)).astype(BF16)
            acc_scr[pl.ds(33 * SUBLANES, SUBLANES), :] += _rows8(dy1 * ob)
            do = dy1 * sc
            acc_scr[pl.ds(32 * SUBLANES, SUBLANES), :] += _rows8(do)
            do_scr[rows, :] = do.astype(BF16)
        _chunks(tm // rc, rc, c_g)

        for g in range(ng):
            cols = pl.ds(g * gc, gc)
            dwg_ref[g] += _dot_tn(d_scr[:, cols], do_scr[:, cols])
            dd_scr[:, cols] = _dot_nt(do_scr[:, cols], wg_ref[g])

        def c_e(base):
            rows = pl.ds(base, rc)
            for g, w in enumerate(POOL_WINDOWS):
                cols = pl.ds(g * gc, gc)
                ee_scr[rows, cols] = dd_scr[rows, cols] * _inv_count(tile, tm, base, rc, 0, w)
        _chunks(tm // rc, rc, c_e)

        def c_du(base):
            rows = pl.ds(base, rc)
            n = rc + POOL_HALO
            for g, w in enumerate(POOL_WINDOWS):
                cols = pl.ds(g * gc, gc)
                s = ee_scr[pl.ds(base, n), cols]
                sh = 1
                while sh < w:
                    s = s + pltpu.roll(s, n - sh, axis=0)
                    sh *= 2
                dp_ref[rows, cols] = (s[:rc, :] - dd_scr[rows, cols]).astype(BF16)
        _chunks(tm // rc, rc, c_du)

        @pl.when(i == nt - 1)
        def _():
            dsm_ref[...] = jnp.zeros_like(dsm_ref)
            for k in (32, 33):
                dsm_ref[pl.ds(k, 1), :] = jnp.sum(acc_scr[pl.ds(k * SUBLANES, SUBLANES), :], axis=0, keepdims=True)

    rev = lambda i: (nt - 1 - i, 0)
    return pl.pallas_call(
        body, name=name, grid=(nt,),
        in_specs=[pl.BlockSpec((tm, d), rev),
                  pl.BlockSpec((tm, 2 * e), rev),
                  pl.BlockSpec((POOL_HALO, 2 * e), lambda i: (jnp.maximum((nt - 1 - i) * hb - 1, 0), 0)),
                  _const((None, ng, gc, gc), (layer, 0, 0, 0)),
                  _const((None, nsm, e), (layer, 0, 0)),
                  _const((None, e, d), (layer, 0, 0))],
        out_specs=[pl.BlockSpec((tm, 2 * e), rev),
                   _const((e, d), (0, 0)),
                   _const((ng, gc, gc), (0, 0, 0)),
                   _const((nsm, e), (0, 0))],
        out_shape=(jax.ShapeDtypeStruct((t, 2 * e), BF16),
                   jax.ShapeDtypeStruct((e, d), F32),
                   jax.ShapeDtypeStruct((ng, gc, gc), F32),
                   jax.ShapeDtypeStruct((nsm, e), F32)),
        scratch_shapes=[pltpu.VMEM((tm + POOL_HALO, e), F32), pltpu.VMEM((tm, e), BF16),
                        pltpu.VMEM((tm, e), F32), pltpu.VMEM((tm, e), BF16),
                        pltpu.VMEM((tm, e), F32), pltpu.VMEM((tm, e), BF16),
                        pltpu.VMEM((tm, e), F32), pltpu.VMEM((tm + POOL_HALO, e), F32),
                        pltpu.VMEM((nsm * SUBLANES, e), F32)],
        compiler_params=_params(),
    )(dho, p, p, w_grp, small, w_out)


def _in_bwd(dp, h, dho, norm_g, layer, w_in, w_layer, name):
    t, d = h.shape
    n = w_in.shape[-1]
    tm = _tile(t, 512)
    nt = t // tm
    rc = 32

    def body(dp_ref, h_ref, dho_ref, g_ref, w_ref, dh_ref, dw_ref, dg_ref, hn_scr, dhn_scr, acc_scr):
        i = pl.program_id(0)

        @pl.when(i == 0)
        def _():
            dw_ref[...] = jnp.zeros_like(dw_ref)
            acc_scr[...] = jnp.zeros_like(acc_scr)

        def c1(base):
            rows = pl.ds(base, rc)
            x = h_ref[rows, :]
            r = lax.rsqrt(jnp.mean(x * x, axis=-1, keepdims=True) + RMS_EPS)
            hn_scr[rows, :] = (x * r * g_ref[...]).astype(BF16)
        _chunks(tm // rc, rc, c1)

        dp = dp_ref[...]
        dhn_scr[...] = _dot_nt(dp, w_ref[...])
        dw_ref[...] += _dot_tn(hn_scr[...], dp)

        def c2(base):
            rows = pl.ds(base, rc)
            x = h_ref[rows, :]
            r = lax.rsqrt(jnp.mean(x * x, axis=-1, keepdims=True) + RMS_EPS)
            nrm = x * r
            dhn = dhn_scr[rows, :]
            acc_scr[...] += _rows8(dhn * nrm)
            dq = dhn * g_ref[...]
            dh_ref[rows, :] = dho_ref[rows, :] + r * (dq - nrm * jnp.mean(dq * nrm, axis=-1, keepdims=True))
        _chunks(tm // rc, rc, c2)

        @pl.when(i == nt - 1)
        def _():
            dg_ref[...] = jnp.zeros_like(dg_ref)
            dg_ref[pl.ds(0, 1), :] = jnp.sum(acc_scr[...], axis=0, keepdims=True)

    return pl.pallas_call(
        body, name=name, grid=(nt,),
        in_specs=[pl.BlockSpec((tm, n), lambda i: (i, 0)),
                  pl.BlockSpec((tm, d), lambda i: (i, 0)),
                  pl.BlockSpec((tm, d), lambda i: (i, 0)),
                  _const((None, 1, d), (layer, 0, 0)),
                  _const((None, d, n), (w_layer, 0, 0))],
        out_specs=[pl.BlockSpec((tm, d), lambda i: (i, 0)),
                   _const((d, n), (0, 0)),
                   _const((SUBLANES, d), (0, 0))],
        out_shape=(jax.ShapeDtypeStruct((t, d), F32),
                   jax.ShapeDtypeStruct((d, n), F32),
                   jax.ShapeDtypeStruct((SUBLANES, d), F32)),
        scratch_shapes=[pltpu.VMEM((tm, d), BF16), pltpu.VMEM((tm, d), F32), pltpu.VMEM((SUBLANES, d), F32)],
        compiler_params=_params(),
    )(dp, h, dho, norm_g, w_in)


def _final(h, target, final_g, name):
    t, d = h.shape
    tm = _tile(t, 512)
    nt = t // tm
    rc = 32

    def body(h_ref, tg_ref, g_ref, dh_ref, dg_ref, loss_ref, acc_scr, lacc_scr):
        i = pl.program_id(0)

        @pl.when(i == 0)
        def _():
            acc_scr[...] = jnp.zeros_like(acc_scr)
            lacc_scr[...] = jnp.zeros_like(lacc_scr)

        def c1(base):
            rows = pl.ds(base, rc)
            x = h_ref[rows, :]
            r = lax.rsqrt(jnp.mean(x * x, axis=-1, keepdims=True) + RMS_EPS)
            nrm = x * r
            err = nrm * g_ref[...] - tg_ref[rows, :]
            lacc_scr[...] += _rows8(err * err)
            dy = err * (1.0 / d)
            acc_scr[...] += _rows8(dy * nrm)
            dq = dy * g_ref[...]
            dh_ref[rows, :] = r * (dq - nrm * jnp.mean(dq * nrm, axis=-1, keepdims=True))
        _chunks(tm // rc, rc, c1)

        @pl.when(i == nt - 1)
        def _():
            dg_ref[...] = jnp.zeros_like(dg_ref)
            dg_ref[pl.ds(0, 1), :] = jnp.sum(acc_scr[...], axis=0, keepdims=True)
            loss_ref[...] = jnp.broadcast_to(jnp.sum(lacc_scr[...]) * (0.5 / d), loss_ref.shape)

    return pl.pallas_call(
        body, name=name, grid=(nt,),
        in_specs=[pl.BlockSpec((tm, d), lambda i: (i, 0)),
                  pl.BlockSpec((tm, d), lambda i: (i, 0)),
                  _const((1, d), (0, 0))],
        out_specs=[pl.BlockSpec((tm, d), lambda i: (i, 0)),
                   _const((SUBLANES, d), (0, 0)),
                   _const((SUBLANES, 128), (0, 0))],
        out_shape=(jax.ShapeDtypeStruct((t, d), F32),
                   jax.ShapeDtypeStruct((SUBLANES, d), F32),
                   jax.ShapeDtypeStruct((SUBLANES, 128), F32)),
        scratch_shapes=[pltpu.VMEM((SUBLANES, d), F32), pltpu.VMEM((SUBLANES, d), F32)],
        compiler_params=_params(),
    )(h, target, final_g)


def _mesh_position():
    return lax.axis_index("x"), lax.axis_index("y"), lax.axis_index("c")


def _peer(j):
    x, y, c = _mesh_position()
    px = 1 - x if j & 4 else x
    py = 1 - y if j & 2 else y
    pc = 1 - c if j & 1 else c
    return (px, py, pc), 4 * px + 2 * py + pc


def _block(ref, axis, index, size):
    idx = [slice(None)] * len(ref.shape)
    idx[axis] = pl.ds(pl.multiple_of(index * size, size), size)
    return ref.at[tuple(idx)]


def _all_gather(blocks, axes, name):
    nten = len(blocks)
    out_shapes = []
    for b, ax in zip(blocks, axes):
        s = list(b.shape)
        s[ax] *= N_DEV
        out_shapes.append(jax.ShapeDtypeStruct(tuple(s), b.dtype))

    def body(*refs):
        ins, outs = refs[:nten], refs[nten:2 * nten]
        send_sems, recv_sems, local_sems = refs[2 * nten:]
        x, y, c = _mesh_position()
        me = 4 * x + 2 * y + c
        local = [pltpu.make_async_copy(ins[k], _block(outs[k], axes[k], me, ins[k].shape[axes[k]]), local_sems.at[k])
                 for k in range(nten)]
        for cp in local:
            cp.start()
        copies = []
        for j in range(1, N_DEV):
            to, _ = _peer(j)
            for k in range(nten):
                copies.append(pltpu.make_async_remote_copy(
                    src_ref=ins[k], dst_ref=_block(outs[k], axes[k], me, ins[k].shape[axes[k]]),
                    send_sem=send_sems.at[j - 1, k], recv_sem=recv_sems.at[j - 1, k],
                    device_id=to, device_id_type=MESH))
        for cp in copies:
            cp.start()
        for j in range(1, N_DEV):
            _, src = _peer(j)
            for k in range(nten):
                pltpu.make_async_remote_copy(
                    src_ref=ins[k], dst_ref=_block(outs[k], axes[k], src, ins[k].shape[axes[k]]),
                    send_sem=send_sems.at[j - 1, k], recv_sem=recv_sems.at[j - 1, k],
                    device_id=_peer(j)[0], device_id_type=MESH).wait_recv()
        for cp in copies:
            cp.wait_send()
        for cp in local:
            cp.wait()

    any_spec = pl.BlockSpec(memory_space=pl.ANY)
    return pl.pallas_call(
        body, name=name,
        in_specs=[any_spec] * nten, out_specs=[any_spec] * nten, out_shape=out_shapes,
        scratch_shapes=[pltpu.SemaphoreType.DMA((N_DEV - 1, nten)), pltpu.SemaphoreType.DMA((N_DEV - 1, nten)),
                        pltpu.SemaphoreType.DMA((nten,))],
    )(*blocks)


def _scatter_partials(groups, axes, name):
    nkind = len(groups)
    blk = []
    out_shapes = []
    flat = []
    for k, (parts, ax) in enumerate(zip(groups, axes)):
        s = list(parts[0].shape)
        if ax is not None:
            s[ax] //= N_DEV
            blk.append(s[ax])
        else:
            blk.append(None)
        out_shapes.append(jax.ShapeDtypeStruct((N_DEV, len(parts)) + tuple(s), parts[0].dtype))
        flat += [(k, l) for l in range(len(parts))]
    nin = len(flat)

    def body(*refs):
        ins, outs = refs[:nin], refs[nin:nin + nkind]
        send_sems, recv_sems, local_sems = refs[nin + nkind:]
        x, y, c = _mesh_position()
        me = 4 * x + 2 * y + c

        def src_of(n, owner):
            k, _ = flat[n]
            return ins[n] if axes[k] is None else _block(ins[n], axes[k], owner, blk[k])

        def dst_of(n, sender):
            k, l = flat[n]
            return outs[k].at[sender, l]

        local = [pltpu.make_async_copy(src_of(n, me), dst_of(n, me), local_sems.at[n]) for n in range(nin)]
        for cp in local:
            cp.start()
        copies = []
        for j in range(1, N_DEV):
            to, to_id = _peer(j)
            for n in range(nin):
                copies.append(pltpu.make_async_remote_copy(
                    src_ref=src_of(n, to_id), dst_ref=dst_of(n, me),
                    send_sem=send_sems.at[j - 1, n], recv_sem=recv_sems.at[j - 1, n],
                    device_id=to, device_id_type=MESH))
        for cp in copies:
            cp.start()
        for j in range(1, N_DEV):
            frm, frm_id = _peer(j)
            for n in range(nin):
                pltpu.make_async_remote_copy(
                    src_ref=src_of(n, me), dst_ref=dst_of(n, frm_id),
                    send_sem=send_sems.at[j - 1, n], recv_sem=recv_sems.at[j - 1, n],
                    device_id=frm, device_id_type=MESH).wait_recv()
        for cp in copies:
            cp.wait_send()
        for cp in local:
            cp.wait()

    any_spec = pl.BlockSpec(memory_space=pl.ANY)
    return pl.pallas_call(
        body, name=name,
        in_specs=[any_spec] * nin, out_specs=[any_spec] * nkind, out_shape=out_shapes,
        scratch_shapes=[pltpu.SemaphoreType.DMA((N_DEV - 1, nin)), pltpu.SemaphoreType.DMA((N_DEV - 1, nin)),
                        pltpu.SemaphoreType.DMA((nin,))],
    )(*[p for parts in groups for p in parts])


def _adamw(stack, w, m, v, name):
    r, c = w.shape
    tr = r
    for cand in (512, 256, 128, 64, 32, 16, 8):
        if r % cand == 0 and r > cand:
            tr = cand
            break
    c1 = 1.0 / (1.0 - ADAM_B1 ** ADAM_STEP)
    c2 = 1.0 / (1.0 - ADAM_B2 ** ADAM_STEP)

    def body(s_ref, w_ref, m_ref, v_ref, g_ref, d_ref, nm_ref, nv_ref):
        g = s_ref[0]
        for k in range(1, N_DEV):
            g = g + s_ref[k]
        nm = ADAM_B1 * m_ref[...] + (1.0 - ADAM_B1) * g
        nv = ADAM_B2 * v_ref[...] + (1.0 - ADAM_B2) * (g * g)
        g_ref[...] = g
        nm_ref[...] = nm
        nv_ref[...] = nv
        d_ref[...] = -ADAM_LR * ((nm * c1) / (jnp.sqrt(nv * c2) + ADAM_EPS) + ADAM_WD * w_ref[...])

    spec = pl.BlockSpec((tr, c), lambda i: (i, 0))
    return pl.pallas_call(
        body, name=name, grid=(r // tr,),
        in_specs=[pl.BlockSpec((N_DEV, tr, c), lambda i: (0, i, 0)), spec, spec, spec],
        out_specs=[spec] * 4,
        out_shape=[jax.ShapeDtypeStruct((r, c), F32)] * 4,
        compiler_params=_params(),
    )(stack, w, m, v)


def _adamw_nd(stack, w, m, v, name):
    shp = w.shape
    r = 1
    for s in shp[:-1]:
        r *= s
    outs = _adamw(stack.reshape(N_DEV, r, shp[-1]), w.reshape(r, shp[-1]), m.reshape(r, shp[-1]),
                  v.reshape(r, shp[-1]), name)
    return [o.reshape(shp) for o in outs]


SMALL_ROWS = 40


def _pack_sharded_small(conv_dw, pool_b_grp, pool_scale):
    nl, _, c = conv_dw.shape
    z = lambda n: jnp.zeros((nl, n, c), F32)
    return jnp.concatenate([conv_dw, z(1), pool_b_grp[:, None, :], pool_scale[:, None, :], z(SMALL_ROWS - 34)], axis=1)


def _unpack_sharded_small(a):
    return a[:, :CONV_TAPS, :], a[:, 32, :], a[:, 33, :]


REP_ROWS = 16


def _pack_replicated(norm_g, final_g, conv_dw_b, conv_ln_g, conv_ln_b):
    d = final_g.shape[0]
    return jnp.concatenate([norm_g, final_g[None, :], conv_dw_b, conv_ln_g, conv_ln_b,
                            jnp.zeros((REP_ROWS - 11, d), F32)], axis=0)


def _unpack_replicated(a):
    return a[0:4], a[4], a[5:7], a[7:9], a[9:11]


def kernel(x, norm_g, final_g, conv_w_in, conv_dw, conv_dw_b, conv_ln_g, conv_ln_b, conv_w_out, pool_w_in, pool_w_grp, pool_b_grp, pool_scale, pool_w_out, loss_target, m_norm_g, m_final_g, m_conv_w_in, m_conv_dw, m_conv_dw_b, m_conv_ln_g, m_conv_ln_b, m_conv_w_out, m_pool_w_in, m_pool_w_grp, m_pool_b_grp, m_pool_scale, m_pool_w_out, v_norm_g, v_final_g, v_conv_w_in, v_conv_dw, v_conv_dw_b, v_conv_ln_g, v_conv_ln_b, v_conv_w_out, v_pool_w_in, v_pool_w_grp, v_pool_b_grp, v_pool_scale, v_pool_w_out):
    h0 = x[0]
    target = loss_target[0]
    d = h0.shape[-1]
    n_layers = norm_g.shape[0]

    small_sh = _pack_sharded_small(conv_dw, pool_b_grp, pool_scale)
    cw_in, cw_out, pw_in, pw_grp, pw_out, small_full = _all_gather(
        [conv_w_in.astype(BF16), conv_w_out.astype(BF16), pool_w_in.astype(BF16), pool_w_grp.astype(BF16),
         pool_w_out.astype(BF16), small_sh],
        [2, 1, 2, 2, 1, 2], "gather_weights")
    conv_small = small_full.at[:, 32, :].set(conv_dw_b).at[:, 33, :].set(conv_ln_g).at[:, 34, :].set(conv_ln_b)
    pool_small = small_full
    ng3 = norm_g[:, None, :]

    hs = [h0]
    saved = []
    for i in range(n_layers):
        j = i // 2
        if i % 2 == 0:
            p = _in_fwd(hs[-1], ng3, i, cw_in, j, f"conv_in_fwd_{j}")
            ho, uc = _conv_fwd(p, hs[-1], conv_small, cw_out, j, f"conv_mix_fwd_{j}")
            saved.append((p, uc))
        else:
            p = _in_fwd(hs[-1], ng3, i, pw_in, j, f"pool_in_fwd_{j}")
            ho = _pool_fwd(p, hs[-1], pw_grp, pool_small, pw_out, j, f"pool_mix_fwd_{j}")
            saved.append((p,))
        hs.append(ho)

    dh, d_final_g, loss_part = _final(hs[-1], target, final_g[None, :], "final_loss")
    d_norm_g = [None] * n_layers
    g_cw_in, g_cw_out, g_pw_in, g_pw_grp, g_pw_out = {}, {}, {}, {}, {}
    g_conv_small, g_pool_small = {}, {}
    for i in reversed(range(n_layers)):
        j = i // 2
        if i % 2 == 0:
            p, uc = saved[i]
            dp, g_cw_out[j], g_conv_small[j] = _conv_bwd(dh, p, uc, conv_small, cw_out, j, f"conv_mix_bwd_{j}")
            dh, g_cw_in[j], dg = _in_bwd(dp, hs[i], dh, ng3, i, cw_in, j, f"conv_in_bwd_{j}")
        else:
            (p,) = saved[i]
            dp, g_pw_out[j], g_pw_grp[j], g_pool_small[j] = _pool_bwd(dh, p, pw_grp, pool_small, pw_out, j,
                                                                      f"pool_mix_bwd_{j}")
            dh, g_pw_in[j], dg = _in_bwd(dp, hs[i], dh, ng3, i, pw_in, j, f"pool_in_bwd_{j}")
        d_norm_g[i] = dg[0]
    grad_x = dh[None]

    nl = n_layers // 2
    layers = range(nl)
    sharded_small_part = jnp.stack([
        jnp.concatenate([g_conv_small[j][:32], g_pool_small[j][32:34], jnp.zeros((SMALL_ROWS - 34, d), F32)], axis=0)
        for j in layers])
    rep_part = jnp.concatenate(
        [jnp.stack(d_norm_g), d_final_g[0:1]]
        + [jnp.stack([g_conv_small[j][r] for j in layers]) for r in (32, 33, 34)]
        + [jnp.broadcast_to(loss_part[0:1, 0:1], (1, d)), jnp.zeros((REP_ROWS - 12, d), F32)], axis=0)
    groups = [[g_cw_in[j] for j in layers], [g_cw_out[j] for j in layers], [g_pw_in[j] for j in layers],
              [g_pw_grp[j] for j in layers], [g_pw_out[j] for j in layers], [sharded_small_part], [rep_part]]
    s_cw_in, s_cw_out, s_pw_in, s_pw_grp, s_pw_out, s_small, s_rep = _scatter_partials(
        groups, [1, 0, 1, 1, 0, 2, None], "scatter_grads")

    res = {}
    res["conv_w_in"] = _adamw_nd(s_cw_in, conv_w_in, m_conv_w_in, v_conv_w_in, "adamw_conv_w_in")
    res["conv_w_out"] = _adamw_nd(s_cw_out, conv_w_out, m_conv_w_out, v_conv_w_out, "adamw_conv_w_out")
    res["pool_w_in"] = _adamw_nd(s_pw_in, pool_w_in, m_pool_w_in, v_pool_w_in, "adamw_pool_w_in")
    res["pool_w_grp"] = _adamw_nd(s_pw_grp, pool_w_grp, m_pool_w_grp, v_pool_w_grp, "adamw_pool_w_grp")
    res["pool_w_out"] = _adamw_nd(s_pw_out, pool_w_out, m_pool_w_out, v_pool_w_out, "adamw_pool_w_out")
    sm = _adamw_nd(s_small[:, 0], small_sh, _pack_sharded_small(m_conv_dw, m_pool_b_grp, m_pool_scale),
                   _pack_sharded_small(v_conv_dw, v_pool_b_grp, v_pool_scale), "adamw_sharded_small")
    rep = _adamw_nd(s_rep[:, 0], _pack_replicated(norm_g, final_g, conv_dw_b, conv_ln_g, conv_ln_b),
                    _pack_replicated(m_norm_g, m_final_g, m_conv_dw_b, m_conv_ln_g, m_conv_ln_b),
                    _pack_replicated(v_norm_g, v_final_g, v_conv_dw_b, v_conv_ln_g, v_conv_ln_b), "adamw_replicated")
    loss = rep[0][11, 0]
    res["conv_dw"], res["pool_b_grp"], res["pool_scale"] = zip(*[_unpack_sharded_small(a) for a in sm])
    (res["norm_g"], res["final_g"], res["conv_dw_b"], res["conv_ln_g"], res["conv_ln_b"]) = zip(
        *[_unpack_replicated(a) for a in rep])

    names = ["norm_g", "final_g", "conv_w_in", "conv_dw", "conv_dw_b", "conv_ln_g", "conv_ln_b", "conv_w_out",
             "pool_w_in", "pool_w_grp", "pool_b_grp", "pool_scale", "pool_w_out"]
    return (loss, grad_x) + tuple(res[n][q] for q in range(4) for n in names)
```

```python
import functools

import jax
import jax.numpy as jnp
from jax import lax
from jax.experimental import pallas as pl
from jax.experimental.pallas import tpu as pltpu

F32 = jnp.float32
BF16 = jnp.bfloat16

RMS_EPS = 1e-6
LN_EPS = 1e-5
CONV_TAPS = 31
CONV_HALO = 32
POOL_WINDOWS = (2, 4, 8, 16)
POOL_HALO = 16
SUBLANES = 8
LANES = 128
N_DEV = 8
V7X_VMEM_LIMIT = 56 * 1024 * 1024

ADAM_LR = 0.001
ADAM_B1 = 0.9
ADAM_B2 = 0.999
ADAM_EPS = 1e-08
ADAM_WD = 0.01
ADAM_STEP = 10

MESH = pl.DeviceIdType.MESH


def _dot(a, b):
    return lax.dot_general(a, b, (((1,), (0,)), ((), ())), preferred_element_type=F32)


def _dot_nt(a, b):
    return lax.dot_general(a, b, (((1,), (1,)), ((), ())), preferred_element_type=F32)


def _dot_tn(a, b):
    return lax.dot_general(a, b, (((0,), (0,)), ((), ())), preferred_element_type=F32)


def _sigmoid(x):
    return jax.nn.sigmoid(x)


def _dsilu(x, s):
    return s * (1.0 + x * (1.0 - s))


def _rows8(x):
    r, c = x.shape
    return jnp.sum(x.reshape(r // SUBLANES, SUBLANES, c), axis=0)


def _tile(t, pref):
    return pref if t >= 2 * pref else t // 2


def _const(shape, index):
    return pl.BlockSpec(shape, lambda i: index, pipeline_mode=pl.Buffered(1))


def _params():
    return pltpu.CompilerParams(dimension_semantics=("arbitrary",), vmem_limit_bytes=V7X_VMEM_LIMIT)


def _chunks(n, rc, fn):
    def step(c, carry):
        fn(pl.multiple_of(c * rc, rc))
        return carry
    lax.fori_loop(0, n, step, 0)


def _in_fwd(h, norm_g, layer, w_in, w_layer, name):
    t, d = h.shape
    n = w_in.shape[-1]
    tm = _tile(t, 512)

    def body(h_ref, g_ref, w_ref, p_ref):
        x = h_ref[...]
        r = lax.rsqrt(jnp.mean(x * x, axis=-1, keepdims=True) + RMS_EPS)
        p_ref[...] = _dot((x * r * g_ref[...]).astype(BF16), w_ref[...])

    return pl.pallas_call(
        body, name=name, grid=(t // tm,),
        in_specs=[pl.BlockSpec((tm, d), lambda i: (i, 0)),
                  _const((None, 1, d), (layer, 0, 0)),
                  _const((None, d, n), (w_layer, 0, 0))],
        out_specs=pl.BlockSpec((tm, n), lambda i: (i, 0)),
        out_shape=jax.ShapeDtypeStruct((t, n), F32),
        compiler_params=_params(),
    )(h, norm_g, w_in)


def _layernorm_rows(uc, lng, lnb):
    mu = jnp.mean(uc, axis=-1, keepdims=True)
    xc = uc - mu
    rstd = lax.rsqrt(jnp.mean(xc * xc, axis=-1, keepdims=True) + LN_EPS)
    xhat = xc * rstd
    return xhat, rstd, xhat * lng + lnb


def _conv_fwd(p, h, small, w_out, layer, name):
    t, d = h.shape
    e = w_out.shape[1]
    tm = _tile(t, 256)
    rc = 32
    nsm = small.shape[1]

    def body(p_ref, h_ref, sm_ref, wo_ref, ho_ref, uc_ref, us_scr):
        i = pl.program_id(0)

        @pl.when(i == 0)
        def _():
            us_scr[:, pl.ds(0, CONV_HALO), :] = jnp.zeros((SUBLANES, CONV_HALO, e), F32)

        @pl.when(i > 0)
        def _():
            us_scr[:, pl.ds(0, CONV_HALO), :] = us_scr[:, pl.ds(tm, CONV_HALO), :]

        us_scr[0, pl.ds(CONV_HALO, tm), :] = p_ref[:, pl.ds(0, e)] * _sigmoid(p_ref[:, pl.ds(e, e)])
        for r in range(1, SUBLANES):
            us_scr[r, pl.ds(CONV_HALO, tm), :] = us_scr[0, pl.ds(CONV_HALO - r, tm), :]

        def c_conv(base):
            for lt in range(e // LANES):
                cols = pl.ds(lt * LANES, LANES)
                acc = jnp.broadcast_to(sm_ref[pl.ds(32, 1), cols], (rc, LANES))
                for r in range(SUBLANES):
                    nq = (CONV_TAPS - 1 - r) // SUBLANES + 1
                    lo = SUBLANES * (nq - 1)
                    win = us_scr[r, pl.ds(pl.multiple_of(CONV_HALO + base - lo, SUBLANES), rc + lo), cols]
                    for q in range(nq):
                        k = CONV_TAPS - 1 - (SUBLANES * q + r)
                        at = lo - SUBLANES * q
                        acc = acc + sm_ref[pl.ds(k, 1), cols] * win[at:at + rc, :]
                uc_ref[pl.ds(base, rc), cols] = acc
        _chunks(tm // rc, rc, c_conv)

        _, _, ul = _layernorm_rows(uc_ref[...], sm_ref[pl.ds(33, 1), :], sm_ref[pl.ds(34, 1), :])
        z = p_ref[:, pl.ds(2 * e, e)]
        v = ((ul * _sigmoid(ul)) * (z * _sigmoid(z))).astype(BF16)
        ho_ref[...] = h_ref[...] + _dot(v, wo_ref[...])

    return pl.pallas_call(
        body, name=name, grid=(t // tm,),
        in_specs=[pl.BlockSpec((tm, 3 * e), lambda i: (i, 0)),
                  pl.BlockSpec((tm, d), lambda i: (i, 0)),
                  _const((None, nsm, e), (layer, 0, 0)),
                  _const((None, e, d), (layer, 0, 0))],
        out_specs=[pl.BlockSpec((tm, d), lambda i: (i, 0)),
                   pl.BlockSpec((tm, e), lambda i: (i, 0))],
        out_shape=(jax.ShapeDtypeStruct((t, d), F32), jax.ShapeDtypeStruct((t, e), F32)),
        scratch_shapes=[pltpu.VMEM((SUBLANES, tm + CONV_HALO, e), F32)],
        compiler_params=_params(),
    )(p, h, small, w_out)


def _conv_bwd(dho, p, uc, small, w_out, layer, name):
    t, d = dho.shape
    e = w_out.shape[1]
    tm = _tile(t, 256)
    nt = t // tm
    rc = 16
    nsm = small.shape[1]

    def body(dho_ref, p_ref, uc_ref, sm_ref, wo_ref, dp_ref, dwo_ref, dsm_ref, ds_scr, acc_scr):
        i = pl.program_id(0)

        @pl.when(i == 0)
        def _():
            dwo_ref[...] = jnp.zeros_like(dwo_ref)
            acc_scr[...] = jnp.zeros_like(acc_scr)
            ds_scr[:, pl.ds(tm, CONV_HALO), :] = jnp.zeros((SUBLANES, CONV_HALO, e), F32)

        @pl.when(i > 0)
        def _():
            ds_scr[:, pl.ds(tm, CONV_HALO), :] = ds_scr[:, pl.ds(0, CONV_HALO), :]

        lng = sm_ref[pl.ds(33, 1), :]
        lnb = sm_ref[pl.ds(34, 1), :]

        xhat, rstd, ul = _layernorm_rows(uc_ref[...], lng, lnb)
        z = p_ref[:, pl.ds(2 * e, e)]
        sg_u = _sigmoid(ul)
        sg_z = _sigmoid(z)
        s_u = ul * sg_u
        s_z = z * sg_z
        v = (s_u * s_z).astype(BF16)
        dy = dho_ref[...].astype(BF16)
        dv = _dot_nt(dy, wo_ref[...])
        dwo_ref[...] += _dot_tn(v, dy)

        dul = dv * s_z * _dsilu(ul, sg_u)
        dp_ref[:, pl.ds(2 * e, e)] = (dv * s_u * _dsilu(z, sg_z)).astype(BF16)
        acc_scr[pl.ds(33 * SUBLANES, SUBLANES), :] += _rows8(dul * xhat)
        acc_scr[pl.ds(34 * SUBLANES, SUBLANES), :] += _rows8(dul)
        dxh = dul * lng
        duc = rstd * (dxh - jnp.mean(dxh, axis=-1, keepdims=True)
                      - xhat * jnp.mean(dxh * xhat, axis=-1, keepdims=True))
        acc_scr[pl.ds(32 * SUBLANES, SUBLANES), :] += _rows8(duc)
        ds_scr[0, pl.ds(0, tm), :] = duc
        for r in range(1, SUBLANES):
            ds_scr[r, pl.ds(0, tm), :] = ds_scr[0, pl.ds(r, tm), :]

        def c_conv(base):
            rows = pl.ds(base, rc)
            a = p_ref[rows, pl.ds(0, e)]
            b = p_ref[rows, pl.ds(e, e)]
            sb = _sigmoid(b)
            u = a * sb
            du = jnp.zeros((rc, e), F32)
            for o in range(CONV_TAPS):
                q, r = divmod(o, SUBLANES)
                k = CONV_TAPS - 1 - o
                sh = ds_scr[r, pl.ds(pl.multiple_of(base + SUBLANES * q, SUBLANES), rc), :]
                du = du + sm_ref[pl.ds(k, 1), :] * sh
                acc_scr[pl.ds(k * SUBLANES, SUBLANES), :] += _rows8(u * sh)
            dp_ref[rows, pl.ds(0, e)] = (du * sb).astype(BF16)
            dp_ref[rows, pl.ds(e, e)] = (du * u * (1.0 - sb)).astype(BF16)
        _chunks(tm // rc, rc, c_conv)

        @pl.when(i == nt - 1)
        def _():
            for k in range(nsm):
                dsm_ref[pl.ds(k, 1), :] = jnp.sum(acc_scr[pl.ds(k * SUBLANES, SUBLANES), :], axis=0, keepdims=True)

    rev = lambda i: (nt - 1 - i, 0)
    return pl.pallas_call(
        body, name=name, grid=(nt,),
        in_specs=[pl.BlockSpec((tm, d), rev),
                  pl.BlockSpec((tm, 3 * e), rev),
                  pl.BlockSpec((tm, e), rev),
                  _const((None, nsm, e), (layer, 0, 0)),
                  _const((None, e, d), (layer, 0, 0))],
        out_specs=[pl.BlockSpec((tm, 3 * e), rev),
                   _const((e, d), (0, 0)),
                   _const((nsm, e), (0, 0))],
        out_shape=(jax.ShapeDtypeStruct((t, 3 * e), BF16),
                   jax.ShapeDtypeStruct((e, d), F32),
                   jax.ShapeDtypeStruct((nsm, e), F32)),
        scratch_shapes=[pltpu.VMEM((SUBLANES, tm + CONV_HALO, e), F32),
                        pltpu.VMEM((nsm * SUBLANES, e), F32)],
        compiler_params=_params(),
    )(dho, p, uc, small, w_out)


def _inv_count(tile, tm, w):
    tpos = tile * tm + lax.broadcasted_iota(jnp.int32, (tm, 1), 0)
    return 1.0 / jnp.minimum(tpos + 1, w).astype(F32)


def _pool_d(ue_scr, tile, tm, gc):
    out = []
    for g, w in enumerate(POOL_WINDOWS):
        win = ue_scr[:, pl.ds(g * gc, gc)]
        s = win
        sh = 1
        while sh < w:
            s = s + pltpu.roll(s, sh, axis=0)
            sh *= 2
        out.append(s[POOL_HALO:, :] * _inv_count(tile, tm, w) - win[POOL_HALO:, :])
    return out


def _pool_fwd(p, h, w_grp, small, w_out, layer, name):
    t, d = h.shape
    e = w_out.shape[1]
    gc = e // len(POOL_WINDOWS)
    tm = _tile(t, 256)
    nsm = small.shape[1]

    def body(p_ref, h_ref, wg_ref, sm_ref, wo_ref, ho_ref, ue_scr, y_scr):
        i = pl.program_id(0)

        @pl.when(i == 0)
        def _():
            ue_scr[pl.ds(0, POOL_HALO), :] = jnp.zeros((POOL_HALO, e), F32)

        @pl.when(i > 0)
        def _():
            ue_scr[pl.ds(0, POOL_HALO), :] = ue_scr[pl.ds(tm, POOL_HALO), :]

        ue_scr[pl.ds(POOL_HALO, tm), :] = p_ref[:, pl.ds(0, e)]

        for g, dg in enumerate(_pool_d(ue_scr, i, tm, gc)):
            cols = pl.ds(g * gc, gc)
            z = p_ref[:, pl.ds(e + g * gc, gc)]
            y1 = (_dot(dg.astype(BF16), wg_ref[g]) + sm_ref[pl.ds(32, 1), cols]) * sm_ref[pl.ds(33, 1), cols]
            y_scr[:, cols] = (y1 * (z * _sigmoid(z))).astype(BF16)

        ho_ref[...] = h_ref[...] + _dot(y_scr[...], wo_ref[...])

    return pl.pallas_call(
        body, name=name, grid=(t // tm,),
        in_specs=[pl.BlockSpec((tm, 2 * e), lambda i: (i, 0)),
                  pl.BlockSpec((tm, d), lambda i: (i, 0)),
                  _const((None, len(POOL_WINDOWS), gc, gc), (layer, 0, 0, 0)),
                  _const((None, nsm, e), (layer, 0, 0)),
                  _const((None, e, d), (layer, 0, 0))],
        out_specs=pl.BlockSpec((tm, d), lambda i: (i, 0)),
        out_shape=jax.ShapeDtypeStruct((t, d), F32),
        scratch_shapes=[pltpu.VMEM((tm + POOL_HALO, e), F32), pltpu.VMEM((tm, e), BF16)],
        compiler_params=_params(),
    )(p, h, w_grp, small, w_out)


def _pool_bwd(dho, p, w_grp, small, w_out, layer, name):
    t, d = dho.shape
    e = w_out.shape[1]
    ng = len(POOL_WINDOWS)
    gc = e // ng
    tm = _tile(t, 256)
    nt = t // tm
    nsm = small.shape[1]
    hb = tm // POOL_HALO

    def body(dho_ref, p_ref, ph_ref, wg_ref, sm_ref, wo_ref, dp_ref, dwo_ref, dwg_ref, dsm_ref,
             ue_scr, ee_scr, acc_scr):
        i = pl.program_id(0)
        tile = nt - 1 - i

        @pl.when(i == 0)
        def _():
            dwo_ref[...] = jnp.zeros_like(dwo_ref)
            dwg_ref[...] = jnp.zeros_like(dwg_ref)
            acc_scr[...] = jnp.zeros_like(acc_scr)
            ee_scr[pl.ds(tm, POOL_HALO), :] = jnp.zeros((POOL_HALO, e), F32)

        @pl.when(i > 0)
        def _():
            ee_scr[pl.ds(tm, POOL_HALO), :] = ee_scr[pl.ds(0, POOL_HALO), :]

        @pl.when(tile == 0)
        def _():
            ue_scr[pl.ds(0, POOL_HALO), :] = jnp.zeros((POOL_HALO, e), F32)

        @pl.when(tile > 0)
        def _():
            ue_scr[pl.ds(0, POOL_HALO), :] = ph_ref[:, pl.ds(0, e)]

        ue_scr[pl.ds(POOL_HALO, tm), :] = p_ref[:, pl.ds(0, e)]

        bg = sm_ref[pl.ds(32, 1), :]
        sc = sm_ref[pl.ds(33, 1), :]
        ds = [dg.astype(BF16) for dg in _pool_d(ue_scr, tile, tm, gc)]
        ob = jnp.concatenate([_dot(ds[g], wg_ref[g]) for g in range(ng)], axis=1) + bg
        z = p_ref[:, pl.ds(e, e)]
        sg_z = _sigmoid(z)
        s_z = z * sg_z
        y1 = ob * sc
        dy = dho_ref[...].astype(BF16)
        dy2 = _dot_nt(dy, wo_ref[...])
        dwo_ref[...] += _dot_tn((y1 * s_z).astype(BF16), dy)
        dy1 = dy2 * s_z
        dp_ref[:, pl.ds(e, e)] = (dy2 * y1 * _dsilu(z, sg_z)).astype(BF16)
        acc_scr[pl.ds(33 * SUBLANES, SUBLANES), :] += _rows8(dy1 * ob)
        do = dy1 * sc
        acc_scr[pl.ds(32 * SUBLANES, SUBLANES), :] += _rows8(do)

        n = tm + POOL_HALO
        for g, w in enumerate(POOL_WINDOWS):
            cols = pl.ds(g * gc, gc)
            do_g = do[:, g * gc:(g + 1) * gc].astype(BF16)
            dwg_ref[g] += _dot_tn(ds[g], do_g)
            dd = _dot_nt(do_g, wg_ref[g])
            ee_scr[pl.ds(0, tm), cols] = dd * _inv_count(tile, tm, w)
            s = ee_scr[:, cols]
            sh = 1
            while sh < w:
                s = s + pltpu.roll(s, n - sh, axis=0)
                sh *= 2
            dp_ref[:, cols] = (s[:tm, :] - dd).astype(BF16)

        @pl.when(i == nt - 1)
        def _():
            dsm_ref[...] = jnp.zeros_like(dsm_ref)
            for k in (32, 33):
                dsm_ref[pl.ds(k, 1), :] = jnp.sum(acc_scr[pl.ds(k * SUBLANES, SUBLANES), :], axis=0, keepdims=True)

    rev = lambda i: (nt - 1 - i, 0)
    return pl.pallas_call(
        body, name=name, grid=(nt,),
        in_specs=[pl.BlockSpec((tm, d), rev),
                  pl.BlockSpec((tm, 2 * e), rev),
                  pl.BlockSpec((POOL_HALO, 2 * e), lambda i: (jnp.maximum((nt - 1 - i) * hb - 1, 0), 0)),
                  _const((None, ng, gc, gc), (layer, 0, 0, 0)),
                  _const((None, nsm, e), (layer, 0, 0)),
                  _const((None, e, d), (layer, 0, 0))],
        out_specs=[pl.BlockSpec((tm, 2 * e), rev),
                   _const((e, d), (0, 0)),
                   _const((ng, gc, gc), (0, 0, 0)),
                   _const((nsm, e), (0, 0))],
        out_shape=(jax.ShapeDtypeStruct((t, 2 * e), BF16),
                   jax.ShapeDtypeStruct((e, d), F32),
                   jax.ShapeDtypeStruct((ng, gc, gc), F32),
                   jax.ShapeDtypeStruct((nsm, e), F32)),
        scratch_shapes=[pltpu.VMEM((tm + POOL_HALO, e), F32), pltpu.VMEM((tm + POOL_HALO, e), F32),
                        pltpu.VMEM((nsm * SUBLANES, e), F32)],
        compiler_params=_params(),
    )(dho, p, p, w_grp, small, w_out)


def _in_bwd(dp, h, dho, norm_g, layer, w_in, w_layer, name):
    t, d = h.shape
    n = w_in.shape[-1]
    tm = _tile(t, 512)
    nt = t // tm

    def body(dp_ref, h_ref, dho_ref, g_ref, w_ref, dh_ref, dw_ref, dg_ref, acc_scr):
        i = pl.program_id(0)

        @pl.when(i == 0)
        def _():
            dw_ref[...] = jnp.zeros_like(dw_ref)
            acc_scr[...] = jnp.zeros_like(acc_scr)

        x = h_ref[...]
        r = lax.rsqrt(jnp.mean(x * x, axis=-1, keepdims=True) + RMS_EPS)
        nrm = x * r
        dp = dp_ref[...]
        dhn = _dot_nt(dp, w_ref[...])
        dw_ref[...] += _dot_tn((nrm * g_ref[...]).astype(BF16), dp)
        acc_scr[...] += _rows8(dhn * nrm)
        dq = dhn * g_ref[...]
        dh_ref[...] = dho_ref[...] + r * (dq - nrm * jnp.mean(dq * nrm, axis=-1, keepdims=True))

        @pl.when(i == nt - 1)
        def _():
            dg_ref[...] = jnp.zeros_like(dg_ref)
            dg_ref[pl.ds(0, 1), :] = jnp.sum(acc_scr[...], axis=0, keepdims=True)

    return pl.pallas_call(
        body, name=name, grid=(nt,),
        in_specs=[pl.BlockSpec((tm, n), lambda i: (i, 0)),
                  pl.BlockSpec((tm, d), lambda i: (i, 0)),
                  pl.BlockSpec((tm, d), lambda i: (i, 0)),
                  _const((None, 1, d), (layer, 0, 0)),
                  _const((None, d, n), (w_layer, 0, 0))],
        out_specs=[pl.BlockSpec((tm, d), lambda i: (i, 0)),
                   _const((d, n), (0, 0)),
                   _const((SUBLANES, d), (0, 0))],
        out_shape=(jax.ShapeDtypeStruct((t, d), F32),
                   jax.ShapeDtypeStruct((d, n), F32),
                   jax.ShapeDtypeStruct((SUBLANES, d), F32)),
        scratch_shapes=[pltpu.VMEM((SUBLANES, d), F32)],
        compiler_params=_params(),
    )(dp, h, dho, norm_g, w_in)


def _final(h, target, final_g, name):
    t, d = h.shape
    tm = _tile(t, 512)
    nt = t // tm

    def body(h_ref, tg_ref, g_ref, dh_ref, dg_ref, loss_ref, acc_scr, lacc_scr):
        i = pl.program_id(0)

        @pl.when(i == 0)
        def _():
            acc_scr[...] = jnp.zeros_like(acc_scr)
            lacc_scr[...] = jnp.zeros_like(lacc_scr)

        x = h_ref[...]
        r = lax.rsqrt(jnp.mean(x * x, axis=-1, keepdims=True) + RMS_EPS)
        nrm = x * r
        err = nrm * g_ref[...] - tg_ref[...]
        lacc_scr[...] += _rows8(err * err)
        dy = err * (1.0 / d)
        acc_scr[...] += _rows8(dy * nrm)
        dq = dy * g_ref[...]
        dh_ref[...] = r * (dq - nrm * jnp.mean(dq * nrm, axis=-1, keepdims=True))

        @pl.when(i == nt - 1)
        def _():
            dg_ref[...] = jnp.zeros_like(dg_ref)
            dg_ref[pl.ds(0, 1), :] = jnp.sum(acc_scr[...], axis=0, keepdims=True)
            loss_ref[...] = jnp.broadcast_to(jnp.sum(lacc_scr[...]) * (0.5 / d), loss_ref.shape)

    return pl.pallas_call(
        body, name=name, grid=(nt,),
        in_specs=[pl.BlockSpec((tm, d), lambda i: (i, 0)),
                  pl.BlockSpec((tm, d), lambda i: (i, 0)),
                  _const((1, d), (0, 0))],
        out_specs=[pl.BlockSpec((tm, d), lambda i: (i, 0)),
                   _const((SUBLANES, d), (0, 0)),
                   _const((SUBLANES, LANES), (0, 0))],
        out_shape=(jax.ShapeDtypeStruct((t, d), F32),
                   jax.ShapeDtypeStruct((SUBLANES, d), F32),
                   jax.ShapeDtypeStruct((SUBLANES, LANES), F32)),
        scratch_shapes=[pltpu.VMEM((SUBLANES, d), F32), pltpu.VMEM((SUBLANES, d), F32)],
        compiler_params=_params(),
    )(h, target, final_g)


def _mesh_position():
    return lax.axis_index("x"), lax.axis_index("y"), lax.axis_index("c")


def _peer(j):
    x, y, c = _mesh_position()
    px = 1 - x if j & 4 else x
    py = 1 - y if j & 2 else y
    pc = 1 - c if j & 1 else c
    return (px, py, pc), 4 * px + 2 * py + pc


def _block(ref, axis, index, size):
    idx = [slice(None)] * len(ref.shape)
    idx[axis] = pl.ds(pl.multiple_of(index * size, size), size)
    return ref.at[tuple(idx)]


def _all_gather(blocks, axes, name):
    nten = len(blocks)
    out_shapes = []
    for b, ax in zip(blocks, axes):
        s = list(b.shape)
        s[ax] *= N_DEV
        out_shapes.append(jax.ShapeDtypeStruct(tuple(s), b.dtype))

    def body(*refs):
        ins, outs = refs[:nten], refs[nten:2 * nten]
        send_sems, recv_sems, local_sems = refs[2 * nten:]
        x, y, c = _mesh_position()
        me = 4 * x + 2 * y + c
        local = [pltpu.make_async_copy(ins[k], _block(outs[k], axes[k], me, ins[k].shape[axes[k]]), local_sems.at[k])
                 for k in range(nten)]
        for cp in local:
            cp.start()
        copies = []
        for j in range(1, N_DEV):
            to, _ = _peer(j)
            for k in range(nten):
                copies.append(pltpu.make_async_remote_copy(
                    src_ref=ins[k], dst_ref=_block(outs[k], axes[k], me, ins[k].shape[axes[k]]),
                    send_sem=send_sems.at[j - 1, k], recv_sem=recv_sems.at[j - 1, k],
                    device_id=to, device_id_type=MESH))
        for cp in copies:
            cp.start()
        for j in range(1, N_DEV):
            _, src = _peer(j)
            for k in range(nten):
                pltpu.make_async_remote_copy(
                    src_ref=ins[k], dst_ref=_block(outs[k], axes[k], src, ins[k].shape[axes[k]]),
                    send_sem=send_sems.at[j - 1, k], recv_sem=recv_sems.at[j - 1, k],
                    device_id=_peer(j)[0], device_id_type=MESH).wait_recv()
        for cp in copies:
            cp.wait_send()
        for cp in local:
            cp.wait()

    any_spec = pl.BlockSpec(memory_space=pl.ANY)
    return pl.pallas_call(
        body, name=name,
        in_specs=[any_spec] * nten, out_specs=[any_spec] * nten, out_shape=out_shapes,
        scratch_shapes=[pltpu.SemaphoreType.DMA((N_DEV - 1, nten)), pltpu.SemaphoreType.DMA((N_DEV - 1, nten)),
                        pltpu.SemaphoreType.DMA((nten,))],
    )(*blocks)


def _scatter_partials(groups, axes, name):
    nkind = len(groups)
    blk = []
    out_shapes = []
    flat = []
    for k, (parts, ax) in enumerate(zip(groups, axes)):
        s = list(parts[0].shape)
        if ax is not None:
            s[ax] //= N_DEV
            blk.append(s[ax])
        else:
            blk.append(None)
        out_shapes.append(jax.ShapeDtypeStruct((N_DEV, len(parts)) + tuple(s), parts[0].dtype))
        flat += [(k, l) for l in range(len(parts))]
    nin = len(flat)

    def body(*refs):
        ins, outs = refs[:nin], refs[nin:nin + nkind]
        send_sems, recv_sems, local_sems = refs[nin + nkind:]
        x, y, c = _mesh_position()
        me = 4 * x + 2 * y + c

        def src_of(n, owner):
            k, _ = flat[n]
            return ins[n] if axes[k] is None else _block(ins[n], axes[k], owner, blk[k])

        def dst_of(n, sender):
            k, l = flat[n]
            return outs[k].at[sender, l]

        local = [pltpu.make_async_copy(src_of(n, me), dst_of(n, me), local_sems.at[n]) for n in range(nin)]
        for cp in local:
            cp.start()
        copies = []
        for j in range(1, N_DEV):
            to, to_id = _peer(j)
            for n in range(nin):
                copies.append(pltpu.make_async_remote_copy(
                    src_ref=src_of(n, to_id), dst_ref=dst_of(n, me),
                    send_sem=send_sems.at[j - 1, n], recv_sem=recv_sems.at[j - 1, n],
                    device_id=to, device_id_type=MESH))
        for cp in copies:
            cp.start()
        for j in range(1, N_DEV):
            frm, frm_id = _peer(j)
            for n in range(nin):
                pltpu.make_async_remote_copy(
                    src_ref=src_of(n, me), dst_ref=dst_of(n, frm_id),
                    send_sem=send_sems.at[j - 1, n], recv_sem=recv_sems.at[j - 1, n],
                    device_id=frm, device_id_type=MESH).wait_recv()
        for cp in copies:
            cp.wait_send()
        for cp in local:
            cp.wait()

    any_spec = pl.BlockSpec(memory_space=pl.ANY)
    return pl.pallas_call(
        body, name=name,
        in_specs=[any_spec] * nin, out_specs=[any_spec] * nkind, out_shape=out_shapes,
        scratch_shapes=[pltpu.SemaphoreType.DMA((N_DEV - 1, nin)), pltpu.SemaphoreType.DMA((N_DEV - 1, nin)),
                        pltpu.SemaphoreType.DMA((nin,))],
    )(*[p for parts in groups for p in parts])


def _adamw(stack, w, m, v, name):
    r, c = w.shape
    tr = r
    for cand in (512, 256, 128, 64, 32, 16, 8):
        if r % cand == 0 and r > cand:
            tr = cand
            break
    c1 = 1.0 / (1.0 - ADAM_B1 ** ADAM_STEP)
    c2 = 1.0 / (1.0 - ADAM_B2 ** ADAM_STEP)

    def body(s_ref, w_ref, m_ref, v_ref, g_ref, d_ref, nm_ref, nv_ref):
        g = s_ref[0]
        for k in range(1, N_DEV):
            g = g + s_ref[k]
        nm = ADAM_B1 * m_ref[...] + (1.0 - ADAM_B1) * g
        nv = ADAM_B2 * v_ref[...] + (1.0 - ADAM_B2) * (g * g)
        g_ref[...] = g
        nm_ref[...] = nm
        nv_ref[...] = nv
        d_ref[...] = -ADAM_LR * ((nm * c1) / (jnp.sqrt(nv * c2) + ADAM_EPS) + ADAM_WD * w_ref[...])

    spec = pl.BlockSpec((tr, c), lambda i: (i, 0))
    return pl.pallas_call(
        body, name=name, grid=(r // tr,),
        in_specs=[pl.BlockSpec((N_DEV, tr, c), lambda i: (0, i, 0)), spec, spec, spec],
        out_specs=[spec] * 4,
        out_shape=[jax.ShapeDtypeStruct((r, c), F32)] * 4,
        compiler_params=_params(),
    )(stack, w, m, v)


def _adamw_nd(stack, w, m, v, name):
    shp = w.shape
    r = 1
    for s in shp[:-1]:
        r *= s
    outs = _adamw(stack.reshape(N_DEV, r, shp[-1]), w.reshape(r, shp[-1]), m.reshape(r, shp[-1]),
                  v.reshape(r, shp[-1]), name)
    return [o.reshape(shp) for o in outs]


SMALL_ROWS = 40


def _pad_rows(a, rows):
    pad = [(0, 0)] * a.ndim
    pad[-2] = (0, rows - a.shape[-2])
    return jnp.pad(a, pad)


def _pack_sharded_small(conv_dw, pool_b_grp, pool_scale):
    return jnp.concatenate([_pad_rows(conv_dw, 32),
                            _pad_rows(jnp.stack([pool_b_grp, pool_scale], axis=1), SMALL_ROWS - 32)], axis=1)


def _unpack_sharded_small(a):
    return a[:, :CONV_TAPS, :], a[:, 32, :], a[:, 33, :]


REP_LOSS_ROW = 40


def _pack_replicated(norm_g, final_g, conv_dw_b, conv_ln_g, conv_ln_b, loss_row=None):
    parts = [norm_g, final_g[None, :], conv_dw_b, conv_ln_g, conv_ln_b,
             jnp.zeros((1, final_g.shape[0]), F32) if loss_row is None else loss_row]
    return jnp.concatenate([_pad_rows(p, SUBLANES) for p in parts], axis=0)


def _unpack_replicated(a):
    return a[0:4], a[8], a[16:18], a[24:26], a[32:34]


def kernel(x, norm_g, final_g, conv_w_in, conv_dw, conv_dw_b, conv_ln_g, conv_ln_b, conv_w_out, pool_w_in, pool_w_grp, pool_b_grp, pool_scale, pool_w_out, loss_target, m_norm_g, m_final_g, m_conv_w_in, m_conv_dw, m_conv_dw_b, m_conv_ln_g, m_conv_ln_b, m_conv_w_out, m_pool_w_in, m_pool_w_grp, m_pool_b_grp, m_pool_scale, m_pool_w_out, v_norm_g, v_final_g, v_conv_w_in, v_conv_dw, v_conv_dw_b, v_conv_ln_g, v_conv_ln_b, v_conv_w_out, v_pool_w_in, v_pool_w_grp, v_pool_b_grp, v_pool_scale, v_pool_w_out):
    h0 = x[0]
    target = loss_target[0]
    d = h0.shape[-1]
    n_layers = norm_g.shape[0]

    small_sh = _pack_sharded_small(conv_dw, pool_b_grp, pool_scale)
    cw_in, cw_out, pw_in, pw_grp, pw_out, small_full = _all_gather(
        [conv_w_in.astype(BF16), conv_w_out.astype(BF16), pool_w_in.astype(BF16), pool_w_grp.astype(BF16),
         pool_w_out.astype(BF16), small_sh],
        [2, 1, 2, 2, 1, 2], "gather_weights")
    conv_small = small_full.at[:, 32, :].set(conv_dw_b).at[:, 33, :].set(conv_ln_g).at[:, 34, :].set(conv_ln_b)
    pool_small = small_full
    ng3 = norm_g[:, None, :]

    hs = [h0]
    saved = []
    for i in range(n_layers):
        j = i // 2
        if i % 2 == 0:
            p = _in_fwd(hs[-1], ng3, i, cw_in, j, f"conv_in_fwd_{j}")
            ho, uc = _conv_fwd(p, hs[-1], conv_small, cw_out, j, f"conv_mix_fwd_{j}")
            saved.append((p, uc))
        else:
            p = _in_fwd(hs[-1], ng3, i, pw_in, j, f"pool_in_fwd_{j}")
            ho = _pool_fwd(p, hs[-1], pw_grp, pool_small, pw_out, j, f"pool_mix_fwd_{j}")
            saved.append((p,))
        hs.append(ho)

    dh, d_final_g, loss_part = _final(hs[-1], target, final_g[None, :], "final_loss")
    d_norm_g = [None] * n_layers
    g_cw_in, g_cw_out, g_pw_in, g_pw_grp, g_pw_out = {}, {}, {}, {}, {}
    g_conv_small, g_pool_small = {}, {}
    for i in reversed(range(n_layers)):
        j = i // 2
        if i % 2 == 0:
            p, uc = saved[i]
            dp, g_cw_out[j], g_conv_small[j] = _conv_bwd(dh, p, uc, conv_small, cw_out, j, f"conv_mix_bwd_{j}")
            dh, g_cw_in[j], dg = _in_bwd(dp, hs[i], dh, ng3, i, cw_in, j, f"conv_in_bwd_{j}")
        else:
            (p,) = saved[i]
            dp, g_pw_out[j], g_pw_grp[j], g_pool_small[j] = _pool_bwd(dh, p, pw_grp, pool_small, pw_out, j,
                                                                      f"pool_mix_bwd_{j}")
            dh, g_pw_in[j], dg = _in_bwd(dp, hs[i], dh, ng3, i, pw_in, j, f"pool_in_bwd_{j}")
        d_norm_g[i] = dg[0]
    grad_x = dh[None]

    nl = n_layers // 2
    layers = range(nl)
    sharded_small_part = jnp.stack([
        jnp.concatenate([g_conv_small[j][:32], _pad_rows(g_pool_small[j][32:34], SMALL_ROWS - 32)], axis=0)
        for j in layers])
    rep_part = _pack_replicated(jnp.stack(d_norm_g), d_final_g[0],
                                *[jnp.stack([g_conv_small[j][r] for j in layers]) for r in (32, 33, 34)],
                                loss_row=jnp.broadcast_to(loss_part[0:1, 0:1], (1, d)))
    groups = [[g_cw_in[j] for j in layers], [g_cw_out[j] for j in layers], [g_pw_in[j] for j in layers],
              [g_pw_grp[j] for j in layers], [g_pw_out[j] for j in layers], [sharded_small_part], [rep_part]]
    s_cw_in, s_cw_out, s_pw_in, s_pw_grp, s_pw_out, s_small, s_rep = _scatter_partials(
        groups, [1, 0, 1, 1, 0, 2, None], "scatter_grads")

    res = {}
    res["conv_w_in"] = _adamw_nd(s_cw_in, conv_w_in, m_conv_w_in, v_conv_w_in, "adamw_conv_w_in")
    res["conv_w_out"] = _adamw_nd(s_cw_out, conv_w_out, m_conv_w_out, v_conv_w_out, "adamw_conv_w_out")
    res["pool_w_in"] = _adamw_nd(s_pw_in, pool_w_in, m_pool_w_in, v_pool_w_in, "adamw_pool_w_in")
    res["pool_w_grp"] = _adamw_nd(s_pw_grp, pool_w_grp, m_pool_w_grp, v_pool_w_grp, "adamw_pool_w_grp")
    res["pool_w_out"] = _adamw_nd(s_pw_out, pool_w_out, m_pool_w_out, v_pool_w_out, "adamw_pool_w_out")
    sm = _adamw_nd(s_small[:, 0], small_sh, _pack_sharded_small(m_conv_dw, m_pool_b_grp, m_pool_scale),
                   _pack_sharded_small(v_conv_dw, v_pool_b_grp, v_pool_scale), "adamw_sharded_small")
    rep = _adamw_nd(s_rep[:, 0], _pack_replicated(norm_g, final_g, conv_dw_b, conv_ln_g, conv_ln_b),
                    _pack_replicated(m_norm_g, m_final_g, m_conv_dw_b, m_conv_ln_g, m_conv_ln_b),
                    _pack_replicated(v_norm_g, v_final_g, v_conv_dw_b, v_conv_ln_g, v_conv_ln_b), "adamw_replicated")
    loss = rep[0][REP_LOSS_ROW, 0]
    res["conv_dw"], res["pool_b_grp"], res["pool_scale"] = zip(*[_unpack_sharded_small(a) for a in sm])
    (res["norm_g"], res["final_g"], res["conv_dw_b"], res["conv_ln_g"], res["conv_ln_b"]) = zip(
        *[_unpack_replicated(a) for a in rep])

    names = ["norm_g", "final_g", "conv_w_in", "conv_dw", "conv_dw_b", "conv_ln_g", "conv_ln_b", "conv_w_out",
             "pool_w_in", "pool_w_grp", "pool_b_grp", "pool_scale", "pool_w_out"]
    return (loss, grad_x) + tuple(res[n][q] for q in range(4) for n in names)
```

```python
import jax
import jax.numpy as jnp
from jax import lax
from jax.experimental import pallas as pl
from jax.experimental.pallas import tpu as pltpu

F32 = jnp.float32
BF16 = jnp.bfloat16

RMS_EPS = 1e-6
LN_EPS = 1e-5
CONV_TAPS = 31
CONV_HALO = 32
POOL_WINDOWS = (2, 4, 8, 16)
POOL_HALO = 16
SUBLANES = 8
LANES = 128
N_DEV = 8
V7X_VMEM_LIMIT = 56 * 1024 * 1024

ADAM_LR = 0.001
ADAM_B1 = 0.9
ADAM_B2 = 0.999
ADAM_EPS = 1e-08
ADAM_WD = 0.01
ADAM_STEP = 10

MESH = pl.DeviceIdType.MESH
ANY = pl.BlockSpec(memory_space=pl.ANY)


def _dot(a, b):
    return lax.dot_general(a, b, (((1,), (0,)), ((), ())), preferred_element_type=F32)


def _dot_nt(a, b):
    return lax.dot_general(a, b, (((1,), (1,)), ((), ())), preferred_element_type=F32)


def _dot_tn(a, b):
    return lax.dot_general(a, b, (((0,), (0,)), ((), ())), preferred_element_type=F32)


def _sigmoid(x):
    return jax.nn.sigmoid(x)


def _dsilu(x, s):
    return s * (1.0 + x * (1.0 - s))


def _rows8(x):
    r, c = x.shape
    return jnp.sum(x.reshape(r // SUBLANES, SUBLANES, c), axis=0)


def _tile(t, pref):
    return pref if t >= 2 * pref else t // 2


def _const(shape, index):
    return pl.BlockSpec(shape, lambda *_: index, pipeline_mode=pl.Buffered(1))


def _params(grid_rank=1):
    return pltpu.CompilerParams(dimension_semantics=("arbitrary",) * grid_rank, vmem_limit_bytes=V7X_VMEM_LIMIT)


def _chunks(n, rc, fn):
    def step(c, carry):
        fn(pl.multiple_of(c * rc, rc))
        return carry
    lax.fori_loop(0, n, step, 0)


def _mesh_position():
    return lax.axis_index("x"), lax.axis_index("y"), lax.axis_index("c")


def _peer(j):
    x, y, c = _mesh_position()
    px = 1 - x if j & 4 else x
    py = 1 - y if j & 2 else y
    pc = 1 - c if j & 1 else c
    return (px, py, pc), 4 * px + 2 * py + pc


def _block(ref, axis, index, size):
    idx = [slice(None)] * len(ref.shape)
    idx[axis] = pl.ds(pl.multiple_of(index * size, size), size)
    return ref.at[tuple(idx)]


class _Exchange:
    def __init__(self, kind, arrays, axes):
        self.kind, self.arrays, self.axes = kind, list(arrays), list(axes)
        self.n = len(self.arrays)
        self.blk, self.out_shapes = [], []
        for a, ax in zip(self.arrays, self.axes):
            s = list(a.shape)
            if kind == "gather":
                self.blk.append(s[ax])
                s[ax] *= N_DEV
                self.out_shapes.append(jax.ShapeDtypeStruct(tuple(s), a.dtype))
            else:
                if ax is not None:
                    s[ax] //= N_DEV
                    self.blk.append(s[ax])
                else:
                    self.blk.append(None)
                self.out_shapes.append(jax.ShapeDtypeStruct((N_DEV,) + tuple(s), a.dtype))

    def sem_shapes(self):
        return [pltpu.SemaphoreType.DMA((N_DEV - 1, self.n)), pltpu.SemaphoreType.DMA((N_DEV - 1, self.n)),
                pltpu.SemaphoreType.DMA((self.n,))]

    def _src(self, ins, k, owner):
        if self.kind == "gather" or self.axes[k] is None:
            return ins[k]
        return _block(ins[k], self.axes[k], owner, self.blk[k])

    def _dst(self, outs, k, sender):
        if self.kind == "gather":
            return _block(outs[k], self.axes[k], sender, self.blk[k])
        return outs[k].at[sender]

    def _copies(self, ins, outs, sems, arriving):
        send, recv, loc = sems
        x, y, c = _mesh_position()
        me = 4 * x + 2 * y + c
        if not arriving:
            local = [pltpu.make_async_copy(self._src(ins, k, me), self._dst(outs, k, me), loc.at[k])
                     for k in range(self.n)]
        else:
            local = []
        remote = []
        for j in range(1, N_DEV):
            peer, peer_id = _peer(j)
            for k in range(self.n):
                owner, sender = (me, peer_id) if arriving else (peer_id, me)
                remote.append(pltpu.make_async_remote_copy(
                    src_ref=self._src(ins, k, owner), dst_ref=self._dst(outs, k, sender),
                    send_sem=send.at[j - 1, k], recv_sem=recv.at[j - 1, k], device_id=peer, device_id_type=MESH))
        return local, remote

    def start(self, ins, outs, sems):
        local, sends = self._copies(ins, outs, sems, arriving=False)
        for cp in local + sends:
            cp.start()

    def finish(self, ins, outs, sems):
        for cp in self._copies(ins, outs, sems, arriving=True)[1]:
            cp.wait_recv()
        local, sends = self._copies(ins, outs, sems, arriving=False)
        for cp in sends:
            cp.wait_send()
        for cp in local:
            cp.wait()


def _exchange_call(ex, name):
    def body(*refs):
        ins, outs, sems = refs[:ex.n], refs[ex.n:2 * ex.n], refs[2 * ex.n:]
        ex.start(ins, outs, sems)
        ex.finish(ins, outs, sems)

    return pl.pallas_call(body, name=name, in_specs=[ANY] * ex.n, out_specs=[ANY] * ex.n,
                          out_shape=ex.out_shapes, scratch_shapes=ex.sem_shapes())(*ex.arrays)


def _launch(body, name, nt, in_specs, out_specs, out_shape, scratch_shapes, args, ex=None):
    if ex is None:
        outs = pl.pallas_call(body, name=name, grid=(nt,), in_specs=in_specs, out_specs=out_specs,
                              out_shape=out_shape, scratch_shapes=scratch_shapes, compiler_params=_params())(*args)
        return list(outs), []
    n_in, n_out, n_scr = len(in_specs), len(out_specs), len(scratch_shapes)

    def riding(*refs):
        a, xa = refs[:n_in], refs[n_in:n_in + ex.n]
        o = refs[n_in + ex.n:n_in + ex.n + n_out]
        xo = refs[n_in + ex.n + n_out:n_in + 2 * ex.n + n_out]
        s = refs[n_in + 2 * ex.n + n_out:n_in + 2 * ex.n + n_out + n_scr]
        sems = refs[n_in + 2 * ex.n + n_out + n_scr:]
        i = pl.program_id(0)

        @pl.when(i == 0)
        def _():
            ex.start(xa, xo, sems)

        body(*a, *o, *s)

        @pl.when(i == nt - 1)
        def _():
            ex.finish(xa, xo, sems)

    outs = pl.pallas_call(
        riding, name=name, grid=(nt,),
        in_specs=list(in_specs) + [ANY] * ex.n, out_specs=list(out_specs) + [ANY] * ex.n,
        out_shape=list(out_shape) + ex.out_shapes, scratch_shapes=list(scratch_shapes) + ex.sem_shapes(),
        compiler_params=_params())(*args, *ex.arrays)
    return list(outs[:n_out]), list(outs[n_out:])


def _in_fwd(h, norm_g, layer, w_in, name, ex=None):
    t, d = h.shape
    n = w_in.shape[-1]
    tm = _tile(t, 512)

    def body(h_ref, g_ref, w_ref, p_ref):
        x = h_ref[...]
        r = lax.rsqrt(jnp.mean(x * x, axis=-1, keepdims=True) + RMS_EPS)
        p_ref[...] = _dot((x * r * g_ref[...]).astype(BF16), w_ref[...])

    (p,), xouts = _launch(
        body, name, t // tm,
        [pl.BlockSpec((tm, d), lambda i: (i, 0)), _const((None, 1, d), (layer, 0, 0)), _const((d, n), (0, 0))],
        [pl.BlockSpec((tm, n), lambda i: (i, 0))],
        [jax.ShapeDtypeStruct((t, n), F32)], [], (h, norm_g, w_in), ex)
    return p, xouts


def _layernorm_rows(uc, lng, lnb):
    mu = jnp.mean(uc, axis=-1, keepdims=True)
    xc = uc - mu
    rstd = lax.rsqrt(jnp.mean(xc * xc, axis=-1, keepdims=True) + LN_EPS)
    xhat = xc * rstd
    return xhat, rstd, xhat * lng + lnb


def _conv_fwd(p, h, small, layer, w_out, name, ex=None):
    t, d = h.shape
    e = w_out.shape[0]
    tm = _tile(t, 256)
    rc = 32
    nsm = small.shape[1]

    def body(p_ref, h_ref, sm_ref, wo_ref, ho_ref, uc_ref, us_scr):
        i = pl.program_id(0)

        @pl.when(i == 0)
        def _():
            us_scr[:, pl.ds(0, CONV_HALO), :] = jnp.zeros((SUBLANES, CONV_HALO, e), F32)

        @pl.when(i > 0)
        def _():
            us_scr[:, pl.ds(0, CONV_HALO), :] = us_scr[:, pl.ds(tm, CONV_HALO), :]

        us_scr[0, pl.ds(CONV_HALO, tm), :] = p_ref[:, pl.ds(0, e)] * _sigmoid(p_ref[:, pl.ds(e, e)])
        for r in range(1, SUBLANES):
            us_scr[r, pl.ds(CONV_HALO, tm), :] = us_scr[0, pl.ds(CONV_HALO - r, tm), :]

        def c_conv(base):
            for lt in range(e // LANES):
                cols = pl.ds(lt * LANES, LANES)
                acc = jnp.broadcast_to(sm_ref[pl.ds(32, 1), cols], (rc, LANES))
                for r in range(SUBLANES):
                    nq = (CONV_TAPS - 1 - r) // SUBLANES + 1
                    lo = SUBLANES * (nq - 1)
                    win = us_scr[r, pl.ds(pl.multiple_of(CONV_HALO + base - lo, SUBLANES), rc + lo), cols]
                    for q in range(nq):
                        k = CONV_TAPS - 1 - (SUBLANES * q + r)
                        at = lo - SUBLANES * q
                        acc = acc + sm_ref[pl.ds(k, 1), cols] * win[at:at + rc, :]
                uc_ref[pl.ds(base, rc), cols] = acc
        _chunks(tm // rc, rc, c_conv)

        _, _, ul = _layernorm_rows(uc_ref[...], sm_ref[pl.ds(33, 1), :], sm_ref[pl.ds(34, 1), :])
        z = p_ref[:, pl.ds(2 * e, e)]
        v = ((ul * _sigmoid(ul)) * (z * _sigmoid(z))).astype(BF16)
        ho_ref[...] = h_ref[...] + _dot(v, wo_ref[...])

    (ho, uc), xouts = _launch(
        body, name, t // tm,
        [pl.BlockSpec((tm, 3 * e), lambda i: (i, 0)), pl.BlockSpec((tm, d), lambda i: (i, 0)),
         _const((None, nsm, e), (layer, 0, 0)), _const((e, d), (0, 0))],
        [pl.BlockSpec((tm, d), lambda i: (i, 0)), pl.BlockSpec((tm, e), lambda i: (i, 0))],
        [jax.ShapeDtypeStruct((t, d), F32), jax.ShapeDtypeStruct((t, e), F32)],
        [pltpu.VMEM((SUBLANES, tm + CONV_HALO, e), F32)], (p, h, small, w_out), ex)
    return ho, uc, xouts


def _conv_bwd(dho, p, uc, small, layer, w_out, name, ex=None):
    t, d = dho.shape
    e = w_out.shape[0]
    tm = _tile(t, 256)
    nt = t // tm
    rc = 16
    nsm = small.shape[1]

    def body(dho_ref, p_ref, uc_ref, sm_ref, wo_ref, dp_ref, dwo_ref, dsm_ref, ds_scr, acc_scr, dwo_scr):
        i = pl.program_id(0)

        @pl.when(i == 0)
        def _():
            dwo_scr[...] = jnp.zeros_like(dwo_scr)
            acc_scr[...] = jnp.zeros_like(acc_scr)
            ds_scr[:, pl.ds(tm, CONV_HALO), :] = jnp.zeros((SUBLANES, CONV_HALO, e), F32)

        @pl.when(i > 0)
        def _():
            ds_scr[:, pl.ds(tm, CONV_HALO), :] = ds_scr[:, pl.ds(0, CONV_HALO), :]

        lng = sm_ref[pl.ds(33, 1), :]
        lnb = sm_ref[pl.ds(34, 1), :]

        xhat, rstd, ul = _layernorm_rows(uc_ref[...], lng, lnb)
        z = p_ref[:, pl.ds(2 * e, e)]
        sg_u = _sigmoid(ul)
        sg_z = _sigmoid(z)
        s_u = ul * sg_u
        s_z = z * sg_z
        v = (s_u * s_z).astype(BF16)
        dy = dho_ref[...].astype(BF16)
        dv = _dot_nt(dy, wo_ref[...])
        dwo_scr[...] += _dot_tn(v, dy)

        dul = dv * s_z * _dsilu(ul, sg_u)
        dp_ref[:, pl.ds(2 * e, e)] = (dv * s_u * _dsilu(z, sg_z)).astype(BF16)
        acc_scr[pl.ds(33 * SUBLANES, SUBLANES), :] += _rows8(dul * xhat)
        acc_scr[pl.ds(34 * SUBLANES, SUBLANES), :] += _rows8(dul)
        dxh = dul * lng
        duc = rstd * (dxh - jnp.mean(dxh, axis=-1, keepdims=True)
                      - xhat * jnp.mean(dxh * xhat, axis=-1, keepdims=True))
        acc_scr[pl.ds(32 * SUBLANES, SUBLANES), :] += _rows8(duc)
        ds_scr[0, pl.ds(0, tm), :] = duc
        for r in range(1, SUBLANES):
            ds_scr[r, pl.ds(0, tm), :] = ds_scr[0, pl.ds(r, tm), :]

        def c_conv(base):
            rows = pl.ds(base, rc)
            a = p_ref[rows, pl.ds(0, e)]
            b = p_ref[rows, pl.ds(e, e)]
            sb = _sigmoid(b)
            u = a * sb
            du = jnp.zeros((rc, e), F32)
            for o in range(CONV_TAPS):
                q, r = divmod(o, SUBLANES)
                k = CONV_TAPS - 1 - o
                sh = ds_scr[r, pl.ds(pl.multiple_of(base + SUBLANES * q, SUBLANES), rc), :]
                du = du + sm_ref[pl.ds(k, 1), :] * sh
                acc_scr[pl.ds(k * SUBLANES, SUBLANES), :] += _rows8(u * sh)
            dp_ref[rows, pl.ds(0, e)] = (du * sb).astype(BF16)
            dp_ref[rows, pl.ds(e, e)] = (du * u * (1.0 - sb)).astype(BF16)
        _chunks(tm // rc, rc, c_conv)

        @pl.when(i == nt - 1)
        def _():
            dwo_ref[...] = dwo_scr[...].astype(BF16)
            for k in range(nsm):
                dsm_ref[pl.ds(k, 1), :] = jnp.sum(acc_scr[pl.ds(k * SUBLANES, SUBLANES), :], axis=0, keepdims=True)

    rev = lambda i: (nt - 1 - i, 0)
    (dp, dwo, dsm), xouts = _launch(
        body, name, nt,
        [pl.BlockSpec((tm, d), rev), pl.BlockSpec((tm, 3 * e), rev), pl.BlockSpec((tm, e), rev),
         _const((None, nsm, e), (layer, 0, 0)), _const((e, d), (0, 0))],
        [pl.BlockSpec((tm, 3 * e), rev), _const((e, d), (0, 0)), _const((nsm, e), (0, 0))],
        [jax.ShapeDtypeStruct((t, 3 * e), BF16), jax.ShapeDtypeStruct((e, d), BF16),
         jax.ShapeDtypeStruct((nsm, e), F32)],
        [pltpu.VMEM((SUBLANES, tm + CONV_HALO, e), F32), pltpu.VMEM((nsm * SUBLANES, e), F32),
         pltpu.VMEM((e, d), F32)],
        (dho, p, uc, small, w_out), ex)
    return dp, dwo, dsm, xouts


def _inv_count(tile, tm, w):
    tpos = tile * tm + lax.broadcasted_iota(jnp.int32, (tm, 1), 0)
    return 1.0 / jnp.minimum(tpos + 1, w).astype(F32)


def _pool_d(ue_scr, tile, tm, gc):
    out = []
    for g, w in enumerate(POOL_WINDOWS):
        win = ue_scr[:, pl.ds(g * gc, gc)]
        s = win
        sh = 1
        while sh < w:
            s = s + pltpu.roll(s, sh, axis=0)
            sh *= 2
        out.append(s[POOL_HALO:, :] * _inv_count(tile, tm, w) - win[POOL_HALO:, :])
    return out


def _pool_fwd(p, h, w_grp, small, layer, w_out, name, ex=None):
    t, d = h.shape
    e = w_out.shape[0]
    ng = len(POOL_WINDOWS)
    gc = e // ng
    tm = _tile(t, 256)
    nsm = small.shape[1]

    def body(p_ref, h_ref, wg_ref, sm_ref, wo_ref, ho_ref, ue_scr, y_scr):
        i = pl.program_id(0)

        @pl.when(i == 0)
        def _():
            ue_scr[pl.ds(0, POOL_HALO), :] = jnp.zeros((POOL_HALO, e), F32)

        @pl.when(i > 0)
        def _():
            ue_scr[pl.ds(0, POOL_HALO), :] = ue_scr[pl.ds(tm, POOL_HALO), :]

        ue_scr[pl.ds(POOL_HALO, tm), :] = p_ref[:, pl.ds(0, e)]

        for g, dg in enumerate(_pool_d(ue_scr, i, tm, gc)):
            cols = pl.ds(g * gc, gc)
            z = p_ref[:, pl.ds(e + g * gc, gc)]
            y1 = (_dot(dg.astype(BF16), wg_ref[g]) + sm_ref[pl.ds(32, 1), cols]) * sm_ref[pl.ds(33, 1), cols]
            y_scr[:, cols] = (y1 * (z * _sigmoid(z))).astype(BF16)

        ho_ref[...] = h_ref[...] + _dot(y_scr[...], wo_ref[...])

    (ho,), xouts = _launch(
        body, name, t // tm,
        [pl.BlockSpec((tm, 2 * e), lambda i: (i, 0)), pl.BlockSpec((tm, d), lambda i: (i, 0)),
         _const((ng, gc, gc), (0, 0, 0)), _const((None, nsm, e), (layer, 0, 0)), _const((e, d), (0, 0))],
        [pl.BlockSpec((tm, d), lambda i: (i, 0))],
        [jax.ShapeDtypeStruct((t, d), F32)],
        [pltpu.VMEM((tm + POOL_HALO, e), F32), pltpu.VMEM((tm, e), BF16)],
        (p, h, w_grp, small, w_out), ex)
    return ho, xouts


def _pool_bwd(dho, p, w_grp, small, layer, w_out, name, ex=None):
    t, d = dho.shape
    e = w_out.shape[0]
    ng = len(POOL_WINDOWS)
    gc = e // ng
    tm = _tile(t, 256)
    nt = t // tm
    nsm = small.shape[1]
    hb = tm // POOL_HALO

    def body(dho_ref, p_ref, ph_ref, wg_ref, sm_ref, wo_ref, dp_ref, dwo_ref, dwg_ref, dsm_ref,
             ue_scr, ee_scr, acc_scr, dwo_scr, dwg_scr):
        i = pl.program_id(0)
        tile = nt - 1 - i

        @pl.when(i == 0)
        def _():
            dwo_scr[...] = jnp.zeros_like(dwo_scr)
            dwg_scr[...] = jnp.zeros_like(dwg_scr)
            acc_scr[...] = jnp.zeros_like(acc_scr)
            ee_scr[pl.ds(tm, POOL_HALO), :] = jnp.zeros((POOL_HALO, e), F32)

        @pl.when(i > 0)
        def _():
            ee_scr[pl.ds(tm, POOL_HALO), :] = ee_scr[pl.ds(0, POOL_HALO), :]

        @pl.when(tile == 0)
        def _():
            ue_scr[pl.ds(0, POOL_HALO), :] = jnp.zeros((POOL_HALO, e), F32)

        @pl.when(tile > 0)
        def _():
            ue_scr[pl.ds(0, POOL_HALO), :] = ph_ref[:, pl.ds(0, e)]

        ue_scr[pl.ds(POOL_HALO, tm), :] = p_ref[:, pl.ds(0, e)]

        bg = sm_ref[pl.ds(32, 1), :]
        sc = sm_ref[pl.ds(33, 1), :]
        ds = [dg.astype(BF16) for dg in _pool_d(ue_scr, tile, tm, gc)]
        ob = jnp.concatenate([_dot(ds[g], wg_ref[g]) for g in range(ng)], axis=1) + bg
        z = p_ref[:, pl.ds(e, e)]
        sg_z = _sigmoid(z)
        s_z = z * sg_z
        y1 = ob * sc
        dy = dho_ref[...].astype(BF16)
        dy2 = _dot_nt(dy, wo_ref[...])
        dwo_scr[...] += _dot_tn((y1 * s_z).astype(BF16), dy)
        dy1 = dy2 * s_z
        dp_ref[:, pl.ds(e, e)] = (dy2 * y1 * _dsilu(z, sg_z)).astype(BF16)
        acc_scr[pl.ds(33 * SUBLANES, SUBLANES), :] += _rows8(dy1 * ob)
        do = dy1 * sc
        acc_scr[pl.ds(32 * SUBLANES, SUBLANES), :] += _rows8(do)

        n = tm + POOL_HALO
        for g, w in enumerate(POOL_WINDOWS):
            cols = pl.ds(g * gc, gc)
            do_g = do[:, g * gc:(g + 1) * gc].astype(BF16)
            dwg_scr[g] += _dot_tn(ds[g], do_g)
            dd = _dot_nt(do_g, wg_ref[g])
            ee_scr[pl.ds(0, tm), cols] = dd * _inv_count(tile, tm, w)
            s = ee_scr[:, cols]
            sh = 1
            while sh < w:
                s = s + pltpu.roll(s, n - sh, axis=0)
                sh *= 2
            dp_ref[:, cols] = (s[:tm, :] - dd).astype(BF16)

        @pl.when(i == nt - 1)
        def _():
            dwo_ref[...] = dwo_scr[...].astype(BF16)
            dwg_ref[...] = dwg_scr[...].astype(BF16)
            dsm_ref[...] = jnp.zeros_like(dsm_ref)
            for k in (32, 33):
                dsm_ref[pl.ds(k, 1), :] = jnp.sum(acc_scr[pl.ds(k * SUBLANES, SUBLANES), :], axis=0, keepdims=True)

    rev = lambda i: (nt - 1 - i, 0)
    (dp, dwo, dwg, dsm), xouts = _launch(
        body, name, nt,
        [pl.BlockSpec((tm, d), rev), pl.BlockSpec((tm, 2 * e), rev),
         pl.BlockSpec((POOL_HALO, 2 * e), lambda i: (jnp.maximum((nt - 1 - i) * hb - 1, 0), 0)),
         _const((ng, gc, gc), (0, 0, 0)), _const((None, nsm, e), (layer, 0, 0)), _const((e, d), (0, 0))],
        [pl.BlockSpec((tm, 2 * e), rev), _const((e, d), (0, 0)), _const((ng, gc, gc), (0, 0, 0)),
         _const((nsm, e), (0, 0))],
        [jax.ShapeDtypeStruct((t, 2 * e), BF16), jax.ShapeDtypeStruct((e, d), BF16),
         jax.ShapeDtypeStruct((ng, gc, gc), BF16), jax.ShapeDtypeStruct((nsm, e), F32)],
        [pltpu.VMEM((tm + POOL_HALO, e), F32), pltpu.VMEM((tm + POOL_HALO, e), F32),
         pltpu.VMEM((nsm * SUBLANES, e), F32), pltpu.VMEM((e, d), F32), pltpu.VMEM((ng, gc, gc), F32)],
        (dho, p, p, w_grp, small, w_out), ex)
    return dp, dwo, dwg, dsm, xouts


def _in_bwd(dp, h, dho, norm_g, layer, w_in, name, ex=None):
    t, d = h.shape
    n = w_in.shape[-1]
    tm = _tile(t, 512)
    nt = t // tm

    def body(dp_ref, h_ref, dho_ref, g_ref, w_ref, dh_ref, dw_ref, dg_ref, acc_scr, dw_scr):
        i = pl.program_id(0)

        @pl.when(i == 0)
        def _():
            dw_scr[...] = jnp.zeros_like(dw_scr)
            acc_scr[...] = jnp.zeros_like(acc_scr)

        x = h_ref[...]
        r = lax.rsqrt(jnp.mean(x * x, axis=-1, keepdims=True) + RMS_EPS)
        nrm = x * r
        dp = dp_ref[...]
        dhn = _dot_nt(dp, w_ref[...])
        dw_scr[...] += _dot_tn((nrm * g_ref[...]).astype(BF16), dp)
        acc_scr[...] += _rows8(dhn * nrm)
        dq = dhn * g_ref[...]
        dh_ref[...] = dho_ref[...] + r * (dq - nrm * jnp.mean(dq * nrm, axis=-1, keepdims=True))

        @pl.when(i == nt - 1)
        def _():
            dw_ref[...] = dw_scr[...].astype(BF16)
            dg_ref[...] = jnp.zeros_like(dg_ref)
            dg_ref[pl.ds(0, 1), :] = jnp.sum(acc_scr[...], axis=0, keepdims=True)

    (dh, dw, dg), xouts = _launch(
        body, name, nt,
        [pl.BlockSpec((tm, n), lambda i: (i, 0)), pl.BlockSpec((tm, d), lambda i: (i, 0)),
         pl.BlockSpec((tm, d), lambda i: (i, 0)), _const((None, 1, d), (layer, 0, 0)), _const((d, n), (0, 0))],
        [pl.BlockSpec((tm, d), lambda i: (i, 0)), _const((d, n), (0, 0)), _const((SUBLANES, d), (0, 0))],
        [jax.ShapeDtypeStruct((t, d), F32), jax.ShapeDtypeStruct((d, n), BF16),
         jax.ShapeDtypeStruct((SUBLANES, d), F32)],
        [pltpu.VMEM((SUBLANES, d), F32), pltpu.VMEM((d, n), F32)],
        (dp, h, dho, norm_g, w_in), ex)
    return dh, dw, dg, xouts


def _final(h, target, final_g, name):
    t, d = h.shape
    tm = _tile(t, 512)
    nt = t // tm

    def body(h_ref, tg_ref, g_ref, dh_ref, dg_ref, loss_ref, acc_scr, lacc_scr):
        i = pl.program_id(0)

        @pl.when(i == 0)
        def _():
            acc_scr[...] = jnp.zeros_like(acc_scr)
            lacc_scr[...] = jnp.zeros_like(lacc_scr)

        x = h_ref[...]
        r = lax.rsqrt(jnp.mean(x * x, axis=-1, keepdims=True) + RMS_EPS)
        nrm = x * r
        err = nrm * g_ref[...] - tg_ref[...]
        lacc_scr[...] += _rows8(err * err)
        dy = err * (1.0 / d)
        acc_scr[...] += _rows8(dy * nrm)
        dq = dy * g_ref[...]
        dh_ref[...] = r * (dq - nrm * jnp.mean(dq * nrm, axis=-1, keepdims=True))

        @pl.when(i == nt - 1)
        def _():
            dg_ref[...] = jnp.zeros_like(dg_ref)
            dg_ref[pl.ds(0, 1), :] = jnp.sum(acc_scr[...], axis=0, keepdims=True)
            loss_ref[...] = jnp.broadcast_to(jnp.sum(lacc_scr[...]) * (0.5 / d), loss_ref.shape)

    outs, _ = _launch(
        body, name, nt,
        [pl.BlockSpec((tm, d), lambda i: (i, 0)), pl.BlockSpec((tm, d), lambda i: (i, 0)), _const((1, d), (0, 0))],
        [pl.BlockSpec((tm, d), lambda i: (i, 0)), _const((SUBLANES, d), (0, 0)), _const((SUBLANES, LANES), (0, 0))],
        [jax.ShapeDtypeStruct((t, d), F32), jax.ShapeDtypeStruct((SUBLANES, d), F32),
         jax.ShapeDtypeStruct((SUBLANES, LANES), F32)],
        [pltpu.VMEM((SUBLANES, d), F32), pltpu.VMEM((SUBLANES, d), F32)],
        (h, target, final_g))
    return outs


def _adamw(stacks, w, m, v, name):
    nl = len(stacks)
    shp = w.shape
    c = shp[-1]
    r = 1
    for s in shp[1:-1]:
        r *= s
    tr = r
    for cand in (512, 256, 128, 64, 32, 16):
        if r % cand == 0 and r > cand:
            tr = cand
            break
    nrb = r // tr
    c1 = 1.0 / (1.0 - ADAM_B1 ** ADAM_STEP)
    c2 = 1.0 / (1.0 - ADAM_B2 ** ADAM_STEP)

    def body(*refs):
        s_refs = refs[:nl]
        w_ref, m_ref, v_ref, g_ref, d_ref, nm_ref, nv_ref = refs[nl:]
        layer = pl.program_id(0)
        for l in range(nl):
            @pl.when(layer == l)
            def _(l=l):
                g = s_refs[l][0].astype(F32)
                for k in range(1, N_DEV):
                    g = g + s_refs[l][k].astype(F32)
                nm = ADAM_B1 * m_ref[...] + (1.0 - ADAM_B1) * g
                nv = ADAM_B2 * v_ref[...] + (1.0 - ADAM_B2) * (g * g)
                g_ref[...] = g
                nm_ref[...] = nm
                nv_ref[...] = nv
                d_ref[...] = -ADAM_LR * ((nm * c1) / (jnp.sqrt(nv * c2) + ADAM_EPS) + ADAM_WD * w_ref[...])

    def stack_spec(l):
        return pl.BlockSpec((N_DEV, tr, c),
                            lambda j, i: (0, jnp.where(j == l, i, jnp.where(j < l, 0, nrb - 1)), 0))

    spec = pl.BlockSpec((None, tr, c), lambda j, i: (j, i, 0))
    outs = pl.pallas_call(
        body, name=name, grid=(nl, nrb),
        in_specs=[stack_spec(l) for l in range(nl)] + [spec, spec, spec],
        out_specs=[spec] * 4,
        out_shape=[jax.ShapeDtypeStruct((nl, r, c), F32)] * 4,
        compiler_params=_params(2),
    )(*[s.reshape(N_DEV, r, c) for s in stacks], w.reshape(nl, r, c), m.reshape(nl, r, c), v.reshape(nl, r, c))
    return [o.reshape(shp) for o in outs]


SMALL_ROWS = 40


def _pad_rows(a, rows):
    pad = [(0, 0)] * a.ndim
    pad[-2] = (0, rows - a.shape[-2])
    return jnp.pad(a, pad)


def _pack_sharded_small(conv_dw, pool_b_grp, pool_scale):
    return jnp.concatenate([_pad_rows(conv_dw, 32),
                            _pad_rows(jnp.stack([pool_b_grp, pool_scale], axis=1), SMALL_ROWS - 32)], axis=1)


def _unpack_sharded_small(a):
    return a[:, :CONV_TAPS, :], a[:, 32, :], a[:, 33, :]


REP_LOSS_ROW = 40


def _pack_replicated(norm_g, final_g, conv_dw_b, conv_ln_g, conv_ln_b, loss_row=None):
    parts = [norm_g, final_g[None, :], conv_dw_b, conv_ln_g, conv_ln_b,
             jnp.zeros((1, final_g.shape[0]), F32) if loss_row is None else loss_row]
    return jnp.concatenate([_pad_rows(p, SUBLANES) for p in parts], axis=0)


def _unpack_replicated(a):
    return a[0:4], a[8], a[16:18], a[24:26], a[32:34]


def kernel(x, norm_g, final_g, conv_w_in, conv_dw, conv_dw_b, conv_ln_g, conv_ln_b, conv_w_out, pool_w_in, pool_w_grp, pool_b_grp, pool_scale, pool_w_out, loss_target, m_norm_g, m_final_g, m_conv_w_in, m_conv_dw, m_conv_dw_b, m_conv_ln_g, m_conv_ln_b, m_conv_w_out, m_pool_w_in, m_pool_w_grp, m_pool_b_grp, m_pool_scale, m_pool_w_out, v_norm_g, v_final_g, v_conv_w_in, v_conv_dw, v_conv_dw_b, v_conv_ln_g, v_conv_ln_b, v_conv_w_out, v_pool_w_in, v_pool_w_grp, v_pool_b_grp, v_pool_scale, v_pool_w_out):
    h0 = x[0]
    target = loss_target[0]
    d = h0.shape[-1]
    ng3 = norm_g[:, None, :]

    cwi, cwo, pwi = conv_w_in.astype(BF16), conv_w_out.astype(BF16), pool_w_in.astype(BF16)
    pwg, pwo = pool_w_grp.astype(BF16), pool_w_out.astype(BF16)
    small_sh = _pack_sharded_small(conv_dw, pool_b_grp, pool_scale)

    (cw_in0,) = _exchange_call(_Exchange("gather", [cwi[0]], [1]), "gather_first")
    p0, (cw_out0, small_full, pw_in0) = _in_fwd(
        h0, ng3, 0, cw_in0, "conv_in_fwd_0", _Exchange("gather", [cwo[0], small_sh, pwi[0]], [0, 2, 1]))
    conv_small = small_full.at[:, 32, :].set(conv_dw_b).at[:, 33, :].set(conv_ln_g).at[:, 34, :].set(conv_ln_b)
    pool_small = small_full
    h1, uc0, (pw_grp0, pw_out0, cw_in1, cw_out1, pw_in1, pw_grp1, pw_out1) = _conv_fwd(
        p0, h0, conv_small, 0, cw_out0, "conv_mix_fwd_0",
        _Exchange("gather", [pwg[0], pwo[0], cwi[1], cwo[1], pwi[1], pwg[1], pwo[1]], [1, 0, 1, 0, 1, 1, 0]))
    p1, _ = _in_fwd(h1, ng3, 1, pw_in0, "pool_in_fwd_0")
    h2, _ = _pool_fwd(p1, h1, pw_grp0, pool_small, 0, pw_out0, "pool_mix_fwd_0")
    p2, _ = _in_fwd(h2, ng3, 2, cw_in1, "conv_in_fwd_1")
    h3, uc2, _ = _conv_fwd(p2, h2, conv_small, 1, cw_out1, "conv_mix_fwd_1")
    p3, _ = _in_fwd(h3, ng3, 3, pw_in1, "pool_in_fwd_1")
    h4, _ = _pool_fwd(p3, h3, pw_grp1, pool_small, 1, pw_out1, "pool_mix_fwd_1")

    dh, d_final_g, loss_part = _final(h4, target, final_g[None, :], "final_loss")
    scatter = lambda arrays, axes: _Exchange("scatter", arrays, axes)
    dp, g_pwo1, g_pwg1, gsm_p1, _ = _pool_bwd(dh, p3, pw_grp1, pool_small, 1, pw_out1, "pool_mix_bwd_1")
    dh, g_pwi1, dg3, (s_pwo1, s_pwg1) = _in_bwd(dp, h3, dh, ng3, 3, pw_in1, "pool_in_bwd_1",
                                                scatter([g_pwo1, g_pwg1], [0, 1]))
    dp, g_cwo1, gsm_c1, (s_pwi1,) = _conv_bwd(dh, p2, uc2, conv_small, 1, cw_out1, "conv_mix_bwd_1",
                                              scatter([g_pwi1], [1]))
    dh, g_cwi1, dg2, (s_cwo1,) = _in_bwd(dp, h2, dh, ng3, 2, cw_in1, "conv_in_bwd_1", scatter([g_cwo1], [0]))
    dp, g_pwo0, g_pwg0, gsm_p0, (s_cwi1,) = _pool_bwd(dh, p1, pw_grp0, pool_small, 0, pw_out0, "pool_mix_bwd_0",
                                                      scatter([g_cwi1], [1]))
    dh, g_pwi0, dg1, (s_pwo0, s_pwg0) = _in_bwd(dp, h1, dh, ng3, 1, pw_in0, "pool_in_bwd_0",
                                                scatter([g_pwo0, g_pwg0], [0, 1]))
    dp, g_cwo0, gsm_c0, (s_pwi0,) = _conv_bwd(dh, p0, uc0, conv_small, 0, cw_out0, "conv_mix_bwd_0",
                                              scatter([g_pwi0], [1]))
    dh, g_cwi0, dg0, _ = _in_bwd(dp, h0, dh, ng3, 0, cw_in0, "conv_in_bwd_0")
    grad_x = dh[None]

    sharded_small_part = jnp.stack([
        jnp.concatenate([gc[:32], _pad_rows(gp[32:34], SMALL_ROWS - 32)], axis=0)
        for gc, gp in ((gsm_c0, gsm_p0), (gsm_c1, gsm_p1))])
    rep_part = _pack_replicated(jnp.stack([dg0[0], dg1[0], dg2[0], dg3[0]]), d_final_g[0],
                                *[jnp.stack([gsm_c0[r], gsm_c1[r]]) for r in (32, 33, 34)],
                                loss_row=jnp.broadcast_to(loss_part[0:1, 0:1], (1, d)))
    s_cwo0, s_cwi0, s_small, s_rep = _exchange_call(
        scatter([g_cwo0, g_cwi0, sharded_small_part, rep_part], [0, 1, 2, None]), "scatter_last")

    res = {}
    res["conv_w_in"] = _adamw([s_cwi0, s_cwi1], conv_w_in, m_conv_w_in, v_conv_w_in, "adamw_conv_w_in")
    res["conv_w_out"] = _adamw([s_cwo0, s_cwo1], conv_w_out, m_conv_w_out, v_conv_w_out, "adamw_conv_w_out")
    res["pool_w_in"] = _adamw([s_pwi0, s_pwi1], pool_w_in, m_pool_w_in, v_pool_w_in, "adamw_pool_w_in")
    res["pool_w_grp"] = _adamw([s_pwg0, s_pwg1], pool_w_grp, m_pool_w_grp, v_pool_w_grp, "adamw_pool_w_grp")
    res["pool_w_out"] = _adamw([s_pwo0, s_pwo1], pool_w_out, m_pool_w_out, v_pool_w_out, "adamw_pool_w_out")
    sm = _adamw([s_small], small_sh[None], _pack_sharded_small(m_conv_dw, m_pool_b_grp, m_pool_scale)[None],
                _pack_sharded_small(v_conv_dw, v_pool_b_grp, v_pool_scale)[None], "adamw_sharded_small")
    rep = _adamw([s_rep], _pack_replicated(norm_g, final_g, conv_dw_b, conv_ln_g, conv_ln_b)[None],
                 _pack_replicated(m_norm_g, m_final_g, m_conv_dw_b, m_conv_ln_g, m_conv_ln_b)[None],
                 _pack_replicated(v_norm_g, v_final_g, v_conv_dw_b, v_conv_ln_g, v_conv_ln_b)[None],
                 "adamw_replicated")
    loss = rep[0][0, REP_LOSS_ROW, 0]
    res["conv_dw"], res["pool_b_grp"], res["pool_scale"] = zip(*[_unpack_sharded_small(a[0]) for a in sm])
    (res["norm_g"], res["final_g"], res["conv_dw_b"], res["conv_ln_g"], res["conv_ln_b"]) = zip(
        *[_unpack_replicated(a[0]) for a in rep])

    names = ["norm_g", "final_g", "conv_w_in", "conv_dw", "conv_dw_b", "conv_ln_g", "conv_ln_b", "conv_w_out",
             "pool_w_in", "pool_w_grp", "pool_b_grp", "pool_scale", "pool_w_out"]
    return (loss, grad_x) + tuple(res[n][q] for q in range(4) for n in names)
```

```python
import jax
import jax.numpy as jnp
from jax import lax
from jax.experimental import pallas as pl
from jax.experimental.pallas import tpu as pltpu

F32 = jnp.float32
BF16 = jnp.bfloat16

RMS_EPS = 1e-6
LN_EPS = 1e-5
CONV_TAPS = 31
CONV_HALO = 32
POOL_WINDOWS = (2, 4, 8, 16)
POOL_HALO = 16
SUBLANES = 8
LANES = 128
N_DEV = 8
V7X_VMEM_LIMIT = 56 * 1024 * 1024

ADAM_LR = 0.001
ADAM_B1 = 0.9
ADAM_B2 = 0.999
ADAM_EPS = 1e-08
ADAM_WD = 0.01
ADAM_STEP = 10

MESH = pl.DeviceIdType.MESH
ANY = pl.BlockSpec(memory_space=pl.ANY)


def _dot(a, b):
    return lax.dot_general(a, b, (((1,), (0,)), ((), ())), preferred_element_type=F32)


def _dot_nt(a, b):
    return lax.dot_general(a, b, (((1,), (1,)), ((), ())), preferred_element_type=F32)


def _dot_tn(a, b):
    return lax.dot_general(a, b, (((0,), (0,)), ((), ())), preferred_element_type=F32)


def _sigmoid(x):
    return jax.nn.sigmoid(x)


def _dsilu(x, s):
    return s * (1.0 + x * (1.0 - s))


def _rows8(x):
    r, c = x.shape
    return jnp.sum(x.reshape(r // SUBLANES, SUBLANES, c), axis=0)


def _tile(t, pref):
    return pref if t >= 2 * pref else t // 2


def _const(shape, index):
    return pl.BlockSpec(shape, lambda *_: index, pipeline_mode=pl.Buffered(1))


def _params(grid_rank=1):
    return pltpu.CompilerParams(dimension_semantics=("arbitrary",) * grid_rank, vmem_limit_bytes=V7X_VMEM_LIMIT)


def _chunks(n, rc, fn):
    def step(c, carry):
        fn(pl.multiple_of(c * rc, rc))
        return carry
    lax.fori_loop(0, n, step, 0)


def _mesh_position():
    return lax.axis_index("x"), lax.axis_index("y"), lax.axis_index("c")


def _peer(j):
    x, y, c = _mesh_position()
    px = 1 - x if j & 4 else x
    py = 1 - y if j & 2 else y
    pc = 1 - c if j & 1 else c
    return (px, py, pc), 4 * px + 2 * py + pc


def _block(ref, axis, index, size):
    idx = [slice(None)] * len(ref.shape)
    idx[axis] = pl.ds(pl.multiple_of(index * size, size), size)
    return ref.at[tuple(idx)]


class _Exchange:
    def __init__(self, kind, arrays, axes):
        self.kind, self.arrays, self.axes = kind, list(arrays), list(axes)
        self.n = len(self.arrays)
        self.blk, self.out_shapes = [], []
        for a, ax in zip(self.arrays, self.axes):
            s = list(a.shape)
            if kind == "gather":
                self.blk.append(s[ax])
                s[ax] *= N_DEV
                self.out_shapes.append(jax.ShapeDtypeStruct(tuple(s), a.dtype))
            else:
                if ax is not None:
                    s[ax] //= N_DEV
                    self.blk.append(s[ax])
                else:
                    self.blk.append(None)
                self.out_shapes.append(jax.ShapeDtypeStruct((N_DEV,) + tuple(s), a.dtype))

    def sem_shapes(self):
        return [pltpu.SemaphoreType.DMA((N_DEV - 1, self.n)), pltpu.SemaphoreType.DMA((N_DEV - 1, self.n)),
                pltpu.SemaphoreType.DMA((self.n,))]

    def _src(self, ins, k, owner):
        if self.kind == "gather" or self.axes[k] is None:
            return ins[k]
        return _block(ins[k], self.axes[k], owner, self.blk[k])

    def _dst(self, outs, k, sender):
        if self.kind == "gather":
            return _block(outs[k], self.axes[k], sender, self.blk[k])
        return outs[k].at[sender]

    def _copies(self, ins, outs, sems, arriving):
        send, recv, loc = sems
        x, y, c = _mesh_position()
        me = 4 * x + 2 * y + c
        if not arriving:
            local = [pltpu.make_async_copy(self._src(ins, k, me), self._dst(outs, k, me), loc.at[k])
                     for k in range(self.n)]
        else:
            local = []
        remote = []
        for j in range(1, N_DEV):
            peer, peer_id = _peer(j)
            for k in range(self.n):
                owner, sender = (me, peer_id) if arriving else (peer_id, me)
                remote.append(pltpu.make_async_remote_copy(
                    src_ref=self._src(ins, k, owner), dst_ref=self._dst(outs, k, sender),
                    send_sem=send.at[j - 1, k], recv_sem=recv.at[j - 1, k], device_id=peer, device_id_type=MESH))
        return local, remote

    def start(self, ins, outs, sems):
        local, sends = self._copies(ins, outs, sems, arriving=False)
        for cp in local + sends:
            cp.start()

    def finish(self, ins, outs, sems):
        for cp in self._copies(ins, outs, sems, arriving=True)[1]:
            cp.wait_recv()
        local, sends = self._copies(ins, outs, sems, arriving=False)
        for cp in sends:
            cp.wait_send()
        for cp in local:
            cp.wait()


def _exchange_call(ex, name):
    def body(*refs):
        ins, outs, sems = refs[:ex.n], refs[ex.n:2 * ex.n], refs[2 * ex.n:]
        ex.start(ins, outs, sems)
        ex.finish(ins, outs, sems)

    return pl.pallas_call(body, name=name, in_specs=[ANY] * ex.n, out_specs=[ANY] * ex.n,
                          out_shape=ex.out_shapes, scratch_shapes=ex.sem_shapes())(*ex.arrays)


def _launch(body, name, nt, in_specs, out_specs, out_shape, scratch_shapes, args, ex=None):
    if ex is None:
        outs = pl.pallas_call(body, name=name, grid=(nt,), in_specs=in_specs, out_specs=out_specs,
                              out_shape=out_shape, scratch_shapes=scratch_shapes, compiler_params=_params())(*args)
        return list(outs), []
    n_in, n_out, n_scr = len(in_specs), len(out_specs), len(scratch_shapes)

    def riding(*refs):
        a, xa = refs[:n_in], refs[n_in:n_in + ex.n]
        o = refs[n_in + ex.n:n_in + ex.n + n_out]
        xo = refs[n_in + ex.n + n_out:n_in + 2 * ex.n + n_out]
        s = refs[n_in + 2 * ex.n + n_out:n_in + 2 * ex.n + n_out + n_scr]
        sems = refs[n_in + 2 * ex.n + n_out + n_scr:]
        i = pl.program_id(0)

        @pl.when(i == 0)
        def _():
            ex.start(xa, xo, sems)

        body(*a, *o, *s)

        @pl.when(i == nt - 1)
        def _():
            ex.finish(xa, xo, sems)

    outs = pl.pallas_call(
        riding, name=name, grid=(nt,),
        in_specs=list(in_specs) + [ANY] * ex.n, out_specs=list(out_specs) + [ANY] * ex.n,
        out_shape=list(out_shape) + ex.out_shapes, scratch_shapes=list(scratch_shapes) + ex.sem_shapes(),
        compiler_params=_params())(*args, *ex.arrays)
    return list(outs[:n_out]), list(outs[n_out:])


def _in_fwd(h, norm_g, layer, w_in, name, ex=None):
    t, d = h.shape
    n = w_in.shape[-1]
    tm = _tile(t, 512)

    def body(h_ref, g_ref, w_ref, p_ref):
        x = h_ref[...]
        r = lax.rsqrt(jnp.mean(x * x, axis=-1, keepdims=True) + RMS_EPS)
        p_ref[...] = _dot((x * r * g_ref[...]).astype(BF16), w_ref[...])

    (p,), xouts = _launch(
        body, name, t // tm,
        [pl.BlockSpec((tm, d), lambda i: (i, 0)), _const((None, 1, d), (layer, 0, 0)), _const((d, n), (0, 0))],
        [pl.BlockSpec((tm, n), lambda i: (i, 0))],
        [jax.ShapeDtypeStruct((t, n), F32)], [], (h, norm_g, w_in), ex)
    return p, xouts


def _layernorm_rows(uc, lng, lnb):
    mu = jnp.mean(uc, axis=-1, keepdims=True)
    xc = uc - mu
    rstd = lax.rsqrt(jnp.mean(xc * xc, axis=-1, keepdims=True) + LN_EPS)
    xhat = xc * rstd
    return xhat, rstd, xhat * lng + lnb


def _conv_fwd(p, h, small, layer, w_out, name, ex=None):
    t, d = h.shape
    e = w_out.shape[0]
    tm = _tile(t, 256)
    rc = 32
    nsm = small.shape[1]

    def body(p_ref, h_ref, sm_ref, wo_ref, ho_ref, uc_ref, us_scr):
        i = pl.program_id(0)

        @pl.when(i == 0)
        def _():
            us_scr[:, pl.ds(0, CONV_HALO), :] = jnp.zeros((SUBLANES, CONV_HALO, e), F32)

        @pl.when(i > 0)
        def _():
            us_scr[:, pl.ds(0, CONV_HALO), :] = us_scr[:, pl.ds(tm, CONV_HALO), :]

        us_scr[0, pl.ds(CONV_HALO, tm), :] = p_ref[:, pl.ds(0, e)] * _sigmoid(p_ref[:, pl.ds(e, e)])
        for r in range(1, SUBLANES):
            us_scr[r, pl.ds(CONV_HALO, tm), :] = us_scr[0, pl.ds(CONV_HALO - r, tm), :]

        def c_conv(base):
            for lt in range(e // LANES):
                cols = pl.ds(lt * LANES, LANES)
                acc = jnp.broadcast_to(sm_ref[pl.ds(32, 1), cols], (rc, LANES))
                for r in range(SUBLANES):
                    nq = (CONV_TAPS - 1 - r) // SUBLANES + 1
                    lo = SUBLANES * (nq - 1)
                    win = us_scr[r, pl.ds(pl.multiple_of(CONV_HALO + base - lo, SUBLANES), rc + lo), cols]
                    for q in range(nq):
                        k = CONV_TAPS - 1 - (SUBLANES * q + r)
                        at = lo - SUBLANES * q
                        acc = acc + sm_ref[pl.ds(k, 1), cols] * win[at:at + rc, :]
                uc_ref[pl.ds(base, rc), cols] = acc
        _chunks(tm // rc, rc, c_conv)

        _, _, ul = _layernorm_rows(uc_ref[...], sm_ref[pl.ds(33, 1), :], sm_ref[pl.ds(34, 1), :])
        z = p_ref[:, pl.ds(2 * e, e)]
        v = ((ul * _sigmoid(ul)) * (z * _sigmoid(z))).astype(BF16)
        ho_ref[...] = h_ref[...] + _dot(v, wo_ref[...])

    (ho, uc), xouts = _launch(
        body, name, t // tm,
        [pl.BlockSpec((tm, 3 * e), lambda i: (i, 0)), pl.BlockSpec((tm, d), lambda i: (i, 0)),
         _const((None, nsm, e), (layer, 0, 0)), _const((e, d), (0, 0))],
        [pl.BlockSpec((tm, d), lambda i: (i, 0)), pl.BlockSpec((tm, e), lambda i: (i, 0))],
        [jax.ShapeDtypeStruct((t, d), F32), jax.ShapeDtypeStruct((t, e), F32)],
        [pltpu.VMEM((SUBLANES, tm + CONV_HALO, e), F32)], (p, h, small, w_out), ex)
    return ho, uc, xouts


def _conv_bwd(dho, p, uc, small, layer, w_out, name, ex=None):
    t, d = dho.shape
    e = w_out.shape[0]
    tm = _tile(t, 256)
    nt = t // tm
    rc = 16
    nsm = small.shape[1]

    def body(dho_ref, p_ref, uc_ref, sm_ref, wo_ref, dp_ref, dwo_ref, dsm_ref, ds_scr, acc_scr, dwo_scr):
        i = pl.program_id(0)

        @pl.when(i == 0)
        def _():
            dwo_scr[...] = jnp.zeros_like(dwo_scr)
            acc_scr[...] = jnp.zeros_like(acc_scr)
            ds_scr[:, pl.ds(tm, CONV_HALO), :] = jnp.zeros((SUBLANES, CONV_HALO, e), F32)

        @pl.when(i > 0)
        def _():
            ds_scr[:, pl.ds(tm, CONV_HALO), :] = ds_scr[:, pl.ds(0, CONV_HALO), :]

        lng = sm_ref[pl.ds(33, 1), :]
        lnb = sm_ref[pl.ds(34, 1), :]

        xhat, rstd, ul = _layernorm_rows(uc_ref[...], lng, lnb)
        z = p_ref[:, pl.ds(2 * e, e)]
        sg_u = _sigmoid(ul)
        sg_z = _sigmoid(z)
        s_u = ul * sg_u
        s_z = z * sg_z
        v = (s_u * s_z).astype(BF16)
        dy = dho_ref[...].astype(BF16)
        dv = _dot_nt(dy, wo_ref[...])
        dwo_scr[...] += _dot_tn(v, dy)

        dul = dv * s_z * _dsilu(ul, sg_u)
        dp_ref[:, pl.ds(2 * e, e)] = (dv * s_u * _dsilu(z, sg_z)).astype(BF16)
        acc_scr[pl.ds(33 * SUBLANES, SUBLANES), :] += _rows8(dul * xhat)
        acc_scr[pl.ds(34 * SUBLANES, SUBLANES), :] += _rows8(dul)
        dxh = dul * lng
        duc = rstd * (dxh - jnp.mean(dxh, axis=-1, keepdims=True)
                      - xhat * jnp.mean(dxh * xhat, axis=-1, keepdims=True))
        acc_scr[pl.ds(32 * SUBLANES, SUBLANES), :] += _rows8(duc)
        ds_scr[0, pl.ds(0, tm), :] = duc
        for r in range(1, SUBLANES):
            ds_scr[r, pl.ds(0, tm), :] = ds_scr[0, pl.ds(r, tm), :]

        def c_conv(base):
            rows = pl.ds(base, rc)
            a = p_ref[rows, pl.ds(0, e)]
            b = p_ref[rows, pl.ds(e, e)]
            sb = _sigmoid(b)
            u = a * sb
            du = jnp.zeros((rc, e), F32)
            for o in range(CONV_TAPS):
                q, r = divmod(o, SUBLANES)
                k = CONV_TAPS - 1 - o
                sh = ds_scr[r, pl.ds(pl.multiple_of(base + SUBLANES * q, SUBLANES), rc), :]
                du = du + sm_ref[pl.ds(k, 1), :] * sh
                acc_scr[pl.ds(k * SUBLANES, SUBLANES), :] += _rows8(u * sh)
            dp_ref[rows, pl.ds(0, e)] = (du * sb).astype(BF16)
            dp_ref[rows, pl.ds(e, e)] = (du * u * (1.0 - sb)).astype(BF16)
        _chunks(tm // rc, rc, c_conv)

        @pl.when(i == nt - 1)
        def _():
            dwo_ref[...] = dwo_scr[...].astype(BF16)
            for k in range(nsm):
                dsm_ref[pl.ds(k, 1), :] = jnp.sum(acc_scr[pl.ds(k * SUBLANES, SUBLANES), :], axis=0, keepdims=True)

    rev = lambda i: (nt - 1 - i, 0)
    (dp, dwo, dsm), xouts = _launch(
        body, name, nt,
        [pl.BlockSpec((tm, d), rev), pl.BlockSpec((tm, 3 * e), rev), pl.BlockSpec((tm, e), rev),
         _const((None, nsm, e), (layer, 0, 0)), _const((e, d), (0, 0))],
        [pl.BlockSpec((tm, 3 * e), rev), _const((e, d), (0, 0)), _const((nsm, e), (0, 0))],
        [jax.ShapeDtypeStruct((t, 3 * e), BF16), jax.ShapeDtypeStruct((e, d), BF16),
         jax.ShapeDtypeStruct((nsm, e), F32)],
        [pltpu.VMEM((SUBLANES, tm + CONV_HALO, e), F32), pltpu.VMEM((nsm * SUBLANES, e), F32),
         pltpu.VMEM((e, d), F32)],
        (dho, p, uc, small, w_out), ex)
    return dp, dwo, dsm, xouts


def _inv_count(tile, tm, w):
    tpos = tile * tm + lax.broadcasted_iota(jnp.int32, (tm, 1), 0)
    return 1.0 / jnp.minimum(tpos + 1, w).astype(F32)


def _pool_d(ue_scr, tile, tm, gc):
    out = []
    for g, w in enumerate(POOL_WINDOWS):
        win = ue_scr[:, pl.ds(g * gc, gc)]
        s = win
        sh = 1
        while sh < w:
            s = s + pltpu.roll(s, sh, axis=0)
            sh *= 2
        out.append(s[POOL_HALO:, :] * _inv_count(tile, tm, w) - win[POOL_HALO:, :])
    return out


def _final_rows(ho, tg_ref, fg_ref, dh_ref, acc_scr, lacc_scr):
    d = ho.shape[-1]
    r = lax.rsqrt(jnp.mean(ho * ho, axis=-1, keepdims=True) + RMS_EPS)
    nrm = ho * r
    err = nrm * fg_ref[...] - tg_ref[...]
    lacc_scr[...] += _rows8(err * err)
    dy = err * (1.0 / d)
    acc_scr[...] += _rows8(dy * nrm)
    dq = dy * fg_ref[...]
    dh_ref[...] = r * (dq - nrm * jnp.mean(dq * nrm, axis=-1, keepdims=True))


def _pool_fwd(h, norm_g, nlayer, w_in, w_grp, small, layer, w_out, name, ex=None, final=None):
    t, d = h.shape
    e = w_out.shape[0]
    ng = len(POOL_WINDOWS)
    gc = e // ng
    tm = _tile(t, 512)
    nt = t // tm
    nsm = small.shape[1]

    def layer_rows(i, h_ref, g_ref, wi_ref, wg_ref, sm_ref, wo_ref, p_ref, ue_scr, y_scr):
        x = h_ref[...]
        r = lax.rsqrt(jnp.mean(x * x, axis=-1, keepdims=True) + RMS_EPS)
        p_ref[...] = _dot((x * r * g_ref[...]).astype(BF16), wi_ref[...])

        @pl.when(i == 0)
        def _():
            ue_scr[pl.ds(0, POOL_HALO), :] = jnp.zeros((POOL_HALO, e), F32)

        @pl.when(i > 0)
        def _():
            ue_scr[pl.ds(0, POOL_HALO), :] = ue_scr[pl.ds(tm, POOL_HALO), :]

        ue_scr[pl.ds(POOL_HALO, tm), :] = p_ref[:, pl.ds(0, e)]

        for g, dg in enumerate(_pool_d(ue_scr, i, tm, gc)):
            cols = pl.ds(g * gc, gc)
            z = p_ref[:, pl.ds(e + g * gc, gc)]
            y1 = (_dot(dg.astype(BF16), wg_ref[g]) + sm_ref[pl.ds(32, 1), cols]) * sm_ref[pl.ds(33, 1), cols]
            y_scr[:, cols] = (y1 * (z * _sigmoid(z))).astype(BF16)

        return x + _dot(y_scr[...], wo_ref[...])

    row = lambda i: (i, 0)
    in_specs = [pl.BlockSpec((tm, d), row), _const((None, 1, d), (nlayer, 0, 0)), _const((d, 2 * e), (0, 0)),
                _const((ng, gc, gc), (0, 0, 0)), _const((None, nsm, e), (layer, 0, 0)), _const((e, d), (0, 0))]
    scratch = [pltpu.VMEM((tm + POOL_HALO, e), F32), pltpu.VMEM((tm, e), BF16)]
    args = (h, norm_g, w_in, w_grp, small, w_out)

    if final is None:
        def body(h_ref, g_ref, wi_ref, wg_ref, sm_ref, wo_ref, ho_ref, p_ref, ue_scr, y_scr):
            ho_ref[...] = layer_rows(pl.program_id(0), h_ref, g_ref, wi_ref, wg_ref, sm_ref, wo_ref, p_ref,
                                     ue_scr, y_scr)

        (ho, p), xouts = _launch(
            body, name, nt, in_specs, [pl.BlockSpec((tm, d), row), pl.BlockSpec((tm, 2 * e), row)],
            [jax.ShapeDtypeStruct((t, d), F32), jax.ShapeDtypeStruct((t, 2 * e), F32)], scratch, args, ex)
        return ho, p, xouts

    target, final_g = final

    def body(h_ref, g_ref, wi_ref, wg_ref, sm_ref, wo_ref, tg_ref, fg_ref, dh_ref, p_ref, dfg_ref, loss_ref,
             ue_scr, y_scr, acc_scr, lacc_scr):
        i = pl.program_id(0)

        @pl.when(i == 0)
        def _():
            acc_scr[...] = jnp.zeros_like(acc_scr)
            lacc_scr[...] = jnp.zeros_like(lacc_scr)

        ho = layer_rows(i, h_ref, g_ref, wi_ref, wg_ref, sm_ref, wo_ref, p_ref, ue_scr, y_scr)
        _final_rows(ho, tg_ref, fg_ref, dh_ref, acc_scr, lacc_scr)

        @pl.when(i == nt - 1)
        def _():
            dfg_ref[...] = jnp.zeros_like(dfg_ref)
            dfg_ref[pl.ds(0, 1), :] = jnp.sum(acc_scr[...], axis=0, keepdims=True)
            loss_ref[...] = jnp.broadcast_to(jnp.sum(lacc_scr[...]) * (0.5 / d), loss_ref.shape)

    (dh, p, dfg, loss), xouts = _launch(
        body, name, nt, in_specs + [pl.BlockSpec((tm, d), row), _const((1, d), (0, 0))],
        [pl.BlockSpec((tm, d), row), pl.BlockSpec((tm, 2 * e), row), _const((SUBLANES, d), (0, 0)),
         _const((SUBLANES, LANES), (0, 0))],
        [jax.ShapeDtypeStruct((t, d), F32), jax.ShapeDtypeStruct((t, 2 * e), F32),
         jax.ShapeDtypeStruct((SUBLANES, d), F32), jax.ShapeDtypeStruct((SUBLANES, LANES), F32)],
        scratch + [pltpu.VMEM((SUBLANES, d), F32), pltpu.VMEM((SUBLANES, d), F32)],
        args + (target, final_g), ex)
    return dh, p, dfg, loss, xouts


def _pool_bwd(dho, p, w_grp, small, layer, w_out, name, ex=None):
    t, d = dho.shape
    e = w_out.shape[0]
    ng = len(POOL_WINDOWS)
    gc = e // ng
    tm = _tile(t, 512)
    nt = t // tm
    nsm = small.shape[1]
    hb = tm // POOL_HALO

    def body(dho_ref, p_ref, ph_ref, wg_ref, sm_ref, wo_ref, dp_ref, dwo_ref, dwg_ref, dsm_ref,
             ue_scr, ee_scr, acc_scr, dwo_scr, dwg_scr):
        i = pl.program_id(0)
        tile = nt - 1 - i

        @pl.when(i == 0)
        def _():
            dwo_scr[...] = jnp.zeros_like(dwo_scr)
            dwg_scr[...] = jnp.zeros_like(dwg_scr)
            acc_scr[...] = jnp.zeros_like(acc_scr)
            ee_scr[pl.ds(tm, POOL_HALO), :] = jnp.zeros((POOL_HALO, e), F32)

        @pl.when(i > 0)
        def _():
            ee_scr[pl.ds(tm, POOL_HALO), :] = ee_scr[pl.ds(0, POOL_HALO), :]

        @pl.when(tile == 0)
        def _():
            ue_scr[pl.ds(0, POOL_HALO), :] = jnp.zeros((POOL_HALO, e), F32)

        @pl.when(tile > 0)
        def _():
            ue_scr[pl.ds(0, POOL_HALO), :] = ph_ref[:, pl.ds(0, e)]

        ue_scr[pl.ds(POOL_HALO, tm), :] = p_ref[:, pl.ds(0, e)]

        bg = sm_ref[pl.ds(32, 1), :]
        sc = sm_ref[pl.ds(33, 1), :]
        ds = [dg.astype(BF16) for dg in _pool_d(ue_scr, tile, tm, gc)]
        ob = jnp.concatenate([_dot(ds[g], wg_ref[g]) for g in range(ng)], axis=1) + bg
        z = p_ref[:, pl.ds(e, e)]
        sg_z = _sigmoid(z)
        s_z = z * sg_z
        y1 = ob * sc
        dy = dho_ref[...].astype(BF16)
        dy2 = _dot_nt(dy, wo_ref[...])
        dwo_scr[...] += _dot_tn((y1 * s_z).astype(BF16), dy)
        dy1 = dy2 * s_z
        dp_ref[:, pl.ds(e, e)] = (dy2 * y1 * _dsilu(z, sg_z)).astype(BF16)
        acc_scr[pl.ds(33 * SUBLANES, SUBLANES), :] += _rows8(dy1 * ob)
        do = dy1 * sc
        acc_scr[pl.ds(32 * SUBLANES, SUBLANES), :] += _rows8(do)

        n = tm + POOL_HALO
        for g, w in enumerate(POOL_WINDOWS):
            cols = pl.ds(g * gc, gc)
            do_g = do[:, g * gc:(g + 1) * gc].astype(BF16)
            dwg_scr[g] += _dot_tn(ds[g], do_g)
            dd = _dot_nt(do_g, wg_ref[g])
            ee_scr[pl.ds(0, tm), cols] = dd * _inv_count(tile, tm, w)
            s = ee_scr[:, cols]
            sh = 1
            while sh < w:
                s = s + pltpu.roll(s, n - sh, axis=0)
                sh *= 2
            dp_ref[:, cols] = (s[:tm, :] - dd).astype(BF16)

        @pl.when(i == nt - 1)
        def _():
            dwo_ref[...] = dwo_scr[...].astype(BF16)
            dwg_ref[...] = dwg_scr[...].astype(BF16)
            dsm_ref[...] = jnp.zeros_like(dsm_ref)
            for k in (32, 33):
                dsm_ref[pl.ds(k, 1), :] = jnp.sum(acc_scr[pl.ds(k * SUBLANES, SUBLANES), :], axis=0, keepdims=True)

    rev = lambda i: (nt - 1 - i, 0)
    (dp, dwo, dwg, dsm), xouts = _launch(
        body, name, nt,
        [pl.BlockSpec((tm, d), rev), pl.BlockSpec((tm, 2 * e), rev),
         pl.BlockSpec((POOL_HALO, 2 * e), lambda i: (jnp.maximum((nt - 1 - i) * hb - 1, 0), 0)),
         _const((ng, gc, gc), (0, 0, 0)), _const((None, nsm, e), (layer, 0, 0)), _const((e, d), (0, 0))],
        [pl.BlockSpec((tm, 2 * e), rev), _const((e, d), (0, 0)), _const((ng, gc, gc), (0, 0, 0)),
         _const((nsm, e), (0, 0))],
        [jax.ShapeDtypeStruct((t, 2 * e), BF16), jax.ShapeDtypeStruct((e, d), BF16),
         jax.ShapeDtypeStruct((ng, gc, gc), BF16), jax.ShapeDtypeStruct((nsm, e), F32)],
        [pltpu.VMEM((tm + POOL_HALO, e), F32), pltpu.VMEM((tm + POOL_HALO, e), F32),
         pltpu.VMEM((nsm * SUBLANES, e), F32), pltpu.VMEM((e, d), F32), pltpu.VMEM((ng, gc, gc), F32)],
        (dho, p, p, w_grp, small, w_out), ex)
    return dp, dwo, dwg, dsm, xouts


def _in_bwd(dp, h, dho, norm_g, layer, w_in, name, ex=None):
    t, d = h.shape
    n = w_in.shape[-1]
    tm = _tile(t, 512)
    nt = t // tm

    def body(dp_ref, h_ref, dho_ref, g_ref, w_ref, dh_ref, dw_ref, dg_ref, acc_scr, dw_scr):
        i = pl.program_id(0)

        @pl.when(i == 0)
        def _():
            dw_scr[...] = jnp.zeros_like(dw_scr)
            acc_scr[...] = jnp.zeros_like(acc_scr)

        x = h_ref[...]
        r = lax.rsqrt(jnp.mean(x * x, axis=-1, keepdims=True) + RMS_EPS)
        nrm = x * r
        dp = dp_ref[...]
        dhn = _dot_nt(dp, w_ref[...])
        dw_scr[...] += _dot_tn((nrm * g_ref[...]).astype(BF16), dp)
        acc_scr[...] += _rows8(dhn * nrm)
        dq = dhn * g_ref[...]
        dh_ref[...] = dho_ref[...] + r * (dq - nrm * jnp.mean(dq * nrm, axis=-1, keepdims=True))

        @pl.when(i == nt - 1)
        def _():
            dw_ref[...] = dw_scr[...].astype(BF16)
            dg_ref[...] = jnp.zeros_like(dg_ref)
            dg_ref[pl.ds(0, 1), :] = jnp.sum(acc_scr[...], axis=0, keepdims=True)

    (dh, dw, dg), xouts = _launch(
        body, name, nt,
        [pl.BlockSpec((tm, n), lambda i: (i, 0)), pl.BlockSpec((tm, d), lambda i: (i, 0)),
         pl.BlockSpec((tm, d), lambda i: (i, 0)), _const((None, 1, d), (layer, 0, 0)), _const((d, n), (0, 0))],
        [pl.BlockSpec((tm, d), lambda i: (i, 0)), _const((d, n), (0, 0)), _const((SUBLANES, d), (0, 0))],
        [jax.ShapeDtypeStruct((t, d), F32), jax.ShapeDtypeStruct((d, n), BF16),
         jax.ShapeDtypeStruct((SUBLANES, d), F32)],
        [pltpu.VMEM((SUBLANES, d), F32), pltpu.VMEM((d, n), F32)],
        (dp, h, dho, norm_g, w_in), ex)
    return dh, dw, dg, xouts


def _adamw(stacks, w, m, v, name):
    nl = len(stacks)
    shp = w.shape
    c = shp[-1]
    r = 1
    for s in shp[1:-1]:
        r *= s
    tr = r
    for cand in (512, 256, 128, 64, 32, 16):
        if r % cand == 0 and r > cand:
            tr = cand
            break
    nrb = r // tr
    c1 = 1.0 / (1.0 - ADAM_B1 ** ADAM_STEP)
    c2 = 1.0 / (1.0 - ADAM_B2 ** ADAM_STEP)

    def body(*refs):
        s_refs = refs[:nl]
        w_ref, m_ref, v_ref, g_ref, d_ref, nm_ref, nv_ref = refs[nl:]
        layer = pl.program_id(0)
        for l in range(nl):
            @pl.when(layer == l)
            def _(l=l):
                g = s_refs[l][0].astype(F32)
                for k in range(1, N_DEV):
                    g = g + s_refs[l][k].astype(F32)
                nm = ADAM_B1 * m_ref[...] + (1.0 - ADAM_B1) * g
                nv = ADAM_B2 * v_ref[...] + (1.0 - ADAM_B2) * (g * g)
                g_ref[...] = g
                nm_ref[...] = nm
                nv_ref[...] = nv
                d_ref[...] = -ADAM_LR * ((nm * c1) / (jnp.sqrt(nv * c2) + ADAM_EPS) + ADAM_WD * w_ref[...])

    def stack_spec(l):
        return pl.BlockSpec((N_DEV, tr, c),
                            lambda j, i: (0, jnp.where(j == l, i, jnp.where(j < l, 0, nrb - 1)), 0))

    spec = pl.BlockSpec((None, tr, c), lambda j, i: (j, i, 0))
    outs = pl.pallas_call(
        body, name=name, grid=(nl, nrb),
        in_specs=[stack_spec(l) for l in range(nl)] + [spec, spec, spec],
        out_specs=[spec] * 4,
        out_shape=[jax.ShapeDtypeStruct((nl, r, c), F32)] * 4,
        compiler_params=_params(2),
    )(*[s.reshape(N_DEV, r, c) for s in stacks], w.reshape(nl, r, c), m.reshape(nl, r, c), v.reshape(nl, r, c))
    return [o.reshape(shp) for o in outs]


SMALL_ROWS = 40


def _pad_rows(a, rows):
    pad = [(0, 0)] * a.ndim
    pad[-2] = (0, rows - a.shape[-2])
    return jnp.pad(a, pad)


def _pack_sharded_small(conv_dw, pool_b_grp, pool_scale):
    return jnp.concatenate([_pad_rows(conv_dw, 32),
                            _pad_rows(jnp.stack([pool_b_grp, pool_scale], axis=1), SMALL_ROWS - 32)], axis=1)


def _unpack_sharded_small(a):
    return a[:, :CONV_TAPS, :], a[:, 32, :], a[:, 33, :]


REP_LOSS_ROW = 40


def _pack_replicated(norm_g, final_g, conv_dw_b, conv_ln_g, conv_ln_b, loss_row=None):
    parts = [norm_g, final_g[None, :], conv_dw_b, conv_ln_g, conv_ln_b,
             jnp.zeros((1, final_g.shape[0]), F32) if loss_row is None else loss_row]
    return jnp.concatenate([_pad_rows(p, SUBLANES) for p in parts], axis=0)


def _unpack_replicated(a):
    return a[0:4], a[8], a[16:18], a[24:26], a[32:34]


def kernel(x, norm_g, final_g, conv_w_in, conv_dw, conv_dw_b, conv_ln_g, conv_ln_b, conv_w_out, pool_w_in, pool_w_grp, pool_b_grp, pool_scale, pool_w_out, loss_target, m_norm_g, m_final_g, m_conv_w_in, m_conv_dw, m_conv_dw_b, m_conv_ln_g, m_conv_ln_b, m_conv_w_out, m_pool_w_in, m_pool_w_grp, m_pool_b_grp, m_pool_scale, m_pool_w_out, v_norm_g, v_final_g, v_conv_w_in, v_conv_dw, v_conv_dw_b, v_conv_ln_g, v_conv_ln_b, v_conv_w_out, v_pool_w_in, v_pool_w_grp, v_pool_b_grp, v_pool_scale, v_pool_w_out):
    h0 = x[0]
    target = loss_target[0]
    d = h0.shape[-1]
    ng3 = norm_g[:, None, :]

    cwi, cwo, pwi = conv_w_in.astype(BF16), conv_w_out.astype(BF16), pool_w_in.astype(BF16)
    pwg, pwo = pool_w_grp.astype(BF16), pool_w_out.astype(BF16)
    small_sh = _pack_sharded_small(conv_dw, pool_b_grp, pool_scale)

    (cw_in0,) = _exchange_call(_Exchange("gather", [cwi[0]], [1]), "gather_first")
    p0, (cw_out0, small_full, pw_in0) = _in_fwd(
        h0, ng3, 0, cw_in0, "conv_in_fwd_0", _Exchange("gather", [cwo[0], small_sh, pwi[0]], [0, 2, 1]))
    conv_small = small_full.at[:, 32, :].set(conv_dw_b).at[:, 33, :].set(conv_ln_g).at[:, 34, :].set(conv_ln_b)
    pool_small = small_full
    h1, uc0, (pw_grp0, pw_out0, cw_in1, cw_out1, pw_in1, pw_grp1, pw_out1) = _conv_fwd(
        p0, h0, conv_small, 0, cw_out0, "conv_mix_fwd_0",
        _Exchange("gather", [pwg[0], pwo[0], cwi[1], cwo[1], pwi[1], pwg[1], pwo[1]], [1, 0, 1, 0, 1, 1, 0]))
    h2, p1, _ = _pool_fwd(h1, ng3, 1, pw_in0, pw_grp0, pool_small, 0, pw_out0, "pool_fwd_0")
    p2, _ = _in_fwd(h2, ng3, 2, cw_in1, "conv_in_fwd_1")
    h3, uc2, _ = _conv_fwd(p2, h2, conv_small, 1, cw_out1, "conv_mix_fwd_1")
    dh, p3, d_final_g, loss_part, _ = _pool_fwd(h3, ng3, 3, pw_in1, pw_grp1, pool_small, 1, pw_out1, "pool_fwd_1",
                                               final=(target, final_g[None, :]))

    scatter = lambda arrays, axes: _Exchange("scatter", arrays, axes)
    dp, g_pwo1, g_pwg1, gsm_p1, _ = _pool_bwd(dh, p3, pw_grp1, pool_small, 1, pw_out1, "pool_mix_bwd_1")
    dh, g_pwi1, dg3, (s_pwo1, s_pwg1) = _in_bwd(dp, h3, dh, ng3, 3, pw_in1, "pool_in_bwd_1",
                                                scatter([g_pwo1, g_pwg1], [0, 1]))
    dp, g_cwo1, gsm_c1, (s_pwi1,) = _conv_bwd(dh, p2, uc2, conv_small, 1, cw_out1, "conv_mix_bwd_1",
                                              scatter([g_pwi1], [1]))
    dh, g_cwi1, dg2, (s_cwo1,) = _in_bwd(dp, h2, dh, ng3, 2, cw_in1, "conv_in_bwd_1", scatter([g_cwo1], [0]))
    dp, g_pwo0, g_pwg0, gsm_p0, (s_cwi1,) = _pool_bwd(dh, p1, pw_grp0, pool_small, 0, pw_out0, "pool_mix_bwd_0",
                                                      scatter([g_cwi1], [1]))
    dh, g_pwi0, dg1, (s_pwo0, s_pwg0) = _in_bwd(dp, h1, dh, ng3, 1, pw_in0, "pool_in_bwd_0",
                                                scatter([g_pwo0, g_pwg0], [0, 1]))
    dp, g_cwo0, gsm_c0, (s_pwi0,) = _conv_bwd(dh, p0, uc0, conv_small, 0, cw_out0, "conv_mix_bwd_0",
                                              scatter([g_pwi0], [1]))
    dh, g_cwi0, dg0, _ = _in_bwd(dp, h0, dh, ng3, 0, cw_in0, "conv_in_bwd_0")
    grad_x = dh[None]

    sharded_small_part = jnp.stack([
        jnp.concatenate([gc[:32], _pad_rows(gp[32:34], SMALL_ROWS - 32)], axis=0)
        for gc, gp in ((gsm_c0, gsm_p0), (gsm_c1, gsm_p1))])
    rep_part = _pack_replicated(jnp.stack([dg0[0], dg1[0], dg2[0], dg3[0]]), d_final_g[0],
                                *[jnp.stack([gsm_c0[r], gsm_c1[r]]) for r in (32, 33, 34)],
                                loss_row=jnp.broadcast_to(loss_part[0:1, 0:1], (1, d)))
    s_cwo0, s_cwi0, s_small, s_rep = _exchange_call(
        scatter([g_cwo0, g_cwi0, sharded_small_part, rep_part], [0, 1, 2, None]), "scatter_last")

    res = {}
    res["conv_w_in"] = _adamw([s_cwi0, s_cwi1], conv_w_in, m_conv_w_in, v_conv_w_in, "adamw_conv_w_in")
    res["conv_w_out"] = _adamw([s_cwo0, s_cwo1], conv_w_out, m_conv_w_out, v_conv_w_out, "adamw_conv_w_out")
    res["pool_w_in"] = _adamw([s_pwi0, s_pwi1], pool_w_in, m_pool_w_in, v_pool_w_in, "adamw_pool_w_in")
    res["pool_w_grp"] = _adamw([s_pwg0, s_pwg1], pool_w_grp, m_pool_w_grp, v_pool_w_grp, "adamw_pool_w_grp")
    res["pool_w_out"] = _adamw([s_pwo0, s_pwo1], pool_w_out, m_pool_w_out, v_pool_w_out, "adamw_pool_w_out")
    sm = _adamw([s_small], small_sh[None], _pack_sharded_small(m_conv_dw, m_pool_b_grp, m_pool_scale)[None],
                _pack_sharded_small(v_conv_dw, v_pool_b_grp, v_pool_scale)[None], "adamw_sharded_small")
    rep = _adamw([s_rep], _pack_replicated(norm_g, final_g, conv_dw_b, conv_ln_g, conv_ln_b)[None],
                 _pack_replicated(m_norm_g, m_final_g, m_conv_dw_b, m_conv_ln_g, m_conv_ln_b)[None],
                 _pack_replicated(v_norm_g, v_final_g, v_conv_dw_b, v_conv_ln_g, v_conv_ln_b)[None],
                 "adamw_replicated")
    loss = rep[0][0, REP_LOSS_ROW, 0]
    res["conv_dw"], res["pool_b_grp"], res["pool_scale"] = zip(*[_unpack_sharded_small(a[0]) for a in sm])
    (res["norm_g"], res["final_g"], res["conv_dw_b"], res["conv_ln_g"], res["conv_ln_b"]) = zip(
        *[_unpack_replicated(a[0]) for a in rep])

    names = ["norm_g", "final_g", "conv_w_in", "conv_dw", "conv_dw_b", "conv_ln_g", "conv_ln_b", "conv_w_out",
             "pool_w_in", "pool_w_grp", "pool_b_grp", "pool_scale", "pool_w_out"]
    return (loss, grad_x) + tuple(res[n][q] for q in range(4) for n in names)
```

```python
import jax
import jax.numpy as jnp
from jax import lax
from jax.experimental import pallas as pl
from jax.experimental.pallas import tpu as pltpu

F32 = jnp.float32
BF16 = jnp.bfloat16

RMS_EPS = 1e-6
LN_EPS = 1e-5
CONV_TAPS = 31
CONV_HALO = 32
POOL_WINDOWS = (2, 4, 8, 16)
POOL_HALO = 16
SUBLANES = 8
LANES = 128
N_DEV = 8
V7X_VMEM_LIMIT = 56 * 1024 * 1024

ADAM_LR = 0.001
ADAM_B1 = 0.9
ADAM_B2 = 0.999
ADAM_EPS = 1e-08
ADAM_WD = 0.01
ADAM_STEP = 10

MESH = pl.DeviceIdType.MESH
ANY = pl.BlockSpec(memory_space=pl.ANY)


def _dot(a, b):
    return lax.dot_general(a, b, (((1,), (0,)), ((), ())), preferred_element_type=F32)


def _dot_nt(a, b):
    return lax.dot_general(a, b, (((1,), (1,)), ((), ())), preferred_element_type=F32)


def _dot_tn(a, b):
    return lax.dot_general(a, b, (((0,), (0,)), ((), ())), preferred_element_type=F32)


def _sigmoid(x):
    return jax.nn.sigmoid(x)


def _dsilu(x, s):
    return s * (1.0 + x * (1.0 - s))


def _rows8(x):
    r, c = x.shape
    return jnp.sum(x.reshape(r // SUBLANES, SUBLANES, c), axis=0)


def _tile(t, pref):
    return pref if t >= 2 * pref else t // 2


def _const(shape, index):
    return pl.BlockSpec(shape, lambda *_: index, pipeline_mode=pl.Buffered(1))


def _params(grid_rank=1):
    return pltpu.CompilerParams(dimension_semantics=("arbitrary",) * grid_rank, vmem_limit_bytes=V7X_VMEM_LIMIT)


def _chunks(n, rc, fn):
    def step(c, carry):
        fn(pl.multiple_of(c * rc, rc))
        return carry
    lax.fori_loop(0, n, step, 0)


def _mesh_position():
    return lax.axis_index("x"), lax.axis_index("y"), lax.axis_index("c")


def _peer(j):
    x, y, c = _mesh_position()
    px = 1 - x if j & 4 else x
    py = 1 - y if j & 2 else y
    pc = 1 - c if j & 1 else c
    return (px, py, pc), 4 * px + 2 * py + pc


def _block(ref, axis, index, size):
    idx = [slice(None)] * len(ref.shape)
    idx[axis] = pl.ds(pl.multiple_of(index * size, size), size)
    return ref.at[tuple(idx)]


class _Exchange:
    def __init__(self, kind, arrays, axes):
        self.kind, self.axes = kind, list(axes)
        self.arrays = [a[0] if isinstance(a, tuple) else a for a in arrays]
        self.layers = [a[1] if isinstance(a, tuple) else None for a in arrays]
        self.n = len(self.arrays)
        self.blk, self.out_shapes = [], []
        for a, layer, ax in zip(self.arrays, self.layers, self.axes):
            s = list(a.shape if layer is None else a.shape[1:])
            if kind == "gather":
                self.blk.append(s[ax])
                s[ax] *= N_DEV
                self.out_shapes.append(jax.ShapeDtypeStruct(tuple(s), a.dtype))
            else:
                if ax is not None:
                    s[ax] //= N_DEV
                    self.blk.append(s[ax])
                else:
                    self.blk.append(None)
                self.out_shapes.append(jax.ShapeDtypeStruct((N_DEV,) + tuple(s), a.dtype))

    def sem_shapes(self):
        return [pltpu.SemaphoreType.DMA((N_DEV - 1, self.n)), pltpu.SemaphoreType.DMA((N_DEV - 1, self.n)),
                pltpu.SemaphoreType.DMA((self.n,))]

    def _src(self, ins, k, owner):
        ref = ins[k] if self.layers[k] is None else ins[k].at[self.layers[k]]
        if self.kind == "gather" or self.axes[k] is None:
            return ref
        return _block(ref, self.axes[k], owner, self.blk[k])

    def _dst(self, outs, k, sender):
        if self.kind == "gather":
            return _block(outs[k], self.axes[k], sender, self.blk[k])
        return outs[k].at[sender]

    def _copies(self, ins, outs, sems, arriving):
        send, recv, loc = sems
        x, y, c = _mesh_position()
        me = 4 * x + 2 * y + c
        if not arriving:
            local = [pltpu.make_async_copy(self._src(ins, k, me), self._dst(outs, k, me), loc.at[k])
                     for k in range(self.n)]
        else:
            local = []
        remote = []
        for j in range(1, N_DEV):
            peer, peer_id = _peer(j)
            for k in range(self.n):
                owner, sender = (me, peer_id) if arriving else (peer_id, me)
                remote.append(pltpu.make_async_remote_copy(
                    src_ref=self._src(ins, k, owner), dst_ref=self._dst(outs, k, sender),
                    send_sem=send.at[j - 1, k], recv_sem=recv.at[j - 1, k], device_id=peer, device_id_type=MESH))
        return local, remote

    def start(self, ins, outs, sems):
        local, sends = self._copies(ins, outs, sems, arriving=False)
        for cp in local + sends:
            cp.start()

    def finish(self, ins, outs, sems):
        for cp in self._copies(ins, outs, sems, arriving=True)[1]:
            cp.wait_recv()
        local, sends = self._copies(ins, outs, sems, arriving=False)
        for cp in sends:
            cp.wait_send()
        for cp in local:
            cp.wait()


def _exchange_call(ex, name):
    def body(*refs):
        ins, outs, sems = refs[:ex.n], refs[ex.n:2 * ex.n], refs[2 * ex.n:]
        ex.start(ins, outs, sems)
        ex.finish(ins, outs, sems)

    return pl.pallas_call(body, name=name, in_specs=[ANY] * ex.n, out_specs=[ANY] * ex.n,
                          out_shape=ex.out_shapes, scratch_shapes=ex.sem_shapes())(*ex.arrays)


def _launch(body, name, nt, in_specs, out_specs, out_shape, scratch_shapes, args, ex=None):
    if ex is None:
        outs = pl.pallas_call(body, name=name, grid=(nt,), in_specs=in_specs, out_specs=out_specs,
                              out_shape=out_shape, scratch_shapes=scratch_shapes, compiler_params=_params())(*args)
        return list(outs), []
    n_in, n_out, n_scr = len(in_specs), len(out_specs), len(scratch_shapes)

    def riding(*refs):
        a, xa = refs[:n_in], refs[n_in:n_in + ex.n]
        o = refs[n_in + ex.n:n_in + ex.n + n_out]
        xo = refs[n_in + ex.n + n_out:n_in + 2 * ex.n + n_out]
        s = refs[n_in + 2 * ex.n + n_out:n_in + 2 * ex.n + n_out + n_scr]
        sems = refs[n_in + 2 * ex.n + n_out + n_scr:]
        i = pl.program_id(0)

        @pl.when(i == 0)
        def _():
            ex.start(xa, xo, sems)

        body(*a, *o, *s)

        @pl.when(i == nt - 1)
        def _():
            ex.finish(xa, xo, sems)

    outs = pl.pallas_call(
        riding, name=name, grid=(nt,),
        in_specs=list(in_specs) + [ANY] * ex.n, out_specs=list(out_specs) + [ANY] * ex.n,
        out_shape=list(out_shape) + ex.out_shapes, scratch_shapes=list(scratch_shapes) + ex.sem_shapes(),
        compiler_params=_params())(*args, *ex.arrays)
    return list(outs[:n_out]), list(outs[n_out:])


def _in_fwd(h, norm_g, layer, w_in, name, ex=None):
    t, d = h.shape
    n = w_in.shape[-1]
    tm = _tile(t, 512)

    def body(h_ref, g_ref, w_ref, p_ref):
        x = h_ref[...]
        r = lax.rsqrt(jnp.mean(x * x, axis=-1, keepdims=True) + RMS_EPS)
        p_ref[...] = _dot((x * r * g_ref[...]).astype(BF16), w_ref[...])

    (p,), xouts = _launch(
        body, name, t // tm,
        [pl.BlockSpec((tm, d), lambda i: (i, 0)), _const((None, 1, d), (layer, 0, 0)), _const((d, n), (0, 0))],
        [pl.BlockSpec((tm, n), lambda i: (i, 0))],
        [jax.ShapeDtypeStruct((t, n), F32)], [], (h, norm_g, w_in), ex)
    return p, xouts


def _layernorm_rows(uc, lng, lnb):
    mu = jnp.mean(uc, axis=-1, keepdims=True)
    xc = uc - mu
    rstd = lax.rsqrt(jnp.mean(xc * xc, axis=-1, keepdims=True) + LN_EPS)
    xhat = xc * rstd
    return xhat, rstd, xhat * lng + lnb


def _conv_fwd(p, h, dw, vecs, layer, w_out, name, ex=None):
    t, d = h.shape
    e = w_out.shape[0]
    tm = _tile(t, 256)
    rc = _tile(tm, 128)

    def body(p_ref, h_ref, dw_ref, dwb_ref, lng_ref, lnb_ref, wo_ref, ho_ref, uc_ref, us_scr):
        i = pl.program_id(0)

        @pl.when(i == 0)
        def _():
            us_scr[:, pl.ds(0, CONV_HALO), :] = jnp.zeros((SUBLANES, CONV_HALO, e), F32)

        @pl.when(i > 0)
        def _():
            us_scr[:, pl.ds(0, CONV_HALO), :] = us_scr[:, pl.ds(tm, CONV_HALO), :]

        us_scr[0, pl.ds(CONV_HALO, tm), :] = p_ref[:, pl.ds(0, e)] * _sigmoid(p_ref[:, pl.ds(e, e)])
        for r in range(1, SUBLANES):
            us_scr[r, pl.ds(CONV_HALO, tm), :] = us_scr[0, pl.ds(CONV_HALO - r, tm), :]

        def c_conv(base):
            for lt in range(e // LANES):
                cols = pl.ds(lt * LANES, LANES)
                acc = jnp.broadcast_to(dwb_ref[:, cols], (rc, LANES))
                for r in range(SUBLANES):
                    nq = (CONV_TAPS - 1 - r) // SUBLANES + 1
                    lo = SUBLANES * (nq - 1)
                    win = us_scr[r, pl.ds(pl.multiple_of(CONV_HALO + base - lo, SUBLANES), rc + lo), cols]
                    for q in range(nq):
                        k = CONV_TAPS - 1 - (SUBLANES * q + r)
                        at = lo - SUBLANES * q
                        acc = acc + dw_ref[pl.ds(k, 1), cols] * win[at:at + rc, :]
                uc_ref[pl.ds(base, rc), cols] = acc
        _chunks(tm // rc, rc, c_conv)

        _, _, ul = _layernorm_rows(uc_ref[...], lng_ref[...], lnb_ref[...])
        z = p_ref[:, pl.ds(2 * e, e)]
        v = ((ul * _sigmoid(ul)) * (z * _sigmoid(z))).astype(BF16)
        ho_ref[...] = h_ref[...] + _dot(v, wo_ref[...])

    vec = _const((None, 1, e), (layer, 0, 0))
    (ho, uc), xouts = _launch(
        body, name, t // tm,
        [pl.BlockSpec((tm, 3 * e), lambda i: (i, 0)), pl.BlockSpec((tm, d), lambda i: (i, 0)),
         _const((None, CONV_TAPS, e), (layer, 0, 0)), vec, vec, vec, _const((e, d), (0, 0))],
        [pl.BlockSpec((tm, d), lambda i: (i, 0)), pl.BlockSpec((tm, e), lambda i: (i, 0))],
        [jax.ShapeDtypeStruct((t, d), F32), jax.ShapeDtypeStruct((t, e), F32)],
        [pltpu.VMEM((SUBLANES, tm + CONV_HALO, e), F32)], (p, h, dw, *vecs, w_out), ex)
    return ho, uc, xouts


def _conv_bwd(dho, p, uc, dw, vecs, layer, w_out, name, ex=None):
    t, d = dho.shape
    e = w_out.shape[0]
    tm = _tile(t, 256)
    nt = t // tm
    rc = 16
    vec0 = CONV_TAPS + 1

    def body(dho_ref, p_ref, uc_ref, dw_ref, lng_ref, lnb_ref, wo_ref, dp_ref, dwo_ref, ddw_ref, dvec_ref,
             ds_scr, acc_scr, dwo_scr):
        i = pl.program_id(0)

        @pl.when(i == 0)
        def _():
            dwo_scr[...] = jnp.zeros_like(dwo_scr)
            acc_scr[...] = jnp.zeros_like(acc_scr)
            ds_scr[:, pl.ds(tm, CONV_HALO), :] = jnp.zeros((SUBLANES, CONV_HALO, e), F32)

        @pl.when(i > 0)
        def _():
            ds_scr[:, pl.ds(tm, CONV_HALO), :] = ds_scr[:, pl.ds(0, CONV_HALO), :]

        lng = lng_ref[...]
        lnb = lnb_ref[...]

        xhat, rstd, ul = _layernorm_rows(uc_ref[...], lng, lnb)
        z = p_ref[:, pl.ds(2 * e, e)]
        sg_u = _sigmoid(ul)
        sg_z = _sigmoid(z)
        s_u = ul * sg_u
        s_z = z * sg_z
        v = (s_u * s_z).astype(BF16)
        dy = dho_ref[...].astype(BF16)
        dv = _dot_nt(dy, wo_ref[...])
        dwo_scr[...] += _dot_tn(v, dy)

        dul = dv * s_z * _dsilu(ul, sg_u)
        dp_ref[:, pl.ds(2 * e, e)] = (dv * s_u * _dsilu(z, sg_z)).astype(BF16)
        acc_scr[pl.ds((vec0 + 1) * SUBLANES, SUBLANES), :] += _rows8(dul * xhat)
        acc_scr[pl.ds((vec0 + 2) * SUBLANES, SUBLANES), :] += _rows8(dul)
        dxh = dul * lng
        duc = rstd * (dxh - jnp.mean(dxh, axis=-1, keepdims=True)
                      - xhat * jnp.mean(dxh * xhat, axis=-1, keepdims=True))
        acc_scr[pl.ds(vec0 * SUBLANES, SUBLANES), :] += _rows8(duc)
        ds_scr[0, pl.ds(0, tm), :] = duc
        for r in range(1, SUBLANES):
            ds_scr[r, pl.ds(0, tm), :] = ds_scr[0, pl.ds(r, tm), :]

        def c_conv(base):
            rows = pl.ds(base, rc)
            a = p_ref[rows, pl.ds(0, e)]
            b = p_ref[rows, pl.ds(e, e)]
            sb = _sigmoid(b)
            u = a * sb
            du = jnp.zeros((rc, e), F32)
            for o in range(CONV_TAPS):
                q, r = divmod(o, SUBLANES)
                k = CONV_TAPS - 1 - o
                sh = ds_scr[r, pl.ds(pl.multiple_of(base + SUBLANES * q, SUBLANES), rc), :]
                du = du + dw_ref[pl.ds(k, 1), :] * sh
                acc_scr[pl.ds(k * SUBLANES, SUBLANES), :] += _rows8(u * sh)
            dp_ref[rows, pl.ds(0, e)] = (du * sb).astype(BF16)
            dp_ref[rows, pl.ds(e, e)] = (du * u * (1.0 - sb)).astype(BF16)
        _chunks(tm // rc, rc, c_conv)

        @pl.when(i == nt - 1)
        def _():
            dwo_ref[...] = dwo_scr[...].astype(BF16)
            slot_sum = lambda k: jnp.sum(acc_scr[pl.ds(k * SUBLANES, SUBLANES), :], axis=0, keepdims=True)
            for k in range(CONV_TAPS):
                ddw_ref[pl.ds(k, 1), :] = slot_sum(k)
            dvec_ref[...] = jnp.zeros_like(dvec_ref)
            for k in range(3):
                dvec_ref[pl.ds(k, 1), :] = slot_sum(vec0 + k)

    rev = lambda i: (nt - 1 - i, 0)
    vec = _const((None, 1, e), (layer, 0, 0))
    (dp, dwo, ddw, dvec), xouts = _launch(
        body, name, nt,
        [pl.BlockSpec((tm, d), rev), pl.BlockSpec((tm, 3 * e), rev), pl.BlockSpec((tm, e), rev),
         _const((None, CONV_TAPS, e), (layer, 0, 0)), vec, vec, _const((e, d), (0, 0))],
        [pl.BlockSpec((tm, 3 * e), rev), _const((e, d), (0, 0)), _const((CONV_TAPS, e), (0, 0)),
         _const((SUBLANES, e), (0, 0))],
        [jax.ShapeDtypeStruct((t, 3 * e), BF16), jax.ShapeDtypeStruct((e, d), BF16),
         jax.ShapeDtypeStruct((CONV_TAPS, e), F32), jax.ShapeDtypeStruct((SUBLANES, e), F32)],
        [pltpu.VMEM((SUBLANES, tm + CONV_HALO, e), F32), pltpu.VMEM(((vec0 + 3) * SUBLANES, e), F32),
         pltpu.VMEM((e, d), F32)],
        (dho, p, uc, dw, vecs[1], vecs[2], w_out), ex)
    return dp, dwo, ddw, dvec, xouts


def _inv_count(tile, tm, w):
    tpos = tile * tm + lax.broadcasted_iota(jnp.int32, (tm, 1), 0)
    return 1.0 / jnp.minimum(tpos + 1, w).astype(F32)


def _pool_d(ue_scr, tile, tm, gc):
    out = []
    for g, w in enumerate(POOL_WINDOWS):
        win = ue_scr[:, pl.ds(g * gc, gc)]
        s = win
        sh = 1
        while sh < w:
            s = s + pltpu.roll(s, sh, axis=0)
            sh *= 2
        out.append(s[POOL_HALO:, :] * _inv_count(tile, tm, w) - win[POOL_HALO:, :])
    return out


def _final_rows(ho, tg_ref, fg_ref, dh_ref, acc_scr, lacc_scr):
    d = ho.shape[-1]
    r = lax.rsqrt(jnp.mean(ho * ho, axis=-1, keepdims=True) + RMS_EPS)
    nrm = ho * r
    err = nrm * fg_ref[...] - tg_ref[...]
    lacc_scr[...] += _rows8(err * err)
    dy = err * (1.0 / d)
    acc_scr[...] += _rows8(dy * nrm)
    dq = dy * fg_ref[...]
    dh_ref[...] = r * (dq - nrm * jnp.mean(dq * nrm, axis=-1, keepdims=True))


def _pool_fwd(h, norm_g, nlayer, w_in, w_grp, vecs, layer, w_out, name, ex=None, final=None):
    t, d = h.shape
    e = w_out.shape[0]
    ng = len(POOL_WINDOWS)
    gc = e // ng
    tm = _tile(t, 512)
    nt = t // tm

    def layer_rows(i, h_ref, g_ref, wi_ref, wg_ref, bg_ref, sc_ref, wo_ref, p_ref, ue_scr, y_scr):
        x = h_ref[...]
        r = lax.rsqrt(jnp.mean(x * x, axis=-1, keepdims=True) + RMS_EPS)
        p_ref[...] = _dot((x * r * g_ref[...]).astype(BF16), wi_ref[...])

        @pl.when(i == 0)
        def _():
            ue_scr[pl.ds(0, POOL_HALO), :] = jnp.zeros((POOL_HALO, e), F32)

        @pl.when(i > 0)
        def _():
            ue_scr[pl.ds(0, POOL_HALO), :] = ue_scr[pl.ds(tm, POOL_HALO), :]

        ue_scr[pl.ds(POOL_HALO, tm), :] = p_ref[:, pl.ds(0, e)]

        for g, dg in enumerate(_pool_d(ue_scr, i, tm, gc)):
            cols = pl.ds(g * gc, gc)
            z = p_ref[:, pl.ds(e + g * gc, gc)]
            y1 = (_dot(dg.astype(BF16), wg_ref[g]) + bg_ref[:, cols]) * sc_ref[:, cols]
            y_scr[:, cols] = (y1 * (z * _sigmoid(z))).astype(BF16)

        return x + _dot(y_scr[...], wo_ref[...])

    row = lambda i: (i, 0)
    vec = _const((None, 1, e), (layer, 0, 0))
    in_specs = [pl.BlockSpec((tm, d), row), _const((None, 1, d), (nlayer, 0, 0)), _const((d, 2 * e), (0, 0)),
                _const((ng, gc, gc), (0, 0, 0)), vec, vec, _const((e, d), (0, 0))]
    scratch = [pltpu.VMEM((tm + POOL_HALO, e), F32), pltpu.VMEM((tm, e), BF16)]
    args = (h, norm_g, w_in, w_grp, *vecs, w_out)

    if final is None:
        def body(h_ref, g_ref, wi_ref, wg_ref, bg_ref, sc_ref, wo_ref, ho_ref, p_ref, ue_scr, y_scr):
            ho_ref[...] = layer_rows(pl.program_id(0), h_ref, g_ref, wi_ref, wg_ref, bg_ref, sc_ref, wo_ref, p_ref,
                                     ue_scr, y_scr)

        (ho, p), xouts = _launch(
            body, name, nt, in_specs, [pl.BlockSpec((tm, d), row), pl.BlockSpec((tm, 2 * e), row)],
            [jax.ShapeDtypeStruct((t, d), F32), jax.ShapeDtypeStruct((t, 2 * e), F32)], scratch, args, ex)
        return ho, p, xouts

    target, final_g = final

    def body(h_ref, g_ref, wi_ref, wg_ref, bg_ref, sc_ref, wo_ref, tg_ref, fg_ref, dh_ref, p_ref, dfg_ref, loss_ref,
             ue_scr, y_scr, acc_scr, lacc_scr):
        i = pl.program_id(0)

        @pl.when(i == 0)
        def _():
            acc_scr[...] = jnp.zeros_like(acc_scr)
            lacc_scr[...] = jnp.zeros_like(lacc_scr)

        ho = layer_rows(i, h_ref, g_ref, wi_ref, wg_ref, bg_ref, sc_ref, wo_ref, p_ref, ue_scr, y_scr)
        _final_rows(ho, tg_ref, fg_ref, dh_ref, acc_scr, lacc_scr)

        @pl.when(i == nt - 1)
        def _():
            dfg_ref[...] = jnp.zeros_like(dfg_ref)
            dfg_ref[pl.ds(0, 1), :] = jnp.sum(acc_scr[...], axis=0, keepdims=True)
            loss_ref[...] = jnp.broadcast_to(jnp.sum(lacc_scr[...]) * (0.5 / d), loss_ref.shape)

    (dh, p, dfg, loss), xouts = _launch(
        body, name, nt, in_specs + [pl.BlockSpec((tm, d), row), _const((1, d), (0, 0))],
        [pl.BlockSpec((tm, d), row), pl.BlockSpec((tm, 2 * e), row), _const((SUBLANES, d), (0, 0)),
         _const((SUBLANES, LANES), (0, 0))],
        [jax.ShapeDtypeStruct((t, d), F32), jax.ShapeDtypeStruct((t, 2 * e), F32),
         jax.ShapeDtypeStruct((SUBLANES, d), F32), jax.ShapeDtypeStruct((SUBLANES, LANES), F32)],
        scratch + [pltpu.VMEM((SUBLANES, d), F32), pltpu.VMEM((SUBLANES, d), F32)],
        args + (target, final_g), ex)
    return dh, p, dfg, loss, xouts


def _pool_bwd(dho, p, w_grp, vecs, layer, w_out, name, ex=None):
    t, d = dho.shape
    e = w_out.shape[0]
    ng = len(POOL_WINDOWS)
    gc = e // ng
    tm = _tile(t, 512)
    nt = t // tm
    hb = tm // POOL_HALO

    def body(dho_ref, p_ref, ph_ref, wg_ref, bg_ref, sc_ref, wo_ref, dp_ref, dwo_ref, dwg_ref, dvec_ref,
             ue_scr, ee_scr, acc_scr, dwo_scr, dwg_scr):
        i = pl.program_id(0)
        tile = nt - 1 - i

        @pl.when(i == 0)
        def _():
            dwo_scr[...] = jnp.zeros_like(dwo_scr)
            dwg_scr[...] = jnp.zeros_like(dwg_scr)
            acc_scr[...] = jnp.zeros_like(acc_scr)
            ee_scr[pl.ds(tm, POOL_HALO), :] = jnp.zeros((POOL_HALO, e), F32)

        @pl.when(i > 0)
        def _():
            ee_scr[pl.ds(tm, POOL_HALO), :] = ee_scr[pl.ds(0, POOL_HALO), :]

        @pl.when(tile == 0)
        def _():
            ue_scr[pl.ds(0, POOL_HALO), :] = jnp.zeros((POOL_HALO, e), F32)

        @pl.when(tile > 0)
        def _():
            ue_scr[pl.ds(0, POOL_HALO), :] = ph_ref[:, pl.ds(0, e)]

        ue_scr[pl.ds(POOL_HALO, tm), :] = p_ref[:, pl.ds(0, e)]

        bg = bg_ref[...]
        sc = sc_ref[...]
        ds = [dg.astype(BF16) for dg in _pool_d(ue_scr, tile, tm, gc)]
        ob = jnp.concatenate([_dot(ds[g], wg_ref[g]) for g in range(ng)], axis=1) + bg
        z = p_ref[:, pl.ds(e, e)]
        sg_z = _sigmoid(z)
        s_z = z * sg_z
        y1 = ob * sc
        dy = dho_ref[...].astype(BF16)
        dy2 = _dot_nt(dy, wo_ref[...])
        dwo_scr[...] += _dot_tn((y1 * s_z).astype(BF16), dy)
        dy1 = dy2 * s_z
        dp_ref[:, pl.ds(e, e)] = (dy2 * y1 * _dsilu(z, sg_z)).astype(BF16)
        acc_scr[pl.ds(SUBLANES, SUBLANES), :] += _rows8(dy1 * ob)
        do = dy1 * sc
        acc_scr[pl.ds(0, SUBLANES), :] += _rows8(do)

        n = tm + POOL_HALO
        for g, w in enumerate(POOL_WINDOWS):
            cols = pl.ds(g * gc, gc)
            do_g = do[:, g * gc:(g + 1) * gc].astype(BF16)
            dwg_scr[g] += _dot_tn(ds[g], do_g)
            dd = _dot_nt(do_g, wg_ref[g])
            ee_scr[pl.ds(0, tm), cols] = dd * _inv_count(tile, tm, w)
            s = ee_scr[:, cols]
            sh = 1
            while sh < w:
                s = s + pltpu.roll(s, n - sh, axis=0)
                sh *= 2
            dp_ref[:, cols] = (s[:tm, :] - dd).astype(BF16)

        @pl.when(i == nt - 1)
        def _():
            dwo_ref[...] = dwo_scr[...].astype(BF16)
            dwg_ref[...] = dwg_scr[...].astype(BF16)
            dvec_ref[...] = jnp.zeros_like(dvec_ref)
            for k in range(2):
                dvec_ref[pl.ds(k, 1), :] = jnp.sum(acc_scr[pl.ds(k * SUBLANES, SUBLANES), :], axis=0, keepdims=True)

    rev = lambda i: (nt - 1 - i, 0)
    vec = _const((None, 1, e), (layer, 0, 0))
    (dp, dwo, dwg, dvec), xouts = _launch(
        body, name, nt,
        [pl.BlockSpec((tm, d), rev), pl.BlockSpec((tm, 2 * e), rev),
         pl.BlockSpec((POOL_HALO, 2 * e), lambda i: (jnp.maximum((nt - 1 - i) * hb - 1, 0), 0)),
         _const((ng, gc, gc), (0, 0, 0)), vec, vec, _const((e, d), (0, 0))],
        [pl.BlockSpec((tm, 2 * e), rev), _const((e, d), (0, 0)), _const((ng, gc, gc), (0, 0, 0)),
         _const((SUBLANES, e), (0, 0))],
        [jax.ShapeDtypeStruct((t, 2 * e), BF16), jax.ShapeDtypeStruct((e, d), BF16),
         jax.ShapeDtypeStruct((ng, gc, gc), BF16), jax.ShapeDtypeStruct((SUBLANES, e), F32)],
        [pltpu.VMEM((tm + POOL_HALO, e), F32), pltpu.VMEM((tm + POOL_HALO, e), F32),
         pltpu.VMEM((2 * SUBLANES, e), F32), pltpu.VMEM((e, d), F32), pltpu.VMEM((ng, gc, gc), F32)],
        (dho, p, p, w_grp, *vecs, w_out), ex)
    return dp, dwo, dwg, dvec, xouts


def _in_bwd(dp, h, dho, norm_g, layer, w_in, name, ex=None):
    t, d = h.shape
    n = w_in.shape[-1]
    tm = _tile(t, 512)
    nt = t // tm

    def body(dp_ref, h_ref, dho_ref, g_ref, w_ref, dh_ref, dw_ref, dg_ref, acc_scr, dw_scr):
        i = pl.program_id(0)

        @pl.when(i == 0)
        def _():
            dw_scr[...] = jnp.zeros_like(dw_scr)
            acc_scr[...] = jnp.zeros_like(acc_scr)

        x = h_ref[...]
        r = lax.rsqrt(jnp.mean(x * x, axis=-1, keepdims=True) + RMS_EPS)
        nrm = x * r
        dp = dp_ref[...]
        dhn = _dot_nt(dp, w_ref[...])
        dw_scr[...] += _dot_tn((nrm * g_ref[...]).astype(BF16), dp)
        acc_scr[...] += _rows8(dhn * nrm)
        dq = dhn * g_ref[...]
        dh_ref[...] = dho_ref[...] + r * (dq - nrm * jnp.mean(dq * nrm, axis=-1, keepdims=True))

        @pl.when(i == nt - 1)
        def _():
            dw_ref[...] = dw_scr[...].astype(BF16)
            dg_ref[...] = jnp.zeros_like(dg_ref)
            dg_ref[pl.ds(0, 1), :] = jnp.sum(acc_scr[...], axis=0, keepdims=True)

    (dh, dw, dg), xouts = _launch(
        body, name, nt,
        [pl.BlockSpec((tm, n), lambda i: (i, 0)), pl.BlockSpec((tm, d), lambda i: (i, 0)),
         pl.BlockSpec((tm, d), lambda i: (i, 0)), _const((None, 1, d), (layer, 0, 0)), _const((d, n), (0, 0))],
        [pl.BlockSpec((tm, d), lambda i: (i, 0)), _const((d, n), (0, 0)), _const((SUBLANES, d), (0, 0))],
        [jax.ShapeDtypeStruct((t, d), F32), jax.ShapeDtypeStruct((d, n), BF16),
         jax.ShapeDtypeStruct((SUBLANES, d), F32)],
        [pltpu.VMEM((SUBLANES, d), F32), pltpu.VMEM((d, n), F32)],
        (dp, h, dho, norm_g, w_in), ex)
    return dh, dw, dg, xouts


def _adam_update(g, w, m, v):
    c1 = 1.0 / (1.0 - ADAM_B1 ** ADAM_STEP)
    c2 = 1.0 / (1.0 - ADAM_B2 ** ADAM_STEP)
    nm = ADAM_B1 * m + (1.0 - ADAM_B1) * g
    nv = ADAM_B2 * v + (1.0 - ADAM_B2) * (g * g)
    return -ADAM_LR * ((nm * c1) / (jnp.sqrt(nv * c2) + ADAM_EPS) + ADAM_WD * w), nm, nv


def _adamw_small(params, stacks, loss_stack, name):
    ns, npar = len(stacks), len(params)

    def body(*refs):
        st = refs[:ns]
        pr = refs[ns:ns + 3 * npar]
        ls_ref = refs[ns + 3 * npar]
        outs = refs[ns + 3 * npar + 1:ns + 7 * npar + 1]
        loss_ref = refs[ns + 7 * npar + 1]
        for q, (w, _, _, pieces) in enumerate(params):
            w_ref, m_ref, v_ref = pr[3 * q:3 * q + 3]
            g_ref, d_ref, nm_ref, nv_ref = outs[4 * q:4 * q + 4]
            for s, row, slab in pieces:
                if w.ndim == 3:
                    take = lambda k: st[s][k]
                    at = slab
                else:
                    take = lambda k: st[s][k, pl.ds(row, 1), :]
                    at = (pl.ds(slab, 1), slice(None))
                g = take(0)
                for k in range(1, N_DEV):
                    g = g + take(k)
                g_ref[at] = g
                d_ref[at], nm_ref[at], nv_ref[at] = _adam_update(g, w_ref[at], m_ref[at], v_ref[at])
        tot = ls_ref[0]
        for k in range(1, N_DEV):
            tot = tot + ls_ref[k]
        loss_ref[...] = tot

    flat = [a for (w, m, v, _) in params for a in (w, m, v)]
    out_shape = [jax.ShapeDtypeStruct(w.shape, F32) for (w, _, _, _) in params for _ in range(4)]
    whole = pl.BlockSpec(memory_space=pltpu.VMEM)
    outs = pl.pallas_call(
        body, name=name, in_specs=[whole] * (ns + 3 * npar + 1), out_specs=[whole] * (4 * npar + 1),
        out_shape=out_shape + [jax.ShapeDtypeStruct(loss_stack.shape[1:], F32)],
    )(*stacks, *flat, loss_stack)
    return [outs[4 * q:4 * q + 4] for q in range(npar)], outs[-1]


def _adamw(stacks, w, m, v, name):
    nl = len(stacks)
    shp = w.shape
    c = shp[-1]
    r = 1
    for s in shp[1:-1]:
        r *= s
    tr = r
    for cand in (512, 256, 128, 64, 32, 16):
        if r % cand == 0 and r > cand:
            tr = cand
            break
    nrb = r // tr

    def body(*refs):
        s_refs = refs[:nl]
        w_ref, m_ref, v_ref, g_ref, d_ref, nm_ref, nv_ref = refs[nl:]
        layer = pl.program_id(0)
        for l in range(nl):
            @pl.when(layer == l)
            def _(l=l):
                g = s_refs[l][0].astype(F32)
                for k in range(1, N_DEV):
                    g = g + s_refs[l][k].astype(F32)
                g_ref[...] = g
                d_ref[...], nm_ref[...], nv_ref[...] = _adam_update(g, w_ref[...], m_ref[...], v_ref[...])

    def stack_spec(l):
        return pl.BlockSpec((N_DEV, tr, c),
                            lambda j, i: (0, jnp.where(j == l, i, jnp.where(j < l, 0, nrb - 1)), 0))

    spec = pl.BlockSpec((None, tr, c), lambda j, i: (j, i, 0))
    outs = pl.pallas_call(
        body, name=name, grid=(nl, nrb),
        in_specs=[stack_spec(l) for l in range(nl)] + [spec, spec, spec],
        out_specs=[spec] * 4,
        out_shape=[jax.ShapeDtypeStruct((nl, r, c), F32)] * 4,
        compiler_params=_params(2),
    )(*[s.reshape(N_DEV, r, c) for s in stacks], w.reshape(nl, r, c), m.reshape(nl, r, c), v.reshape(nl, r, c))
    return [o.reshape(shp) for o in outs]


def kernel(x, norm_g, final_g, conv_w_in, conv_dw, conv_dw_b, conv_ln_g, conv_ln_b, conv_w_out, pool_w_in, pool_w_grp, pool_b_grp, pool_scale, pool_w_out, loss_target, m_norm_g, m_final_g, m_conv_w_in, m_conv_dw, m_conv_dw_b, m_conv_ln_g, m_conv_ln_b, m_conv_w_out, m_pool_w_in, m_pool_w_grp, m_pool_b_grp, m_pool_scale, m_pool_w_out, v_norm_g, v_final_g, v_conv_w_in, v_conv_dw, v_conv_dw_b, v_conv_ln_g, v_conv_ln_b, v_conv_w_out, v_pool_w_in, v_pool_w_grp, v_pool_b_grp, v_pool_scale, v_pool_w_out):
    h0 = x[0]
    target = loss_target[0]
    ng3 = norm_g[:, None, :]
    row3 = lambda a: a[:, None, :]
    conv_vecs = (row3(conv_dw_b), row3(conv_ln_g), row3(conv_ln_b))
    gather = lambda arrays, axes: _Exchange("gather", arrays, axes)
    scatter = lambda arrays, axes: _Exchange("scatter", arrays, axes)

    cwi, cwo, pwi = conv_w_in.astype(BF16), conv_w_out.astype(BF16), pool_w_in.astype(BF16)
    pwg, pwo = pool_w_grp.astype(BF16), pool_w_out.astype(BF16)

    (cw_in0,) = _exchange_call(gather([(cwi, 0)], [1]), "gather_first")
    p0, (cw_out0, dw_full, bg_full, sc_full, pw_in0) = _in_fwd(
        h0, ng3, 0, cw_in0, "conv_in_fwd_0",
        gather([(cwo, 0), conv_dw, pool_b_grp, pool_scale, (pwi, 0)], [0, 2, 1, 1, 1]))
    pool_vecs = (row3(bg_full), row3(sc_full))
    h1, uc0, (pw_grp0, pw_out0, cw_in1, cw_out1, pw_in1, pw_grp1, pw_out1) = _conv_fwd(
        p0, h0, dw_full, conv_vecs, 0, cw_out0, "conv_mix_fwd_0",
        gather([(pwg, 0), (pwo, 0), (cwi, 1), (cwo, 1), (pwi, 1), (pwg, 1), (pwo, 1)], [1, 0, 1, 0, 1, 1, 0]))
    h2, p1, _ = _pool_fwd(h1, ng3, 1, pw_in0, pw_grp0, pool_vecs, 0, pw_out0, "pool_fwd_0")
    p2, _ = _in_fwd(h2, ng3, 2, cw_in1, "conv_in_fwd_1")
    h3, uc2, _ = _conv_fwd(p2, h2, dw_full, conv_vecs, 1, cw_out1, "conv_mix_fwd_1")
    dh, p3, d_final_g, loss_part, _ = _pool_fwd(h3, ng3, 3, pw_in1, pw_grp1, pool_vecs, 1, pw_out1, "pool_fwd_1",
                                               final=(target, final_g[None, :]))

    dp, g_pwo1, g_pwg1, dpv1, _ = _pool_bwd(dh, p3, pw_grp1, pool_vecs, 1, pw_out1, "pool_mix_bwd_1")
    dh, g_pwi1, dg3, (s_pwo1, s_pwg1) = _in_bwd(dp, h3, dh, ng3, 3, pw_in1, "pool_in_bwd_1",
                                                scatter([g_pwo1, g_pwg1], [0, 1]))
    dp, g_cwo1, ddw1, dcv1, (s_pwi1,) = _conv_bwd(dh, p2, uc2, dw_full, conv_vecs, 1, cw_out1, "conv_mix_bwd_1",
                                                  scatter([g_pwi1], [1]))
    dh, g_cwi1, dg2, (s_cwo1,) = _in_bwd(dp, h2, dh, ng3, 2, cw_in1, "conv_in_bwd_1", scatter([g_cwo1], [0]))
    dp, g_pwo0, g_pwg0, dpv0, (s_cwi1,) = _pool_bwd(dh, p1, pw_grp0, pool_vecs, 0, pw_out0, "pool_mix_bwd_0",
                                                    scatter([g_cwi1], [1]))
    dh, g_pwi0, dg1, (s_pwo0, s_pwg0) = _in_bwd(dp, h1, dh, ng3, 1, pw_in0, "pool_in_bwd_0",
                                                scatter([g_pwo0, g_pwg0], [0, 1]))
    dp, g_cwo0, ddw0, dcv0, (s_pwi0, s_ddw1, s_dcv1, s_dpv0, s_dpv1, s_dg1, s_dg2, s_dg3, s_dfg, s_loss) = _conv_bwd(
        dh, p0, uc0, dw_full, conv_vecs, 0, cw_out0, "conv_mix_bwd_0",
        scatter([g_pwi0, ddw1, dcv1, dpv0, dpv1, dg1, dg2, dg3, d_final_g, loss_part],
                [1, 1, None, 1, 1, None, None, None, None, None]))
    dh, g_cwi0, dg0, _ = _in_bwd(dp, h0, dh, ng3, 0, cw_in0, "conv_in_bwd_0")
    grad_x = dh[None]
    s_cwo0, s_cwi0, s_ddw0, s_dcv0, s_dg0 = _exchange_call(
        scatter([g_cwo0, g_cwi0, ddw0, dcv0, dg0], [0, 1, 1, None, None]), "scatter_last")

    res = {}
    res["conv_w_in"] = _adamw([s_cwi0, s_cwi1], conv_w_in, m_conv_w_in, v_conv_w_in, "adamw_conv_w_in")
    res["conv_w_out"] = _adamw([s_cwo0, s_cwo1], conv_w_out, m_conv_w_out, v_conv_w_out, "adamw_conv_w_out")
    res["pool_w_in"] = _adamw([s_pwi0, s_pwi1], pool_w_in, m_pool_w_in, v_pool_w_in, "adamw_pool_w_in")
    res["pool_w_grp"] = _adamw([s_pwg0, s_pwg1], pool_w_grp, m_pool_w_grp, v_pool_w_grp, "adamw_pool_w_grp")
    res["pool_w_out"] = _adamw([s_pwo0, s_pwo1], pool_w_out, m_pool_w_out, v_pool_w_out, "adamw_pool_w_out")
    stacks = [s_ddw0, s_ddw1, s_dpv0, s_dpv1, s_dg0, s_dg1, s_dg2, s_dg3, s_dfg, s_dcv0, s_dcv1]
    small = [
        ("conv_dw", conv_dw, m_conv_dw, v_conv_dw, [(0, None, 0), (1, None, 1)]),
        ("pool_b_grp", pool_b_grp, m_pool_b_grp, v_pool_b_grp, [(2, 0, 0), (3, 0, 1)]),
        ("pool_scale", pool_scale, m_pool_scale, v_pool_scale, [(2, 1, 0), (3, 1, 1)]),
        ("norm_g", norm_g, m_norm_g, v_norm_g, [(4, 0, 0), (5, 0, 1), (6, 0, 2), (7, 0, 3)]),
        ("final_g", final_g[None, :], m_final_g[None, :], v_final_g[None, :], [(8, 0, 0)]),
        ("conv_dw_b", conv_dw_b, m_conv_dw_b, v_conv_dw_b, [(9, 0, 0), (10, 0, 1)]),
        ("conv_ln_g", conv_ln_g, m_conv_ln_g, v_conv_ln_g, [(9, 1, 0), (10, 1, 1)]),
        ("conv_ln_b", conv_ln_b, m_conv_ln_b, v_conv_ln_b, [(9, 2, 0), (10, 2, 1)]),
    ]
    small_res, loss_block = _adamw_small([s[1:] for s in small], stacks, s_loss, "adamw_small")
    for (name, *_), r in zip(small, small_res):
        res[name] = [a[0] for a in r] if name == "final_g" else r
    loss = loss_block[0, 0]

    names = ["norm_g", "final_g", "conv_w_in", "conv_dw", "conv_dw_b", "conv_ln_g", "conv_ln_b", "conv_w_out",
             "pool_w_in", "pool_w_grp", "pool_b_grp", "pool_scale", "pool_w_out"]
    return (loss, grad_x) + tuple(res[n][q] for q in range(4) for n in names)
```

```python
import jax
import jax.numpy as jnp
from jax import lax
from jax.experimental import pallas as pl
from jax.experimental.pallas import tpu as pltpu

F32 = jnp.float32
BF16 = jnp.bfloat16

RMS_EPS = 1e-6
LN_EPS = 1e-5
CONV_TAPS = 31
CONV_HALO = 32
POOL_WINDOWS = (2, 4, 8, 16)
POOL_HALO = 16
SUBLANES = 8
LANES = 128
N_DEV = 8
V7X_VMEM_LIMIT = 56 * 1024 * 1024

ADAM_LR = 0.001
ADAM_B1 = 0.9
ADAM_B2 = 0.999
ADAM_EPS = 1e-08
ADAM_WD = 0.01
ADAM_STEP = 10

MESH = pl.DeviceIdType.MESH
ANY = pl.BlockSpec(memory_space=pl.ANY)


def _dot(a, b):
    return lax.dot_general(a, b, (((1,), (0,)), ((), ())), preferred_element_type=F32)


def _dot_nt(a, b):
    return lax.dot_general(a, b, (((1,), (1,)), ((), ())), preferred_element_type=F32)


def _dot_tn(a, b):
    return lax.dot_general(a, b, (((0,), (0,)), ((), ())), preferred_element_type=F32)


def _sigmoid(x):
    return jax.nn.sigmoid(x)


def _dsilu(x, s):
    return s * (1.0 + x * (1.0 - s))


def _rows8(x):
    r, c = x.shape
    return jnp.sum(x.reshape(r // SUBLANES, SUBLANES, c), axis=0)


def _tile(t, pref):
    return pref if t >= 2 * pref else t // 2


def _const(shape, index):
    return pl.BlockSpec(shape, lambda *_: index, pipeline_mode=pl.Buffered(1))


def _params(grid_rank=1):
    return pltpu.CompilerParams(dimension_semantics=("arbitrary",) * grid_rank, vmem_limit_bytes=V7X_VMEM_LIMIT)


def _chunks(n, rc, fn):
    def step(c, carry):
        fn(pl.multiple_of(c * rc, rc))
        return carry
    lax.fori_loop(0, n, step, 0)


def _mesh_position():
    return lax.axis_index("x"), lax.axis_index("y"), lax.axis_index("c")


def _peer(j):
    x, y, c = _mesh_position()
    px = 1 - x if j & 4 else x
    py = 1 - y if j & 2 else y
    pc = 1 - c if j & 1 else c
    return (px, py, pc), 4 * px + 2 * py + pc


def _block(ref, axis, index, size):
    idx = [slice(None)] * len(ref.shape)
    idx[axis] = pl.ds(pl.multiple_of(index * size, size), size)
    return ref.at[tuple(idx)]


class _Exchange:
    def __init__(self, kind, arrays, axes):
        self.kind, self.axes = kind, list(axes)
        self.arrays = [a[0] if isinstance(a, tuple) else a for a in arrays]
        self.layers = [a[1] if isinstance(a, tuple) else None for a in arrays]
        self.n = len(self.arrays)
        self.blk, self.out_shapes = [], []
        for a, layer, ax in zip(self.arrays, self.layers, self.axes):
            s = list(a.shape if layer is None else a.shape[1:])
            if kind == "gather":
                self.blk.append(s[ax])
                s[ax] *= N_DEV
                self.out_shapes.append(jax.ShapeDtypeStruct(tuple(s), a.dtype))
            else:
                if ax is not None:
                    s[ax] //= N_DEV
                    self.blk.append(s[ax])
                else:
                    self.blk.append(None)
                self.out_shapes.append(jax.ShapeDtypeStruct((N_DEV,) + tuple(s), a.dtype))

    def sem_shapes(self):
        return [pltpu.SemaphoreType.DMA((N_DEV - 1, self.n)), pltpu.SemaphoreType.DMA((N_DEV - 1, self.n)),
                pltpu.SemaphoreType.DMA((self.n,))]

    def _src(self, ins, k, owner):
        ref = ins[k] if self.layers[k] is None else ins[k].at[self.layers[k]]
        if self.kind == "gather" or self.axes[k] is None:
            return ref
        return _block(ref, self.axes[k], owner, self.blk[k])

    def _dst(self, outs, k, sender):
        if self.kind == "gather":
            return _block(outs[k], self.axes[k], sender, self.blk[k])
        return outs[k].at[sender]

    def _copies(self, ins, outs, sems, arriving):
        send, recv, loc = sems
        x, y, c = _mesh_position()
        me = 4 * x + 2 * y + c
        if not arriving:
            local = [pltpu.make_async_copy(self._src(ins, k, me), self._dst(outs, k, me), loc.at[k])
                     for k in range(self.n)]
        else:
            local = []
        remote = []
        for j in range(1, N_DEV):
            peer, peer_id = _peer(j)
            for k in range(self.n):
                owner, sender = (me, peer_id) if arriving else (peer_id, me)
                remote.append(pltpu.make_async_remote_copy(
                    src_ref=self._src(ins, k, owner), dst_ref=self._dst(outs, k, sender),
                    send_sem=send.at[j - 1, k], recv_sem=recv.at[j - 1, k], device_id=peer, device_id_type=MESH))
        return local, remote

    def start(self, ins, outs, sems):
        local, sends = self._copies(ins, outs, sems, arriving=False)
        for cp in local + sends:
            cp.start()

    def finish(self, ins, outs, sems):
        for cp in self._copies(ins, outs, sems, arriving=True)[1]:
            cp.wait_recv()
        local, sends = self._copies(ins, outs, sems, arriving=False)
        for cp in sends:
            cp.wait_send()
        for cp in local:
            cp.wait()


def _exchange_call(ex, name):
    def body(*refs):
        ins, outs, sems = refs[:ex.n], refs[ex.n:2 * ex.n], refs[2 * ex.n:]
        ex.start(ins, outs, sems)
        ex.finish(ins, outs, sems)

    return pl.pallas_call(body, name=name, in_specs=[ANY] * ex.n, out_specs=[ANY] * ex.n,
                          out_shape=ex.out_shapes, scratch_shapes=ex.sem_shapes())(*ex.arrays)


def _gather_via_sibling_call(ex, name):
    n = ex.n

    def body(*refs):
        ins, outs = refs[:n], refs[n:2 * n]
        send, recv, loc = refs[2 * n:]
        x, y, c = _mesh_position()
        ident = lambda px, py, pc: 4 * px + 2 * py + pc
        me, sibling = ident(x, y, c), (x, y, 1 - c)
        chips = [(1 - x, y), (x, 1 - y), (1 - x, 1 - y)]

        def copy(row, k, block, to, src=None):
            place = ex._dst(outs, k, block)
            return pltpu.make_async_remote_copy(
                src_ref=place if src is None else src, dst_ref=place,
                send_sem=send.at[row, k], recv_sem=recv.at[row, k], device_id=to, device_id_type=MESH)

        local = [pltpu.make_async_copy(ex._src(ins, k, me), ex._dst(outs, k, me), loc.at[k]) for k in range(n)]
        first = []
        for k in range(n):
            mine = ex._src(ins, k, me)
            first.append(copy(0, k, me, sibling, src=mine))
            first += [copy(1 + j, k, me, (*chip, c), src=mine) for j, chip in enumerate(chips)]
        for cp in local + first:
            cp.start()
        passed = []
        for j, chip in enumerate(chips):
            for k in range(n):
                copy(1 + j, k, ident(*chip, c), sibling).wait_recv()
                passed.append(copy(4 + j, k, ident(*chip, c), sibling))
                passed[-1].start()
        for k in range(n):
            copy(0, k, ident(x, y, 1 - c), sibling).wait_recv()
            for j, chip in enumerate(chips):
                copy(4 + j, k, ident(*chip, 1 - c), sibling).wait_recv()
        for cp in first + passed:
            cp.wait_send()
        for cp in local:
            cp.wait()

    return pl.pallas_call(body, name=name, in_specs=[ANY] * n, out_specs=[ANY] * n, out_shape=ex.out_shapes,
                          scratch_shapes=ex.sem_shapes())(*ex.arrays)


def _launch(body, name, nt, in_specs, out_specs, out_shape, scratch_shapes, args, ex=None):
    if ex is None:
        outs = pl.pallas_call(body, name=name, grid=(nt,), in_specs=in_specs, out_specs=out_specs,
                              out_shape=out_shape, scratch_shapes=scratch_shapes, compiler_params=_params())(*args)
        return list(outs), []
    n_in, n_out, n_scr = len(in_specs), len(out_specs), len(scratch_shapes)

    def riding(*refs):
        a, xa = refs[:n_in], refs[n_in:n_in + ex.n]
        o = refs[n_in + ex.n:n_in + ex.n + n_out]
        xo = refs[n_in + ex.n + n_out:n_in + 2 * ex.n + n_out]
        s = refs[n_in + 2 * ex.n + n_out:n_in + 2 * ex.n + n_out + n_scr]
        sems = refs[n_in + 2 * ex.n + n_out + n_scr:]
        i = pl.program_id(0)

        @pl.when(i == 0)
        def _():
            ex.start(xa, xo, sems)

        body(*a, *o, *s)

        @pl.when(i == nt - 1)
        def _():
            ex.finish(xa, xo, sems)

    outs = pl.pallas_call(
        riding, name=name, grid=(nt,),
        in_specs=list(in_specs) + [ANY] * ex.n, out_specs=list(out_specs) + [ANY] * ex.n,
        out_shape=list(out_shape) + ex.out_shapes, scratch_shapes=list(scratch_shapes) + ex.sem_shapes(),
        compiler_params=_params())(*args, *ex.arrays)
    return list(outs[:n_out]), list(outs[n_out:])


def _in_fwd(h, norm_g, layer, w_in, name, ex=None):
    t, d = h.shape
    n = w_in.shape[-1]
    tm = _tile(t, 512)

    def body(h_ref, g_ref, w_ref, p_ref):
        x = h_ref[...]
        r = lax.rsqrt(jnp.mean(x * x, axis=-1, keepdims=True) + RMS_EPS)
        p_ref[...] = _dot((x * r * g_ref[...]).astype(BF16), w_ref[...])

    (p,), xouts = _launch(
        body, name, t // tm,
        [pl.BlockSpec((tm, d), lambda i: (i, 0)), _const((None, 1, d), (layer, 0, 0)), _const((d, n), (0, 0))],
        [pl.BlockSpec((tm, n), lambda i: (i, 0))],
        [jax.ShapeDtypeStruct((t, n), F32)], [], (h, norm_g, w_in), ex)
    return p, xouts


def _layernorm_rows(uc, lng, lnb):
    mu = jnp.mean(uc, axis=-1, keepdims=True)
    xc = uc - mu
    rstd = lax.rsqrt(jnp.mean(xc * xc, axis=-1, keepdims=True) + LN_EPS)
    xhat = xc * rstd
    return xhat, rstd, xhat * lng + lnb


def _conv_fwd(p, h, dw, vecs, layer, w_out, name, ex=None):
    t, d = h.shape
    e = w_out.shape[0]
    tm = _tile(t, 256)
    rc = _tile(tm, 128)

    def body(p_ref, h_ref, dw_ref, dwb_ref, lng_ref, lnb_ref, wo_ref, ho_ref, uc_ref, us_scr):
        i = pl.program_id(0)

        @pl.when(i == 0)
        def _():
            us_scr[:, pl.ds(0, CONV_HALO), :] = jnp.zeros((SUBLANES, CONV_HALO, e), F32)

        @pl.when(i > 0)
        def _():
            us_scr[:, pl.ds(0, CONV_HALO), :] = us_scr[:, pl.ds(tm, CONV_HALO), :]

        us_scr[0, pl.ds(CONV_HALO, tm), :] = p_ref[:, pl.ds(0, e)] * _sigmoid(p_ref[:, pl.ds(e, e)])
        for r in range(1, SUBLANES):
            us_scr[r, pl.ds(CONV_HALO, tm), :] = us_scr[0, pl.ds(CONV_HALO - r, tm), :]

        def c_conv(base):
            for lt in range(e // LANES):
                cols = pl.ds(lt * LANES, LANES)
                acc = jnp.broadcast_to(dwb_ref[:, cols], (rc, LANES))
                for r in range(SUBLANES):
                    nq = (CONV_TAPS - 1 - r) // SUBLANES + 1
                    lo = SUBLANES * (nq - 1)
                    win = us_scr[r, pl.ds(pl.multiple_of(CONV_HALO + base - lo, SUBLANES), rc + lo), cols]
                    for q in range(nq):
                        k = CONV_TAPS - 1 - (SUBLANES * q + r)
                        at = lo - SUBLANES * q
                        acc = acc + dw_ref[pl.ds(k, 1), cols] * win[at:at + rc, :]
                uc_ref[pl.ds(base, rc), cols] = acc
        _chunks(tm // rc, rc, c_conv)

        _, _, ul = _layernorm_rows(uc_ref[...], lng_ref[...], lnb_ref[...])
        z = p_ref[:, pl.ds(2 * e, e)]
        v = ((ul * _sigmoid(ul)) * (z * _sigmoid(z))).astype(BF16)
        ho_ref[...] = h_ref[...] + _dot(v, wo_ref[...])

    vec = _const((None, 1, e), (layer, 0, 0))
    (ho, uc), xouts = _launch(
        body, name, t // tm,
        [pl.BlockSpec((tm, 3 * e), lambda i: (i, 0)), pl.BlockSpec((tm, d), lambda i: (i, 0)),
         _const((None, CONV_TAPS, e), (layer, 0, 0)), vec, vec, vec, _const((e, d), (0, 0))],
        [pl.BlockSpec((tm, d), lambda i: (i, 0)), pl.BlockSpec((tm, e), lambda i: (i, 0))],
        [jax.ShapeDtypeStruct((t, d), F32), jax.ShapeDtypeStruct((t, e), F32)],
        [pltpu.VMEM((SUBLANES, tm + CONV_HALO, e), F32)], (p, h, dw, *vecs, w_out), ex)
    return ho, uc, xouts


def _conv_bwd(dho, p, uc, dw, vecs, layer, w_out, name, ex=None):
    t, d = dho.shape
    e = w_out.shape[0]
    tm = _tile(t, 256)
    nt = t // tm
    rc = 16
    vec0 = CONV_TAPS + 1

    def body(dho_ref, p_ref, uc_ref, dw_ref, lng_ref, lnb_ref, wo_ref, dp_ref, dwo_ref, ddw_ref, dvec_ref,
             ds_scr, acc_scr, dwo_scr):
        i = pl.program_id(0)

        @pl.when(i == 0)
        def _():
            dwo_scr[...] = jnp.zeros_like(dwo_scr)
            acc_scr[...] = jnp.zeros_like(acc_scr)
            ds_scr[:, pl.ds(tm, CONV_HALO), :] = jnp.zeros((SUBLANES, CONV_HALO, e), F32)

        @pl.when(i > 0)
        def _():
            ds_scr[:, pl.ds(tm, CONV_HALO), :] = ds_scr[:, pl.ds(0, CONV_HALO), :]

        lng = lng_ref[...]
        lnb = lnb_ref[...]

        xhat, rstd, ul = _layernorm_rows(uc_ref[...], lng, lnb)
        z = p_ref[:, pl.ds(2 * e, e)]
        sg_u = _sigmoid(ul)
        sg_z = _sigmoid(z)
        s_u = ul * sg_u
        s_z = z * sg_z
        v = (s_u * s_z).astype(BF16)
        dy = dho_ref[...].astype(BF16)
        dv = _dot_nt(dy, wo_ref[...])
        dwo_scr[...] += _dot_tn(v, dy)

        dul = dv * s_z * _dsilu(ul, sg_u)
        dp_ref[:, pl.ds(2 * e, e)] = (dv * s_u * _dsilu(z, sg_z)).astype(BF16)
        acc_scr[pl.ds((vec0 + 1) * SUBLANES, SUBLANES), :] += _rows8(dul * xhat)
        acc_scr[pl.ds((vec0 + 2) * SUBLANES, SUBLANES), :] += _rows8(dul)
        dxh = dul * lng
        duc = rstd * (dxh - jnp.mean(dxh, axis=-1, keepdims=True)
                      - xhat * jnp.mean(dxh * xhat, axis=-1, keepdims=True))
        acc_scr[pl.ds(vec0 * SUBLANES, SUBLANES), :] += _rows8(duc)
        ds_scr[0, pl.ds(0, tm), :] = duc
        for r in range(1, SUBLANES):
            ds_scr[r, pl.ds(0, tm), :] = ds_scr[0, pl.ds(r, tm), :]

        def c_conv(base):
            rows = pl.ds(base, rc)
            a = p_ref[rows, pl.ds(0, e)]
            b = p_ref[rows, pl.ds(e, e)]
            sb = _sigmoid(b)
            u = a * sb
            du = jnp.zeros((rc, e), F32)
            for o in range(CONV_TAPS):
                q, r = divmod(o, SUBLANES)
                k = CONV_TAPS - 1 - o
                sh = ds_scr[r, pl.ds(pl.multiple_of(base + SUBLANES * q, SUBLANES), rc), :]
                du = du + dw_ref[pl.ds(k, 1), :] * sh
                acc_scr[pl.ds(k * SUBLANES, SUBLANES), :] += _rows8(u * sh)
            dp_ref[rows, pl.ds(0, e)] = (du * sb).astype(BF16)
            dp_ref[rows, pl.ds(e, e)] = (du * u * (1.0 - sb)).astype(BF16)
        _chunks(tm // rc, rc, c_conv)

        @pl.when(i == nt - 1)
        def _():
            dwo_ref[...] = dwo_scr[...].astype(BF16)
            slot_sum = lambda k: jnp.sum(acc_scr[pl.ds(k * SUBLANES, SUBLANES), :], axis=0, keepdims=True)
            for k in range(CONV_TAPS):
                ddw_ref[pl.ds(k, 1), :] = slot_sum(k)
            dvec_ref[...] = jnp.zeros_like(dvec_ref)
            for k in range(3):
                dvec_ref[pl.ds(k, 1), :] = slot_sum(vec0 + k)

    rev = lambda i: (nt - 1 - i, 0)
    vec = _const((None, 1, e), (layer, 0, 0))
    (dp, dwo, ddw, dvec), xouts = _launch(
        body, name, nt,
        [pl.BlockSpec((tm, d), rev), pl.BlockSpec((tm, 3 * e), rev), pl.BlockSpec((tm, e), rev),
         _const((None, CONV_TAPS, e), (layer, 0, 0)), vec, vec, _const((e, d), (0, 0))],
        [pl.BlockSpec((tm, 3 * e), rev), _const((e, d), (0, 0)), _const((CONV_TAPS, e), (0, 0)),
         _const((SUBLANES, e), (0, 0))],
        [jax.ShapeDtypeStruct((t, 3 * e), BF16), jax.ShapeDtypeStruct((e, d), BF16),
         jax.ShapeDtypeStruct((CONV_TAPS, e), F32), jax.ShapeDtypeStruct((SUBLANES, e), F32)],
        [pltpu.VMEM((SUBLANES, tm + CONV_HALO, e), F32), pltpu.VMEM(((vec0 + 3) * SUBLANES, e), F32),
         pltpu.VMEM((e, d), F32)],
        (dho, p, uc, dw, vecs[1], vecs[2], w_out), ex)
    return dp, dwo, ddw, dvec, xouts


def _inv_count(tile, tm, w):
    tpos = tile * tm + lax.broadcasted_iota(jnp.int32, (tm, 1), 0)
    return 1.0 / jnp.minimum(tpos + 1, w).astype(F32)


def _pool_d_group(ue_scr, tile, tm, gc, g):
    w = POOL_WINDOWS[g]
    win = ue_scr[:, pl.ds(g * gc, gc)]
    s = win
    sh = 1
    while sh < w:
        s = s + pltpu.roll(s, sh, axis=0)
        sh *= 2
    return s[POOL_HALO:, :] * _inv_count(tile, tm, w) - win[POOL_HALO:, :]


def _pool_d(ue_scr, tile, tm, gc):
    return [_pool_d_group(ue_scr, tile, tm, gc, g) for g in range(len(POOL_WINDOWS))]


def _final_rows(ho, tg_ref, fg_ref, dh_ref, acc_scr, lacc_scr):
    d = ho.shape[-1]
    r = lax.rsqrt(jnp.mean(ho * ho, axis=-1, keepdims=True) + RMS_EPS)
    nrm = ho * r
    err = nrm * fg_ref[...] - tg_ref[...]
    lacc_scr[...] += _rows8(err * err)
    dy = err * (1.0 / d)
    acc_scr[...] += _rows8(dy * nrm)
    dq = dy * fg_ref[...]
    dh_ref[...] = r * (dq - nrm * jnp.mean(dq * nrm, axis=-1, keepdims=True))


def _pool_fwd(h, norm_g, nlayer, w_in, w_grp, vecs, layer, w_out, name, ex=None, final=None):
    t, d = h.shape
    e = w_out.shape[0]
    ng = len(POOL_WINDOWS)
    gc = e // ng
    tm = _tile(t, 512)
    nt = t // tm
    cw = 2 * e // (2 * ng)

    def layer_rows(i, hn_ref, hc_ref, g_ref, wi_ref, wg_ref, bg_ref, sc_ref, wo_ref, p_ref,
                   ue_scr, y_scr, pbuf, z_scr, hn_scr):
        tile = jnp.maximum(i - 1, 0)

        @pl.when(i == 0)
        def _():
            pbuf[...] = jnp.zeros_like(pbuf)

        @pl.when(i <= 1)
        def _():
            ue_scr[pl.ds(0, POOL_HALO), :] = jnp.zeros((POOL_HALO, e), F32)

        @pl.when(i > 1)
        def _():
            ue_scr[pl.ds(0, POOL_HALO), :] = ue_scr[pl.ds(tm, POOL_HALO), :]

        ue_scr[pl.ds(POOL_HALO, tm), :] = pbuf[:, pl.ds(0, e)]
        z_scr[...] = pbuf[:, pl.ds(e, e)]

        x = hn_ref[...]
        r = lax.rsqrt(jnp.mean(x * x, axis=-1, keepdims=True) + RMS_EPS)
        hn_scr[...] = (x * r * g_ref[...]).astype(BF16)

        def project(c):
            part = _dot(hn_scr[...], wi_ref[:, pl.ds(c * cw, cw)])
            pbuf[:, pl.ds(c * cw, cw)] = part
            p_ref[:, pl.ds(c * cw, cw)] = part

        for g in range(ng):
            project(2 * g)
            cols = pl.ds(g * gc, gc)
            dg = _pool_d_group(ue_scr, tile, tm, gc, g)
            z = z_scr[:, cols]
            y1 = (_dot(dg.astype(BF16), wg_ref[g]) + bg_ref[:, cols]) * sc_ref[:, cols]
            y_scr[:, cols] = (y1 * (z * _sigmoid(z))).astype(BF16)
            project(2 * g + 1)

        return hc_ref[...] + _dot(y_scr[...], wo_ref[...])

    nxt = lambda i: (jnp.minimum(i, nt - 1), 0)
    cur = lambda i: (jnp.maximum(i - 1, 0), 0)
    vec = _const((None, 1, e), (layer, 0, 0))
    in_specs = [pl.BlockSpec((tm, d), nxt), pl.BlockSpec((tm, d), cur), _const((None, 1, d), (nlayer, 0, 0)),
                _const((d, 2 * e), (0, 0)), _const((ng, gc, gc), (0, 0, 0)), vec, vec, _const((e, d), (0, 0))]
    scratch = [pltpu.VMEM((tm + POOL_HALO, e), F32), pltpu.VMEM((tm, e), BF16), pltpu.VMEM((tm, 2 * e), F32),
               pltpu.VMEM((tm, e), F32), pltpu.VMEM((tm, d), BF16)]
    args = (h, h, norm_g, w_in, w_grp, *vecs, w_out)

    if final is None:
        def body(hn_ref, hc_ref, g_ref, wi_ref, wg_ref, bg_ref, sc_ref, wo_ref, ho_ref, p_ref, *scr):
            ho_ref[...] = layer_rows(pl.program_id(0), hn_ref, hc_ref, g_ref, wi_ref, wg_ref, bg_ref, sc_ref, wo_ref,
                                     p_ref, *scr)

        (ho, p), xouts = _launch(
            body, name, nt + 1, in_specs, [pl.BlockSpec((tm, d), cur), pl.BlockSpec((tm, 2 * e), nxt)],
            [jax.ShapeDtypeStruct((t, d), F32), jax.ShapeDtypeStruct((t, 2 * e), F32)], scratch, args, ex)
        return ho, p, xouts

    target, final_g = final

    def body(hn_ref, hc_ref, g_ref, wi_ref, wg_ref, bg_ref, sc_ref, wo_ref, tg_ref, fg_ref,
             dh_ref, p_ref, dfg_ref, loss_ref, ue_scr, y_scr, pbuf, z_scr, hn_scr, acc_scr, lacc_scr):
        i = pl.program_id(0)

        @pl.when(i <= 1)
        def _():
            acc_scr[...] = jnp.zeros_like(acc_scr)
            lacc_scr[...] = jnp.zeros_like(lacc_scr)

        ho = layer_rows(i, hn_ref, hc_ref, g_ref, wi_ref, wg_ref, bg_ref, sc_ref, wo_ref, p_ref,
                        ue_scr, y_scr, pbuf, z_scr, hn_scr)
        _final_rows(ho, tg_ref, fg_ref, dh_ref, acc_scr, lacc_scr)

        @pl.when(i == nt)
        def _():
            dfg_ref[...] = jnp.zeros_like(dfg_ref)
            dfg_ref[pl.ds(0, 1), :] = jnp.sum(acc_scr[...], axis=0, keepdims=True)
            loss_ref[...] = jnp.broadcast_to(jnp.sum(lacc_scr[...]) * (0.5 / d), loss_ref.shape)

    (dh, p, dfg, loss), xouts = _launch(
        body, name, nt + 1, in_specs + [pl.BlockSpec((tm, d), cur), _const((1, d), (0, 0))],
        [pl.BlockSpec((tm, d), cur), pl.BlockSpec((tm, 2 * e), nxt), _const((SUBLANES, d), (0, 0)),
         _const((SUBLANES, LANES), (0, 0))],
        [jax.ShapeDtypeStruct((t, d), F32), jax.ShapeDtypeStruct((t, 2 * e), F32),
         jax.ShapeDtypeStruct((SUBLANES, d), F32), jax.ShapeDtypeStruct((SUBLANES, LANES), F32)],
        scratch + [pltpu.VMEM((SUBLANES, d), F32), pltpu.VMEM((SUBLANES, d), F32)],
        args + (target, final_g), ex)
    return dh, p, dfg, loss, xouts


def _pool_bwd(dho, p, w_grp, vecs, layer, w_out, name, ex=None):
    t, d = dho.shape
    e = w_out.shape[0]
    ng = len(POOL_WINDOWS)
    gc = e // ng
    tm = _tile(t, 512)
    nt = t // tm
    hb = tm // POOL_HALO

    def body(dho_ref, p_ref, ph_ref, wg_ref, bg_ref, sc_ref, wo_ref, dp_ref, dwo_ref, dwg_ref, dvec_ref,
             ue_scr, ee_scr, acc_scr, dwo_scr, dwg_scr):
        i = pl.program_id(0)
        tile = nt - 1 - i

        @pl.when(i == 0)
        def _():
            dwo_scr[...] = jnp.zeros_like(dwo_scr)
            dwg_scr[...] = jnp.zeros_like(dwg_scr)
            acc_scr[...] = jnp.zeros_like(acc_scr)
            ee_scr[pl.ds(tm, POOL_HALO), :] = jnp.zeros((POOL_HALO, e), F32)

        @pl.when(i > 0)
        def _():
            ee_scr[pl.ds(tm, POOL_HALO), :] = ee_scr[pl.ds(0, POOL_HALO), :]

        @pl.when(tile == 0)
        def _():
            ue_scr[pl.ds(0, POOL_HALO), :] = jnp.zeros((POOL_HALO, e), F32)

        @pl.when(tile > 0)
        def _():
            ue_scr[pl.ds(0, POOL_HALO), :] = ph_ref[:, pl.ds(0, e)]

        ue_scr[pl.ds(POOL_HALO, tm), :] = p_ref[:, pl.ds(0, e)]

        bg = bg_ref[...]
        sc = sc_ref[...]
        ds = [dg.astype(BF16) for dg in _pool_d(ue_scr, tile, tm, gc)]
        ob = jnp.concatenate([_dot(ds[g], wg_ref[g]) for g in range(ng)], axis=1) + bg
        z = p_ref[:, pl.ds(e, e)]
        sg_z = _sigmoid(z)
        s_z = z * sg_z
        y1 = ob * sc
        dy = dho_ref[...].astype(BF16)
        dy2 = _dot_nt(dy, wo_ref[...])
        dwo_scr[...] += _dot_tn((y1 * s_z).astype(BF16), dy)
        dy1 = dy2 * s_z
        dp_ref[:, pl.ds(e, e)] = (dy2 * y1 * _dsilu(z, sg_z)).astype(BF16)
        acc_scr[pl.ds(SUBLANES, SUBLANES), :] += _rows8(dy1 * ob)
        do = dy1 * sc
        acc_scr[pl.ds(0, SUBLANES), :] += _rows8(do)

        n = tm + POOL_HALO
        for g, w in enumerate(POOL_WINDOWS):
            cols = pl.ds(g * gc, gc)
            do_g = do[:, g * gc:(g + 1) * gc].astype(BF16)
            dwg_scr[g] += _dot_tn(ds[g], do_g)
            dd = _dot_nt(do_g, wg_ref[g])
            ee_scr[pl.ds(0, tm), cols] = dd * _inv_count(tile, tm, w)
            s = ee_scr[:, cols]
            sh = 1
            while sh < w:
                s = s + pltpu.roll(s, n - sh, axis=0)
                sh *= 2
            dp_ref[:, cols] = (s[:tm, :] - dd).astype(BF16)

        @pl.when(i == nt - 1)
        def _():
            dwo_ref[...] = dwo_scr[...].astype(BF16)
            dwg_ref[...] = dwg_scr[...].astype(BF16)
            dvec_ref[...] = jnp.zeros_like(dvec_ref)
            for k in range(2):
                dvec_ref[pl.ds(k, 1), :] = jnp.sum(acc_scr[pl.ds(k * SUBLANES, SUBLANES), :], axis=0, keepdims=True)

    rev = lambda i: (nt - 1 - i, 0)
    vec = _const((None, 1, e), (layer, 0, 0))
    (dp, dwo, dwg, dvec), xouts = _launch(
        body, name, nt,
        [pl.BlockSpec((tm, d), rev), pl.BlockSpec((tm, 2 * e), rev),
         pl.BlockSpec((POOL_HALO, 2 * e), lambda i: (jnp.maximum((nt - 1 - i) * hb - 1, 0), 0)),
         _const((ng, gc, gc), (0, 0, 0)), vec, vec, _const((e, d), (0, 0))],
        [pl.BlockSpec((tm, 2 * e), rev), _const((e, d), (0, 0)), _const((ng, gc, gc), (0, 0, 0)),
         _const((SUBLANES, e), (0, 0))],
        [jax.ShapeDtypeStruct((t, 2 * e), BF16), jax.ShapeDtypeStruct((e, d), BF16),
         jax.ShapeDtypeStruct((ng, gc, gc), BF16), jax.ShapeDtypeStruct((SUBLANES, e), F32)],
        [pltpu.VMEM((tm + POOL_HALO, e), F32), pltpu.VMEM((tm + POOL_HALO, e), F32),
         pltpu.VMEM((2 * SUBLANES, e), F32), pltpu.VMEM((e, d), F32), pltpu.VMEM((ng, gc, gc), F32)],
        (dho, p, p, w_grp, *vecs, w_out), ex)
    return dp, dwo, dwg, dvec, xouts


def _in_bwd(dp, h, dho, norm_g, layer, w_in, name, ex=None):
    t, d = h.shape
    n = w_in.shape[-1]
    tm = _tile(t, 512)
    nt = t // tm

    def body(dp_ref, h_ref, dho_ref, g_ref, w_ref, dh_ref, dw_ref, dg_ref, acc_scr, dw_scr):
        i = pl.program_id(0)

        @pl.when(i == 0)
        def _():
            dw_scr[...] = jnp.zeros_like(dw_scr)
            acc_scr[...] = jnp.zeros_like(acc_scr)

        x = h_ref[...]
        r = lax.rsqrt(jnp.mean(x * x, axis=-1, keepdims=True) + RMS_EPS)
        nrm = x * r
        dp = dp_ref[...]
        dhn = _dot_nt(dp, w_ref[...])
        dw_scr[...] += _dot_tn((nrm * g_ref[...]).astype(BF16), dp)
        acc_scr[...] += _rows8(dhn * nrm)
        dq = dhn * g_ref[...]
        dh_ref[...] = dho_ref[...] + r * (dq - nrm * jnp.mean(dq * nrm, axis=-1, keepdims=True))

        @pl.when(i == nt - 1)
        def _():
            dw_ref[...] = dw_scr[...].astype(BF16)
            dg_ref[...] = jnp.zeros_like(dg_ref)
            dg_ref[pl.ds(0, 1), :] = jnp.sum(acc_scr[...], axis=0, keepdims=True)

    (dh, dw, dg), xouts = _launch(
        body, name, nt,
        [pl.BlockSpec((tm, n), lambda i: (i, 0)), pl.BlockSpec((tm, d), lambda i: (i, 0)),
         pl.BlockSpec((tm, d), lambda i: (i, 0)), _const((None, 1, d), (layer, 0, 0)), _const((d, n), (0, 0))],
        [pl.BlockSpec((tm, d), lambda i: (i, 0)), _const((d, n), (0, 0)), _const((SUBLANES, d), (0, 0))],
        [jax.ShapeDtypeStruct((t, d), F32), jax.ShapeDtypeStruct((d, n), BF16),
         jax.ShapeDtypeStruct((SUBLANES, d), F32)],
        [pltpu.VMEM((SUBLANES, d), F32), pltpu.VMEM((d, n), F32)],
        (dp, h, dho, norm_g, w_in), ex)
    return dh, dw, dg, xouts


def _adam_update(g, w, m, v):
    c1 = 1.0 / (1.0 - ADAM_B1 ** ADAM_STEP)
    c2 = 1.0 / (1.0 - ADAM_B2 ** ADAM_STEP)
    nm = ADAM_B1 * m + (1.0 - ADAM_B1) * g
    nv = ADAM_B2 * v + (1.0 - ADAM_B2) * (g * g)
    return -ADAM_LR * ((nm * c1) / (jnp.sqrt(nv * c2) + ADAM_EPS) + ADAM_WD * w), nm, nv


def _adamw_small(params, stacks, loss_stack, name):
    ns, npar = len(stacks), len(params)

    def body(*refs):
        st = refs[:ns]
        pr = refs[ns:ns + 3 * npar]
        ls_ref = refs[ns + 3 * npar]
        outs = refs[ns + 3 * npar + 1:ns + 7 * npar + 1]
        loss_ref = refs[ns + 7 * npar + 1]
        for q, (w, _, _, pieces) in enumerate(params):
            w_ref, m_ref, v_ref = pr[3 * q:3 * q + 3]
            g_ref, d_ref, nm_ref, nv_ref = outs[4 * q:4 * q + 4]
            for s, row, slab in pieces:
                if w.ndim == 3:
                    take = lambda k: st[s][k]
                    at = slab
                else:
                    take = lambda k: st[s][k, pl.ds(row, 1), :]
                    at = (pl.ds(slab, 1), slice(None))
                g = take(0)
                for k in range(1, N_DEV):
                    g = g + take(k)
                g_ref[at] = g
                d_ref[at], nm_ref[at], nv_ref[at] = _adam_update(g, w_ref[at], m_ref[at], v_ref[at])
        tot = ls_ref[0]
        for k in range(1, N_DEV):
            tot = tot + ls_ref[k]
        loss_ref[...] = tot

    flat = [a for (w, m, v, _) in params for a in (w, m, v)]
    out_shape = [jax.ShapeDtypeStruct(w.shape, F32) for (w, _, _, _) in params for _ in range(4)]
    whole = pl.BlockSpec(memory_space=pltpu.VMEM)
    outs = pl.pallas_call(
        body, name=name, in_specs=[whole] * (ns + 3 * npar + 1), out_specs=[whole] * (4 * npar + 1),
        out_shape=out_shape + [jax.ShapeDtypeStruct(loss_stack.shape[1:], F32)],
    )(*stacks, *flat, loss_stack)
    return [outs[4 * q:4 * q + 4] for q in range(npar)], outs[-1]


def _adamw(stacks, w, m, v, name):
    nl = len(stacks)
    shp = w.shape
    c = shp[-1]
    r = 1
    for s in shp[1:-1]:
        r *= s
    tr = r
    for cand in (512, 256, 128, 64, 32, 16):
        if r % cand == 0 and r > cand:
            tr = cand
            break
    nrb = r // tr

    def body(*refs):
        s_refs = refs[:nl]
        w_ref, m_ref, v_ref, g_ref, d_ref, nm_ref, nv_ref = refs[nl:]
        layer = pl.program_id(0)
        for l in range(nl):
            @pl.when(layer == l)
            def _(l=l):
                g = s_refs[l][0].astype(F32)
                for k in range(1, N_DEV):
                    g = g + s_refs[l][k].astype(F32)
                g_ref[...] = g
                d_ref[...], nm_ref[...], nv_ref[...] = _adam_update(g, w_ref[...], m_ref[...], v_ref[...])

    def stack_spec(l):
        return pl.BlockSpec((N_DEV, tr, c),
                            lambda j, i: (0, jnp.where(j == l, i, jnp.where(j < l, 0, nrb - 1)), 0))

    spec = pl.BlockSpec((None, tr, c), lambda j, i: (j, i, 0))
    outs = pl.pallas_call(
        body, name=name, grid=(nl, nrb),
        in_specs=[stack_spec(l) for l in range(nl)] + [spec, spec, spec],
        out_specs=[spec] * 4,
        out_shape=[jax.ShapeDtypeStruct((nl, r, c), F32)] * 4,
        compiler_params=_params(2),
    )(*[s.reshape(N_DEV, r, c) for s in stacks], w.reshape(nl, r, c), m.reshape(nl, r, c), v.reshape(nl, r, c))
    return [o.reshape(shp) for o in outs]


def kernel(x, norm_g, final_g, conv_w_in, conv_dw, conv_dw_b, conv_ln_g, conv_ln_b, conv_w_out, pool_w_in, pool_w_grp, pool_b_grp, pool_scale, pool_w_out, loss_target, m_norm_g, m_final_g, m_conv_w_in, m_conv_dw, m_conv_dw_b, m_conv_ln_g, m_conv_ln_b, m_conv_w_out, m_pool_w_in, m_pool_w_grp, m_pool_b_grp, m_pool_scale, m_pool_w_out, v_norm_g, v_final_g, v_conv_w_in, v_conv_dw, v_conv_dw_b, v_conv_ln_g, v_conv_ln_b, v_conv_w_out, v_pool_w_in, v_pool_w_grp, v_pool_b_grp, v_pool_scale, v_pool_w_out):
    h0 = x[0]
    target = loss_target[0]
    ng3 = norm_g[:, None, :]
    row3 = lambda a: a[:, None, :]
    conv_vecs = (row3(conv_dw_b), row3(conv_ln_g), row3(conv_ln_b))
    gather = lambda arrays, axes: _Exchange("gather", arrays, axes)
    scatter = lambda arrays, axes: _Exchange("scatter", arrays, axes)

    cwi, cwo, pwi = conv_w_in.astype(BF16), conv_w_out.astype(BF16), pool_w_in.astype(BF16)
    pwg, pwo = pool_w_grp.astype(BF16), pool_w_out.astype(BF16)

    (cw_in0,) = _gather_via_sibling_call(gather([(cwi, 0)], [1]), "gather_first")
    p0, (cw_out0, dw_full, bg_full, sc_full, pw_in0) = _in_fwd(
        h0, ng3, 0, cw_in0, "conv_in_fwd_0",
        gather([(cwo, 0), conv_dw, pool_b_grp, pool_scale, (pwi, 0)], [0, 2, 1, 1, 1]))
    pool_vecs = (row3(bg_full), row3(sc_full))
    h1, uc0, (pw_grp0, pw_out0, cw_in1, cw_out1, pw_in1, pw_grp1, pw_out1) = _conv_fwd(
        p0, h0, dw_full, conv_vecs, 0, cw_out0, "conv_mix_fwd_0",
        gather([(pwg, 0), (pwo, 0), (cwi, 1), (cwo, 1), (pwi, 1), (pwg, 1), (pwo, 1)], [1, 0, 1, 0, 1, 1, 0]))
    h2, p1, _ = _pool_fwd(h1, ng3, 1, pw_in0, pw_grp0, pool_vecs, 0, pw_out0, "pool_fwd_0")
    p2, _ = _in_fwd(h2, ng3, 2, cw_in1, "conv_in_fwd_1")
    h3, uc2, _ = _conv_fwd(p2, h2, dw_full, conv_vecs, 1, cw_out1, "conv_mix_fwd_1")
    dh, p3, d_final_g, loss_part, _ = _pool_fwd(h3, ng3, 3, pw_in1, pw_grp1, pool_vecs, 1, pw_out1, "pool_fwd_1",
                                               final=(target, final_g[None, :]))

    dp, g_pwo1, g_pwg1, dpv1, _ = _pool_bwd(dh, p3, pw_grp1, pool_vecs, 1, pw_out1, "pool_mix_bwd_1")
    dh, g_pwi1, dg3, (s_pwo1, s_pwg1) = _in_bwd(dp, h3, dh, ng3, 3, pw_in1, "pool_in_bwd_1",
                                                scatter([g_pwo1, g_pwg1], [0, 1]))
    dp, g_cwo1, ddw1, dcv1, (s_pwi1,) = _conv_bwd(dh, p2, uc2, dw_full, conv_vecs, 1, cw_out1, "conv_mix_bwd_1",
                                                  scatter([g_pwi1], [1]))
    dh, g_cwi1, dg2, (s_cwo1,) = _in_bwd(dp, h2, dh, ng3, 2, cw_in1, "conv_in_bwd_1", scatter([g_cwo1], [0]))
    dp, g_pwo0, g_pwg0, dpv0, (s_cwi1,) = _pool_bwd(dh, p1, pw_grp0, pool_vecs, 0, pw_out0, "pool_mix_bwd_0",
                                                    scatter([g_cwi1], [1]))
    dh, g_pwi0, dg1, (s_pwo0, s_pwg0) = _in_bwd(dp, h1, dh, ng3, 1, pw_in0, "pool_in_bwd_0",
                                                scatter([g_pwo0, g_pwg0], [0, 1]))
    dp, g_cwo0, ddw0, dcv0, (s_pwi0, s_ddw1, s_dcv1, s_dpv0, s_dpv1, s_dg1, s_dg2, s_dg3, s_dfg, s_loss) = _conv_bwd(
        dh, p0, uc0, dw_full, conv_vecs, 0, cw_out0, "conv_mix_bwd_0",
        scatter([g_pwi0, ddw1, dcv1, dpv0, dpv1, dg1, dg2, dg3, d_final_g, loss_part],
                [1, 1, None, 1, 1, None, None, None, None, None]))
    dh, g_cwi0, dg0, _ = _in_bwd(dp, h0, dh, ng3, 0, cw_in0, "conv_in_bwd_0")
    grad_x = dh[None]
    s_cwo0, s_cwi0, s_ddw0, s_dcv0, s_dg0 = _exchange_call(
        scatter([g_cwo0, g_cwi0, ddw0, dcv0, dg0], [0, 1, 1, None, None]), "scatter_last")

    res = {}
    res["conv_w_in"] = _adamw([s_cwi0, s_cwi1], conv_w_in, m_conv_w_in, v_conv_w_in, "adamw_conv_w_in")
    res["conv_w_out"] = _adamw([s_cwo0, s_cwo1], conv_w_out, m_conv_w_out, v_conv_w_out, "adamw_conv_w_out")
    res["pool_w_in"] = _adamw([s_pwi0, s_pwi1], pool_w_in, m_pool_w_in, v_pool_w_in, "adamw_pool_w_in")
    res["pool_w_grp"] = _adamw([s_pwg0, s_pwg1], pool_w_grp, m_pool_w_grp, v_pool_w_grp, "adamw_pool_w_grp")
    res["pool_w_out"] = _adamw([s_pwo0, s_pwo1], pool_w_out, m_pool_w_out, v_pool_w_out, "adamw_pool_w_out")
    stacks = [s_ddw0, s_ddw1, s_dpv0, s_dpv1, s_dg0, s_dg1, s_dg2, s_dg3, s_dfg, s_dcv0, s_dcv1]
    small = [
        ("conv_dw", conv_dw, m_conv_dw, v_conv_dw, [(0, None, 0), (1, None, 1)]),
        ("pool_b_grp", pool_b_grp, m_pool_b_grp, v_pool_b_grp, [(2, 0, 0), (3, 0, 1)]),
        ("pool_scale", pool_scale, m_pool_scale, v_pool_scale, [(2, 1, 0), (3, 1, 1)]),
        ("norm_g", norm_g, m_norm_g, v_norm_g, [(4, 0, 0), (5, 0, 1), (6, 0, 2), (7, 0, 3)]),
        ("final_g", final_g[None, :], m_final_g[None, :], v_final_g[None, :], [(8, 0, 0)]),
        ("conv_dw_b", conv_dw_b, m_conv_dw_b, v_conv_dw_b, [(9, 0, 0), (10, 0, 1)]),
        ("conv_ln_g", conv_ln_g, m_conv_ln_g, v_conv_ln_g, [(9, 1, 0), (10, 1, 1)]),
        ("conv_ln_b", conv_ln_b, m_conv_ln_b, v_conv_ln_b, [(9, 2, 0), (10, 2, 1)]),
    ]
    small_res, loss_block = _adamw_small([s[1:] for s in small], stacks, s_loss, "adamw_small")
    for (name, *_), r in zip(small, small_res):
        res[name] = [a[0] for a in r] if name == "final_g" else r
    loss = loss_block[0, 0]

    names = ["norm_g", "final_g", "conv_w_in", "conv_dw", "conv_dw_b", "conv_ln_g", "conv_ln_b", "conv_w_out",
             "pool_w_in", "pool_w_grp", "pool_b_grp", "pool_scale", "pool_w_out"]
    return (loss, grad_x) + tuple(res[n][q] for q in range(4) for n in names)
```

```python
import jax
import jax.numpy as jnp
from jax import lax
from jax.experimental import pallas as pl
from jax.experimental.pallas import tpu as pltpu

F32 = jnp.float32
BF16 = jnp.bfloat16

RMS_EPS = 1e-6
LN_EPS = 1e-5
CONV_TAPS = 31
CONV_HALO = 32
POOL_WINDOWS = (2, 4, 8, 16)
POOL_HALO = 16
SUBLANES = 8
LANES = 128
N_DEV = 8
V7X_VMEM_LIMIT = 56 * 1024 * 1024

ADAM_LR = 0.001
ADAM_B1 = 0.9
ADAM_B2 = 0.999
ADAM_EPS = 1e-08
ADAM_WD = 0.01
ADAM_STEP = 10

MESH = pl.DeviceIdType.MESH
ANY = pl.BlockSpec(memory_space=pl.ANY)


def _dot(a, b):
    return lax.dot_general(a, b, (((1,), (0,)), ((), ())), preferred_element_type=F32)


def _dot_nt(a, b):
    return lax.dot_general(a, b, (((1,), (1,)), ((), ())), preferred_element_type=F32)


def _dot_tn(a, b):
    return lax.dot_general(a, b, (((0,), (0,)), ((), ())), preferred_element_type=F32)


def _sigmoid(x):
    return jax.nn.sigmoid(x)


def _dsilu(x, s):
    return s * (1.0 + x * (1.0 - s))


def _rows8(x):
    r, c = x.shape
    return jnp.sum(x.reshape(r // SUBLANES, SUBLANES, c), axis=0)


def _tile(t, pref):
    return pref if t >= 2 * pref else t // 2


def _const(shape, index):
    return pl.BlockSpec(shape, lambda *_: index, pipeline_mode=pl.Buffered(1))


def _params(grid_rank=1):
    return pltpu.CompilerParams(dimension_semantics=("arbitrary",) * grid_rank, vmem_limit_bytes=V7X_VMEM_LIMIT)


def _chunks(n, rc, fn):
    def step(c, carry):
        fn(pl.multiple_of(c * rc, rc))
        return carry
    lax.fori_loop(0, n, step, 0)


def _mesh_position():
    return lax.axis_index("x"), lax.axis_index("y"), lax.axis_index("c")


def _peer(j):
    x, y, c = _mesh_position()
    px = 1 - x if j & 4 else x
    py = 1 - y if j & 2 else y
    pc = 1 - c if j & 1 else c
    return (px, py, pc), 4 * px + 2 * py + pc


def _block(ref, axis, index, size):
    idx = [slice(None)] * len(ref.shape)
    idx[axis] = pl.ds(pl.multiple_of(index * size, size), size)
    return ref.at[tuple(idx)]


class _Exchange:
    def __init__(self, kind, arrays, axes):
        self.kind, self.axes = kind, list(axes)
        self.arrays = [a[0] if isinstance(a, tuple) else a for a in arrays]
        self.layers = [a[1] if isinstance(a, tuple) else None for a in arrays]
        self.n = len(self.arrays)
        self.blk, self.out_shapes = [], []
        for a, layer, ax in zip(self.arrays, self.layers, self.axes):
            s = list(a.shape if layer is None else a.shape[1:])
            if kind == "gather":
                self.blk.append(s[ax])
                s[ax] *= N_DEV
                self.out_shapes.append(jax.ShapeDtypeStruct(tuple(s), a.dtype))
            else:
                if ax is not None:
                    s[ax] //= N_DEV
                    self.blk.append(s[ax])
                else:
                    self.blk.append(None)
                self.out_shapes.append(jax.ShapeDtypeStruct((N_DEV,) + tuple(s), a.dtype))

    def sem_shapes(self):
        return [pltpu.SemaphoreType.DMA((N_DEV - 1, self.n)), pltpu.SemaphoreType.DMA((N_DEV - 1, self.n)),
                pltpu.SemaphoreType.DMA((self.n,))]

    def _src(self, ins, k, owner):
        ref = ins[k] if self.layers[k] is None else ins[k].at[self.layers[k]]
        if self.kind == "gather" or self.axes[k] is None:
            return ref
        return _block(ref, self.axes[k], owner, self.blk[k])

    def _dst(self, outs, k, sender):
        if self.kind == "gather":
            return _block(outs[k], self.axes[k], sender, self.blk[k])
        return outs[k].at[sender]

    def _copies(self, ins, outs, sems, arriving):
        send, recv, loc = sems
        x, y, c = _mesh_position()
        me = 4 * x + 2 * y + c
        if not arriving:
            local = [pltpu.make_async_copy(self._src(ins, k, me), self._dst(outs, k, me), loc.at[k])
                     for k in range(self.n)]
        else:
            local = []
        remote = []
        for j in range(1, N_DEV):
            peer, peer_id = _peer(j)
            for k in range(self.n):
                owner, sender = (me, peer_id) if arriving else (peer_id, me)
                remote.append(pltpu.make_async_remote_copy(
                    src_ref=self._src(ins, k, owner), dst_ref=self._dst(outs, k, sender),
                    send_sem=send.at[j - 1, k], recv_sem=recv.at[j - 1, k], device_id=peer, device_id_type=MESH))
        return local, remote

    def start(self, ins, outs, sems):
        local, sends = self._copies(ins, outs, sems, arriving=False)
        for cp in local + sends:
            cp.start()

    def finish(self, ins, outs, sems):
        for cp in self._copies(ins, outs, sems, arriving=True)[1]:
            cp.wait_recv()
        local, sends = self._copies(ins, outs, sems, arriving=False)
        for cp in sends:
            cp.wait_send()
        for cp in local:
            cp.wait()


def _exchange_call(ex, name):
    def body(*refs):
        ins, outs, sems = refs[:ex.n], refs[ex.n:2 * ex.n], refs[2 * ex.n:]
        ex.start(ins, outs, sems)
        ex.finish(ins, outs, sems)

    return pl.pallas_call(body, name=name, in_specs=[ANY] * ex.n, out_specs=[ANY] * ex.n,
                          out_shape=ex.out_shapes, scratch_shapes=ex.sem_shapes())(*ex.arrays)


def _gather_via_sibling_call(ex, name):
    n = ex.n

    def body(*refs):
        ins, outs = refs[:n], refs[n:2 * n]
        send, recv, loc = refs[2 * n:]
        x, y, c = _mesh_position()
        ident = lambda px, py, pc: 4 * px + 2 * py + pc
        me, sibling = ident(x, y, c), (x, y, 1 - c)
        chips = [(1 - x, y), (x, 1 - y), (1 - x, 1 - y)]

        def copy(row, k, block, to, src=None):
            place = ex._dst(outs, k, block)
            return pltpu.make_async_remote_copy(
                src_ref=place if src is None else src, dst_ref=place,
                send_sem=send.at[row, k], recv_sem=recv.at[row, k], device_id=to, device_id_type=MESH)

        local = [pltpu.make_async_copy(ex._src(ins, k, me), ex._dst(outs, k, me), loc.at[k]) for k in range(n)]
        first = []
        for k in range(n):
            mine = ex._src(ins, k, me)
            first.append(copy(0, k, me, sibling, src=mine))
            first += [copy(1 + j, k, me, (*chip, c), src=mine) for j, chip in enumerate(chips)]
        for cp in local + first:
            cp.start()
        passed = []
        for j, chip in enumerate(chips):
            for k in range(n):
                copy(1 + j, k, ident(*chip, c), sibling).wait_recv()
                passed.append(copy(4 + j, k, ident(*chip, c), sibling))
                passed[-1].start()
        for k in range(n):
            copy(0, k, ident(x, y, 1 - c), sibling).wait_recv()
            for j, chip in enumerate(chips):
                copy(4 + j, k, ident(*chip, 1 - c), sibling).wait_recv()
        for cp in first + passed:
            cp.wait_send()
        for cp in local:
            cp.wait()

    return pl.pallas_call(body, name=name, in_specs=[ANY] * n, out_specs=[ANY] * n, out_shape=ex.out_shapes,
                          scratch_shapes=ex.sem_shapes())(*ex.arrays)


def _launch(body, name, nt, in_specs, out_specs, out_shape, scratch_shapes, args, ex=None):
    if ex is None:
        outs = pl.pallas_call(body, name=name, grid=(nt,), in_specs=in_specs, out_specs=out_specs,
                              out_shape=out_shape, scratch_shapes=scratch_shapes, compiler_params=_params())(*args)
        return list(outs), []
    n_in, n_out, n_scr = len(in_specs), len(out_specs), len(scratch_shapes)

    def riding(*refs):
        a, xa = refs[:n_in], refs[n_in:n_in + ex.n]
        o = refs[n_in + ex.n:n_in + ex.n + n_out]
        xo = refs[n_in + ex.n + n_out:n_in + 2 * ex.n + n_out]
        s = refs[n_in + 2 * ex.n + n_out:n_in + 2 * ex.n + n_out + n_scr]
        sems = refs[n_in + 2 * ex.n + n_out + n_scr:]
        i = pl.program_id(0)

        @pl.when(i == 0)
        def _():
            ex.start(xa, xo, sems)

        body(*a, *o, *s)

        @pl.when(i == nt - 1)
        def _():
            ex.finish(xa, xo, sems)

    outs = pl.pallas_call(
        riding, name=name, grid=(nt,),
        in_specs=list(in_specs) + [ANY] * ex.n, out_specs=list(out_specs) + [ANY] * ex.n,
        out_shape=list(out_shape) + ex.out_shapes, scratch_shapes=list(scratch_shapes) + ex.sem_shapes(),
        compiler_params=_params())(*args, *ex.arrays)
    return list(outs[:n_out]), list(outs[n_out:])


def _in_fwd(h, norm_g, layer, w_in, name, ex=None):
    t, d = h.shape
    n = w_in.shape[-1]
    tm = _tile(t, 512)

    def body(h_ref, g_ref, w_ref, p_ref):
        x = h_ref[...]
        r = lax.rsqrt(jnp.mean(x * x, axis=-1, keepdims=True) + RMS_EPS)
        p_ref[...] = _dot((x * r * g_ref[...]).astype(BF16), w_ref[...])

    (p,), xouts = _launch(
        body, name, t // tm,
        [pl.BlockSpec((tm, d), lambda i: (i, 0)), _const((None, 1, d), (layer, 0, 0)), _const((d, n), (0, 0))],
        [pl.BlockSpec((tm, n), lambda i: (i, 0))],
        [jax.ShapeDtypeStruct((t, n), F32)], [], (h, norm_g, w_in), ex)
    return p, xouts


def _layernorm_rows(uc, lng, lnb):
    mu = jnp.mean(uc, axis=-1, keepdims=True)
    xc = uc - mu
    rstd = lax.rsqrt(jnp.mean(xc * xc, axis=-1, keepdims=True) + LN_EPS)
    xhat = xc * rstd
    return xhat, rstd, xhat * lng + lnb


def _conv_fwd(p, h, dw, vecs, layer, w_out, name, ex=None):
    t, d = h.shape
    e = w_out.shape[0]
    tm = _tile(t, 256)
    rc = _tile(tm, 128)

    def body(p_ref, h_ref, dw_ref, dwb_ref, lng_ref, lnb_ref, wo_ref, ho_ref, uc_ref, us_scr):
        i = pl.program_id(0)

        @pl.when(i == 0)
        def _():
            us_scr[:, pl.ds(0, CONV_HALO), :] = jnp.zeros((SUBLANES, CONV_HALO, e), F32)

        @pl.when(i > 0)
        def _():
            us_scr[:, pl.ds(0, CONV_HALO), :] = us_scr[:, pl.ds(tm, CONV_HALO), :]

        us_scr[0, pl.ds(CONV_HALO, tm), :] = p_ref[:, pl.ds(0, e)] * _sigmoid(p_ref[:, pl.ds(e, e)])
        for r in range(1, SUBLANES):
            us_scr[r, pl.ds(CONV_HALO, tm), :] = us_scr[0, pl.ds(CONV_HALO - r, tm), :]

        def c_conv(base):
            for lt in range(e // LANES):
                cols = pl.ds(lt * LANES, LANES)
                acc = jnp.broadcast_to(dwb_ref[:, cols], (rc, LANES))
                for r in range(SUBLANES):
                    nq = (CONV_TAPS - 1 - r) // SUBLANES + 1
                    lo = SUBLANES * (nq - 1)
                    win = us_scr[r, pl.ds(pl.multiple_of(CONV_HALO + base - lo, SUBLANES), rc + lo), cols]
                    for q in range(nq):
                        k = CONV_TAPS - 1 - (SUBLANES * q + r)
                        at = lo - SUBLANES * q
                        acc = acc + dw_ref[pl.ds(k, 1), cols] * win[at:at + rc, :]
                uc_ref[pl.ds(base, rc), cols] = acc
        _chunks(tm // rc, rc, c_conv)

        _, _, ul = _layernorm_rows(uc_ref[...], lng_ref[...], lnb_ref[...])
        z = p_ref[:, pl.ds(2 * e, e)]
        v = ((ul * _sigmoid(ul)) * (z * _sigmoid(z))).astype(BF16)
        ho_ref[...] = h_ref[...] + _dot(v, wo_ref[...])

    vec = _const((None, 1, e), (layer, 0, 0))
    (ho, uc), xouts = _launch(
        body, name, t // tm,
        [pl.BlockSpec((tm, 3 * e), lambda i: (i, 0)), pl.BlockSpec((tm, d), lambda i: (i, 0)),
         _const((None, CONV_TAPS, e), (layer, 0, 0)), vec, vec, vec, _const((e, d), (0, 0))],
        [pl.BlockSpec((tm, d), lambda i: (i, 0)), pl.BlockSpec((tm, e), lambda i: (i, 0))],
        [jax.ShapeDtypeStruct((t, d), F32), jax.ShapeDtypeStruct((t, e), F32)],
        [pltpu.VMEM((SUBLANES, tm + CONV_HALO, e), F32)], (p, h, dw, *vecs, w_out), ex)
    return ho, uc, xouts


def _conv_bwd(dho, p, uc, dw, vecs, layer, w_out, name, ex=None):
    t, d = dho.shape
    e = w_out.shape[0]
    tm = _tile(t, 256)
    nt = t // tm
    rc = 16
    vec0 = CONV_TAPS + 1

    def body(dho_ref, p_ref, uc_ref, dw_ref, lng_ref, lnb_ref, wo_ref, dp_ref, dwo_ref, ddw_ref, dvec_ref,
             ds_scr, acc_scr, dwo_scr):
        i = pl.program_id(0)

        @pl.when(i == 0)
        def _():
            dwo_scr[...] = jnp.zeros_like(dwo_scr)
            acc_scr[...] = jnp.zeros_like(acc_scr)
            ds_scr[:, pl.ds(tm, CONV_HALO), :] = jnp.zeros((SUBLANES, CONV_HALO, e), F32)

        @pl.when(i > 0)
        def _():
            ds_scr[:, pl.ds(tm, CONV_HALO), :] = ds_scr[:, pl.ds(0, CONV_HALO), :]

        lng = lng_ref[...]
        lnb = lnb_ref[...]

        xhat, rstd, ul = _layernorm_rows(uc_ref[...], lng, lnb)
        z = p_ref[:, pl.ds(2 * e, e)]
        sg_u = _sigmoid(ul)
        sg_z = _sigmoid(z)
        s_u = ul * sg_u
        s_z = z * sg_z
        v = (s_u * s_z).astype(BF16)
        dy = dho_ref[...].astype(BF16)
        dv = _dot_nt(dy, wo_ref[...])
        dwo_scr[...] += _dot_tn(v, dy)

        dul = dv * s_z * _dsilu(ul, sg_u)
        dp_ref[:, pl.ds(2 * e, e)] = (dv * s_u * _dsilu(z, sg_z)).astype(BF16)
        acc_scr[pl.ds((vec0 + 1) * SUBLANES, SUBLANES), :] += _rows8(dul * xhat)
        acc_scr[pl.ds((vec0 + 2) * SUBLANES, SUBLANES), :] += _rows8(dul)
        dxh = dul * lng
        duc = rstd * (dxh - jnp.mean(dxh, axis=-1, keepdims=True)
                      - xhat * jnp.mean(dxh * xhat, axis=-1, keepdims=True))
        acc_scr[pl.ds(vec0 * SUBLANES, SUBLANES), :] += _rows8(duc)
        ds_scr[0, pl.ds(0, tm), :] = duc
        for r in range(1, SUBLANES):
            ds_scr[r, pl.ds(0, tm), :] = ds_scr[0, pl.ds(r, tm), :]

        def c_conv(base):
            rows = pl.ds(base, rc)
            a = p_ref[rows, pl.ds(0, e)]
            b = p_ref[rows, pl.ds(e, e)]
            sb = _sigmoid(b)
            u = a * sb
            du = jnp.zeros((rc, e), F32)
            for o in range(CONV_TAPS):
                q, r = divmod(o, SUBLANES)
                k = CONV_TAPS - 1 - o
                sh = ds_scr[r, pl.ds(pl.multiple_of(base + SUBLANES * q, SUBLANES), rc), :]
                du = du + dw_ref[pl.ds(k, 1), :] * sh
                acc_scr[pl.ds(k * SUBLANES, SUBLANES), :] += _rows8(u * sh)
            dp_ref[rows, pl.ds(0, e)] = (du * sb).astype(BF16)
            dp_ref[rows, pl.ds(e, e)] = (du * u * (1.0 - sb)).astype(BF16)
        _chunks(tm // rc, rc, c_conv)

        @pl.when(i == nt - 1)
        def _():
            dwo_ref[...] = dwo_scr[...].astype(BF16)
            slot_sum = lambda k: jnp.sum(acc_scr[pl.ds(k * SUBLANES, SUBLANES), :], axis=0, keepdims=True)
            for k in range(CONV_TAPS):
                ddw_ref[pl.ds(k, 1), :] = slot_sum(k)
            dvec_ref[...] = jnp.zeros_like(dvec_ref)
            for k in range(3):
                dvec_ref[pl.ds(k, 1), :] = slot_sum(vec0 + k)

    rev = lambda i: (nt - 1 - i, 0)
    vec = _const((None, 1, e), (layer, 0, 0))
    (dp, dwo, ddw, dvec), xouts = _launch(
        body, name, nt,
        [pl.BlockSpec((tm, d), rev), pl.BlockSpec((tm, 3 * e), rev), pl.BlockSpec((tm, e), rev),
         _const((None, CONV_TAPS, e), (layer, 0, 0)), vec, vec, _const((e, d), (0, 0))],
        [pl.BlockSpec((tm, 3 * e), rev), _const((e, d), (0, 0)), _const((CONV_TAPS, e), (0, 0)),
         _const((SUBLANES, e), (0, 0))],
        [jax.ShapeDtypeStruct((t, 3 * e), BF16), jax.ShapeDtypeStruct((e, d), BF16),
         jax.ShapeDtypeStruct((CONV_TAPS, e), F32), jax.ShapeDtypeStruct((SUBLANES, e), F32)],
        [pltpu.VMEM((SUBLANES, tm + CONV_HALO, e), F32), pltpu.VMEM(((vec0 + 3) * SUBLANES, e), F32),
         pltpu.VMEM((e, d), F32)],
        (dho, p, uc, dw, vecs[1], vecs[2], w_out), ex)
    return dp, dwo, ddw, dvec, xouts


def _inv_count(tile, tm, w):
    tpos = tile * tm + lax.broadcasted_iota(jnp.int32, (tm, 1), 0)
    return 1.0 / jnp.minimum(tpos + 1, w).astype(F32)


def _pool_d_group(ue_scr, tile, tm, gc, g):
    w = POOL_WINDOWS[g]
    win = ue_scr[:, pl.ds(g * gc, gc)]
    s = win
    sh = 1
    while sh < w:
        s = s + pltpu.roll(s, sh, axis=0)
        sh *= 2
    return s[POOL_HALO:, :] * _inv_count(tile, tm, w) - win[POOL_HALO:, :]


def _pool_d(ue_scr, tile, tm, gc):
    return [_pool_d_group(ue_scr, tile, tm, gc, g) for g in range(len(POOL_WINDOWS))]


def _final_rows(ho, tg_ref, fg_ref, dh_ref, acc_scr, lacc_scr):
    d = ho.shape[-1]
    r = lax.rsqrt(jnp.mean(ho * ho, axis=-1, keepdims=True) + RMS_EPS)
    nrm = ho * r
    err = nrm * fg_ref[...] - tg_ref[...]
    lacc_scr[...] += _rows8(err * err)
    dy = err * (1.0 / d)
    acc_scr[...] += _rows8(dy * nrm)
    dq = dy * fg_ref[...]
    dh_ref[...] = r * (dq - nrm * jnp.mean(dq * nrm, axis=-1, keepdims=True))


def _pool_fwd(h, norm_g, nlayer, w_in, w_grp, vecs, layer, w_out, name, ex=None, final=None):
    t, d = h.shape
    e = w_out.shape[0]
    ng = len(POOL_WINDOWS)
    gc = e // ng
    tm = _tile(t, 512)
    nt = t // tm
    cw = 2 * e // (2 * ng)

    def layer_rows(i, hn_ref, hc_ref, g_ref, wi_ref, wg_ref, bg_ref, sc_ref, wo_ref, saved,
                   ue_scr, pbuf, z_scr, hn_scr):
        d_ref, ob_ref, sz_ref, yz_ref, y2_ref = saved
        tile = jnp.maximum(i - 1, 0)

        @pl.when(i == 0)
        def _():
            pbuf[...] = jnp.zeros_like(pbuf)

        @pl.when(i <= 1)
        def _():
            ue_scr[pl.ds(0, POOL_HALO), :] = jnp.zeros((POOL_HALO, e), F32)

        @pl.when(i > 1)
        def _():
            ue_scr[pl.ds(0, POOL_HALO), :] = ue_scr[pl.ds(tm, POOL_HALO), :]

        ue_scr[pl.ds(POOL_HALO, tm), :] = pbuf[:, pl.ds(0, e)]
        z_scr[...] = pbuf[:, pl.ds(e, e)]

        x = hn_ref[...]
        r = lax.rsqrt(jnp.mean(x * x, axis=-1, keepdims=True) + RMS_EPS)
        hn_scr[...] = (x * r * g_ref[...]).astype(BF16)

        def project(c):
            pbuf[:, pl.ds(c * cw, cw)] = _dot(hn_scr[...], wi_ref[:, pl.ds(c * cw, cw)])

        for g in range(ng):
            project(2 * g)
            cols = pl.ds(g * gc, gc)
            sc = sc_ref[:, cols]
            d_g = _pool_d_group(ue_scr, tile, tm, gc, g).astype(BF16)
            ob = _dot(d_g, wg_ref[g]) + bg_ref[:, cols]
            z = z_scr[:, cols]
            sg_z = _sigmoid(z)
            s_z = z * sg_z
            y1 = ob * sc
            d_ref[:, cols] = d_g
            ob_ref[:, cols] = ob
            sz_ref[:, cols] = s_z
            yz_ref[:, cols] = y1 * _dsilu(z, sg_z)
            y2_ref[:, cols] = (y1 * s_z).astype(BF16)
            project(2 * g + 1)

        return hc_ref[...] + _dot(y2_ref[...], wo_ref[...])

    nxt = lambda i: (jnp.minimum(i, nt - 1), 0)
    cur = lambda i: (jnp.maximum(i - 1, 0), 0)
    vec = _const((None, 1, e), (layer, 0, 0))
    in_specs = [pl.BlockSpec((tm, d), nxt), pl.BlockSpec((tm, d), cur), _const((None, 1, d), (nlayer, 0, 0)),
                _const((d, 2 * e), (0, 0)), _const((ng, gc, gc), (0, 0, 0)), vec, vec, _const((e, d), (0, 0))]
    saved_specs = [pl.BlockSpec((tm, e), cur)] * 5
    saved_shapes = [jax.ShapeDtypeStruct((t, e), dt) for dt in (BF16, F32, F32, F32, BF16)]
    scratch = [pltpu.VMEM((tm + POOL_HALO, e), F32), pltpu.VMEM((tm, 2 * e), F32),
               pltpu.VMEM((tm, e), F32), pltpu.VMEM((tm, d), BF16)]
    args = (h, h, norm_g, w_in, w_grp, *vecs, w_out)

    if final is None:
        def body(hn_ref, hc_ref, g_ref, wi_ref, wg_ref, bg_ref, sc_ref, wo_ref, ho_ref, *rest):
            ho_ref[...] = layer_rows(pl.program_id(0), hn_ref, hc_ref, g_ref, wi_ref, wg_ref, bg_ref, sc_ref, wo_ref,
                                     rest[:5], *rest[5:])

        outs, xouts = _launch(
            body, name, nt + 1, in_specs, [pl.BlockSpec((tm, d), cur)] + saved_specs,
            [jax.ShapeDtypeStruct((t, d), F32)] + saved_shapes, scratch, args, ex)
        return outs[0], tuple(outs[1:]), xouts

    target, final_g = final

    def body(hn_ref, hc_ref, g_ref, wi_ref, wg_ref, bg_ref, sc_ref, wo_ref, tg_ref, fg_ref,
             dh_ref, d_ref, ob_ref, sz_ref, yz_ref, y2_ref, dfg_ref, loss_ref,
             ue_scr, pbuf, z_scr, hn_scr, acc_scr, lacc_scr):
        i = pl.program_id(0)

        @pl.when(i <= 1)
        def _():
            acc_scr[...] = jnp.zeros_like(acc_scr)
            lacc_scr[...] = jnp.zeros_like(lacc_scr)

        ho = layer_rows(i, hn_ref, hc_ref, g_ref, wi_ref, wg_ref, bg_ref, sc_ref, wo_ref,
                        (d_ref, ob_ref, sz_ref, yz_ref, y2_ref), ue_scr, pbuf, z_scr, hn_scr)
        _final_rows(ho, tg_ref, fg_ref, dh_ref, acc_scr, lacc_scr)

        @pl.when(i == nt)
        def _():
            dfg_ref[...] = jnp.zeros_like(dfg_ref)
            dfg_ref[pl.ds(0, 1), :] = jnp.sum(acc_scr[...], axis=0, keepdims=True)
            loss_ref[...] = jnp.broadcast_to(jnp.sum(lacc_scr[...]) * (0.5 / d), loss_ref.shape)

    outs, xouts = _launch(
        body, name, nt + 1, in_specs + [pl.BlockSpec((tm, d), cur), _const((1, d), (0, 0))],
        [pl.BlockSpec((tm, d), cur)] + saved_specs + [_const((SUBLANES, d), (0, 0)), _const((SUBLANES, LANES), (0, 0))],
        [jax.ShapeDtypeStruct((t, d), F32)] + saved_shapes
        + [jax.ShapeDtypeStruct((SUBLANES, d), F32), jax.ShapeDtypeStruct((SUBLANES, LANES), F32)],
        scratch + [pltpu.VMEM((SUBLANES, d), F32), pltpu.VMEM((SUBLANES, d), F32)],
        args + (target, final_g), ex)
    return outs[0], tuple(outs[1:6]), outs[6], outs[7], xouts


def _pool_bwd(dho, saved, w_grp, vecs, layer, w_out, name, ex=None):
    t, d = dho.shape
    e = w_out.shape[0]
    ng = len(POOL_WINDOWS)
    gc = e // ng
    tm = _tile(t, 512)
    nt = t // tm

    def body(dho_ref, d_ref, ob_ref, sz_ref, yz_ref, y2_ref, wg_ref, sc_ref, wo_ref,
             dp_ref, dwo_ref, dwg_ref, dvec_ref, ee_scr, acc_scr, dwo_scr, dwg_scr):
        i = pl.program_id(0)
        tile = nt - 1 - i

        @pl.when(i == 0)
        def _():
            dwo_scr[...] = jnp.zeros_like(dwo_scr)
            dwg_scr[...] = jnp.zeros_like(dwg_scr)
            acc_scr[...] = jnp.zeros_like(acc_scr)
            ee_scr[pl.ds(tm, POOL_HALO), :] = jnp.zeros((POOL_HALO, e), F32)

        @pl.when(i > 0)
        def _():
            ee_scr[pl.ds(tm, POOL_HALO), :] = ee_scr[pl.ds(0, POOL_HALO), :]

        dy = dho_ref[...].astype(BF16)
        dy2 = _dot_nt(dy, wo_ref[...])
        dwo_scr[...] += _dot_tn(y2_ref[...], dy)
        dy1 = dy2 * sz_ref[...]
        dp_ref[:, pl.ds(e, e)] = (dy2 * yz_ref[...]).astype(BF16)
        acc_scr[pl.ds(SUBLANES, SUBLANES), :] += _rows8(dy1 * ob_ref[...])
        do = dy1 * sc_ref[...]
        acc_scr[pl.ds(0, SUBLANES), :] += _rows8(do)

        n = tm + POOL_HALO
        for g, w in enumerate(POOL_WINDOWS):
            cols = pl.ds(g * gc, gc)
            do_g = do[:, g * gc:(g + 1) * gc].astype(BF16)
            dwg_scr[g] += _dot_tn(d_ref[:, cols], do_g)
            dd = _dot_nt(do_g, wg_ref[g])
            ee_scr[pl.ds(0, tm), cols] = dd * _inv_count(tile, tm, w)
            s = ee_scr[:, cols]
            sh = 1
            while sh < w:
                s = s + pltpu.roll(s, n - sh, axis=0)
                sh *= 2
            dp_ref[:, cols] = (s[:tm, :] - dd).astype(BF16)

        @pl.when(i == nt - 1)
        def _():
            dwo_ref[...] = dwo_scr[...].astype(BF16)
            dwg_ref[...] = dwg_scr[...].astype(BF16)
            dvec_ref[...] = jnp.zeros_like(dvec_ref)
            for k in range(2):
                dvec_ref[pl.ds(k, 1), :] = jnp.sum(acc_scr[pl.ds(k * SUBLANES, SUBLANES), :], axis=0, keepdims=True)

    rev = lambda i: (nt - 1 - i, 0)
    (dp, dwo, dwg, dvec), xouts = _launch(
        body, name, nt,
        [pl.BlockSpec((tm, d), rev)] + [pl.BlockSpec((tm, e), rev)] * 5
        + [_const((ng, gc, gc), (0, 0, 0)), _const((None, 1, e), (layer, 0, 0)), _const((e, d), (0, 0))],
        [pl.BlockSpec((tm, 2 * e), rev), _const((e, d), (0, 0)), _const((ng, gc, gc), (0, 0, 0)),
         _const((SUBLANES, e), (0, 0))],
        [jax.ShapeDtypeStruct((t, 2 * e), BF16), jax.ShapeDtypeStruct((e, d), BF16),
         jax.ShapeDtypeStruct((ng, gc, gc), BF16), jax.ShapeDtypeStruct((SUBLANES, e), F32)],
        [pltpu.VMEM((tm + POOL_HALO, e), F32), pltpu.VMEM((2 * SUBLANES, e), F32), pltpu.VMEM((e, d), F32),
         pltpu.VMEM((ng, gc, gc), F32)],
        (dho, *saved, w_grp, vecs[1], w_out), ex)
    return dp, dwo, dwg, dvec, xouts


def _in_bwd(dp, h, dho, norm_g, layer, w_in, name, ex=None):
    t, d = h.shape
    n = w_in.shape[-1]
    tm = _tile(t, 512)
    nt = t // tm

    def body(dp_ref, h_ref, dho_ref, g_ref, w_ref, dh_ref, dw_ref, dg_ref, acc_scr, dw_scr):
        i = pl.program_id(0)

        @pl.when(i == 0)
        def _():
            dw_scr[...] = jnp.zeros_like(dw_scr)
            acc_scr[...] = jnp.zeros_like(acc_scr)

        x = h_ref[...]
        r = lax.rsqrt(jnp.mean(x * x, axis=-1, keepdims=True) + RMS_EPS)
        nrm = x * r
        dp = dp_ref[...]
        dhn = _dot_nt(dp, w_ref[...])
        dw_scr[...] += _dot_tn((nrm * g_ref[...]).astype(BF16), dp)
        acc_scr[...] += _rows8(dhn * nrm)
        dq = dhn * g_ref[...]
        dh_ref[...] = dho_ref[...] + r * (dq - nrm * jnp.mean(dq * nrm, axis=-1, keepdims=True))

        @pl.when(i == nt - 1)
        def _():
            dw_ref[...] = dw_scr[...].astype(BF16)
            dg_ref[...] = jnp.zeros_like(dg_ref)
            dg_ref[pl.ds(0, 1), :] = jnp.sum(acc_scr[...], axis=0, keepdims=True)

    (dh, dw, dg), xouts = _launch(
        body, name, nt,
        [pl.BlockSpec((tm, n), lambda i: (i, 0)), pl.BlockSpec((tm, d), lambda i: (i, 0)),
         pl.BlockSpec((tm, d), lambda i: (i, 0)), _const((None, 1, d), (layer, 0, 0)), _const((d, n), (0, 0))],
        [pl.BlockSpec((tm, d), lambda i: (i, 0)), _const((d, n), (0, 0)), _const((SUBLANES, d), (0, 0))],
        [jax.ShapeDtypeStruct((t, d), F32), jax.ShapeDtypeStruct((d, n), BF16),
         jax.ShapeDtypeStruct((SUBLANES, d), F32)],
        [pltpu.VMEM((SUBLANES, d), F32), pltpu.VMEM((d, n), F32)],
        (dp, h, dho, norm_g, w_in), ex)
    return dh, dw, dg, xouts


def _adam_update(g, w, m, v):
    c1 = 1.0 / (1.0 - ADAM_B1 ** ADAM_STEP)
    c2 = 1.0 / (1.0 - ADAM_B2 ** ADAM_STEP)
    nm = ADAM_B1 * m + (1.0 - ADAM_B1) * g
    nv = ADAM_B2 * v + (1.0 - ADAM_B2) * (g * g)
    return -ADAM_LR * ((nm * c1) / (jnp.sqrt(nv * c2) + ADAM_EPS) + ADAM_WD * w), nm, nv


def _adamw_small(params, stacks, loss_stack, name):
    ns, npar = len(stacks), len(params)

    def body(*refs):
        st = refs[:ns]
        pr = refs[ns:ns + 3 * npar]
        ls_ref = refs[ns + 3 * npar]
        outs = refs[ns + 3 * npar + 1:ns + 7 * npar + 1]
        loss_ref = refs[ns + 7 * npar + 1]
        for q, (w, _, _, pieces) in enumerate(params):
            w_ref, m_ref, v_ref = pr[3 * q:3 * q + 3]
            g_ref, d_ref, nm_ref, nv_ref = outs[4 * q:4 * q + 4]
            for s, row, slab in pieces:
                if w.ndim == 3:
                    take = lambda k: st[s][k]
                    at = slab
                else:
                    take = lambda k: st[s][k, pl.ds(row, 1), :]
                    at = (pl.ds(slab, 1), slice(None))
                g = take(0)
                for k in range(1, N_DEV):
                    g = g + take(k)
                g_ref[at] = g
                d_ref[at], nm_ref[at], nv_ref[at] = _adam_update(g, w_ref[at], m_ref[at], v_ref[at])
        tot = ls_ref[0]
        for k in range(1, N_DEV):
            tot = tot + ls_ref[k]
        loss_ref[...] = tot

    flat = [a for (w, m, v, _) in params for a in (w, m, v)]
    out_shape = [jax.ShapeDtypeStruct(w.shape, F32) for (w, _, _, _) in params for _ in range(4)]
    whole = pl.BlockSpec(memory_space=pltpu.VMEM)
    outs = pl.pallas_call(
        body, name=name, in_specs=[whole] * (ns + 3 * npar + 1), out_specs=[whole] * (4 * npar + 1),
        out_shape=out_shape + [jax.ShapeDtypeStruct(loss_stack.shape[1:], F32)],
    )(*stacks, *flat, loss_stack)
    return [outs[4 * q:4 * q + 4] for q in range(npar)], outs[-1]


def _adamw(stacks, w, m, v, name):
    nl = len(stacks)
    shp = w.shape
    c = shp[-1]
    r = 1
    for s in shp[1:-1]:
        r *= s
    tr = r
    for cand in (512, 256, 128, 64, 32, 16):
        if r % cand == 0 and r > cand:
            tr = cand
            break
    nrb = r // tr

    def body(*refs):
        s_refs = refs[:nl]
        w_ref, m_ref, v_ref, g_ref, d_ref, nm_ref, nv_ref = refs[nl:]
        layer = pl.program_id(0)
        for l in range(nl):
            @pl.when(layer == l)
            def _(l=l):
                g = s_refs[l][0].astype(F32)
                for k in range(1, N_DEV):
                    g = g + s_refs[l][k].astype(F32)
                g_ref[...] = g
                d_ref[...], nm_ref[...], nv_ref[...] = _adam_update(g, w_ref[...], m_ref[...], v_ref[...])

    def stack_spec(l):
        return pl.BlockSpec((N_DEV, tr, c),
                            lambda j, i: (0, jnp.where(j == l, i, jnp.where(j < l, 0, nrb - 1)), 0))

    spec = pl.BlockSpec((None, tr, c), lambda j, i: (j, i, 0))
    outs = pl.pallas_call(
        body, name=name, grid=(nl, nrb),
        in_specs=[stack_spec(l) for l in range(nl)] + [spec, spec, spec],
        out_specs=[spec] * 4,
        out_shape=[jax.ShapeDtypeStruct((nl, r, c), F32)] * 4,
        compiler_params=_params(2),
    )(*[s.reshape(N_DEV, r, c) for s in stacks], w.reshape(nl, r, c), m.reshape(nl, r, c), v.reshape(nl, r, c))
    return [o.reshape(shp) for o in outs]


def kernel(x, norm_g, final_g, conv_w_in, conv_dw, conv_dw_b, conv_ln_g, conv_ln_b, conv_w_out, pool_w_in, pool_w_grp, pool_b_grp, pool_scale, pool_w_out, loss_target, m_norm_g, m_final_g, m_conv_w_in, m_conv_dw, m_conv_dw_b, m_conv_ln_g, m_conv_ln_b, m_conv_w_out, m_pool_w_in, m_pool_w_grp, m_pool_b_grp, m_pool_scale, m_pool_w_out, v_norm_g, v_final_g, v_conv_w_in, v_conv_dw, v_conv_dw_b, v_conv_ln_g, v_conv_ln_b, v_conv_w_out, v_pool_w_in, v_pool_w_grp, v_pool_b_grp, v_pool_scale, v_pool_w_out):
    h0 = x[0]
    target = loss_target[0]
    ng3 = norm_g[:, None, :]
    row3 = lambda a: a[:, None, :]
    conv_vecs = (row3(conv_dw_b), row3(conv_ln_g), row3(conv_ln_b))
    gather = lambda arrays, axes: _Exchange("gather", arrays, axes)
    scatter = lambda arrays, axes: _Exchange("scatter", arrays, axes)

    cwi, cwo, pwi = conv_w_in.astype(BF16), conv_w_out.astype(BF16), pool_w_in.astype(BF16)
    pwg, pwo = pool_w_grp.astype(BF16), pool_w_out.astype(BF16)

    (cw_in0,) = _gather_via_sibling_call(gather([(cwi, 0)], [1]), "gather_first")
    p0, (cw_out0, dw_full, bg_full, sc_full, pw_in0) = _in_fwd(
        h0, ng3, 0, cw_in0, "conv_in_fwd_0",
        gather([(cwo, 0), conv_dw, pool_b_grp, pool_scale, (pwi, 0)], [0, 2, 1, 1, 1]))
    pool_vecs = (row3(bg_full), row3(sc_full))
    h1, uc0, (pw_grp0, pw_out0, cw_in1, cw_out1, pw_in1, pw_grp1, pw_out1) = _conv_fwd(
        p0, h0, dw_full, conv_vecs, 0, cw_out0, "conv_mix_fwd_0",
        gather([(pwg, 0), (pwo, 0), (cwi, 1), (cwo, 1), (pwi, 1), (pwg, 1), (pwo, 1)], [1, 0, 1, 0, 1, 1, 0]))
    h2, saved1, _ = _pool_fwd(h1, ng3, 1, pw_in0, pw_grp0, pool_vecs, 0, pw_out0, "pool_fwd_0")
    p2, _ = _in_fwd(h2, ng3, 2, cw_in1, "conv_in_fwd_1")
    h3, uc2, _ = _conv_fwd(p2, h2, dw_full, conv_vecs, 1, cw_out1, "conv_mix_fwd_1")
    dh, saved3, d_final_g, loss_part, _ = _pool_fwd(h3, ng3, 3, pw_in1, pw_grp1, pool_vecs, 1, pw_out1, "pool_fwd_1",
                                               final=(target, final_g[None, :]))

    dp, g_pwo1, g_pwg1, dpv1, _ = _pool_bwd(dh, saved3,pw_grp1, pool_vecs, 1, pw_out1, "pool_mix_bwd_1")
    dh, g_pwi1, dg3, (s_pwo1, s_pwg1) = _in_bwd(dp, h3, dh, ng3, 3, pw_in1, "pool_in_bwd_1",
                                                scatter([g_pwo1, g_pwg1], [0, 1]))
    dp, g_cwo1, ddw1, dcv1, (s_pwi1,) = _conv_bwd(dh, p2, uc2, dw_full, conv_vecs, 1, cw_out1, "conv_mix_bwd_1",
                                                  scatter([g_pwi1], [1]))
    dh, g_cwi1, dg2, (s_cwo1,) = _in_bwd(dp, h2, dh, ng3, 2, cw_in1, "conv_in_bwd_1", scatter([g_cwo1], [0]))
    dp, g_pwo0, g_pwg0, dpv0, (s_cwi1,) = _pool_bwd(dh, saved1,pw_grp0, pool_vecs, 0, pw_out0, "pool_mix_bwd_0",
                                                    scatter([g_cwi1], [1]))
    dh, g_pwi0, dg1, (s_pwo0, s_pwg0) = _in_bwd(dp, h1, dh, ng3, 1, pw_in0, "pool_in_bwd_0",
                                                scatter([g_pwo0, g_pwg0], [0, 1]))
    dp, g_cwo0, ddw0, dcv0, (s_pwi0, s_ddw1, s_dcv1, s_dpv0, s_dpv1, s_dg1, s_dg2, s_dg3, s_dfg, s_loss) = _conv_bwd(
        dh, p0, uc0, dw_full, conv_vecs, 0, cw_out0, "conv_mix_bwd_0",
        scatter([g_pwi0, ddw1, dcv1, dpv0, dpv1, dg1, dg2, dg3, d_final_g, loss_part],
                [1, 1, None, 1, 1, None, None, None, None, None]))
    dh, g_cwi0, dg0, _ = _in_bwd(dp, h0, dh, ng3, 0, cw_in0, "conv_in_bwd_0")
    grad_x = dh[None]
    s_cwo0, s_cwi0, s_ddw0, s_dcv0, s_dg0 = _exchange_call(
        scatter([g_cwo0, g_cwi0, ddw0, dcv0, dg0], [0, 1, 1, None, None]), "scatter_last")

    res = {}
    res["conv_w_in"] = _adamw([s_cwi0, s_cwi1], conv_w_in, m_conv_w_in, v_conv_w_in, "adamw_conv_w_in")
    res["conv_w_out"] = _adamw([s_cwo0, s_cwo1], conv_w_out, m_conv_w_out, v_conv_w_out, "adamw_conv_w_out")
    res["pool_w_in"] = _adamw([s_pwi0, s_pwi1], pool_w_in, m_pool_w_in, v_pool_w_in, "adamw_pool_w_in")
    res["pool_w_grp"] = _adamw([s_pwg0, s_pwg1], pool_w_grp, m_pool_w_grp, v_pool_w_grp, "adamw_pool_w_grp")
    res["pool_w_out"] = _adamw([s_pwo0, s_pwo1], pool_w_out, m_pool_w_out, v_pool_w_out, "adamw_pool_w_out")
    stacks = [s_ddw0, s_ddw1, s_dpv0, s_dpv1, s_dg0, s_dg1, s_dg2, s_dg3, s_dfg, s_dcv0, s_dcv1]
    small = [
        ("conv_dw", conv_dw, m_conv_dw, v_conv_dw, [(0, None, 0), (1, None, 1)]),
        ("pool_b_grp", pool_b_grp, m_pool_b_grp, v_pool_b_grp, [(2, 0, 0), (3, 0, 1)]),
        ("pool_scale", pool_scale, m_pool_scale, v_pool_scale, [(2, 1, 0), (3, 1, 1)]),
        ("norm_g", norm_g, m_norm_g, v_norm_g, [(4, 0, 0), (5, 0, 1), (6, 0, 2), (7, 0, 3)]),
        ("final_g", final_g[None, :], m_final_g[None, :], v_final_g[None, :], [(8, 0, 0)]),
        ("conv_dw_b", conv_dw_b, m_conv_dw_b, v_conv_dw_b, [(9, 0, 0), (10, 0, 1)]),
        ("conv_ln_g", conv_ln_g, m_conv_ln_g, v_conv_ln_g, [(9, 1, 0), (10, 1, 1)]),
        ("conv_ln_b", conv_ln_b, m_conv_ln_b, v_conv_ln_b, [(9, 2, 0), (10, 2, 1)]),
    ]
    small_res, loss_block = _adamw_small([s[1:] for s in small], stacks, s_loss, "adamw_small")
    for (name, *_), r in zip(small, small_res):
        res[name] = [a[0] for a in r] if name == "final_g" else r
    loss = loss_block[0, 0]

    names = ["norm_g", "final_g", "conv_w_in", "conv_dw", "conv_dw_b", "conv_ln_g", "conv_ln_b", "conv_w_out",
             "pool_w_in", "pool_w_grp", "pool_b_grp", "pool_scale", "pool_w_out"]
    return (loss, grad_x) + tuple(res[n][q] for q in range(4) for n in names)
```

```python
import jax
import jax.numpy as jnp
from jax import lax
from jax.experimental import pallas as pl
from jax.experimental.pallas import tpu as pltpu

F32 = jnp.float32
BF16 = jnp.bfloat16

RMS_EPS = 1e-6
LN_EPS = 1e-5
CONV_TAPS = 31
CONV_HALO = 32
POOL_WINDOWS = (2, 4, 8, 16)
POOL_HALO = 16
SUBLANES = 8
LANES = 128
N_DEV = 8
V7X_VMEM_LIMIT = 56 * 1024 * 1024

ADAM_LR = 0.001
ADAM_B1 = 0.9
ADAM_B2 = 0.999
ADAM_EPS = 1e-08
ADAM_WD = 0.01
ADAM_STEP = 10

MESH = pl.DeviceIdType.MESH
ANY = pl.BlockSpec(memory_space=pl.ANY)


def _dot(a, b):
    return lax.dot_general(a, b, (((1,), (0,)), ((), ())), preferred_element_type=F32)


def _dot_nt(a, b):
    return lax.dot_general(a, b, (((1,), (1,)), ((), ())), preferred_element_type=F32)


def _dot_tn(a, b):
    return lax.dot_general(a, b, (((0,), (0,)), ((), ())), preferred_element_type=F32)


def _sigmoid(x):
    return jax.nn.sigmoid(x)


def _dsilu(x, s):
    return s * (1.0 + x * (1.0 - s))


def _rows8(x):
    r, c = x.shape
    return jnp.sum(x.reshape(r // SUBLANES, SUBLANES, c), axis=0)


def _tile(t, pref):
    return pref if t >= 2 * pref else t // 2


def _const(shape, index):
    return pl.BlockSpec(shape, lambda *_: index, pipeline_mode=pl.Buffered(1))


def _params(grid_rank=1):
    return pltpu.CompilerParams(dimension_semantics=("arbitrary",) * grid_rank, vmem_limit_bytes=V7X_VMEM_LIMIT)


def _chunks(n, rc, fn):
    def step(c, carry):
        fn(pl.multiple_of(c * rc, rc))
        return carry
    lax.fori_loop(0, n, step, 0)


def _mesh_position():
    return lax.axis_index("x"), lax.axis_index("y"), lax.axis_index("c")


def _peer(j):
    x, y, c = _mesh_position()
    px = 1 - x if j & 4 else x
    py = 1 - y if j & 2 else y
    pc = 1 - c if j & 1 else c
    return (px, py, pc), 4 * px + 2 * py + pc


def _block(ref, axis, index, size):
    idx = [slice(None)] * len(ref.shape)
    idx[axis] = pl.ds(pl.multiple_of(index * size, size), size)
    return ref.at[tuple(idx)]


class _Exchange:
    def __init__(self, kind, arrays, axes):
        self.kind, self.axes = kind, list(axes)
        self.arrays = [a[0] if isinstance(a, tuple) else a for a in arrays]
        self.layers = [a[1] if isinstance(a, tuple) else None for a in arrays]
        self.n = len(self.arrays)
        self.blk, self.out_shapes = [], []
        for a, layer, ax in zip(self.arrays, self.layers, self.axes):
            s = list(a.shape if layer is None else a.shape[1:])
            if kind == "gather":
                self.blk.append(s[ax])
                s[ax] *= N_DEV
                self.out_shapes.append(jax.ShapeDtypeStruct(tuple(s), a.dtype))
            else:
                if ax is not None:
                    s[ax] //= N_DEV
                    self.blk.append(s[ax])
                else:
                    self.blk.append(None)
                self.out_shapes.append(jax.ShapeDtypeStruct((N_DEV,) + tuple(s), a.dtype))

    def sem_shapes(self):
        return [pltpu.SemaphoreType.DMA((N_DEV - 1, self.n)), pltpu.SemaphoreType.DMA((N_DEV - 1, self.n)),
                pltpu.SemaphoreType.DMA((self.n,))]

    def _src(self, ins, k, owner):
        ref = ins[k] if self.layers[k] is None else ins[k].at[self.layers[k]]
        if self.kind == "gather" or self.axes[k] is None:
            return ref
        return _block(ref, self.axes[k], owner, self.blk[k])

    def _dst(self, outs, k, sender):
        if self.kind == "gather":
            return _block(outs[k], self.axes[k], sender, self.blk[k])
        return outs[k].at[sender]

    def _copies(self, ins, outs, sems, arriving):
        send, recv, loc = sems
        x, y, c = _mesh_position()
        me = 4 * x + 2 * y + c
        if not arriving:
            local = [pltpu.make_async_copy(self._src(ins, k, me), self._dst(outs, k, me), loc.at[k])
                     for k in range(self.n)]
        else:
            local = []
        remote = []
        for j in range(1, N_DEV):
            peer, peer_id = _peer(j)
            for k in range(self.n):
                owner, sender = (me, peer_id) if arriving else (peer_id, me)
                remote.append(pltpu.make_async_remote_copy(
                    src_ref=self._src(ins, k, owner), dst_ref=self._dst(outs, k, sender),
                    send_sem=send.at[j - 1, k], recv_sem=recv.at[j - 1, k], device_id=peer, device_id_type=MESH))
        return local, remote

    def start(self, ins, outs, sems):
        local, sends = self._copies(ins, outs, sems, arriving=False)
        for cp in local + sends:
            cp.start()

    def finish(self, ins, outs, sems):
        for cp in self._copies(ins, outs, sems, arriving=True)[1]:
            cp.wait_recv()
        local, sends = self._copies(ins, outs, sems, arriving=False)
        for cp in sends:
            cp.wait_send()
        for cp in local:
            cp.wait()


def _exchange_call(ex, name):
    def body(*refs):
        ins, outs, sems = refs[:ex.n], refs[ex.n:2 * ex.n], refs[2 * ex.n:]
        ex.start(ins, outs, sems)
        ex.finish(ins, outs, sems)

    return pl.pallas_call(body, name=name, in_specs=[ANY] * ex.n, out_specs=[ANY] * ex.n,
                          out_shape=ex.out_shapes, scratch_shapes=ex.sem_shapes())(*ex.arrays)


def _gather_via_sibling_call(ex, name):
    n = ex.n

    def body(*refs):
        ins, outs = refs[:n], refs[n:2 * n]
        send, recv, loc = refs[2 * n:]
        x, y, c = _mesh_position()
        ident = lambda px, py, pc: 4 * px + 2 * py + pc
        me, sibling = ident(x, y, c), (x, y, 1 - c)
        chips = [(1 - x, y), (x, 1 - y), (1 - x, 1 - y)]

        def copy(row, k, block, to, src=None):
            place = ex._dst(outs, k, block)
            return pltpu.make_async_remote_copy(
                src_ref=place if src is None else src, dst_ref=place,
                send_sem=send.at[row, k], recv_sem=recv.at[row, k], device_id=to, device_id_type=MESH)

        local = [pltpu.make_async_copy(ex._src(ins, k, me), ex._dst(outs, k, me), loc.at[k]) for k in range(n)]
        first = []
        for k in range(n):
            mine = ex._src(ins, k, me)
            first.append(copy(0, k, me, sibling, src=mine))
            first += [copy(1 + j, k, me, (*chip, c), src=mine) for j, chip in enumerate(chips)]
        for cp in local + first:
            cp.start()
        passed = []
        for j, chip in enumerate(chips):
            for k in range(n):
                copy(1 + j, k, ident(*chip, c), sibling).wait_recv()
                passed.append(copy(4 + j, k, ident(*chip, c), sibling))
                passed[-1].start()
        for k in range(n):
            copy(0, k, ident(x, y, 1 - c), sibling).wait_recv()
            for j, chip in enumerate(chips):
                copy(4 + j, k, ident(*chip, 1 - c), sibling).wait_recv()
        for cp in first + passed:
            cp.wait_send()
        for cp in local:
            cp.wait()

    return pl.pallas_call(body, name=name, in_specs=[ANY] * n, out_specs=[ANY] * n, out_shape=ex.out_shapes,
                          scratch_shapes=ex.sem_shapes())(*ex.arrays)


def _launch(body, name, nt, in_specs, out_specs, out_shape, scratch_shapes, args, ex=None):
    if ex is None:
        outs = pl.pallas_call(body, name=name, grid=(nt,), in_specs=in_specs, out_specs=out_specs,
                              out_shape=out_shape, scratch_shapes=scratch_shapes, compiler_params=_params())(*args)
        return list(outs), []
    n_in, n_out, n_scr = len(in_specs), len(out_specs), len(scratch_shapes)

    def riding(*refs):
        a, xa = refs[:n_in], refs[n_in:n_in + ex.n]
        o = refs[n_in + ex.n:n_in + ex.n + n_out]
        xo = refs[n_in + ex.n + n_out:n_in + 2 * ex.n + n_out]
        s = refs[n_in + 2 * ex.n + n_out:n_in + 2 * ex.n + n_out + n_scr]
        sems = refs[n_in + 2 * ex.n + n_out + n_scr:]
        i = pl.program_id(0)

        @pl.when(i == 0)
        def _():
            ex.start(xa, xo, sems)

        body(*a, *o, *s)

        @pl.when(i == nt - 1)
        def _():
            ex.finish(xa, xo, sems)

    outs = pl.pallas_call(
        riding, name=name, grid=(nt,),
        in_specs=list(in_specs) + [ANY] * ex.n, out_specs=list(out_specs) + [ANY] * ex.n,
        out_shape=list(out_shape) + ex.out_shapes, scratch_shapes=list(scratch_shapes) + ex.sem_shapes(),
        compiler_params=_params())(*args, *ex.arrays)
    return list(outs[:n_out]), list(outs[n_out:])


def _in_fwd(h, norm_g, layer, w_in, name, ex=None):
    t, d = h.shape
    n = w_in.shape[-1]
    tm = _tile(t, 512)

    def body(h_ref, g_ref, w_ref, p_ref):
        x = h_ref[...]
        r = lax.rsqrt(jnp.mean(x * x, axis=-1, keepdims=True) + RMS_EPS)
        p_ref[...] = _dot((x * r * g_ref[...]).astype(BF16), w_ref[...])

    (p,), xouts = _launch(
        body, name, t // tm,
        [pl.BlockSpec((tm, d), lambda i: (i, 0)), _const((None, 1, d), (layer, 0, 0)), _const((d, n), (0, 0))],
        [pl.BlockSpec((tm, n), lambda i: (i, 0))],
        [jax.ShapeDtypeStruct((t, n), F32)], [], (h, norm_g, w_in), ex)
    return p, xouts


def _layernorm_rows(uc, lng, lnb):
    mu = jnp.mean(uc, axis=-1, keepdims=True)
    xc = uc - mu
    rstd = lax.rsqrt(jnp.mean(xc * xc, axis=-1, keepdims=True) + LN_EPS)
    xhat = xc * rstd
    return xhat, rstd, xhat * lng + lnb


def _conv_fwd(p, h, dw, vecs, layer, w_out, name, ex=None):
    t, d = h.shape
    e = w_out.shape[0]
    tm = _tile(t, 256)
    rc = _tile(tm, 128)

    def body(p_ref, h_ref, dw_ref, dwb_ref, lng_ref, lnb_ref, wo_ref, ho_ref, xh_ref, rs_ref, su_ref, sz_ref,
             us_scr, uc_ref):
        i = pl.program_id(0)

        @pl.when(i == 0)
        def _():
            us_scr[:, pl.ds(0, CONV_HALO), :] = jnp.zeros((SUBLANES, CONV_HALO, e), F32)

        @pl.when(i > 0)
        def _():
            us_scr[:, pl.ds(0, CONV_HALO), :] = us_scr[:, pl.ds(tm, CONV_HALO), :]

        us_scr[0, pl.ds(CONV_HALO, tm), :] = p_ref[:, pl.ds(0, e)] * _sigmoid(p_ref[:, pl.ds(e, e)])
        for r in range(1, SUBLANES):
            us_scr[r, pl.ds(CONV_HALO, tm), :] = us_scr[0, pl.ds(CONV_HALO - r, tm), :]

        def c_conv(base):
            for lt in range(e // LANES):
                cols = pl.ds(lt * LANES, LANES)
                acc = jnp.broadcast_to(dwb_ref[:, cols], (rc, LANES))
                for r in range(SUBLANES):
                    nq = (CONV_TAPS - 1 - r) // SUBLANES + 1
                    lo = SUBLANES * (nq - 1)
                    win = us_scr[r, pl.ds(pl.multiple_of(CONV_HALO + base - lo, SUBLANES), rc + lo), cols]
                    for q in range(nq):
                        k = CONV_TAPS - 1 - (SUBLANES * q + r)
                        at = lo - SUBLANES * q
                        acc = acc + dw_ref[pl.ds(k, 1), cols] * win[at:at + rc, :]
                uc_ref[pl.ds(base, rc), cols] = acc
        _chunks(tm // rc, rc, c_conv)

        xhat, rstd, ul = _layernorm_rows(uc_ref[...], lng_ref[...], lnb_ref[...])
        z = p_ref[:, pl.ds(2 * e, e)]
        sg_u = _sigmoid(ul)
        sg_z = _sigmoid(z)
        xh_ref[...] = xhat
        rs_ref[...] = jnp.broadcast_to(rstd, rs_ref.shape)
        su_ref[...] = sg_u
        sz_ref[...] = sg_z
        v = ((ul * sg_u) * (z * sg_z)).astype(BF16)
        ho_ref[...] = h_ref[...] + _dot(v, wo_ref[...])

    vec = _const((None, 1, e), (layer, 0, 0))
    row = lambda i: (i, 0)
    outs, xouts = _launch(
        body, name, t // tm,
        [pl.BlockSpec((tm, 3 * e), row), pl.BlockSpec((tm, d), row),
         _const((None, CONV_TAPS, e), (layer, 0, 0)), vec, vec, vec, _const((e, d), (0, 0))],
        [pl.BlockSpec((tm, d), row), pl.BlockSpec((tm, e), row), pl.BlockSpec((tm, LANES), row),
         pl.BlockSpec((tm, e), row), pl.BlockSpec((tm, e), row)],
        [jax.ShapeDtypeStruct((t, d), F32), jax.ShapeDtypeStruct((t, e), F32), jax.ShapeDtypeStruct((t, LANES), F32),
         jax.ShapeDtypeStruct((t, e), F32), jax.ShapeDtypeStruct((t, e), F32)],
        [pltpu.VMEM((SUBLANES, tm + CONV_HALO, e), F32), pltpu.VMEM((tm, e), F32)], (p, h, dw, *vecs, w_out), ex)
    return outs[0], tuple(outs[1:]), xouts


def _conv_bwd(dho, p, saved, dw, vecs, layer, w_out, name, ex=None):
    t, d = dho.shape
    e = w_out.shape[0]
    tm = _tile(t, 256)
    nt = t // tm
    rc = 16
    vec0 = CONV_TAPS + 1

    def body(dho_ref, p_ref, xh_ref, rs_ref, su_ref, sz_ref, dw_ref, lng_ref, lnb_ref, wo_ref,
             dp_ref, dwo_ref, ddw_ref, dvec_ref, ds_scr, acc_scr, dwo_scr):
        i = pl.program_id(0)

        @pl.when(i == 0)
        def _():
            dwo_scr[...] = jnp.zeros_like(dwo_scr)
            acc_scr[...] = jnp.zeros_like(acc_scr)
            ds_scr[:, pl.ds(tm, CONV_HALO), :] = jnp.zeros((SUBLANES, CONV_HALO, e), F32)

        @pl.when(i > 0)
        def _():
            ds_scr[:, pl.ds(tm, CONV_HALO), :] = ds_scr[:, pl.ds(0, CONV_HALO), :]

        lng = lng_ref[...]
        lnb = lnb_ref[...]

        xhat = xh_ref[...]
        rstd = rs_ref[:, pl.ds(0, 1)]
        ul = xhat * lng + lnb
        z = p_ref[:, pl.ds(2 * e, e)]
        sg_u = su_ref[...]
        sg_z = sz_ref[...]
        s_u = ul * sg_u
        s_z = z * sg_z
        v = (s_u * s_z).astype(BF16)
        dy = dho_ref[...].astype(BF16)
        dv = _dot_nt(dy, wo_ref[...])
        dwo_scr[...] += _dot_tn(v, dy)

        dul = dv * s_z * _dsilu(ul, sg_u)
        dp_ref[:, pl.ds(2 * e, e)] = (dv * s_u * _dsilu(z, sg_z)).astype(BF16)
        acc_scr[pl.ds((vec0 + 1) * SUBLANES, SUBLANES), :] += _rows8(dul * xhat)
        acc_scr[pl.ds((vec0 + 2) * SUBLANES, SUBLANES), :] += _rows8(dul)
        dxh = dul * lng
        duc = rstd * (dxh - jnp.mean(dxh, axis=-1, keepdims=True)
                      - xhat * jnp.mean(dxh * xhat, axis=-1, keepdims=True))
        acc_scr[pl.ds(vec0 * SUBLANES, SUBLANES), :] += _rows8(duc)
        ds_scr[0, pl.ds(0, tm), :] = duc
        for r in range(1, SUBLANES):
            ds_scr[r, pl.ds(0, tm), :] = ds_scr[0, pl.ds(r, tm), :]

        def c_conv(base):
            rows = pl.ds(base, rc)
            a = p_ref[rows, pl.ds(0, e)]
            b = p_ref[rows, pl.ds(e, e)]
            sb = _sigmoid(b)
            u = a * sb
            du = jnp.zeros((rc, e), F32)
            for o in range(CONV_TAPS):
                q, r = divmod(o, SUBLANES)
                k = CONV_TAPS - 1 - o
                sh = ds_scr[r, pl.ds(pl.multiple_of(base + SUBLANES * q, SUBLANES), rc), :]
                du = du + dw_ref[pl.ds(k, 1), :] * sh
                acc_scr[pl.ds(k * SUBLANES, SUBLANES), :] += _rows8(u * sh)
            dp_ref[rows, pl.ds(0, e)] = (du * sb).astype(BF16)
            dp_ref[rows, pl.ds(e, e)] = (du * u * (1.0 - sb)).astype(BF16)
        _chunks(tm // rc, rc, c_conv)

        @pl.when(i == nt - 1)
        def _():
            dwo_ref[...] = dwo_scr[...].astype(BF16)
            slot_sum = lambda k: jnp.sum(acc_scr[pl.ds(k * SUBLANES, SUBLANES), :], axis=0, keepdims=True)
            for k in range(CONV_TAPS):
                ddw_ref[pl.ds(k, 1), :] = slot_sum(k)
            dvec_ref[...] = jnp.zeros_like(dvec_ref)
            for k in range(3):
                dvec_ref[pl.ds(k, 1), :] = slot_sum(vec0 + k)

    rev = lambda i: (nt - 1 - i, 0)
    vec = _const((None, 1, e), (layer, 0, 0))
    (dp, dwo, ddw, dvec), xouts = _launch(
        body, name, nt,
        [pl.BlockSpec((tm, d), rev), pl.BlockSpec((tm, 3 * e), rev), pl.BlockSpec((tm, e), rev),
         pl.BlockSpec((tm, LANES), rev), pl.BlockSpec((tm, e), rev), pl.BlockSpec((tm, e), rev),
         _const((None, CONV_TAPS, e), (layer, 0, 0)), vec, vec, _const((e, d), (0, 0))],
        [pl.BlockSpec((tm, 3 * e), rev), _const((e, d), (0, 0)), _const((CONV_TAPS, e), (0, 0)),
         _const((SUBLANES, e), (0, 0))],
        [jax.ShapeDtypeStruct((t, 3 * e), BF16), jax.ShapeDtypeStruct((e, d), BF16),
         jax.ShapeDtypeStruct((CONV_TAPS, e), F32), jax.ShapeDtypeStruct((SUBLANES, e), F32)],
        [pltpu.VMEM((SUBLANES, tm + CONV_HALO, e), F32), pltpu.VMEM(((vec0 + 3) * SUBLANES, e), F32),
         pltpu.VMEM((e, d), F32)],
        (dho, p, *saved, dw, vecs[1], vecs[2], w_out), ex)
    return dp, dwo, ddw, dvec, xouts


def _inv_count(tile, tm, w):
    tpos = tile * tm + lax.broadcasted_iota(jnp.int32, (tm, 1), 0)
    return 1.0 / jnp.minimum(tpos + 1, w).astype(F32)


def _pool_d_group(ue_scr, tile, tm, gc, g):
    w = POOL_WINDOWS[g]
    win = ue_scr[:, pl.ds(g * gc, gc)]
    s = win
    sh = 1
    while sh < w:
        s = s + pltpu.roll(s, sh, axis=0)
        sh *= 2
    return s[POOL_HALO:, :] * _inv_count(tile, tm, w) - win[POOL_HALO:, :]


def _pool_d(ue_scr, tile, tm, gc):
    return [_pool_d_group(ue_scr, tile, tm, gc, g) for g in range(len(POOL_WINDOWS))]


def _final_rows(ho, tg_ref, fg_ref, dh_ref, acc_scr, lacc_scr):
    d = ho.shape[-1]
    r = lax.rsqrt(jnp.mean(ho * ho, axis=-1, keepdims=True) + RMS_EPS)
    nrm = ho * r
    err = nrm * fg_ref[...] - tg_ref[...]
    lacc_scr[...] += _rows8(err * err)
    dy = err * (1.0 / d)
    acc_scr[...] += _rows8(dy * nrm)
    dq = dy * fg_ref[...]
    dh_ref[...] = r * (dq - nrm * jnp.mean(dq * nrm, axis=-1, keepdims=True))


def _pool_fwd(h, norm_g, nlayer, w_in, w_grp, vecs, layer, w_out, name, ex=None, final=None):
    t, d = h.shape
    e = w_out.shape[0]
    ng = len(POOL_WINDOWS)
    gc = e // ng
    tm = _tile(t, 512)
    nt = t // tm
    cw = 2 * e // (2 * ng)

    def layer_rows(i, hn_ref, hc_ref, g_ref, wi_ref, wg_ref, bg_ref, sc_ref, wo_ref, p_ref,
                   ue_scr, y_scr, pbuf, z_scr, hn_scr):
        tile = jnp.maximum(i - 1, 0)

        @pl.when(i == 0)
        def _():
            pbuf[...] = jnp.zeros_like(pbuf)

        @pl.when(i <= 1)
        def _():
            ue_scr[pl.ds(0, POOL_HALO), :] = jnp.zeros((POOL_HALO, e), F32)

        @pl.when(i > 1)
        def _():
            ue_scr[pl.ds(0, POOL_HALO), :] = ue_scr[pl.ds(tm, POOL_HALO), :]

        ue_scr[pl.ds(POOL_HALO, tm), :] = pbuf[:, pl.ds(0, e)]
        z_scr[...] = pbuf[:, pl.ds(e, e)]

        x = hn_ref[...]
        r = lax.rsqrt(jnp.mean(x * x, axis=-1, keepdims=True) + RMS_EPS)
        hn_scr[...] = (x * r * g_ref[...]).astype(BF16)

        def project(c):
            part = _dot(hn_scr[...], wi_ref[:, pl.ds(c * cw, cw)])
            pbuf[:, pl.ds(c * cw, cw)] = part
            p_ref[:, pl.ds(c * cw, cw)] = part

        for g in range(ng):
            project(2 * g)
            cols = pl.ds(g * gc, gc)
            dg = _pool_d_group(ue_scr, tile, tm, gc, g)
            z = z_scr[:, cols]
            y1 = (_dot(dg.astype(BF16), wg_ref[g]) + bg_ref[:, cols]) * sc_ref[:, cols]
            y_scr[:, cols] = (y1 * (z * _sigmoid(z))).astype(BF16)
            project(2 * g + 1)

        return hc_ref[...] + _dot(y_scr[...], wo_ref[...])

    nxt = lambda i: (jnp.minimum(i, nt - 1), 0)
    cur = lambda i: (jnp.maximum(i - 1, 0), 0)
    vec = _const((None, 1, e), (layer, 0, 0))
    in_specs = [pl.BlockSpec((tm, d), nxt), pl.BlockSpec((tm, d), cur), _const((None, 1, d), (nlayer, 0, 0)),
                _const((d, 2 * e), (0, 0)), _const((ng, gc, gc), (0, 0, 0)), vec, vec, _const((e, d), (0, 0))]
    scratch = [pltpu.VMEM((tm + POOL_HALO, e), F32), pltpu.VMEM((tm, e), BF16), pltpu.VMEM((tm, 2 * e), F32),
               pltpu.VMEM((tm, e), F32), pltpu.VMEM((tm, d), BF16)]
    args = (h, h, norm_g, w_in, w_grp, *vecs, w_out)

    if final is None:
        def body(hn_ref, hc_ref, g_ref, wi_ref, wg_ref, bg_ref, sc_ref, wo_ref, ho_ref, p_ref, *scr):
            ho_ref[...] = layer_rows(pl.program_id(0), hn_ref, hc_ref, g_ref, wi_ref, wg_ref, bg_ref, sc_ref, wo_ref,
                                     p_ref, *scr)

        (ho, p), xouts = _launch(
            body, name, nt + 1, in_specs, [pl.BlockSpec((tm, d), cur), pl.BlockSpec((tm, 2 * e), nxt)],
            [jax.ShapeDtypeStruct((t, d), F32), jax.ShapeDtypeStruct((t, 2 * e), F32)], scratch, args, ex)
        return ho, p, xouts

    target, final_g = final

    def body(hn_ref, hc_ref, g_ref, wi_ref, wg_ref, bg_ref, sc_ref, wo_ref, tg_ref, fg_ref,
             dh_ref, p_ref, dfg_ref, loss_ref, ue_scr, y_scr, pbuf, z_scr, hn_scr, acc_scr, lacc_scr):
        i = pl.program_id(0)

        @pl.when(i <= 1)
        def _():
            acc_scr[...] = jnp.zeros_like(acc_scr)
            lacc_scr[...] = jnp.zeros_like(lacc_scr)

        ho = layer_rows(i, hn_ref, hc_ref, g_ref, wi_ref, wg_ref, bg_ref, sc_ref, wo_ref, p_ref,
                        ue_scr, y_scr, pbuf, z_scr, hn_scr)
        _final_rows(ho, tg_ref, fg_ref, dh_ref, acc_scr, lacc_scr)

        @pl.when(i == nt)
        def _():
            dfg_ref[...] = jnp.zeros_like(dfg_ref)
            dfg_ref[pl.ds(0, 1), :] = jnp.sum(acc_scr[...], axis=0, keepdims=True)
            loss_ref[...] = jnp.broadcast_to(jnp.sum(lacc_scr[...]) * (0.5 / d), loss_ref.shape)

    (dh, p, dfg, loss), xouts = _launch(
        body, name, nt + 1, in_specs + [pl.BlockSpec((tm, d), cur), _const((1, d), (0, 0))],
        [pl.BlockSpec((tm, d), cur), pl.BlockSpec((tm, 2 * e), nxt), _const((SUBLANES, d), (0, 0)),
         _const((SUBLANES, LANES), (0, 0))],
        [jax.ShapeDtypeStruct((t, d), F32), jax.ShapeDtypeStruct((t, 2 * e), F32),
         jax.ShapeDtypeStruct((SUBLANES, d), F32), jax.ShapeDtypeStruct((SUBLANES, LANES), F32)],
        scratch + [pltpu.VMEM((SUBLANES, d), F32), pltpu.VMEM((SUBLANES, d), F32)],
        args + (target, final_g), ex)
    return dh, p, dfg, loss, xouts


def _pool_bwd(dho, p, w_grp, vecs, layer, w_out, name, ex=None):
    t, d = dho.shape
    e = w_out.shape[0]
    ng = len(POOL_WINDOWS)
    gc = e // ng
    tm = _tile(t, 512)
    nt = t // tm
    hb = tm // POOL_HALO

    def body(dho_ref, p_ref, ph_ref, wg_ref, bg_ref, sc_ref, wo_ref, dp_ref, dwo_ref, dwg_ref, dvec_ref,
             ue_scr, ee_scr, acc_scr, dwo_scr, dwg_scr):
        i = pl.program_id(0)
        tile = nt - 1 - i

        @pl.when(i == 0)
        def _():
            dwo_scr[...] = jnp.zeros_like(dwo_scr)
            dwg_scr[...] = jnp.zeros_like(dwg_scr)
            acc_scr[...] = jnp.zeros_like(acc_scr)
            ee_scr[pl.ds(tm, POOL_HALO), :] = jnp.zeros((POOL_HALO, e), F32)

        @pl.when(i > 0)
        def _():
            ee_scr[pl.ds(tm, POOL_HALO), :] = ee_scr[pl.ds(0, POOL_HALO), :]

        @pl.when(tile == 0)
        def _():
            ue_scr[pl.ds(0, POOL_HALO), :] = jnp.zeros((POOL_HALO, e), F32)

        @pl.when(tile > 0)
        def _():
            ue_scr[pl.ds(0, POOL_HALO), :] = ph_ref[:, pl.ds(0, e)]

        ue_scr[pl.ds(POOL_HALO, tm), :] = p_ref[:, pl.ds(0, e)]

        bg = bg_ref[...]
        sc = sc_ref[...]
        ds = [dg.astype(BF16) for dg in _pool_d(ue_scr, tile, tm, gc)]
        ob = jnp.concatenate([_dot(ds[g], wg_ref[g]) for g in range(ng)], axis=1) + bg
        z = p_ref[:, pl.ds(e, e)]
        sg_z = _sigmoid(z)
        s_z = z * sg_z
        y1 = ob * sc
        dy = dho_ref[...].astype(BF16)
        dy2 = _dot_nt(dy, wo_ref[...])
        dwo_scr[...] += _dot_tn((y1 * s_z).astype(BF16), dy)
        dy1 = dy2 * s_z
        dp_ref[:, pl.ds(e, e)] = (dy2 * y1 * _dsilu(z, sg_z)).astype(BF16)
        acc_scr[pl.ds(SUBLANES, SUBLANES), :] += _rows8(dy1 * ob)
        do = dy1 * sc
        acc_scr[pl.ds(0, SUBLANES), :] += _rows8(do)

        n = tm + POOL_HALO
        for g, w in enumerate(POOL_WINDOWS):
            cols = pl.ds(g * gc, gc)
            do_g = do[:, g * gc:(g + 1) * gc].astype(BF16)
            dwg_scr[g] += _dot_tn(ds[g], do_g)
            dd = _dot_nt(do_g, wg_ref[g])
            ee_scr[pl.ds(0, tm), cols] = dd * _inv_count(tile, tm, w)
            s = ee_scr[:, cols]
            sh = 1
            while sh < w:
                s = s + pltpu.roll(s, n - sh, axis=0)
                sh *= 2
            dp_ref[:, cols] = (s[:tm, :] - dd).astype(BF16)

        @pl.when(i == nt - 1)
        def _():
            dwo_ref[...] = dwo_scr[...].astype(BF16)
            dwg_ref[...] = dwg_scr[...].astype(BF16)
            dvec_ref[...] = jnp.zeros_like(dvec_ref)
            for k in range(2):
                dvec_ref[pl.ds(k, 1), :] = jnp.sum(acc_scr[pl.ds(k * SUBLANES, SUBLANES), :], axis=0, keepdims=True)

    rev = lambda i: (nt - 1 - i, 0)
    vec = _const((None, 1, e), (layer, 0, 0))
    (dp, dwo, dwg, dvec), xouts = _launch(
        body, name, nt,
        [pl.BlockSpec((tm, d), rev), pl.BlockSpec((tm, 2 * e), rev),
         pl.BlockSpec((POOL_HALO, 2 * e), lambda i: (jnp.maximum((nt - 1 - i) * hb - 1, 0), 0)),
         _const((ng, gc, gc), (0, 0, 0)), vec, vec, _const((e, d), (0, 0))],
        [pl.BlockSpec((tm, 2 * e), rev), _const((e, d), (0, 0)), _const((ng, gc, gc), (0, 0, 0)),
         _const((SUBLANES, e), (0, 0))],
        [jax.ShapeDtypeStruct((t, 2 * e), BF16), jax.ShapeDtypeStruct((e, d), BF16),
         jax.ShapeDtypeStruct((ng, gc, gc), BF16), jax.ShapeDtypeStruct((SUBLANES, e), F32)],
        [pltpu.VMEM((tm + POOL_HALO, e), F32), pltpu.VMEM((tm + POOL_HALO, e), F32),
         pltpu.VMEM((2 * SUBLANES, e), F32), pltpu.VMEM((e, d), F32), pltpu.VMEM((ng, gc, gc), F32)],
        (dho, p, p, w_grp, *vecs, w_out), ex)
    return dp, dwo, dwg, dvec, xouts


def _in_bwd(dp, h, dho, norm_g, layer, w_in, name, ex=None):
    t, d = h.shape
    n = w_in.shape[-1]
    tm = _tile(t, 512)
    nt = t // tm

    def body(dp_ref, h_ref, dho_ref, g_ref, w_ref, dh_ref, dw_ref, dg_ref, acc_scr, dw_scr):
        i = pl.program_id(0)

        @pl.when(i == 0)
        def _():
            dw_scr[...] = jnp.zeros_like(dw_scr)
            acc_scr[...] = jnp.zeros_like(acc_scr)

        x = h_ref[...]
        r = lax.rsqrt(jnp.mean(x * x, axis=-1, keepdims=True) + RMS_EPS)
        nrm = x * r
        dp = dp_ref[...]
        dhn = _dot_nt(dp, w_ref[...])
        dw_scr[...] += _dot_tn((nrm * g_ref[...]).astype(BF16), dp)
        acc_scr[...] += _rows8(dhn * nrm)
        dq = dhn * g_ref[...]
        dh_ref[...] = dho_ref[...] + r * (dq - nrm * jnp.mean(dq * nrm, axis=-1, keepdims=True))

        @pl.when(i == nt - 1)
        def _():
            dw_ref[...] = dw_scr[...].astype(BF16)
            dg_ref[...] = jnp.zeros_like(dg_ref)
            dg_ref[pl.ds(0, 1), :] = jnp.sum(acc_scr[...], axis=0, keepdims=True)

    (dh, dw, dg), xouts = _launch(
        body, name, nt,
        [pl.BlockSpec((tm, n), lambda i: (i, 0)), pl.BlockSpec((tm, d), lambda i: (i, 0)),
         pl.BlockSpec((tm, d), lambda i: (i, 0)), _const((None, 1, d), (layer, 0, 0)), _const((d, n), (0, 0))],
        [pl.BlockSpec((tm, d), lambda i: (i, 0)), _const((d, n), (0, 0)), _const((SUBLANES, d), (0, 0))],
        [jax.ShapeDtypeStruct((t, d), F32), jax.ShapeDtypeStruct((d, n), BF16),
         jax.ShapeDtypeStruct((SUBLANES, d), F32)],
        [pltpu.VMEM((SUBLANES, d), F32), pltpu.VMEM((d, n), F32)],
        (dp, h, dho, norm_g, w_in), ex)
    return dh, dw, dg, xouts


def _adam_update(g, w, m, v):
    c1 = 1.0 / (1.0 - ADAM_B1 ** ADAM_STEP)
    c2 = 1.0 / (1.0 - ADAM_B2 ** ADAM_STEP)
    nm = ADAM_B1 * m + (1.0 - ADAM_B1) * g
    nv = ADAM_B2 * v + (1.0 - ADAM_B2) * (g * g)
    return -ADAM_LR * ((nm * c1) / (jnp.sqrt(nv * c2) + ADAM_EPS) + ADAM_WD * w), nm, nv


def _adamw_small(params, stacks, loss_stack, name):
    ns, npar = len(stacks), len(params)

    def body(*refs):
        st = refs[:ns]
        pr = refs[ns:ns + 3 * npar]
        ls_ref = refs[ns + 3 * npar]
        outs = refs[ns + 3 * npar + 1:ns + 7 * npar + 1]
        loss_ref = refs[ns + 7 * npar + 1]
        for q, (w, _, _, pieces) in enumerate(params):
            w_ref, m_ref, v_ref = pr[3 * q:3 * q + 3]
            g_ref, d_ref, nm_ref, nv_ref = outs[4 * q:4 * q + 4]
            for s, row, slab in pieces:
                if w.ndim == 3:
                    take = lambda k: st[s][k]
                    at = slab
                else:
                    take = lambda k: st[s][k, pl.ds(row, 1), :]
                    at = (pl.ds(slab, 1), slice(None))
                g = take(0)
                for k in range(1, N_DEV):
                    g = g + take(k)
                g_ref[at] = g
                d_ref[at], nm_ref[at], nv_ref[at] = _adam_update(g, w_ref[at], m_ref[at], v_ref[at])
        tot = ls_ref[0]
        for k in range(1, N_DEV):
            tot = tot + ls_ref[k]
        loss_ref[...] = tot

    flat = [a for (w, m, v, _) in params for a in (w, m, v)]
    out_shape = [jax.ShapeDtypeStruct(w.shape, F32) for (w, _, _, _) in params for _ in range(4)]
    whole = pl.BlockSpec(memory_space=pltpu.VMEM)
    outs = pl.pallas_call(
        body, name=name, in_specs=[whole] * (ns + 3 * npar + 1), out_specs=[whole] * (4 * npar + 1),
        out_shape=out_shape + [jax.ShapeDtypeStruct(loss_stack.shape[1:], F32)],
    )(*stacks, *flat, loss_stack)
    return [outs[4 * q:4 * q + 4] for q in range(npar)], outs[-1]


def _adamw(stacks, w, m, v, name):
    nl = len(stacks)
    shp = w.shape
    c = shp[-1]
    r = 1
    for s in shp[1:-1]:
        r *= s
    tr = r
    for cand in (512, 256, 128, 64, 32, 16):
        if r % cand == 0 and r > cand:
            tr = cand
            break
    nrb = r // tr

    def body(*refs):
        s_refs = refs[:nl]
        w_ref, m_ref, v_ref, g_ref, d_ref, nm_ref, nv_ref = refs[nl:]
        layer = pl.program_id(0)
        for l in range(nl):
            @pl.when(layer == l)
            def _(l=l):
                g = s_refs[l][0].astype(F32)
                for k in range(1, N_DEV):
                    g = g + s_refs[l][k].astype(F32)
                g_ref[...] = g
                d_ref[...], nm_ref[...], nv_ref[...] = _adam_update(g, w_ref[...], m_ref[...], v_ref[...])

    def stack_spec(l):
        return pl.BlockSpec((N_DEV, tr, c),
                            lambda j, i: (0, jnp.where(j == l, i, jnp.where(j < l, 0, nrb - 1)), 0))

    spec = pl.BlockSpec((None, tr, c), lambda j, i: (j, i, 0))
    outs = pl.pallas_call(
        body, name=name, grid=(nl, nrb),
        in_specs=[stack_spec(l) for l in range(nl)] + [spec, spec, spec],
        out_specs=[spec] * 4,
        out_shape=[jax.ShapeDtypeStruct((nl, r, c), F32)] * 4,
        compiler_params=_params(2),
    )(*[s.reshape(N_DEV, r, c) for s in stacks], w.reshape(nl, r, c), m.reshape(nl, r, c), v.reshape(nl, r, c))
    return [o.reshape(shp) for o in outs]


def kernel(x, norm_g, final_g, conv_w_in, conv_dw, conv_dw_b, conv_ln_g, conv_ln_b, conv_w_out, pool_w_in, pool_w_grp, pool_b_grp, pool_scale, pool_w_out, loss_target, m_norm_g, m_final_g, m_conv_w_in, m_conv_dw, m_conv_dw_b, m_conv_ln_g, m_conv_ln_b, m_conv_w_out, m_pool_w_in, m_pool_w_grp, m_pool_b_grp, m_pool_scale, m_pool_w_out, v_norm_g, v_final_g, v_conv_w_in, v_conv_dw, v_conv_dw_b, v_conv_ln_g, v_conv_ln_b, v_conv_w_out, v_pool_w_in, v_pool_w_grp, v_pool_b_grp, v_pool_scale, v_pool_w_out):
    h0 = x[0]
    target = loss_target[0]
    ng3 = norm_g[:, None, :]
    row3 = lambda a: a[:, None, :]
    conv_vecs = (row3(conv_dw_b), row3(conv_ln_g), row3(conv_ln_b))
    gather = lambda arrays, axes: _Exchange("gather", arrays, axes)
    scatter = lambda arrays, axes: _Exchange("scatter", arrays, axes)

    cwi, cwo, pwi = conv_w_in.astype(BF16), conv_w_out.astype(BF16), pool_w_in.astype(BF16)
    pwg, pwo = pool_w_grp.astype(BF16), pool_w_out.astype(BF16)

    (cw_in0,) = _gather_via_sibling_call(gather([(cwi, 0)], [1]), "gather_first")
    p0, (cw_out0, dw_full, bg_full, sc_full, pw_in0) = _in_fwd(
        h0, ng3, 0, cw_in0, "conv_in_fwd_0",
        gather([(cwo, 0), conv_dw, pool_b_grp, pool_scale, (pwi, 0)], [0, 2, 1, 1, 1]))
    pool_vecs = (row3(bg_full), row3(sc_full))
    h1, uc0, (pw_grp0, pw_out0, cw_in1, cw_out1, pw_in1, pw_grp1, pw_out1) = _conv_fwd(
        p0, h0, dw_full, conv_vecs, 0, cw_out0, "conv_mix_fwd_0",
        gather([(pwg, 0), (pwo, 0), (cwi, 1), (cwo, 1), (pwi, 1), (pwg, 1), (pwo, 1)], [1, 0, 1, 0, 1, 1, 0]))
    h2, p1, _ = _pool_fwd(h1, ng3, 1, pw_in0, pw_grp0, pool_vecs, 0, pw_out0, "pool_fwd_0")
    p2, _ = _in_fwd(h2, ng3, 2, cw_in1, "conv_in_fwd_1")
    h3, uc2, _ = _conv_fwd(p2, h2, dw_full, conv_vecs, 1, cw_out1, "conv_mix_fwd_1")
    dh, p3, d_final_g, loss_part, _ = _pool_fwd(h3, ng3, 3, pw_in1, pw_grp1, pool_vecs, 1, pw_out1, "pool_fwd_1",
                                               final=(target, final_g[None, :]))

    dp, g_pwo1, g_pwg1, dpv1, _ = _pool_bwd(dh, p3, pw_grp1, pool_vecs, 1, pw_out1, "pool_mix_bwd_1")
    dh, g_pwi1, dg3, (s_pwo1, s_pwg1) = _in_bwd(dp, h3, dh, ng3, 3, pw_in1, "pool_in_bwd_1",
                                                scatter([g_pwo1, g_pwg1], [0, 1]))
    dp, g_cwo1, ddw1, dcv1, (s_pwi1,) = _conv_bwd(dh, p2, uc2, dw_full, conv_vecs, 1, cw_out1, "conv_mix_bwd_1",
                                                  scatter([g_pwi1], [1]))
    dh, g_cwi1, dg2, (s_cwo1,) = _in_bwd(dp, h2, dh, ng3, 2, cw_in1, "conv_in_bwd_1", scatter([g_cwo1], [0]))
    dp, g_pwo0, g_pwg0, dpv0, (s_cwi1,) = _pool_bwd(dh, p1, pw_grp0, pool_vecs, 0, pw_out0, "pool_mix_bwd_0",
                                                    scatter([g_cwi1], [1]))
    dh, g_pwi0, dg1, (s_pwo0, s_pwg0) = _in_bwd(dp, h1, dh, ng3, 1, pw_in0, "pool_in_bwd_0",
                                                scatter([g_pwo0, g_pwg0], [0, 1]))
    dp, g_cwo0, ddw0, dcv0, (s_pwi0, s_ddw1, s_dcv1, s_dpv0, s_dpv1, s_dg1, s_dg2, s_dg3, s_dfg, s_loss) = _conv_bwd(
        dh, p0, uc0, dw_full, conv_vecs, 0, cw_out0, "conv_mix_bwd_0",
        scatter([g_pwi0, ddw1, dcv1, dpv0, dpv1, dg1, dg2, dg3, d_final_g, loss_part],
                [1, 1, None, 1, 1, None, None, None, None, None]))
    dh, g_cwi0, dg0, _ = _in_bwd(dp, h0, dh, ng3, 0, cw_in0, "conv_in_bwd_0")
    grad_x = dh[None]
    s_cwo0, s_cwi0, s_ddw0, s_dcv0, s_dg0 = _exchange_call(
        scatter([g_cwo0, g_cwi0, ddw0, dcv0, dg0], [0, 1, 1, None, None]), "scatter_last")

    res = {}
    res["conv_w_in"] = _adamw([s_cwi0, s_cwi1], conv_w_in, m_conv_w_in, v_conv_w_in, "adamw_conv_w_in")
    res["conv_w_out"] = _adamw([s_cwo0, s_cwo1], conv_w_out, m_conv_w_out, v_conv_w_out, "adamw_conv_w_out")
    res["pool_w_in"] = _adamw([s_pwi0, s_pwi1], pool_w_in, m_pool_w_in, v_pool_w_in, "adamw_pool_w_in")
    res["pool_w_grp"] = _adamw([s_pwg0, s_pwg1], pool_w_grp, m_pool_w_grp, v_pool_w_grp, "adamw_pool_w_grp")
    res["pool_w_out"] = _adamw([s_pwo0, s_pwo1], pool_w_out, m_pool_w_out, v_pool_w_out, "adamw_pool_w_out")
    stacks = [s_ddw0, s_ddw1, s_dpv0, s_dpv1, s_dg0, s_dg1, s_dg2, s_dg3, s_dfg, s_dcv0, s_dcv1]
    small = [
        ("conv_dw", conv_dw, m_conv_dw, v_conv_dw, [(0, None, 0), (1, None, 1)]),
        ("pool_b_grp", pool_b_grp, m_pool_b_grp, v_pool_b_grp, [(2, 0, 0), (3, 0, 1)]),
        ("pool_scale", pool_scale, m_pool_scale, v_pool_scale, [(2, 1, 0), (3, 1, 1)]),
        ("norm_g", norm_g, m_norm_g, v_norm_g, [(4, 0, 0), (5, 0, 1), (6, 0, 2), (7, 0, 3)]),
        ("final_g", final_g[None, :], m_final_g[None, :], v_final_g[None, :], [(8, 0, 0)]),
        ("conv_dw_b", conv_dw_b, m_conv_dw_b, v_conv_dw_b, [(9, 0, 0), (10, 0, 1)]),
        ("conv_ln_g", conv_ln_g, m_conv_ln_g, v_conv_ln_g, [(9, 1, 0), (10, 1, 1)]),
        ("conv_ln_b", conv_ln_b, m_conv_ln_b, v_conv_ln_b, [(9, 2, 0), (10, 2, 1)]),
    ]
    small_res, loss_block = _adamw_small([s[1:] for s in small], stacks, s_loss, "adamw_small")
    for (name, *_), r in zip(small, small_res):
        res[name] = [a[0] for a in r] if name == "final_g" else r
    loss = loss_block[0, 0]

    names = ["norm_g", "final_g", "conv_w_in", "conv_dw", "conv_dw_b", "conv_ln_g", "conv_ln_b", "conv_w_out",
             "pool_w_in", "pool_w_grp", "pool_b_grp", "pool_scale", "pool_w_out"]
    return (loss, grad_x) + tuple(res[n][q] for q in range(4) for n in names)
```

```python
import jax
import jax.numpy as jnp
from jax import lax
from jax.experimental import pallas as pl
from jax.experimental.pallas import tpu as pltpu

F32 = jnp.float32
BF16 = jnp.bfloat16

RMS_EPS = 1e-6
LN_EPS = 1e-5
CONV_TAPS = 31
CONV_HALO = 32
POOL_WINDOWS = (2, 4, 8, 16)
POOL_HALO = 16
SUBLANES = 8
LANES = 128
N_DEV = 8
V7X_VMEM_LIMIT = 56 * 1024 * 1024

ADAM_LR = 0.001
ADAM_B1 = 0.9
ADAM_B2 = 0.999
ADAM_EPS = 1e-08
ADAM_WD = 0.01
ADAM_STEP = 10

MESH = pl.DeviceIdType.MESH
ANY = pl.BlockSpec(memory_space=pl.ANY)


def _dot(a, b):
    return lax.dot_general(a, b, (((1,), (0,)), ((), ())), preferred_element_type=F32)


def _dot_nt(a, b):
    return lax.dot_general(a, b, (((1,), (1,)), ((), ())), preferred_element_type=F32)


def _dot_tn(a, b):
    return lax.dot_general(a, b, (((0,), (0,)), ((), ())), preferred_element_type=F32)


def _sigmoid(x):
    return jax.nn.sigmoid(x)


def _dsilu(x, s):
    return s * (1.0 + x * (1.0 - s))


def _rows8(x):
    r, c = x.shape
    return jnp.sum(x.reshape(r // SUBLANES, SUBLANES, c), axis=0)


def _tile(t, pref):
    return pref if t >= 2 * pref else t // 2


def _const(shape, index):
    return pl.BlockSpec(shape, lambda *_: index, pipeline_mode=pl.Buffered(1))


def _params(grid_rank=1):
    return pltpu.CompilerParams(dimension_semantics=("arbitrary",) * grid_rank, vmem_limit_bytes=V7X_VMEM_LIMIT)


def _chunks(n, rc, fn):
    def step(c, carry):
        fn(pl.multiple_of(c * rc, rc))
        return carry
    lax.fori_loop(0, n, step, 0)


def _mesh_position():
    return lax.axis_index("x"), lax.axis_index("y"), lax.axis_index("c")


def _peer(j):
    x, y, c = _mesh_position()
    px = 1 - x if j & 4 else x
    py = 1 - y if j & 2 else y
    pc = 1 - c if j & 1 else c
    return (px, py, pc), 4 * px + 2 * py + pc


def _block(ref, axis, index, size):
    idx = [slice(None)] * len(ref.shape)
    idx[axis] = pl.ds(pl.multiple_of(index * size, size), size)
    return ref.at[tuple(idx)]


class _Exchange:
    def __init__(self, kind, arrays, axes):
        self.kind, self.axes = kind, list(axes)
        self.arrays = [a[0] if isinstance(a, tuple) else a for a in arrays]
        self.layers = [a[1] if isinstance(a, tuple) else None for a in arrays]
        self.n = len(self.arrays)
        self.blk, self.out_shapes = [], []
        for a, layer, ax in zip(self.arrays, self.layers, self.axes):
            s = list(a.shape if layer is None else a.shape[1:])
            if kind == "gather":
                self.blk.append(s[ax])
                s[ax] *= N_DEV
                self.out_shapes.append(jax.ShapeDtypeStruct(tuple(s), a.dtype))
            else:
                if ax is not None:
                    s[ax] //= N_DEV
                    self.blk.append(s[ax])
                else:
                    self.blk.append(None)
                self.out_shapes.append(jax.ShapeDtypeStruct((N_DEV,) + tuple(s), a.dtype))

    def sem_shapes(self):
        return [pltpu.SemaphoreType.DMA((N_DEV - 1, self.n)), pltpu.SemaphoreType.DMA((N_DEV - 1, self.n)),
                pltpu.SemaphoreType.DMA((self.n,))]

    def _src(self, ins, k, owner):
        ref = ins[k] if self.layers[k] is None else ins[k].at[self.layers[k]]
        if self.kind == "gather" or self.axes[k] is None:
            return ref
        return _block(ref, self.axes[k], owner, self.blk[k])

    def _dst(self, outs, k, sender):
        if self.kind == "gather":
            return _block(outs[k], self.axes[k], sender, self.blk[k])
        return outs[k].at[sender]

    def _copies(self, ins, outs, sems, arriving):
        send, recv, loc = sems
        x, y, c = _mesh_position()
        me = 4 * x + 2 * y + c
        if not arriving:
            local = [pltpu.make_async_copy(self._src(ins, k, me), self._dst(outs, k, me), loc.at[k])
                     for k in range(self.n)]
        else:
            local = []
        remote = []
        for j in range(1, N_DEV):
            peer, peer_id = _peer(j)
            for k in range(self.n):
                owner, sender = (me, peer_id) if arriving else (peer_id, me)
                remote.append(pltpu.make_async_remote_copy(
                    src_ref=self._src(ins, k, owner), dst_ref=self._dst(outs, k, sender),
                    send_sem=send.at[j - 1, k], recv_sem=recv.at[j - 1, k], device_id=peer, device_id_type=MESH))
        return local, remote

    def start(self, ins, outs, sems):
        local, sends = self._copies(ins, outs, sems, arriving=False)
        for cp in local + sends:
            cp.start()

    def finish(self, ins, outs, sems):
        for cp in self._copies(ins, outs, sems, arriving=True)[1]:
            cp.wait_recv()
        local, sends = self._copies(ins, outs, sems, arriving=False)
        for cp in sends:
            cp.wait_send()
        for cp in local:
            cp.wait()


def _exchange_call(ex, name):
    def body(*refs):
        ins, outs, sems = refs[:ex.n], refs[ex.n:2 * ex.n], refs[2 * ex.n:]
        ex.start(ins, outs, sems)
        ex.finish(ins, outs, sems)

    return pl.pallas_call(body, name=name, in_specs=[ANY] * ex.n, out_specs=[ANY] * ex.n,
                          out_shape=ex.out_shapes, scratch_shapes=ex.sem_shapes())(*ex.arrays)


def _gather_via_sibling_call(ex, name):
    n = ex.n

    def body(*refs):
        ins, outs = refs[:n], refs[n:2 * n]
        send, recv, loc = refs[2 * n:]
        x, y, c = _mesh_position()
        ident = lambda px, py, pc: 4 * px + 2 * py + pc
        me, sibling = ident(x, y, c), (x, y, 1 - c)
        chips = [(1 - x, y), (x, 1 - y), (1 - x, 1 - y)]

        def copy(row, k, block, to, src=None):
            place = ex._dst(outs, k, block)
            return pltpu.make_async_remote_copy(
                src_ref=place if src is None else src, dst_ref=place,
                send_sem=send.at[row, k], recv_sem=recv.at[row, k], device_id=to, device_id_type=MESH)

        local = [pltpu.make_async_copy(ex._src(ins, k, me), ex._dst(outs, k, me), loc.at[k]) for k in range(n)]
        first = []
        for k in range(n):
            mine = ex._src(ins, k, me)
            first.append(copy(0, k, me, sibling, src=mine))
            first += [copy(1 + j, k, me, (*chip, c), src=mine) for j, chip in enumerate(chips)]
        for cp in local + first:
            cp.start()
        passed = []
        for j, chip in enumerate(chips):
            for k in range(n):
                copy(1 + j, k, ident(*chip, c), sibling).wait_recv()
                passed.append(copy(4 + j, k, ident(*chip, c), sibling))
                passed[-1].start()
        for k in range(n):
            copy(0, k, ident(x, y, 1 - c), sibling).wait_recv()
            for j, chip in enumerate(chips):
                copy(4 + j, k, ident(*chip, 1 - c), sibling).wait_recv()
        for cp in first + passed:
            cp.wait_send()
        for cp in local:
            cp.wait()

    return pl.pallas_call(body, name=name, in_specs=[ANY] * n, out_specs=[ANY] * n, out_shape=ex.out_shapes,
                          scratch_shapes=ex.sem_shapes())(*ex.arrays)


def _launch(body, name, nt, in_specs, out_specs, out_shape, scratch_shapes, args, ex=None):
    if ex is None:
        outs = pl.pallas_call(body, name=name, grid=(nt,), in_specs=in_specs, out_specs=out_specs,
                              out_shape=out_shape, scratch_shapes=scratch_shapes, compiler_params=_params())(*args)
        return list(outs), []
    n_in, n_out, n_scr = len(in_specs), len(out_specs), len(scratch_shapes)

    def riding(*refs):
        a, xa = refs[:n_in], refs[n_in:n_in + ex.n]
        o = refs[n_in + ex.n:n_in + ex.n + n_out]
        xo = refs[n_in + ex.n + n_out:n_in + 2 * ex.n + n_out]
        s = refs[n_in + 2 * ex.n + n_out:n_in + 2 * ex.n + n_out + n_scr]
        sems = refs[n_in + 2 * ex.n + n_out + n_scr:]
        i = pl.program_id(0)

        @pl.when(i == 0)
        def _():
            ex.start(xa, xo, sems)

        body(*a, *o, *s)

        @pl.when(i == nt - 1)
        def _():
            ex.finish(xa, xo, sems)

    outs = pl.pallas_call(
        riding, name=name, grid=(nt,),
        in_specs=list(in_specs) + [ANY] * ex.n, out_specs=list(out_specs) + [ANY] * ex.n,
        out_shape=list(out_shape) + ex.out_shapes, scratch_shapes=list(scratch_shapes) + ex.sem_shapes(),
        compiler_params=_params())(*args, *ex.arrays)
    return list(outs[:n_out]), list(outs[n_out:])


def _in_fwd(h, norm_g, layer, w_in, name, ex=None):
    t, d = h.shape
    n = w_in.shape[-1]
    tm = _tile(t, 512)

    def body(h_ref, g_ref, w_ref, p_ref):
        x = h_ref[...]
        r = lax.rsqrt(jnp.mean(x * x, axis=-1, keepdims=True) + RMS_EPS)
        p_ref[...] = _dot((x * r * g_ref[...]).astype(BF16), w_ref[...])

    (p,), xouts = _launch(
        body, name, t // tm,
        [pl.BlockSpec((tm, d), lambda i: (i, 0)), _const((None, 1, d), (layer, 0, 0)), _const((d, n), (0, 0))],
        [pl.BlockSpec((tm, n), lambda i: (i, 0))],
        [jax.ShapeDtypeStruct((t, n), F32)], [], (h, norm_g, w_in), ex)
    return p, xouts


def _layernorm_rows(uc, lng, lnb):
    mu = jnp.mean(uc, axis=-1, keepdims=True)
    xc = uc - mu
    rstd = lax.rsqrt(jnp.mean(xc * xc, axis=-1, keepdims=True) + LN_EPS)
    xhat = xc * rstd
    return xhat, rstd, xhat * lng + lnb


def _conv_fwd(p, h, dw, vecs, layer, w_out, name, ex=None):
    t, d = h.shape
    e = w_out.shape[0]
    tm = _tile(t, 256)
    rc = _tile(tm, 128)

    def body(p_ref, h_ref, dw_ref, dwb_ref, lng_ref, lnb_ref, wo_ref, ho_ref, xr_ref, sg_ref, us_scr, uc_ref):
        i = pl.program_id(0)

        @pl.when(i == 0)
        def _():
            us_scr[:, pl.ds(0, CONV_HALO), :] = jnp.zeros((SUBLANES, CONV_HALO, e), F32)

        @pl.when(i > 0)
        def _():
            us_scr[:, pl.ds(0, CONV_HALO), :] = us_scr[:, pl.ds(tm, CONV_HALO), :]

        us_scr[0, pl.ds(CONV_HALO, tm), :] = p_ref[:, pl.ds(0, e)] * _sigmoid(p_ref[:, pl.ds(e, e)])
        for r in range(1, SUBLANES):
            us_scr[r, pl.ds(CONV_HALO, tm), :] = us_scr[0, pl.ds(CONV_HALO - r, tm), :]

        def c_conv(base):
            for lt in range(e // LANES):
                cols = pl.ds(lt * LANES, LANES)
                acc = jnp.broadcast_to(dwb_ref[:, cols], (rc, LANES))
                for r in range(SUBLANES):
                    nq = (CONV_TAPS - 1 - r) // SUBLANES + 1
                    lo = SUBLANES * (nq - 1)
                    win = us_scr[r, pl.ds(pl.multiple_of(CONV_HALO + base - lo, SUBLANES), rc + lo), cols]
                    for q in range(nq):
                        k = CONV_TAPS - 1 - (SUBLANES * q + r)
                        at = lo - SUBLANES * q
                        acc = acc + dw_ref[pl.ds(k, 1), cols] * win[at:at + rc, :]
                uc_ref[pl.ds(base, rc), cols] = acc
        _chunks(tm // rc, rc, c_conv)

        xhat, rstd, ul = _layernorm_rows(uc_ref[...], lng_ref[...], lnb_ref[...])
        z = p_ref[:, pl.ds(2 * e, e)]
        sg_u = _sigmoid(ul)
        sg_z = _sigmoid(z)
        xr_ref[:, pl.ds(0, e)] = xhat
        xr_ref[:, pl.ds(e, LANES)] = jnp.broadcast_to(rstd, (tm, LANES))
        sg_ref[:, pl.ds(0, e)] = sg_u
        sg_ref[:, pl.ds(e, e)] = sg_z
        v = ((ul * sg_u) * (z * sg_z)).astype(BF16)
        ho_ref[...] = h_ref[...] + _dot(v, wo_ref[...])

    vec = _const((None, 1, e), (layer, 0, 0))
    row = lambda i: (i, 0)
    outs, xouts = _launch(
        body, name, t // tm,
        [pl.BlockSpec((tm, 3 * e), row), pl.BlockSpec((tm, d), row),
         _const((None, CONV_TAPS, e), (layer, 0, 0)), vec, vec, vec, _const((e, d), (0, 0))],
        [pl.BlockSpec((tm, d), row), pl.BlockSpec((tm, e + LANES), row), pl.BlockSpec((tm, 2 * e), row)],
        [jax.ShapeDtypeStruct((t, d), F32), jax.ShapeDtypeStruct((t, e + LANES), F32),
         jax.ShapeDtypeStruct((t, 2 * e), F32)],
        [pltpu.VMEM((SUBLANES, tm + CONV_HALO, e), F32), pltpu.VMEM((tm, e), F32)], (p, h, dw, *vecs, w_out), ex)
    return outs[0], tuple(outs[1:]), xouts


def _conv_bwd(dho, p, saved, dw, vecs, layer, w_out, name, ex=None):
    t, d = dho.shape
    e = w_out.shape[0]
    tm = _tile(t, 256)
    nt = t // tm
    rc = 16
    vec0 = CONV_TAPS + 1

    def body(dho_ref, p_ref, xr_ref, sg_ref, dw_ref, lng_ref, lnb_ref, wo_ref,
             dp_ref, dwo_ref, ddw_ref, dvec_ref, ds_scr, acc_scr, dwo_scr):
        i = pl.program_id(0)

        @pl.when(i == 0)
        def _():
            dwo_scr[...] = jnp.zeros_like(dwo_scr)
            acc_scr[...] = jnp.zeros_like(acc_scr)
            ds_scr[:, pl.ds(tm, CONV_HALO), :] = jnp.zeros((SUBLANES, CONV_HALO, e), F32)

        @pl.when(i > 0)
        def _():
            ds_scr[:, pl.ds(tm, CONV_HALO), :] = ds_scr[:, pl.ds(0, CONV_HALO), :]

        lng = lng_ref[...]
        lnb = lnb_ref[...]

        xhat = xr_ref[:, pl.ds(0, e)]
        rstd = xr_ref[:, pl.ds(e, 1)]
        ul = xhat * lng + lnb
        z = p_ref[:, pl.ds(2 * e, e)]
        sg_u = sg_ref[:, pl.ds(0, e)]
        sg_z = sg_ref[:, pl.ds(e, e)]
        s_u = ul * sg_u
        s_z = z * sg_z
        v = (s_u * s_z).astype(BF16)
        dy = dho_ref[...].astype(BF16)
        dv = _dot_nt(dy, wo_ref[...])
        dwo_scr[...] += _dot_tn(v, dy)

        dul = dv * s_z * _dsilu(ul, sg_u)
        dp_ref[:, pl.ds(2 * e, e)] = (dv * s_u * _dsilu(z, sg_z)).astype(BF16)
        acc_scr[pl.ds((vec0 + 1) * SUBLANES, SUBLANES), :] += _rows8(dul * xhat)
        acc_scr[pl.ds((vec0 + 2) * SUBLANES, SUBLANES), :] += _rows8(dul)
        dxh = dul * lng
        duc = rstd * (dxh - jnp.mean(dxh, axis=-1, keepdims=True)
                      - xhat * jnp.mean(dxh * xhat, axis=-1, keepdims=True))
        acc_scr[pl.ds(vec0 * SUBLANES, SUBLANES), :] += _rows8(duc)
        ds_scr[0, pl.ds(0, tm), :] = duc
        for r in range(1, SUBLANES):
            ds_scr[r, pl.ds(0, tm), :] = ds_scr[0, pl.ds(r, tm), :]

        def c_conv(base):
            rows = pl.ds(base, rc)
            a = p_ref[rows, pl.ds(0, e)]
            b = p_ref[rows, pl.ds(e, e)]
            sb = _sigmoid(b)
            u = a * sb
            du = jnp.zeros((rc, e), F32)
            for o in range(CONV_TAPS):
                q, r = divmod(o, SUBLANES)
                k = CONV_TAPS - 1 - o
                sh = ds_scr[r, pl.ds(pl.multiple_of(base + SUBLANES * q, SUBLANES), rc), :]
                du = du + dw_ref[pl.ds(k, 1), :] * sh
                acc_scr[pl.ds(k * SUBLANES, SUBLANES), :] += _rows8(u * sh)
            dp_ref[rows, pl.ds(0, e)] = (du * sb).astype(BF16)
            dp_ref[rows, pl.ds(e, e)] = (du * u * (1.0 - sb)).astype(BF16)
        _chunks(tm // rc, rc, c_conv)

        @pl.when(i == nt - 1)
        def _():
            dwo_ref[...] = dwo_scr[...].astype(BF16)
            slot_sum = lambda k: jnp.sum(acc_scr[pl.ds(k * SUBLANES, SUBLANES), :], axis=0, keepdims=True)
            for k in range(CONV_TAPS):
                ddw_ref[pl.ds(k, 1), :] = slot_sum(k)
            dvec_ref[...] = jnp.zeros_like(dvec_ref)
            for k in range(3):
                dvec_ref[pl.ds(k, 1), :] = slot_sum(vec0 + k)

    rev = lambda i: (nt - 1 - i, 0)
    vec = _const((None, 1, e), (layer, 0, 0))
    (dp, dwo, ddw, dvec), xouts = _launch(
        body, name, nt,
        [pl.BlockSpec((tm, d), rev), pl.BlockSpec((tm, 3 * e), rev), pl.BlockSpec((tm, e + LANES), rev),
         pl.BlockSpec((tm, 2 * e), rev),
         _const((None, CONV_TAPS, e), (layer, 0, 0)), vec, vec, _const((e, d), (0, 0))],
        [pl.BlockSpec((tm, 3 * e), rev), _const((e, d), (0, 0)), _const((CONV_TAPS, e), (0, 0)),
         _const((SUBLANES, e), (0, 0))],
        [jax.ShapeDtypeStruct((t, 3 * e), BF16), jax.ShapeDtypeStruct((e, d), BF16),
         jax.ShapeDtypeStruct((CONV_TAPS, e), F32), jax.ShapeDtypeStruct((SUBLANES, e), F32)],
        [pltpu.VMEM((SUBLANES, tm + CONV_HALO, e), F32), pltpu.VMEM(((vec0 + 3) * SUBLANES, e), F32),
         pltpu.VMEM((e, d), F32)],
        (dho, p, *saved, dw, vecs[1], vecs[2], w_out), ex)
    return dp, dwo, ddw, dvec, xouts


def _inv_count(tile, tm, w):
    tpos = tile * tm + lax.broadcasted_iota(jnp.int32, (tm, 1), 0)
    return 1.0 / jnp.minimum(tpos + 1, w).astype(F32)


def _pool_d_group(ue_scr, tile, tm, gc, g):
    w = POOL_WINDOWS[g]
    win = ue_scr[:, pl.ds(g * gc, gc)]
    s = win
    sh = 1
    while sh < w:
        s = s + pltpu.roll(s, sh, axis=0)
        sh *= 2
    return s[POOL_HALO:, :] * _inv_count(tile, tm, w) - win[POOL_HALO:, :]


def _pool_d(ue_scr, tile, tm, gc):
    return [_pool_d_group(ue_scr, tile, tm, gc, g) for g in range(len(POOL_WINDOWS))]


def _final_rows(ho, tg_ref, fg_ref, dh_ref, acc_scr, lacc_scr):
    d = ho.shape[-1]
    r = lax.rsqrt(jnp.mean(ho * ho, axis=-1, keepdims=True) + RMS_EPS)
    nrm = ho * r
    err = nrm * fg_ref[...] - tg_ref[...]
    lacc_scr[...] += _rows8(err * err)
    dy = err * (1.0 / d)
    acc_scr[...] += _rows8(dy * nrm)
    dq = dy * fg_ref[...]
    dh_ref[...] = r * (dq - nrm * jnp.mean(dq * nrm, axis=-1, keepdims=True))


def _pool_fwd(h, norm_g, nlayer, w_in, w_grp, vecs, layer, w_out, name, ex=None, final=None):
    t, d = h.shape
    e = w_out.shape[0]
    ng = len(POOL_WINDOWS)
    gc = e // ng
    tm = _tile(t, 512)
    nt = t // tm
    cw = 2 * e // (2 * ng)

    def layer_rows(i, hn_ref, hc_ref, g_ref, wi_ref, wg_ref, bg_ref, sc_ref, wo_ref, p_ref,
                   ue_scr, y_scr, pbuf, z_scr, hn_scr):
        tile = jnp.maximum(i - 1, 0)

        @pl.when(i == 0)
        def _():
            pbuf[...] = jnp.zeros_like(pbuf)

        @pl.when(i <= 1)
        def _():
            ue_scr[pl.ds(0, POOL_HALO), :] = jnp.zeros((POOL_HALO, e), F32)

        @pl.when(i > 1)
        def _():
            ue_scr[pl.ds(0, POOL_HALO), :] = ue_scr[pl.ds(tm, POOL_HALO), :]

        ue_scr[pl.ds(POOL_HALO, tm), :] = pbuf[:, pl.ds(0, e)]
        z_scr[...] = pbuf[:, pl.ds(e, e)]

        x = hn_ref[...]
        r = lax.rsqrt(jnp.mean(x * x, axis=-1, keepdims=True) + RMS_EPS)
        hn_scr[...] = (x * r * g_ref[...]).astype(BF16)

        def project(c):
            part = _dot(hn_scr[...], wi_ref[:, pl.ds(c * cw, cw)])
            pbuf[:, pl.ds(c * cw, cw)] = part
            p_ref[:, pl.ds(c * cw, cw)] = part

        for g in range(ng):
            project(2 * g)
            cols = pl.ds(g * gc, gc)
            dg = _pool_d_group(ue_scr, tile, tm, gc, g)
            z = z_scr[:, cols]
            y1 = (_dot(dg.astype(BF16), wg_ref[g]) + bg_ref[:, cols]) * sc_ref[:, cols]
            y_scr[:, cols] = (y1 * (z * _sigmoid(z))).astype(BF16)
            project(2 * g + 1)

        return hc_ref[...] + _dot(y_scr[...], wo_ref[...])

    nxt = lambda i: (jnp.minimum(i, nt - 1), 0)
    cur = lambda i: (jnp.maximum(i - 1, 0), 0)
    vec = _const((None, 1, e), (layer, 0, 0))
    in_specs = [pl.BlockSpec((tm, d), nxt), pl.BlockSpec((tm, d), cur), _const((None, 1, d), (nlayer, 0, 0)),
                _const((d, 2 * e), (0, 0)), _const((ng, gc, gc), (0, 0, 0)), vec, vec, _const((e, d), (0, 0))]
    scratch = [pltpu.VMEM((tm + POOL_HALO, e), F32), pltpu.VMEM((tm, e), BF16), pltpu.VMEM((tm, 2 * e), F32),
               pltpu.VMEM((tm, e), F32), pltpu.VMEM((tm, d), BF16)]
    args = (h, h, norm_g, w_in, w_grp, *vecs, w_out)

    if final is None:
        def body(hn_ref, hc_ref, g_ref, wi_ref, wg_ref, bg_ref, sc_ref, wo_ref, ho_ref, p_ref, *scr):
            ho_ref[...] = layer_rows(pl.program_id(0), hn_ref, hc_ref, g_ref, wi_ref, wg_ref, bg_ref, sc_ref, wo_ref,
                                     p_ref, *scr)

        (ho, p), xouts = _launch(
            body, name, nt + 1, in_specs, [pl.BlockSpec((tm, d), cur), pl.BlockSpec((tm, 2 * e), nxt)],
            [jax.ShapeDtypeStruct((t, d), F32), jax.ShapeDtypeStruct((t, 2 * e), F32)], scratch, args, ex)
        return ho, p, xouts

    target, final_g = final

    def body(hn_ref, hc_ref, g_ref, wi_ref, wg_ref, bg_ref, sc_ref, wo_ref, tg_ref, fg_ref,
             dh_ref, p_ref, dfg_ref, loss_ref, ue_scr, y_scr, pbuf, z_scr, hn_scr, acc_scr, lacc_scr):
        i = pl.program_id(0)

        @pl.when(i <= 1)
        def _():
            acc_scr[...] = jnp.zeros_like(acc_scr)
            lacc_scr[...] = jnp.zeros_like(lacc_scr)

        ho = layer_rows(i, hn_ref, hc_ref, g_ref, wi_ref, wg_ref, bg_ref, sc_ref, wo_ref, p_ref,
                        ue_scr, y_scr, pbuf, z_scr, hn_scr)
        _final_rows(ho, tg_ref, fg_ref, dh_ref, acc_scr, lacc_scr)

        @pl.when(i == nt)
        def _():
            dfg_ref[...] = jnp.zeros_like(dfg_ref)
            dfg_ref[pl.ds(0, 1), :] = jnp.sum(acc_scr[...], axis=0, keepdims=True)
            loss_ref[...] = jnp.broadcast_to(jnp.sum(lacc_scr[...]) * (0.5 / d), loss_ref.shape)

    (dh, p, dfg, loss), xouts = _launch(
        body, name, nt + 1, in_specs + [pl.BlockSpec((tm, d), cur), _const((1, d), (0, 0))],
        [pl.BlockSpec((tm, d), cur), pl.BlockSpec((tm, 2 * e), nxt), _const((SUBLANES, d), (0, 0)),
         _const((SUBLANES, LANES), (0, 0))],
        [jax.ShapeDtypeStruct((t, d), F32), jax.ShapeDtypeStruct((t, 2 * e), F32),
         jax.ShapeDtypeStruct((SUBLANES, d), F32), jax.ShapeDtypeStruct((SUBLANES, LANES), F32)],
        scratch + [pltpu.VMEM((SUBLANES, d), F32), pltpu.VMEM((SUBLANES, d), F32)],
        args + (target, final_g), ex)
    return dh, p, dfg, loss, xouts


def _pool_bwd(dho, p, w_grp, vecs, layer, w_out, name, ex=None):
    t, d = dho.shape
    e = w_out.shape[0]
    ng = len(POOL_WINDOWS)
    gc = e // ng
    tm = _tile(t, 512)
    nt = t // tm
    hb = tm // POOL_HALO

    def body(dho_ref, p_ref, ph_ref, wg_ref, bg_ref, sc_ref, wo_ref, dp_ref, dwo_ref, dwg_ref, dvec_ref,
             ue_scr, ee_scr, acc_scr, dwo_scr, dwg_scr):
        i = pl.program_id(0)
        tile = nt - 1 - i

        @pl.when(i == 0)
        def _():
            dwo_scr[...] = jnp.zeros_like(dwo_scr)
            dwg_scr[...] = jnp.zeros_like(dwg_scr)
            acc_scr[...] = jnp.zeros_like(acc_scr)
            ee_scr[pl.ds(tm, POOL_HALO), :] = jnp.zeros((POOL_HALO, e), F32)

        @pl.when(i > 0)
        def _():
            ee_scr[pl.ds(tm, POOL_HALO), :] = ee_scr[pl.ds(0, POOL_HALO), :]

        @pl.when(tile == 0)
        def _():
            ue_scr[pl.ds(0, POOL_HALO), :] = jnp.zeros((POOL_HALO, e), F32)

        @pl.when(tile > 0)
        def _():
            ue_scr[pl.ds(0, POOL_HALO), :] = ph_ref[:, pl.ds(0, e)]

        ue_scr[pl.ds(POOL_HALO, tm), :] = p_ref[:, pl.ds(0, e)]

        bg = bg_ref[...]
        sc = sc_ref[...]
        ds = [dg.astype(BF16) for dg in _pool_d(ue_scr, tile, tm, gc)]
        ob = jnp.concatenate([_dot(ds[g], wg_ref[g]) for g in range(ng)], axis=1) + bg
        z = p_ref[:, pl.ds(e, e)]
        sg_z = _sigmoid(z)
        s_z = z * sg_z
        y1 = ob * sc
        dy = dho_ref[...].astype(BF16)
        dy2 = _dot_nt(dy, wo_ref[...])
        dwo_scr[...] += _dot_tn((y1 * s_z).astype(BF16), dy)
        dy1 = dy2 * s_z
        dp_ref[:, pl.ds(e, e)] = (dy2 * y1 * _dsilu(z, sg_z)).astype(BF16)
        acc_scr[pl.ds(SUBLANES, SUBLANES), :] += _rows8(dy1 * ob)
        do = dy1 * sc
        acc_scr[pl.ds(0, SUBLANES), :] += _rows8(do)

        n = tm + POOL_HALO
        for g, w in enumerate(POOL_WINDOWS):
            cols = pl.ds(g * gc, gc)
            do_g = do[:, g * gc:(g + 1) * gc].astype(BF16)
            dwg_scr[g] += _dot_tn(ds[g], do_g)
            dd = _dot_nt(do_g, wg_ref[g])
            ee_scr[pl.ds(0, tm), cols] = dd * _inv_count(tile, tm, w)
            s = ee_scr[:, cols]
            sh = 1
            while sh < w:
                s = s + pltpu.roll(s, n - sh, axis=0)
                sh *= 2
            dp_ref[:, cols] = (s[:tm, :] - dd).astype(BF16)

        @pl.when(i == nt - 1)
        def _():
            dwo_ref[...] = dwo_scr[...].astype(BF16)
            dwg_ref[...] = dwg_scr[...].astype(BF16)
            dvec_ref[...] = jnp.zeros_like(dvec_ref)
            for k in range(2):
                dvec_ref[pl.ds(k, 1), :] = jnp.sum(acc_scr[pl.ds(k * SUBLANES, SUBLANES), :], axis=0, keepdims=True)

    rev = lambda i: (nt - 1 - i, 0)
    vec = _const((None, 1, e), (layer, 0, 0))
    (dp, dwo, dwg, dvec), xouts = _launch(
        body, name, nt,
        [pl.BlockSpec((tm, d), rev), pl.BlockSpec((tm, 2 * e), rev),
         pl.BlockSpec((POOL_HALO, 2 * e), lambda i: (jnp.maximum((nt - 1 - i) * hb - 1, 0), 0)),
         _const((ng, gc, gc), (0, 0, 0)), vec, vec, _const((e, d), (0, 0))],
        [pl.BlockSpec((tm, 2 * e), rev), _const((e, d), (0, 0)), _const((ng, gc, gc), (0, 0, 0)),
         _const((SUBLANES, e), (0, 0))],
        [jax.ShapeDtypeStruct((t, 2 * e), BF16), jax.ShapeDtypeStruct((e, d), BF16),
         jax.ShapeDtypeStruct((ng, gc, gc), BF16), jax.ShapeDtypeStruct((SUBLANES, e), F32)],
        [pltpu.VMEM((tm + POOL_HALO, e), F32), pltpu.VMEM((tm + POOL_HALO, e), F32),
         pltpu.VMEM((2 * SUBLANES, e), F32), pltpu.VMEM((e, d), F32), pltpu.VMEM((ng, gc, gc), F32)],
        (dho, p, p, w_grp, *vecs, w_out), ex)
    return dp, dwo, dwg, dvec, xouts


def _in_bwd(dp, h, dho, norm_g, layer, w_in, name, ex=None):
    t, d = h.shape
    n = w_in.shape[-1]
    tm = _tile(t, 512)
    nt = t // tm

    def body(dp_ref, h_ref, dho_ref, g_ref, w_ref, dh_ref, dw_ref, dg_ref, acc_scr, dw_scr):
        i = pl.program_id(0)

        @pl.when(i == 0)
        def _():
            dw_scr[...] = jnp.zeros_like(dw_scr)
            acc_scr[...] = jnp.zeros_like(acc_scr)

        x = h_ref[...]
        r = lax.rsqrt(jnp.mean(x * x, axis=-1, keepdims=True) + RMS_EPS)
        nrm = x * r
        dp = dp_ref[...]
        dhn = _dot_nt(dp, w_ref[...])
        dw_scr[...] += _dot_tn((nrm * g_ref[...]).astype(BF16), dp)
        acc_scr[...] += _rows8(dhn * nrm)
        dq = dhn * g_ref[...]
        dh_ref[...] = dho_ref[...] + r * (dq - nrm * jnp.mean(dq * nrm, axis=-1, keepdims=True))

        @pl.when(i == nt - 1)
        def _():
            dw_ref[...] = dw_scr[...].astype(BF16)
            dg_ref[...] = jnp.zeros_like(dg_ref)
            dg_ref[pl.ds(0, 1), :] = jnp.sum(acc_scr[...], axis=0, keepdims=True)

    (dh, dw, dg), xouts = _launch(
        body, name, nt,
        [pl.BlockSpec((tm, n), lambda i: (i, 0)), pl.BlockSpec((tm, d), lambda i: (i, 0)),
         pl.BlockSpec((tm, d), lambda i: (i, 0)), _const((None, 1, d), (layer, 0, 0)), _const((d, n), (0, 0))],
        [pl.BlockSpec((tm, d), lambda i: (i, 0)), _const((d, n), (0, 0)), _const((SUBLANES, d), (0, 0))],
        [jax.ShapeDtypeStruct((t, d), F32), jax.ShapeDtypeStruct((d, n), BF16),
         jax.ShapeDtypeStruct((SUBLANES, d), F32)],
        [pltpu.VMEM((SUBLANES, d), F32), pltpu.VMEM((d, n), F32)],
        (dp, h, dho, norm_g, w_in), ex)
    return dh, dw, dg, xouts


def _adam_update(g, w, m, v):
    c1 = 1.0 / (1.0 - ADAM_B1 ** ADAM_STEP)
    c2 = 1.0 / (1.0 - ADAM_B2 ** ADAM_STEP)
    nm = ADAM_B1 * m + (1.0 - ADAM_B1) * g
    nv = ADAM_B2 * v + (1.0 - ADAM_B2) * (g * g)
    return -ADAM_LR * ((nm * c1) / (jnp.sqrt(nv * c2) + ADAM_EPS) + ADAM_WD * w), nm, nv


def _adamw_small(params, stacks, loss_stack, name):
    ns, npar = len(stacks), len(params)

    def body(*refs):
        st = refs[:ns]
        pr = refs[ns:ns + 3 * npar]
        ls_ref = refs[ns + 3 * npar]
        outs = refs[ns + 3 * npar + 1:ns + 7 * npar + 1]
        loss_ref = refs[ns + 7 * npar + 1]
        for q, (w, _, _, pieces) in enumerate(params):
            w_ref, m_ref, v_ref = pr[3 * q:3 * q + 3]
            g_ref, d_ref, nm_ref, nv_ref = outs[4 * q:4 * q + 4]
            for s, row, slab in pieces:
                if w.ndim == 3:
                    take = lambda k: st[s][k]
                    at = slab
                else:
                    take = lambda k: st[s][k, pl.ds(row, 1), :]
                    at = (pl.ds(slab, 1), slice(None))
                g = take(0)
                for k in range(1, N_DEV):
                    g = g + take(k)
                g_ref[at] = g
                d_ref[at], nm_ref[at], nv_ref[at] = _adam_update(g, w_ref[at], m_ref[at], v_ref[at])
        tot = ls_ref[0]
        for k in range(1, N_DEV):
            tot = tot + ls_ref[k]
        loss_ref[...] = tot

    flat = [a for (w, m, v, _) in params for a in (w, m, v)]
    out_shape = [jax.ShapeDtypeStruct(w.shape, F32) for (w, _, _, _) in params for _ in range(4)]
    whole = pl.BlockSpec(memory_space=pltpu.VMEM)
    outs = pl.pallas_call(
        body, name=name, in_specs=[whole] * (ns + 3 * npar + 1), out_specs=[whole] * (4 * npar + 1),
        out_shape=out_shape + [jax.ShapeDtypeStruct(loss_stack.shape[1:], F32)],
    )(*stacks, *flat, loss_stack)
    return [outs[4 * q:4 * q + 4] for q in range(npar)], outs[-1]


def _adamw(stacks, w, m, v, name):
    nl = len(stacks)
    shp = w.shape
    c = shp[-1]
    r = 1
    for s in shp[1:-1]:
        r *= s
    tr = r
    for cand in (512, 256, 128, 64, 32, 16):
        if r % cand == 0 and r > cand:
            tr = cand
            break
    nrb = r // tr

    def body(*refs):
        s_refs = refs[:nl]
        w_ref, m_ref, v_ref, g_ref, d_ref, nm_ref, nv_ref = refs[nl:]
        layer = pl.program_id(0)
        for l in range(nl):
            @pl.when(layer == l)
            def _(l=l):
                g = s_refs[l][0].astype(F32)
                for k in range(1, N_DEV):
                    g = g + s_refs[l][k].astype(F32)
                g_ref[...] = g
                d_ref[...], nm_ref[...], nv_ref[...] = _adam_update(g, w_ref[...], m_ref[...], v_ref[...])

    def stack_spec(l):
        return pl.BlockSpec((N_DEV, tr, c),
                            lambda j, i: (0, jnp.where(j == l, i, jnp.where(j < l, 0, nrb - 1)), 0))

    spec = pl.BlockSpec((None, tr, c), lambda j, i: (j, i, 0))
    outs = pl.pallas_call(
        body, name=name, grid=(nl, nrb),
        in_specs=[stack_spec(l) for l in range(nl)] + [spec, spec, spec],
        out_specs=[spec] * 4,
        out_shape=[jax.ShapeDtypeStruct((nl, r, c), F32)] * 4,
        compiler_params=_params(2),
    )(*[s.reshape(N_DEV, r, c) for s in stacks], w.reshape(nl, r, c), m.reshape(nl, r, c), v.reshape(nl, r, c))
    return [o.reshape(shp) for o in outs]


def kernel(x, norm_g, final_g, conv_w_in, conv_dw, conv_dw_b, conv_ln_g, conv_ln_b, conv_w_out, pool_w_in, pool_w_grp, pool_b_grp, pool_scale, pool_w_out, loss_target, m_norm_g, m_final_g, m_conv_w_in, m_conv_dw, m_conv_dw_b, m_conv_ln_g, m_conv_ln_b, m_conv_w_out, m_pool_w_in, m_pool_w_grp, m_pool_b_grp, m_pool_scale, m_pool_w_out, v_norm_g, v_final_g, v_conv_w_in, v_conv_dw, v_conv_dw_b, v_conv_ln_g, v_conv_ln_b, v_conv_w_out, v_pool_w_in, v_pool_w_grp, v_pool_b_grp, v_pool_scale, v_pool_w_out):
    h0 = x[0]
    target = loss_target[0]
    ng3 = norm_g[:, None, :]
    row3 = lambda a: a[:, None, :]
    conv_vecs = (row3(conv_dw_b), row3(conv_ln_g), row3(conv_ln_b))
    gather = lambda arrays, axes: _Exchange("gather", arrays, axes)
    scatter = lambda arrays, axes: _Exchange("scatter", arrays, axes)

    cwi, cwo, pwi = conv_w_in.astype(BF16), conv_w_out.astype(BF16), pool_w_in.astype(BF16)
    pwg, pwo = pool_w_grp.astype(BF16), pool_w_out.astype(BF16)

    (cw_in0,) = _gather_via_sibling_call(gather([(cwi, 0)], [1]), "gather_first")
    p0, (cw_out0, dw_full, bg_full, sc_full, pw_in0) = _in_fwd(
        h0, ng3, 0, cw_in0, "conv_in_fwd_0",
        gather([(cwo, 0), conv_dw, pool_b_grp, pool_scale, (pwi, 0)], [0, 2, 1, 1, 1]))
    pool_vecs = (row3(bg_full), row3(sc_full))
    h1, uc0, (pw_grp0, pw_out0, cw_in1, cw_out1, pw_in1, pw_grp1, pw_out1) = _conv_fwd(
        p0, h0, dw_full, conv_vecs, 0, cw_out0, "conv_mix_fwd_0",
        gather([(pwg, 0), (pwo, 0), (cwi, 1), (cwo, 1), (pwi, 1), (pwg, 1), (pwo, 1)], [1, 0, 1, 0, 1, 1, 0]))
    h2, p1, _ = _pool_fwd(h1, ng3, 1, pw_in0, pw_grp0, pool_vecs, 0, pw_out0, "pool_fwd_0")
    p2, _ = _in_fwd(h2, ng3, 2, cw_in1, "conv_in_fwd_1")
    h3, uc2, _ = _conv_fwd(p2, h2, dw_full, conv_vecs, 1, cw_out1, "conv_mix_fwd_1")
    dh, p3, d_final_g, loss_part, _ = _pool_fwd(h3, ng3, 3, pw_in1, pw_grp1, pool_vecs, 1, pw_out1, "pool_fwd_1",
                                               final=(target, final_g[None, :]))

    dp, g_pwo1, g_pwg1, dpv1, _ = _pool_bwd(dh, p3, pw_grp1, pool_vecs, 1, pw_out1, "pool_mix_bwd_1")
    dh, g_pwi1, dg3, (s_pwo1, s_pwg1) = _in_bwd(dp, h3, dh, ng3, 3, pw_in1, "pool_in_bwd_1",
                                                scatter([g_pwo1, g_pwg1], [0, 1]))
    dp, g_cwo1, ddw1, dcv1, (s_pwi1,) = _conv_bwd(dh, p2, uc2, dw_full, conv_vecs, 1, cw_out1, "conv_mix_bwd_1",
                                                  scatter([g_pwi1], [1]))
    dh, g_cwi1, dg2, (s_cwo1,) = _in_bwd(dp, h2, dh, ng3, 2, cw_in1, "conv_in_bwd_1", scatter([g_cwo1], [0]))
    dp, g_pwo0, g_pwg0, dpv0, (s_cwi1,) = _pool_bwd(dh, p1, pw_grp0, pool_vecs, 0, pw_out0, "pool_mix_bwd_0",
                                                    scatter([g_cwi1], [1]))
    dh, g_pwi0, dg1, (s_pwo0, s_pwg0) = _in_bwd(dp, h1, dh, ng3, 1, pw_in0, "pool_in_bwd_0",
                                                scatter([g_pwo0, g_pwg0], [0, 1]))
    dp, g_cwo0, ddw0, dcv0, (s_pwi0, s_ddw1, s_dcv1, s_dpv0, s_dpv1, s_dg1, s_dg2, s_dg3, s_dfg, s_loss) = _conv_bwd(
        dh, p0, uc0, dw_full, conv_vecs, 0, cw_out0, "conv_mix_bwd_0",
        scatter([g_pwi0, ddw1, dcv1, dpv0, dpv1, dg1, dg2, dg3, d_final_g, loss_part],
                [1, 1, None, 1, 1, None, None, None, None, None]))
    dh, g_cwi0, dg0, _ = _in_bwd(dp, h0, dh, ng3, 0, cw_in0, "conv_in_bwd_0")
    grad_x = dh[None]
    s_cwo0, s_cwi0, s_ddw0, s_dcv0, s_dg0 = _exchange_call(
        scatter([g_cwo0, g_cwi0, ddw0, dcv0, dg0], [0, 1, 1, None, None]), "scatter_last")

    res = {}
    res["conv_w_in"] = _adamw([s_cwi0, s_cwi1], conv_w_in, m_conv_w_in, v_conv_w_in, "adamw_conv_w_in")
    res["conv_w_out"] = _adamw([s_cwo0, s_cwo1], conv_w_out, m_conv_w_out, v_conv_w_out, "adamw_conv_w_out")
    res["pool_w_in"] = _adamw([s_pwi0, s_pwi1], pool_w_in, m_pool_w_in, v_pool_w_in, "adamw_pool_w_in")
    res["pool_w_grp"] = _adamw([s_pwg0, s_pwg1], pool_w_grp, m_pool_w_grp, v_pool_w_grp, "adamw_pool_w_grp")
    res["pool_w_out"] = _adamw([s_pwo0, s_pwo1], pool_w_out, m_pool_w_out, v_pool_w_out, "adamw_pool_w_out")
    stacks = [s_ddw0, s_ddw1, s_dpv0, s_dpv1, s_dg0, s_dg1, s_dg2, s_dg3, s_dfg, s_dcv0, s_dcv1]
    small = [
        ("conv_dw", conv_dw, m_conv_dw, v_conv_dw, [(0, None, 0), (1, None, 1)]),
        ("pool_b_grp", pool_b_grp, m_pool_b_grp, v_pool_b_grp, [(2, 0, 0), (3, 0, 1)]),
        ("pool_scale", pool_scale, m_pool_scale, v_pool_scale, [(2, 1, 0), (3, 1, 1)]),
        ("norm_g", norm_g, m_norm_g, v_norm_g, [(4, 0, 0), (5, 0, 1), (6, 0, 2), (7, 0, 3)]),
        ("final_g", final_g[None, :], m_final_g[None, :], v_final_g[None, :], [(8, 0, 0)]),
        ("conv_dw_b", conv_dw_b, m_conv_dw_b, v_conv_dw_b, [(9, 0, 0), (10, 0, 1)]),
        ("conv_ln_g", conv_ln_g, m_conv_ln_g, v_conv_ln_g, [(9, 1, 0), (10, 1, 1)]),
        ("conv_ln_b", conv_ln_b, m_conv_ln_b, v_conv_ln_b, [(9, 2, 0), (10, 2, 1)]),
    ]
    small_res, loss_block = _adamw_small([s[1:] for s in small], stacks, s_loss, "adamw_small")
    for (name, *_), r in zip(small, small_res):
        res[name] = [a[0] for a in r] if name == "final_g" else r
    loss = loss_block[0, 0]

    names = ["norm_g", "final_g", "conv_w_in", "conv_dw", "conv_dw_b", "conv_ln_g", "conv_ln_b", "conv_w_out",
             "pool_w_in", "pool_w_grp", "pool_b_grp", "pool_scale", "pool_w_out"]
    return (loss, grad_x) + tuple(res[n][q] for q in range(4) for n in names)
```

```python
import jax
import jax.numpy as jnp
from jax import lax
from jax.experimental import pallas as pl
from jax.experimental.pallas import tpu as pltpu

F32 = jnp.float32
BF16 = jnp.bfloat16

RMS_EPS = 1e-6
LN_EPS = 1e-5
CONV_TAPS = 31
CONV_HALO = 32
POOL_WINDOWS = (2, 4, 8, 16)
POOL_HALO = 16
SUBLANES = 8
LANES = 128
N_DEV = 8
V7X_VMEM_LIMIT = 56 * 1024 * 1024

ADAM_LR = 0.001
ADAM_B1 = 0.9
ADAM_B2 = 0.999
ADAM_EPS = 1e-08
ADAM_WD = 0.01
ADAM_STEP = 10

MESH = pl.DeviceIdType.MESH
ANY = pl.BlockSpec(memory_space=pl.ANY)


def _dot(a, b):
    return lax.dot_general(a, b, (((1,), (0,)), ((), ())), preferred_element_type=F32)


def _dot_nt(a, b):
    return lax.dot_general(a, b, (((1,), (1,)), ((), ())), preferred_element_type=F32)


def _dot_tn(a, b):
    return lax.dot_general(a, b, (((0,), (0,)), ((), ())), preferred_element_type=F32)


def _sigmoid(x):
    return jax.nn.sigmoid(x)


def _dsilu(x, s):
    return s * (1.0 + x * (1.0 - s))


def _rows8(x):
    r, c = x.shape
    return jnp.sum(x.reshape(r // SUBLANES, SUBLANES, c), axis=0)


def _tile(t, pref):
    return pref if t >= 2 * pref else t // 2


def _const(shape, index):
    return pl.BlockSpec(shape, lambda *_: index, pipeline_mode=pl.Buffered(1))


def _params(grid_rank=1):
    return pltpu.CompilerParams(dimension_semantics=("arbitrary",) * grid_rank, vmem_limit_bytes=V7X_VMEM_LIMIT)


def _chunks(n, rc, fn):
    def step(c, carry):
        fn(pl.multiple_of(c * rc, rc))
        return carry
    lax.fori_loop(0, n, step, 0)


def _mesh_position():
    return lax.axis_index("x"), lax.axis_index("y"), lax.axis_index("c")


def _peer(j):
    x, y, c = _mesh_position()
    px = 1 - x if j & 4 else x
    py = 1 - y if j & 2 else y
    pc = 1 - c if j & 1 else c
    return (px, py, pc), 4 * px + 2 * py + pc


def _block(ref, axis, index, size):
    idx = [slice(None)] * len(ref.shape)
    idx[axis] = pl.ds(pl.multiple_of(index * size, size), size)
    return ref.at[tuple(idx)]


class _Exchange:
    def __init__(self, kind, arrays, axes):
        self.kind, self.axes = kind, list(axes)
        self.arrays = [a[0] if isinstance(a, tuple) else a for a in arrays]
        self.layers = [a[1] if isinstance(a, tuple) else None for a in arrays]
        self.n = len(self.arrays)
        self.blk, self.out_shapes = [], []
        for a, layer, ax in zip(self.arrays, self.layers, self.axes):
            s = list(a.shape if layer is None else a.shape[1:])
            if kind == "gather":
                self.blk.append(s[ax])
                s[ax] *= N_DEV
                self.out_shapes.append(jax.ShapeDtypeStruct(tuple(s), a.dtype))
            else:
                if ax is not None:
                    s[ax] //= N_DEV
                    self.blk.append(s[ax])
                else:
                    self.blk.append(None)
                self.out_shapes.append(jax.ShapeDtypeStruct((N_DEV,) + tuple(s), a.dtype))

    def sem_shapes(self):
        return [pltpu.SemaphoreType.DMA((N_DEV - 1, self.n)), pltpu.SemaphoreType.DMA((N_DEV - 1, self.n)),
                pltpu.SemaphoreType.DMA((self.n,))]

    def _src(self, ins, k, owner):
        ref = ins[k] if self.layers[k] is None else ins[k].at[self.layers[k]]
        if self.kind == "gather" or self.axes[k] is None:
            return ref
        return _block(ref, self.axes[k], owner, self.blk[k])

    def _dst(self, outs, k, sender):
        if self.kind == "gather":
            return _block(outs[k], self.axes[k], sender, self.blk[k])
        return outs[k].at[sender]

    def _copies(self, ins, outs, sems, arriving):
        send, recv, loc = sems
        x, y, c = _mesh_position()
        me = 4 * x + 2 * y + c
        if not arriving:
            local = [pltpu.make_async_copy(self._src(ins, k, me), self._dst(outs, k, me), loc.at[k])
                     for k in range(self.n)]
        else:
            local = []
        remote = []
        for j in range(1, N_DEV):
            peer, peer_id = _peer(j)
            for k in range(self.n):
                owner, sender = (me, peer_id) if arriving else (peer_id, me)
                remote.append(pltpu.make_async_remote_copy(
                    src_ref=self._src(ins, k, owner), dst_ref=self._dst(outs, k, sender),
                    send_sem=send.at[j - 1, k], recv_sem=recv.at[j - 1, k], device_id=peer, device_id_type=MESH))
        return local, remote

    def start(self, ins, outs, sems):
        local, sends = self._copies(ins, outs, sems, arriving=False)
        for cp in local + sends:
            cp.start()

    def finish(self, ins, outs, sems):
        for cp in self._copies(ins, outs, sems, arriving=True)[1]:
            cp.wait_recv()
        local, sends = self._copies(ins, outs, sems, arriving=False)
        for cp in sends:
            cp.wait_send()
        for cp in local:
            cp.wait()


def _exchange_call(ex, name):
    def body(*refs):
        ins, outs, sems = refs[:ex.n], refs[ex.n:2 * ex.n], refs[2 * ex.n:]
        ex.start(ins, outs, sems)
        ex.finish(ins, outs, sems)

    return pl.pallas_call(body, name=name, in_specs=[ANY] * ex.n, out_specs=[ANY] * ex.n,
                          out_shape=ex.out_shapes, scratch_shapes=ex.sem_shapes())(*ex.arrays)


def _gather_via_sibling_call(ex, name):
    n = ex.n

    def body(*refs):
        ins, outs = refs[:n], refs[n:2 * n]
        send, recv, loc = refs[2 * n:]
        x, y, c = _mesh_position()
        ident = lambda px, py, pc: 4 * px + 2 * py + pc
        me, sibling = ident(x, y, c), (x, y, 1 - c)
        chips = [(1 - x, y), (x, 1 - y), (1 - x, 1 - y)]

        def copy(row, k, block, to, src=None):
            place = ex._dst(outs, k, block)
            return pltpu.make_async_remote_copy(
                src_ref=place if src is None else src, dst_ref=place,
                send_sem=send.at[row, k], recv_sem=recv.at[row, k], device_id=to, device_id_type=MESH)

        local = [pltpu.make_async_copy(ex._src(ins, k, me), ex._dst(outs, k, me), loc.at[k]) for k in range(n)]
        first = []
        for k in range(n):
            mine = ex._src(ins, k, me)
            first.append(copy(0, k, me, sibling, src=mine))
            first += [copy(1 + j, k, me, (*chip, c), src=mine) for j, chip in enumerate(chips)]
        for cp in local + first:
            cp.start()
        passed = []
        for j, chip in enumerate(chips):
            for k in range(n):
                copy(1 + j, k, ident(*chip, c), sibling).wait_recv()
                passed.append(copy(4 + j, k, ident(*chip, c), sibling))
                passed[-1].start()
        for k in range(n):
            copy(0, k, ident(x, y, 1 - c), sibling).wait_recv()
            for j, chip in enumerate(chips):
                copy(4 + j, k, ident(*chip, 1 - c), sibling).wait_recv()
        for cp in first + passed:
            cp.wait_send()
        for cp in local:
            cp.wait()

    return pl.pallas_call(body, name=name, in_specs=[ANY] * n, out_specs=[ANY] * n, out_shape=ex.out_shapes,
                          scratch_shapes=ex.sem_shapes())(*ex.arrays)


def _launch(body, name, nt, in_specs, out_specs, out_shape, scratch_shapes, args, ex=None):
    if ex is None:
        outs = pl.pallas_call(body, name=name, grid=(nt,), in_specs=in_specs, out_specs=out_specs,
                              out_shape=out_shape, scratch_shapes=scratch_shapes, compiler_params=_params())(*args)
        return list(outs), []
    n_in, n_out, n_scr = len(in_specs), len(out_specs), len(scratch_shapes)

    def riding(*refs):
        a, xa = refs[:n_in], refs[n_in:n_in + ex.n]
        o = refs[n_in + ex.n:n_in + ex.n + n_out]
        xo = refs[n_in + ex.n + n_out:n_in + 2 * ex.n + n_out]
        s = refs[n_in + 2 * ex.n + n_out:n_in + 2 * ex.n + n_out + n_scr]
        sems = refs[n_in + 2 * ex.n + n_out + n_scr:]
        i = pl.program_id(0)

        @pl.when(i == 0)
        def _():
            ex.start(xa, xo, sems)

        body(*a, *o, *s)

        @pl.when(i == nt - 1)
        def _():
            ex.finish(xa, xo, sems)

    outs = pl.pallas_call(
        riding, name=name, grid=(nt,),
        in_specs=list(in_specs) + [ANY] * ex.n, out_specs=list(out_specs) + [ANY] * ex.n,
        out_shape=list(out_shape) + ex.out_shapes, scratch_shapes=list(scratch_shapes) + ex.sem_shapes(),
        compiler_params=_params())(*args, *ex.arrays)
    return list(outs[:n_out]), list(outs[n_out:])


def _in_fwd(h, norm_g, layer, w_in, name, ex=None):
    t, d = h.shape
    n = w_in.shape[-1]
    tm = _tile(t, 512)

    def body(h_ref, g_ref, w_ref, p_ref):
        x = h_ref[...]
        r = lax.rsqrt(jnp.mean(x * x, axis=-1, keepdims=True) + RMS_EPS)
        p_ref[...] = _dot((x * r * g_ref[...]).astype(BF16), w_ref[...])

    (p,), xouts = _launch(
        body, name, t // tm,
        [pl.BlockSpec((tm, d), lambda i: (i, 0)), _const((None, 1, d), (layer, 0, 0)), _const((d, n), (0, 0))],
        [pl.BlockSpec((tm, n), lambda i: (i, 0))],
        [jax.ShapeDtypeStruct((t, n), F32)], [], (h, norm_g, w_in), ex)
    return p, xouts


def _layernorm_rows(uc, lng, lnb):
    mu = jnp.mean(uc, axis=-1, keepdims=True)
    xc = uc - mu
    rstd = lax.rsqrt(jnp.mean(xc * xc, axis=-1, keepdims=True) + LN_EPS)
    xhat = xc * rstd
    return xhat, rstd, xhat * lng + lnb


def _conv_fwd(p, h, dw, vecs, layer, w_out, name, ex=None):
    t, d = h.shape
    e = w_out.shape[0]
    tm = _tile(t, 256)
    rc = _tile(tm, 128)

    def body(p_ref, h_ref, dw_ref, dwb_ref, lng_ref, lnb_ref, wo_ref, ho_ref, xr_ref, sg_ref, us_scr, uc_ref):
        i = pl.program_id(0)

        @pl.when(i == 0)
        def _():
            us_scr[:, pl.ds(0, CONV_HALO), :] = jnp.zeros((SUBLANES, CONV_HALO, e), F32)

        @pl.when(i > 0)
        def _():
            us_scr[:, pl.ds(0, CONV_HALO), :] = us_scr[:, pl.ds(tm, CONV_HALO), :]

        us_scr[0, pl.ds(CONV_HALO, tm), :] = p_ref[:, pl.ds(0, e)] * _sigmoid(p_ref[:, pl.ds(e, e)])
        for r in range(1, SUBLANES):
            us_scr[r, pl.ds(CONV_HALO, tm), :] = us_scr[0, pl.ds(CONV_HALO - r, tm), :]

        def c_conv(base):
            for lt in range(e // LANES):
                cols = pl.ds(lt * LANES, LANES)
                acc = jnp.broadcast_to(dwb_ref[:, cols], (rc, LANES))
                for r in range(SUBLANES):
                    nq = (CONV_TAPS - 1 - r) // SUBLANES + 1
                    lo = SUBLANES * (nq - 1)
                    win = us_scr[r, pl.ds(pl.multiple_of(CONV_HALO + base - lo, SUBLANES), rc + lo), cols]
                    for q in range(nq):
                        k = CONV_TAPS - 1 - (SUBLANES * q + r)
                        at = lo - SUBLANES * q
                        acc = acc + dw_ref[pl.ds(k, 1), cols] * win[at:at + rc, :]
                uc_ref[pl.ds(base, rc), cols] = acc
        _chunks(tm // rc, rc, c_conv)

        xhat, rstd, ul = _layernorm_rows(uc_ref[...], lng_ref[...], lnb_ref[...])
        z = p_ref[:, pl.ds(2 * e, e)]
        sg_u = _sigmoid(ul)
        sg_z = _sigmoid(z)
        xr_ref[:, pl.ds(0, e)] = xhat
        xr_ref[:, pl.ds(e, LANES)] = jnp.broadcast_to(rstd, (tm, LANES))
        sg_ref[:, pl.ds(0, e)] = sg_u
        sg_ref[:, pl.ds(e, e)] = sg_z
        v = ((ul * sg_u) * (z * sg_z)).astype(BF16)
        ho_ref[...] = h_ref[...] + _dot(v, wo_ref[...])

    vec = _const((None, 1, e), (layer, 0, 0))
    row = lambda i: (i, 0)
    outs, xouts = _launch(
        body, name, t // tm,
        [pl.BlockSpec((tm, 3 * e), row), pl.BlockSpec((tm, d), row),
         _const((None, CONV_TAPS, e), (layer, 0, 0)), vec, vec, vec, _const((e, d), (0, 0))],
        [pl.BlockSpec((tm, d), row), pl.BlockSpec((tm, e + LANES), row), pl.BlockSpec((tm, 2 * e), row)],
        [jax.ShapeDtypeStruct((t, d), F32), jax.ShapeDtypeStruct((t, e + LANES), F32),
         jax.ShapeDtypeStruct((t, 2 * e), F32)],
        [pltpu.VMEM((SUBLANES, tm + CONV_HALO, e), F32), pltpu.VMEM((tm, e), F32)], (p, h, dw, *vecs, w_out), ex)
    return outs[0], tuple(outs[1:]), xouts


def _conv_bwd(dho, p, saved, dw, vecs, layer, w_out, name, ex=None):
    t, d = dho.shape
    e = w_out.shape[0]
    tm = _tile(t, 256)
    nt = t // tm
    rc = 16
    vec0 = CONV_TAPS + 1

    def body(dho_ref, p_ref, xr_ref, sg_ref, dw_ref, lng_ref, lnb_ref, wo_ref,
             dp_ref, dwo_ref, ddw_ref, dvec_ref, ds_scr, acc_scr, dwo_scr):
        i = pl.program_id(0)

        @pl.when(i == 0)
        def _():
            dwo_scr[...] = jnp.zeros_like(dwo_scr)
            acc_scr[...] = jnp.zeros_like(acc_scr)
            ds_scr[:, pl.ds(tm, CONV_HALO), :] = jnp.zeros((SUBLANES, CONV_HALO, e), F32)

        @pl.when(i > 0)
        def _():
            ds_scr[:, pl.ds(tm, CONV_HALO), :] = ds_scr[:, pl.ds(0, CONV_HALO), :]

        lng = lng_ref[...]
        lnb = lnb_ref[...]

        xhat = xr_ref[:, pl.ds(0, e)]
        rstd = xr_ref[:, pl.ds(e, 1)]
        ul = xhat * lng + lnb
        z = p_ref[:, pl.ds(2 * e, e)]
        sg_u = sg_ref[:, pl.ds(0, e)]
        sg_z = sg_ref[:, pl.ds(e, e)]
        s_u = ul * sg_u
        s_z = z * sg_z
        v = (s_u * s_z).astype(BF16)
        dy = dho_ref[...].astype(BF16)
        dv = _dot_nt(dy, wo_ref[...])
        dwo_scr[...] += _dot_tn(v, dy)

        dul = dv * s_z * _dsilu(ul, sg_u)
        dp_ref[:, pl.ds(2 * e, e)] = (dv * s_u * _dsilu(z, sg_z)).astype(BF16)
        acc_scr[pl.ds((vec0 + 1) * SUBLANES, SUBLANES), :] += _rows8(dul * xhat)
        acc_scr[pl.ds((vec0 + 2) * SUBLANES, SUBLANES), :] += _rows8(dul)
        dxh = dul * lng
        duc = rstd * (dxh - jnp.mean(dxh, axis=-1, keepdims=True)
                      - xhat * jnp.mean(dxh * xhat, axis=-1, keepdims=True))
        acc_scr[pl.ds(vec0 * SUBLANES, SUBLANES), :] += _rows8(duc)
        ds_scr[0, pl.ds(0, tm), :] = duc
        for r in range(1, SUBLANES):
            ds_scr[r, pl.ds(0, tm), :] = ds_scr[0, pl.ds(r, tm), :]

        def c_conv(base):
            rows = pl.ds(base, rc)
            a = p_ref[rows, pl.ds(0, e)]
            b = p_ref[rows, pl.ds(e, e)]
            sb = _sigmoid(b)
            u = a * sb
            du = jnp.zeros((rc, e), F32)
            for o in range(CONV_TAPS):
                q, r = divmod(o, SUBLANES)
                k = CONV_TAPS - 1 - o
                sh = ds_scr[r, pl.ds(pl.multiple_of(base + SUBLANES * q, SUBLANES), rc), :]
                du = du + dw_ref[pl.ds(k, 1), :] * sh
                acc_scr[pl.ds(k * SUBLANES, SUBLANES), :] += _rows8(u * sh)
            dp_ref[rows, pl.ds(0, e)] = (du * sb).astype(BF16)
            dp_ref[rows, pl.ds(e, e)] = (du * u * (1.0 - sb)).astype(BF16)
        _chunks(tm // rc, rc, c_conv)

        @pl.when(i == nt - 1)
        def _():
            dwo_ref[...] = dwo_scr[...].astype(BF16)
            slot_sum = lambda k: jnp.sum(acc_scr[pl.ds(k * SUBLANES, SUBLANES), :], axis=0, keepdims=True)
            for k in range(CONV_TAPS):
                ddw_ref[pl.ds(k, 1), :] = slot_sum(k)
            dvec_ref[...] = jnp.zeros_like(dvec_ref)
            for k in range(3):
                dvec_ref[pl.ds(k, 1), :] = slot_sum(vec0 + k)

    rev = lambda i: (nt - 1 - i, 0)
    vec = _const((None, 1, e), (layer, 0, 0))
    (dp, dwo, ddw, dvec), xouts = _launch(
        body, name, nt,
        [pl.BlockSpec((tm, d), rev), pl.BlockSpec((tm, 3 * e), rev), pl.BlockSpec((tm, e + LANES), rev),
         pl.BlockSpec((tm, 2 * e), rev),
         _const((None, CONV_TAPS, e), (layer, 0, 0)), vec, vec, _const((e, d), (0, 0))],
        [pl.BlockSpec((tm, 3 * e), rev), _const((e, d), (0, 0)), _const((CONV_TAPS, e), (0, 0)),
         _const((SUBLANES, e), (0, 0))],
        [jax.ShapeDtypeStruct((t, 3 * e), BF16), jax.ShapeDtypeStruct((e, d), BF16),
         jax.ShapeDtypeStruct((CONV_TAPS, e), F32), jax.ShapeDtypeStruct((SUBLANES, e), F32)],
        [pltpu.VMEM((SUBLANES, tm + CONV_HALO, e), F32), pltpu.VMEM(((vec0 + 3) * SUBLANES, e), F32),
         pltpu.VMEM((e, d), F32)],
        (dho, p, *saved, dw, vecs[1], vecs[2], w_out), ex)
    return dp, dwo, ddw, dvec, xouts


def _inv_count(tile, tm, w):
    tpos = tile * tm + lax.broadcasted_iota(jnp.int32, (tm, 1), 0)
    return 1.0 / jnp.minimum(tpos + 1, w).astype(F32)


def _pool_d_group(ue_scr, tile, tm, gc, g):
    w = POOL_WINDOWS[g]
    win = ue_scr[:, pl.ds(g * gc, gc)]
    s = win
    sh = 1
    while sh < w:
        s = s + pltpu.roll(s, sh, axis=0)
        sh *= 2
    return s[POOL_HALO:, :] * _inv_count(tile, tm, w) - win[POOL_HALO:, :]


def _pool_d(ue_scr, tile, tm, gc):
    return [_pool_d_group(ue_scr, tile, tm, gc, g) for g in range(len(POOL_WINDOWS))]


def _final_rows(ho, tg_ref, fg_ref, dh_ref, acc_scr, lacc_scr):
    d = ho.shape[-1]
    r = lax.rsqrt(jnp.mean(ho * ho, axis=-1, keepdims=True) + RMS_EPS)
    nrm = ho * r
    err = nrm * fg_ref[...] - tg_ref[...]
    lacc_scr[...] += _rows8(err * err)
    dy = err * (1.0 / d)
    acc_scr[...] += _rows8(dy * nrm)
    dq = dy * fg_ref[...]
    dh_ref[...] = r * (dq - nrm * jnp.mean(dq * nrm, axis=-1, keepdims=True))


def _pool_fwd(h, norm_g, nlayer, w_in, w_grp, vecs, layer, w_out, name, ex=None, final=None):
    t, d = h.shape
    e = w_out.shape[0]
    ng = len(POOL_WINDOWS)
    gc = e // ng
    tm = _tile(t, 512)
    nt = t // tm
    cw = 2 * e // (2 * ng)

    def layer_rows(i, hn_ref, hc_ref, g_ref, wi_ref, wg_ref, bg_ref, sc_ref, wo_ref, p_ref,
                   ue_scr, y_scr, pbuf, z_scr, hn_scr):
        tile = jnp.maximum(i - 1, 0)

        @pl.when(i == 0)
        def _():
            pbuf[...] = jnp.zeros_like(pbuf)

        @pl.when(i <= 1)
        def _():
            ue_scr[pl.ds(0, POOL_HALO), :] = jnp.zeros((POOL_HALO, e), F32)

        @pl.when(i > 1)
        def _():
            ue_scr[pl.ds(0, POOL_HALO), :] = ue_scr[pl.ds(tm, POOL_HALO), :]

        ue_scr[pl.ds(POOL_HALO, tm), :] = pbuf[:, pl.ds(0, e)]
        z_scr[...] = pbuf[:, pl.ds(e, e)]

        x = hn_ref[...]
        r = lax.rsqrt(jnp.mean(x * x, axis=-1, keepdims=True) + RMS_EPS)
        hn_scr[...] = (x * r * g_ref[...]).astype(BF16)

        def project(c):
            part = _dot(hn_scr[...], wi_ref[:, pl.ds(c * cw, cw)])
            pbuf[:, pl.ds(c * cw, cw)] = part
            p_ref[:, pl.ds(c * cw, cw)] = part

        for g in range(ng):
            project(2 * g)
            cols = pl.ds(g * gc, gc)
            dg = _pool_d_group(ue_scr, tile, tm, gc, g)
            z = z_scr[:, cols]
            y1 = (_dot(dg.astype(BF16), wg_ref[g]) + bg_ref[:, cols]) * sc_ref[:, cols]
            y_scr[:, cols] = (y1 * (z * _sigmoid(z))).astype(BF16)
            project(2 * g + 1)

        return hc_ref[...] + _dot(y_scr[...], wo_ref[...])

    nxt = lambda i: (jnp.minimum(i, nt - 1), 0)
    cur = lambda i: (jnp.maximum(i - 1, 0), 0)
    vec = _const((None, 1, e), (layer, 0, 0))
    in_specs = [pl.BlockSpec((tm, d), nxt), pl.BlockSpec((tm, d), cur), _const((None, 1, d), (nlayer, 0, 0)),
                _const((d, 2 * e), (0, 0)), _const((ng, gc, gc), (0, 0, 0)), vec, vec, _const((e, d), (0, 0))]
    scratch = [pltpu.VMEM((tm + POOL_HALO, e), F32), pltpu.VMEM((tm, e), BF16), pltpu.VMEM((tm, 2 * e), F32),
               pltpu.VMEM((tm, e), F32), pltpu.VMEM((tm, d), BF16)]
    args = (h, h, norm_g, w_in, w_grp, *vecs, w_out)

    if final is None:
        def body(hn_ref, hc_ref, g_ref, wi_ref, wg_ref, bg_ref, sc_ref, wo_ref, ho_ref, p_ref, *scr):
            ho_ref[...] = layer_rows(pl.program_id(0), hn_ref, hc_ref, g_ref, wi_ref, wg_ref, bg_ref, sc_ref, wo_ref,
                                     p_ref, *scr)

        (ho, p), xouts = _launch(
            body, name, nt + 1, in_specs, [pl.BlockSpec((tm, d), cur), pl.BlockSpec((tm, 2 * e), nxt)],
            [jax.ShapeDtypeStruct((t, d), F32), jax.ShapeDtypeStruct((t, 2 * e), F32)], scratch, args, ex)
        return ho, p, xouts

    target, final_g = final

    def body(hn_ref, hc_ref, g_ref, wi_ref, wg_ref, bg_ref, sc_ref, wo_ref, tg_ref, fg_ref,
             dh_ref, p_ref, dfg_ref, loss_ref, ue_scr, y_scr, pbuf, z_scr, hn_scr, acc_scr, lacc_scr):
        i = pl.program_id(0)

        @pl.when(i <= 1)
        def _():
            acc_scr[...] = jnp.zeros_like(acc_scr)
            lacc_scr[...] = jnp.zeros_like(lacc_scr)

        ho = layer_rows(i, hn_ref, hc_ref, g_ref, wi_ref, wg_ref, bg_ref, sc_ref, wo_ref, p_ref,
                        ue_scr, y_scr, pbuf, z_scr, hn_scr)
        _final_rows(ho, tg_ref, fg_ref, dh_ref, acc_scr, lacc_scr)

        @pl.when(i == nt)
        def _():
            dfg_ref[...] = jnp.zeros_like(dfg_ref)
            dfg_ref[pl.ds(0, 1), :] = jnp.sum(acc_scr[...], axis=0, keepdims=True)
            loss_ref[...] = jnp.broadcast_to(jnp.sum(lacc_scr[...]) * (0.5 / d), loss_ref.shape)

    (dh, p, dfg, loss), xouts = _launch(
        body, name, nt + 1, in_specs + [pl.BlockSpec((tm, d), cur), _const((1, d), (0, 0))],
        [pl.BlockSpec((tm, d), cur), pl.BlockSpec((tm, 2 * e), nxt), _const((SUBLANES, d), (0, 0)),
         _const((SUBLANES, LANES), (0, 0))],
        [jax.ShapeDtypeStruct((t, d), F32), jax.ShapeDtypeStruct((t, 2 * e), F32),
         jax.ShapeDtypeStruct((SUBLANES, d), F32), jax.ShapeDtypeStruct((SUBLANES, LANES), F32)],
        scratch + [pltpu.VMEM((SUBLANES, d), F32), pltpu.VMEM((SUBLANES, d), F32)],
        args + (target, final_g), ex)
    return dh, p, dfg, loss, xouts


def _pool_bwd(dho, p, w_grp, vecs, layer, w_out, name, ex=None):
    t, d = dho.shape
    e = w_out.shape[0]
    ng = len(POOL_WINDOWS)
    gc = e // ng
    tm = _tile(t, 512)
    nt = t // tm
    hb = tm // POOL_HALO

    def body(dho_ref, p_ref, ph_ref, wg_ref, bg_ref, sc_ref, wo_ref, dp_ref, dwo_ref, dwg_ref, dvec_ref,
             ue_scr, ee_scr, acc_scr, dwo_scr, dwg_scr):
        i = pl.program_id(0)
        tile = nt - 1 - i

        @pl.when(i == 0)
        def _():
            dwo_scr[...] = jnp.zeros_like(dwo_scr)
            dwg_scr[...] = jnp.zeros_like(dwg_scr)
            acc_scr[...] = jnp.zeros_like(acc_scr)
            ee_scr[pl.ds(tm, POOL_HALO), :] = jnp.zeros((POOL_HALO, e), F32)

        @pl.when(i > 0)
        def _():
            ee_scr[pl.ds(tm, POOL_HALO), :] = ee_scr[pl.ds(0, POOL_HALO), :]

        @pl.when(tile == 0)
        def _():
            ue_scr[pl.ds(0, POOL_HALO), :] = jnp.zeros((POOL_HALO, e), F32)

        @pl.when(tile > 0)
        def _():
            ue_scr[pl.ds(0, POOL_HALO), :] = ph_ref[:, pl.ds(0, e)]

        ue_scr[pl.ds(POOL_HALO, tm), :] = p_ref[:, pl.ds(0, e)]

        bg = bg_ref[...]
        sc = sc_ref[...]
        ds = [dg.astype(BF16) for dg in _pool_d(ue_scr, tile, tm, gc)]
        ob = jnp.concatenate([_dot(ds[g], wg_ref[g]) for g in range(ng)], axis=1) + bg
        z = p_ref[:, pl.ds(e, e)]
        sg_z = _sigmoid(z)
        s_z = z * sg_z
        y1 = ob * sc
        dy = dho_ref[...].astype(BF16)
        dy2 = _dot_nt(dy, wo_ref[...])
        dwo_scr[...] += _dot_tn((y1 * s_z).astype(BF16), dy)
        dy1 = dy2 * s_z
        dp_ref[:, pl.ds(e, e)] = (dy2 * y1 * _dsilu(z, sg_z)).astype(BF16)
        acc_scr[pl.ds(SUBLANES, SUBLANES), :] += _rows8(dy1 * ob)
        do = dy1 * sc
        acc_scr[pl.ds(0, SUBLANES), :] += _rows8(do)

        n = tm + POOL_HALO
        for g, w in enumerate(POOL_WINDOWS):
            cols = pl.ds(g * gc, gc)
            do_g = do[:, g * gc:(g + 1) * gc].astype(BF16)
            dwg_scr[g] += _dot_tn(ds[g], do_g)
            dd = _dot_nt(do_g, wg_ref[g])
            ee_scr[pl.ds(0, tm), cols] = dd * _inv_count(tile, tm, w)
            s = ee_scr[:, cols]
            sh = 1
            while sh < w:
                s = s + pltpu.roll(s, n - sh, axis=0)
                sh *= 2
            dp_ref[:, cols] = (s[:tm, :] - dd).astype(BF16)

        @pl.when(i == nt - 1)
        def _():
            dwo_ref[...] = dwo_scr[...].astype(BF16)
            dwg_ref[...] = dwg_scr[...].astype(BF16)
            dvec_ref[...] = jnp.zeros_like(dvec_ref)
            for k in range(2):
                dvec_ref[pl.ds(k, 1), :] = jnp.sum(acc_scr[pl.ds(k * SUBLANES, SUBLANES), :], axis=0, keepdims=True)

    rev = lambda i: (nt - 1 - i, 0)
    vec = _const((None, 1, e), (layer, 0, 0))
    (dp, dwo, dwg, dvec), xouts = _launch(
        body, name, nt,
        [pl.BlockSpec((tm, d), rev), pl.BlockSpec((tm, 2 * e), rev),
         pl.BlockSpec((POOL_HALO, 2 * e), lambda i: (jnp.maximum((nt - 1 - i) * hb - 1, 0), 0)),
         _const((ng, gc, gc), (0, 0, 0)), vec, vec, _const((e, d), (0, 0))],
        [pl.BlockSpec((tm, 2 * e), rev), _const((e, d), (0, 0)), _const((ng, gc, gc), (0, 0, 0)),
         _const((SUBLANES, e), (0, 0))],
        [jax.ShapeDtypeStruct((t, 2 * e), BF16), jax.ShapeDtypeStruct((e, d), BF16),
         jax.ShapeDtypeStruct((ng, gc, gc), BF16), jax.ShapeDtypeStruct((SUBLANES, e), F32)],
        [pltpu.VMEM((tm + POOL_HALO, e), F32), pltpu.VMEM((tm + POOL_HALO, e), F32),
         pltpu.VMEM((2 * SUBLANES, e), F32), pltpu.VMEM((e, d), F32), pltpu.VMEM((ng, gc, gc), F32)],
        (dho, p, p, w_grp, *vecs, w_out), ex)
    return dp, dwo, dwg, dvec, xouts


def _in_bwd(dp, h, dho, norm_g, layer, w_in, name, ex=None):
    t, d = h.shape
    n = w_in.shape[-1]
    tm = _tile(t, 512)
    nt = t // tm

    def body(dp_ref, h_ref, dho_ref, g_ref, w_ref, dh_ref, dw_ref, dg_ref, acc_scr, dw_scr):
        i = pl.program_id(0)

        @pl.when(i == 0)
        def _():
            dw_scr[...] = jnp.zeros_like(dw_scr)
            acc_scr[...] = jnp.zeros_like(acc_scr)

        x = h_ref[...]
        r = lax.rsqrt(jnp.mean(x * x, axis=-1, keepdims=True) + RMS_EPS)
        nrm = x * r
        dp = dp_ref[...]
        dhn = _dot_nt(dp, w_ref[...])
        dw_scr[...] += _dot_tn((nrm * g_ref[...]).astype(BF16), dp)
        acc_scr[...] += _rows8(dhn * nrm)
        dq = dhn * g_ref[...]
        dh_ref[...] = dho_ref[...] + r * (dq - nrm * jnp.mean(dq * nrm, axis=-1, keepdims=True))

        @pl.when(i == nt - 1)
        def _():
            dw_ref[...] = dw_scr[...].astype(BF16)
            dg_ref[...] = jnp.zeros_like(dg_ref)
            dg_ref[pl.ds(0, 1), :] = jnp.sum(acc_scr[...], axis=0, keepdims=True)

    (dh, dw, dg), xouts = _launch(
        body, name, nt,
        [pl.BlockSpec((tm, n), lambda i: (i, 0)), pl.BlockSpec((tm, d), lambda i: (i, 0)),
         pl.BlockSpec((tm, d), lambda i: (i, 0)), _const((None, 1, d), (layer, 0, 0)), _const((d, n), (0, 0))],
        [pl.BlockSpec((tm, d), lambda i: (i, 0)), _const((d, n), (0, 0)), _const((SUBLANES, d), (0, 0))],
        [jax.ShapeDtypeStruct((t, d), F32), jax.ShapeDtypeStruct((d, n), BF16),
         jax.ShapeDtypeStruct((SUBLANES, d), F32)],
        [pltpu.VMEM((SUBLANES, d), F32), pltpu.VMEM((d, n), F32)],
        (dp, h, dho, norm_g, w_in), ex)
    return dh, dw, dg, xouts


def _adam_update(g, w, m, v):
    c1 = 1.0 / (1.0 - ADAM_B1 ** ADAM_STEP)
    c2 = 1.0 / (1.0 - ADAM_B2 ** ADAM_STEP)
    nm = ADAM_B1 * m + (1.0 - ADAM_B1) * g
    nv = ADAM_B2 * v + (1.0 - ADAM_B2) * (g * g)
    return -ADAM_LR * ((nm * c1) / (jnp.sqrt(nv * c2) + ADAM_EPS) + ADAM_WD * w), nm, nv


def _adamw_small(params, stacks, loss_stack, name):
    ns, npar = len(stacks), len(params)

    def body(*refs):
        st = refs[:ns]
        pr = refs[ns:ns + 3 * npar]
        ls_ref = refs[ns + 3 * npar]
        outs = refs[ns + 3 * npar + 1:ns + 7 * npar + 1]
        loss_ref = refs[ns + 7 * npar + 1]
        for q, (w, _, _, pieces) in enumerate(params):
            w_ref, m_ref, v_ref = pr[3 * q:3 * q + 3]
            g_ref, d_ref, nm_ref, nv_ref = outs[4 * q:4 * q + 4]
            for s, row, slab in pieces:
                if w.ndim == 3:
                    take = lambda k: st[s][k]
                    at = slab
                else:
                    take = lambda k: st[s][k, pl.ds(row, 1), :]
                    at = (pl.ds(slab, 1), slice(None))
                g = take(0)
                for k in range(1, N_DEV):
                    g = g + take(k)
                g_ref[at] = g
                d_ref[at], nm_ref[at], nv_ref[at] = _adam_update(g, w_ref[at], m_ref[at], v_ref[at])
        tot = ls_ref[0]
        for k in range(1, N_DEV):
            tot = tot + ls_ref[k]
        loss_ref[...] = tot

    flat = [a for (w, m, v, _) in params for a in (w, m, v)]
    out_shape = [jax.ShapeDtypeStruct(w.shape, F32) for (w, _, _, _) in params for _ in range(4)]
    whole = pl.BlockSpec(memory_space=pltpu.VMEM)
    outs = pl.pallas_call(
        body, name=name, in_specs=[whole] * (ns + 3 * npar + 1), out_specs=[whole] * (4 * npar + 1),
        out_shape=out_shape + [jax.ShapeDtypeStruct(loss_stack.shape[1:], F32)],
    )(*stacks, *flat, loss_stack)
    return [outs[4 * q:4 * q + 4] for q in range(npar)], outs[-1]


def _adamw(stacks, w, m, v, name):
    nl = len(stacks)
    shp = w.shape
    c = shp[-1]
    r = 1
    for s in shp[1:-1]:
        r *= s
    tr = r
    for cand in (512, 256, 128, 64, 32, 16):
        if r % cand == 0 and r > cand:
            tr = cand
            break
    nrb = r // tr

    def body(*refs):
        s_refs = refs[:nl]
        w_ref, m_ref, v_ref, g_ref, d_ref, nm_ref, nv_ref = refs[nl:]
        layer = pl.program_id(0)
        for l in range(nl):
            @pl.when(layer == l)
            def _(l=l):
                g = s_refs[l][0].astype(F32)
                for k in range(1, N_DEV):
                    g = g + s_refs[l][k].astype(F32)
                g_ref[...] = g
                d_ref[...], nm_ref[...], nv_ref[...] = _adam_update(g, w_ref[...], m_ref[...], v_ref[...])

    def stack_spec(l):
        return pl.BlockSpec((N_DEV, tr, c),
                            lambda j, i: (0, jnp.where(j == l, i, jnp.where(j < l, 0, nrb - 1)), 0))

    spec = pl.BlockSpec((None, tr, c), lambda j, i: (j, i, 0))
    outs = pl.pallas_call(
        body, name=name, grid=(nl, nrb),
        in_specs=[stack_spec(l) for l in range(nl)] + [spec, spec, spec],
        out_specs=[spec] * 4,
        out_shape=[jax.ShapeDtypeStruct((nl, r, c), F32)] * 4,
        compiler_params=_params(2),
    )(*[s.reshape(N_DEV, r, c) for s in stacks], w.reshape(nl, r, c), m.reshape(nl, r, c), v.reshape(nl, r, c))
    return [o.reshape(shp) for o in outs]


def kernel(x, norm_g, final_g, conv_w_in, conv_dw, conv_dw_b, conv_ln_g, conv_ln_b, conv_w_out, pool_w_in, pool_w_grp, pool_b_grp, pool_scale, pool_w_out, loss_target, m_norm_g, m_final_g, m_conv_w_in, m_conv_dw, m_conv_dw_b, m_conv_ln_g, m_conv_ln_b, m_conv_w_out, m_pool_w_in, m_pool_w_grp, m_pool_b_grp, m_pool_scale, m_pool_w_out, v_norm_g, v_final_g, v_conv_w_in, v_conv_dw, v_conv_dw_b, v_conv_ln_g, v_conv_ln_b, v_conv_w_out, v_pool_w_in, v_pool_w_grp, v_pool_b_grp, v_pool_scale, v_pool_w_out):
    h0 = x[0]
    target = loss_target[0]
    ng3 = norm_g[:, None, :]
    row3 = lambda a: a[:, None, :]
    conv_vecs = (row3(conv_dw_b), row3(conv_ln_g), row3(conv_ln_b))
    gather = lambda arrays, axes: _Exchange("gather", arrays, axes)
    scatter = lambda arrays, axes: _Exchange("scatter", arrays, axes)

    cwi, cwo, pwi = conv_w_in.astype(BF16), conv_w_out.astype(BF16), pool_w_in.astype(BF16)
    pwg, pwo = pool_w_grp.astype(BF16), pool_w_out.astype(BF16)

    (cw_in0,) = _gather_via_sibling_call(gather([(cwi, 0)], [1]), "gather_first")
    p0, (cw_out0, dw_full, bg_full, sc_full, pw_in0) = _in_fwd(
        h0, ng3, 0, cw_in0, "conv_in_fwd_0",
        gather([(cwo, 0), conv_dw, pool_b_grp, pool_scale, (pwi, 0)], [0, 2, 1, 1, 1]))
    pool_vecs = (row3(bg_full), row3(sc_full))
    h1, uc0, (pw_grp0, pw_out0, cw_in1, cw_out1, pw_in1, pw_grp1, pw_out1) = _conv_fwd(
        p0, h0, dw_full, conv_vecs, 0, cw_out0, "conv_mix_fwd_0",
        gather([(pwg, 0), (pwo, 0), (cwi, 1), (cwo, 1), (pwi, 1), (pwg, 1), (pwo, 1)], [1, 0, 1, 0, 1, 1, 0]))
    h2, p1, _ = _pool_fwd(h1, ng3, 1, pw_in0, pw_grp0, pool_vecs, 0, pw_out0, "pool_fwd_0")
    p2, _ = _in_fwd(h2, ng3, 2, cw_in1, "conv_in_fwd_1")
    h3, uc2, _ = _conv_fwd(p2, h2, dw_full, conv_vecs, 1, cw_out1, "conv_mix_fwd_1")
    dh, p3, d_final_g, loss_part, _ = _pool_fwd(h3, ng3, 3, pw_in1, pw_grp1, pool_vecs, 1, pw_out1, "pool_fwd_1",
                                               final=(target, final_g[None, :]))

    dp, g_pwo1, g_pwg1, dpv1, _ = _pool_bwd(dh, p3, pw_grp1, pool_vecs, 1, pw_out1, "pool_mix_bwd_1")
    dh, g_pwi1, dg3, _ = _in_bwd(dp, h3, dh, ng3, 3, pw_in1, "pool_in_bwd_1")
    dp, g_cwo1, ddw1, dcv1, (s_pwo1, s_pwg1, s_pwi1) = _conv_bwd(
        dh, p2, uc2, dw_full, conv_vecs, 1, cw_out1, "conv_mix_bwd_1", scatter([g_pwo1, g_pwg1, g_pwi1], [0, 1, 1]))
    dh, g_cwi1, dg2, _ = _in_bwd(dp, h2, dh, ng3, 2, cw_in1, "conv_in_bwd_1")
    dp, g_pwo0, g_pwg0, dpv0, (s_cwo1, s_cwi1) = _pool_bwd(dh, p1, pw_grp0, pool_vecs, 0, pw_out0, "pool_mix_bwd_0",
                                                           scatter([g_cwo1, g_cwi1], [0, 1]))
    dh, g_pwi0, dg1, _ = _in_bwd(dp, h1, dh, ng3, 1, pw_in0, "pool_in_bwd_0")
    dp, g_cwo0, ddw0, dcv0, (s_pwo0, s_pwg0, s_pwi0, s_ddw1, s_dcv1, s_dpv0, s_dpv1, s_dg1, s_dg2, s_dg3, s_dfg,
                             s_loss) = _conv_bwd(
        dh, p0, uc0, dw_full, conv_vecs, 0, cw_out0, "conv_mix_bwd_0",
        scatter([g_pwo0, g_pwg0, g_pwi0, ddw1, dcv1, dpv0, dpv1, dg1, dg2, dg3, d_final_g, loss_part],
                [0, 1, 1, 1, None, 1, 1, None, None, None, None, None]))
    dh, g_cwi0, dg0, _ = _in_bwd(dp, h0, dh, ng3, 0, cw_in0, "conv_in_bwd_0")
    grad_x = dh[None]
    s_cwo0, s_cwi0, s_ddw0, s_dcv0, s_dg0 = _exchange_call(
        scatter([g_cwo0, g_cwi0, ddw0, dcv0, dg0], [0, 1, 1, None, None]), "scatter_last")

    res = {}
    res["conv_w_in"] = _adamw([s_cwi0, s_cwi1], conv_w_in, m_conv_w_in, v_conv_w_in, "adamw_conv_w_in")
    res["conv_w_out"] = _adamw([s_cwo0, s_cwo1], conv_w_out, m_conv_w_out, v_conv_w_out, "adamw_conv_w_out")
    res["pool_w_in"] = _adamw([s_pwi0, s_pwi1], pool_w_in, m_pool_w_in, v_pool_w_in, "adamw_pool_w_in")
    res["pool_w_grp"] = _adamw([s_pwg0, s_pwg1], pool_w_grp, m_pool_w_grp, v_pool_w_grp, "adamw_pool_w_grp")
    res["pool_w_out"] = _adamw([s_pwo0, s_pwo1], pool_w_out, m_pool_w_out, v_pool_w_out, "adamw_pool_w_out")
    stacks = [s_ddw0, s_ddw1, s_dpv0, s_dpv1, s_dg0, s_dg1, s_dg2, s_dg3, s_dfg, s_dcv0, s_dcv1]
    small = [
        ("conv_dw", conv_dw, m_conv_dw, v_conv_dw, [(0, None, 0), (1, None, 1)]),
        ("pool_b_grp", pool_b_grp, m_pool_b_grp, v_pool_b_grp, [(2, 0, 0), (3, 0, 1)]),
        ("pool_scale", pool_scale, m_pool_scale, v_pool_scale, [(2, 1, 0), (3, 1, 1)]),
        ("norm_g", norm_g, m_norm_g, v_norm_g, [(4, 0, 0), (5, 0, 1), (6, 0, 2), (7, 0, 3)]),
        ("final_g", final_g[None, :], m_final_g[None, :], v_final_g[None, :], [(8, 0, 0)]),
        ("conv_dw_b", conv_dw_b, m_conv_dw_b, v_conv_dw_b, [(9, 0, 0), (10, 0, 1)]),
        ("conv_ln_g", conv_ln_g, m_conv_ln_g, v_conv_ln_g, [(9, 1, 0), (10, 1, 1)]),
        ("conv_ln_b", conv_ln_b, m_conv_ln_b, v_conv_ln_b, [(9, 2, 0), (10, 2, 1)]),
    ]
    small_res, loss_block = _adamw_small([s[1:] for s in small], stacks, s_loss, "adamw_small")
    for (name, *_), r in zip(small, small_res):
        res[name] = [a[0] for a in r] if name == "final_g" else r
    loss = loss_block[0, 0]

    names = ["norm_g", "final_g", "conv_w_in", "conv_dw", "conv_dw_b", "conv_ln_g", "conv_ln_b", "conv_w_out",
             "pool_w_in", "pool_w_grp", "pool_b_grp", "pool_scale", "pool_w_out"]
    return (loss, grad_x) + tuple(res[n][q] for q in range(4) for n in names)
```

```python
import jax
import jax.numpy as jnp
from jax import lax
from jax.experimental import pallas as pl
from jax.experimental.pallas import tpu as pltpu

F32 = jnp.float32
BF16 = jnp.bfloat16

RMS_EPS = 1e-6
LN_EPS = 1e-5
CONV_TAPS = 31
CONV_HALO = 32
POOL_WINDOWS = (2, 4, 8, 16)
POOL_HALO = 16
SUBLANES = 8
LANES = 128
N_DEV = 8
V7X_VMEM_LIMIT = 60 * 1024 * 1024

ADAM_LR = 0.001
ADAM_B1 = 0.9
ADAM_B2 = 0.999
ADAM_EPS = 1e-08
ADAM_WD = 0.01
ADAM_STEP = 10

MESH = pl.DeviceIdType.MESH
ANY = pl.BlockSpec(memory_space=pl.ANY)


def _dot(a, b):
    return lax.dot_general(a, b, (((1,), (0,)), ((), ())), preferred_element_type=F32)


def _dot_nt(a, b):
    return lax.dot_general(a, b, (((1,), (1,)), ((), ())), preferred_element_type=F32)


def _dot_tn(a, b):
    return lax.dot_general(a, b, (((0,), (0,)), ((), ())), preferred_element_type=F32)


def _sigmoid(x):
    return jax.nn.sigmoid(x)


def _dsilu(x, s):
    return s * (1.0 + x * (1.0 - s))


def _rows8(x):
    r, c = x.shape
    return jnp.sum(x.reshape(r // SUBLANES, SUBLANES, c), axis=0)


def _tile(t, pref):
    return pref if t >= 2 * pref else t // 2


def _const(shape, index):
    return pl.BlockSpec(shape, lambda *_: index, pipeline_mode=pl.Buffered(1))


def _params(grid_rank=1):
    return pltpu.CompilerParams(dimension_semantics=("arbitrary",) * grid_rank, vmem_limit_bytes=V7X_VMEM_LIMIT)


def _chunks(n, rc, fn):
    def step(c, carry):
        fn(pl.multiple_of(c * rc, rc))
        return carry
    lax.fori_loop(0, n, step, 0)


def _mesh_position():
    return lax.axis_index("x"), lax.axis_index("y"), lax.axis_index("c")


def _peer(j):
    x, y, c = _mesh_position()
    px = 1 - x if j & 4 else x
    py = 1 - y if j & 2 else y
    pc = 1 - c if j & 1 else c
    return (px, py, pc), 4 * px + 2 * py + pc


def _block(ref, axis, index, size):
    idx = [slice(None)] * len(ref.shape)
    idx[axis] = pl.ds(pl.multiple_of(index * size, size), size)
    return ref.at[tuple(idx)]


class _Exchange:
    def __init__(self, kind, arrays, axes):
        self.kind, self.axes = kind, list(axes)
        self.arrays = [a[0] if isinstance(a, tuple) else a for a in arrays]
        self.layers = [a[1] if isinstance(a, tuple) else None for a in arrays]
        self.n = len(self.arrays)
        self.blk, self.out_shapes = [], []
        for a, layer, ax in zip(self.arrays, self.layers, self.axes):
            s = list(a.shape if layer is None else a.shape[1:])
            if kind == "gather":
                self.blk.append(s[ax])
                s[ax] *= N_DEV
                self.out_shapes.append(jax.ShapeDtypeStruct(tuple(s), a.dtype))
            else:
                if ax is not None:
                    s[ax] //= N_DEV
                    self.blk.append(s[ax])
                else:
                    self.blk.append(None)
                self.out_shapes.append(jax.ShapeDtypeStruct((N_DEV,) + tuple(s), a.dtype))

    def sem_shapes(self):
        return [pltpu.SemaphoreType.DMA((N_DEV - 1, self.n)), pltpu.SemaphoreType.DMA((N_DEV - 1, self.n)),
                pltpu.SemaphoreType.DMA((self.n,))]

    def _src(self, ins, k, owner):
        ref = ins[k] if self.layers[k] is None else ins[k].at[self.layers[k]]
        if self.kind == "gather" or self.axes[k] is None:
            return ref
        return _block(ref, self.axes[k], owner, self.blk[k])

    def _dst(self, outs, k, sender):
        if self.kind == "gather":
            return _block(outs[k], self.axes[k], sender, self.blk[k])
        return outs[k].at[sender]

    def _copies(self, ins, outs, sems, arriving):
        send, recv, loc = sems
        x, y, c = _mesh_position()
        me = 4 * x + 2 * y + c
        if not arriving:
            local = [pltpu.make_async_copy(self._src(ins, k, me), self._dst(outs, k, me), loc.at[k])
                     for k in range(self.n)]
        else:
            local = []
        remote = []
        for j in range(1, N_DEV):
            peer, peer_id = _peer(j)
            for k in range(self.n):
                owner, sender = (me, peer_id) if arriving else (peer_id, me)
                remote.append(pltpu.make_async_remote_copy(
                    src_ref=self._src(ins, k, owner), dst_ref=self._dst(outs, k, sender),
                    send_sem=send.at[j - 1, k], recv_sem=recv.at[j - 1, k], device_id=peer, device_id_type=MESH))
        return local, remote

    def start(self, ins, outs, sems):
        local, sends = self._copies(ins, outs, sems, arriving=False)
        for cp in local + sends:
            cp.start()

    def finish(self, ins, outs, sems):
        for cp in self._copies(ins, outs, sems, arriving=True)[1]:
            cp.wait_recv()
        local, sends = self._copies(ins, outs, sems, arriving=False)
        for cp in sends:
            cp.wait_send()
        for cp in local:
            cp.wait()


def _exchange_call(ex, name):
    def body(*refs):
        ins, outs, sems = refs[:ex.n], refs[ex.n:2 * ex.n], refs[2 * ex.n:]
        ex.start(ins, outs, sems)
        ex.finish(ins, outs, sems)

    return pl.pallas_call(body, name=name, in_specs=[ANY] * ex.n, out_specs=[ANY] * ex.n,
                          out_shape=ex.out_shapes, scratch_shapes=ex.sem_shapes())(*ex.arrays)


def _gather_via_sibling_call(ex, name):
    n = ex.n

    def body(*refs):
        ins, outs = refs[:n], refs[n:2 * n]
        send, recv, loc = refs[2 * n:]
        x, y, c = _mesh_position()
        ident = lambda px, py, pc: 4 * px + 2 * py + pc
        me, sibling = ident(x, y, c), (x, y, 1 - c)
        chips = [(1 - x, y), (x, 1 - y), (1 - x, 1 - y)]

        def copy(row, k, block, to, src=None):
            place = ex._dst(outs, k, block)
            return pltpu.make_async_remote_copy(
                src_ref=place if src is None else src, dst_ref=place,
                send_sem=send.at[row, k], recv_sem=recv.at[row, k], device_id=to, device_id_type=MESH)

        local = [pltpu.make_async_copy(ex._src(ins, k, me), ex._dst(outs, k, me), loc.at[k]) for k in range(n)]
        first = []
        for k in range(n):
            mine = ex._src(ins, k, me)
            first.append(copy(0, k, me, sibling, src=mine))
            first += [copy(1 + j, k, me, (*chip, c), src=mine) for j, chip in enumerate(chips)]
        for cp in local + first:
            cp.start()
        passed = []
        for j, chip in enumerate(chips):
            for k in range(n):
                copy(1 + j, k, ident(*chip, c), sibling).wait_recv()
                passed.append(copy(4 + j, k, ident(*chip, c), sibling))
                passed[-1].start()
        for k in range(n):
            copy(0, k, ident(x, y, 1 - c), sibling).wait_recv()
            for j, chip in enumerate(chips):
                copy(4 + j, k, ident(*chip, 1 - c), sibling).wait_recv()
        for cp in first + passed:
            cp.wait_send()
        for cp in local:
            cp.wait()

    return pl.pallas_call(body, name=name, in_specs=[ANY] * n, out_specs=[ANY] * n, out_shape=ex.out_shapes,
                          scratch_shapes=ex.sem_shapes())(*ex.arrays)


def _launch(body, name, nt, in_specs, out_specs, out_shape, scratch_shapes, args, ex=None):
    if ex is None:
        outs = pl.pallas_call(body, name=name, grid=(nt,), in_specs=in_specs, out_specs=out_specs,
                              out_shape=out_shape, scratch_shapes=scratch_shapes, compiler_params=_params())(*args)
        return list(outs), []
    n_in, n_out, n_scr = len(in_specs), len(out_specs), len(scratch_shapes)

    def riding(*refs):
        a, xa = refs[:n_in], refs[n_in:n_in + ex.n]
        o = refs[n_in + ex.n:n_in + ex.n + n_out]
        xo = refs[n_in + ex.n + n_out:n_in + 2 * ex.n + n_out]
        s = refs[n_in + 2 * ex.n + n_out:n_in + 2 * ex.n + n_out + n_scr]
        sems = refs[n_in + 2 * ex.n + n_out + n_scr:]
        i = pl.program_id(0)

        @pl.when(i == 0)
        def _():
            ex.start(xa, xo, sems)

        body(*a, *o, *s)

        @pl.when(i == nt - 1)
        def _():
            ex.finish(xa, xo, sems)

    outs = pl.pallas_call(
        riding, name=name, grid=(nt,),
        in_specs=list(in_specs) + [ANY] * ex.n, out_specs=list(out_specs) + [ANY] * ex.n,
        out_shape=list(out_shape) + ex.out_shapes, scratch_shapes=list(scratch_shapes) + ex.sem_shapes(),
        compiler_params=_params())(*args, *ex.arrays)
    return list(outs[:n_out]), list(outs[n_out:])


def _in_fwd(h, norm_g, layer, w_in, name, ex=None):
    t, d = h.shape
    n = w_in.shape[-1]
    tm = _tile(t, 512)

    def body(h_ref, g_ref, w_ref, p_ref):
        x = h_ref[...]
        r = lax.rsqrt(jnp.mean(x * x, axis=-1, keepdims=True) + RMS_EPS)
        p_ref[...] = _dot((x * r * g_ref[...]).astype(BF16), w_ref[...])

    (p,), xouts = _launch(
        body, name, t // tm,
        [pl.BlockSpec((tm, d), lambda i: (i, 0)), _const((None, 1, d), (layer, 0, 0)), _const((d, n), (0, 0))],
        [pl.BlockSpec((tm, n), lambda i: (i, 0))],
        [jax.ShapeDtypeStruct((t, n), F32)], [], (h, norm_g, w_in), ex)
    return p, xouts


def _layernorm_rows(uc, lng, lnb):
    mu = jnp.mean(uc, axis=-1, keepdims=True)
    xc = uc - mu
    rstd = lax.rsqrt(jnp.mean(xc * xc, axis=-1, keepdims=True) + LN_EPS)
    xhat = xc * rstd
    return xhat, rstd, xhat * lng + lnb


def _conv_fwd(p, h, dw, vecs, layer, w_out, name, ex=None):
    t, d = h.shape
    e = w_out.shape[0]
    tm = _tile(t, 256)
    rc = _tile(tm, 128)
    nsub = 2 if t >= 4 * tm else 1
    bt = nsub * tm

    def body(p_ref, h_ref, dw_ref, dwb_ref, lng_ref, lnb_ref, wo_ref, ho_ref, xr_ref, sg_ref, us_scr, uc_ref):
        i = pl.program_id(0)

        def sub_tile(s, carry):
            rows = pl.ds(pl.multiple_of(s * tm, tm), tm)
            first = jnp.logical_and(i == 0, s == 0)

            @pl.when(first)
            def _():
                us_scr[:, pl.ds(0, CONV_HALO), :] = jnp.zeros((SUBLANES, CONV_HALO, e), F32)

            @pl.when(jnp.logical_not(first))
            def _():
                us_scr[:, pl.ds(0, CONV_HALO), :] = us_scr[:, pl.ds(tm, CONV_HALO), :]

            us_scr[0, pl.ds(CONV_HALO, tm), :] = p_ref[rows, pl.ds(0, e)] * _sigmoid(p_ref[rows, pl.ds(e, e)])
            for r in range(1, SUBLANES):
                us_scr[r, pl.ds(CONV_HALO, tm), :] = us_scr[0, pl.ds(CONV_HALO - r, tm), :]

            def c_conv(base):
                for lt in range(e // LANES):
                    cols = pl.ds(lt * LANES, LANES)
                    acc = jnp.broadcast_to(dwb_ref[:, cols], (rc, LANES))
                    for r in range(SUBLANES):
                        nq = (CONV_TAPS - 1 - r) // SUBLANES + 1
                        lo = SUBLANES * (nq - 1)
                        win = us_scr[r, pl.ds(pl.multiple_of(CONV_HALO + base - lo, SUBLANES), rc + lo), cols]
                        for q in range(nq):
                            k = CONV_TAPS - 1 - (SUBLANES * q + r)
                            at = lo - SUBLANES * q
                            acc = acc + dw_ref[pl.ds(k, 1), cols] * win[at:at + rc, :]
                    uc_ref[pl.ds(base, rc), cols] = acc
            _chunks(tm // rc, rc, c_conv)

            xhat, rstd, ul = _layernorm_rows(uc_ref[...], lng_ref[...], lnb_ref[...])
            z = p_ref[rows, pl.ds(2 * e, e)]
            sg_u = _sigmoid(ul)
            sg_z = _sigmoid(z)
            xr_ref[rows, pl.ds(0, e)] = xhat
            xr_ref[rows, pl.ds(e, LANES)] = jnp.broadcast_to(rstd, (tm, LANES))
            sg_ref[rows, pl.ds(0, e)] = sg_u
            sg_ref[rows, pl.ds(e, e)] = sg_z
            v = ((ul * sg_u) * (z * sg_z)).astype(BF16)
            ho_ref[rows, :] = h_ref[rows, :] + _dot(v, wo_ref[...])
            return carry

        lax.fori_loop(0, nsub, sub_tile, 0)

    vec = _const((None, 1, e), (layer, 0, 0))
    row = lambda i: (i, 0)
    outs, xouts = _launch(
        body, name, t // bt,
        [pl.BlockSpec((bt, 3 * e), row), pl.BlockSpec((bt, d), row),
         _const((None, CONV_TAPS, e), (layer, 0, 0)), vec, vec, vec, _const((e, d), (0, 0))],
        [pl.BlockSpec((bt, d), row), pl.BlockSpec((bt, e + LANES), row), pl.BlockSpec((bt, 2 * e), row)],
        [jax.ShapeDtypeStruct((t, d), F32), jax.ShapeDtypeStruct((t, e + LANES), F32),
         jax.ShapeDtypeStruct((t, 2 * e), F32)],
        [pltpu.VMEM((SUBLANES, tm + CONV_HALO, e), F32), pltpu.VMEM((tm, e), F32)], (p, h, dw, *vecs, w_out), ex)
    return outs[0], tuple(outs[1:]), xouts


def _conv_bwd(dho, p, saved, dw, vecs, layer, w_out, name, ex=None):
    t, d = dho.shape
    e = w_out.shape[0]
    tm = _tile(t, 256)
    nsub = 2 if t >= 4 * tm else 1
    bt = nsub * tm
    nt = t // bt
    rc = 16
    vec0 = CONV_TAPS + 1

    def body(dho_ref, p_ref, xr_ref, sg_ref, dw_ref, lng_ref, lnb_ref, wo_ref,
             dp_ref, dwo_ref, ddw_ref, dvec_ref, ds_scr, acc_scr, dwo_scr):
        i = pl.program_id(0)

        @pl.when(i == 0)
        def _():
            dwo_scr[...] = jnp.zeros_like(dwo_scr)
            acc_scr[...] = jnp.zeros_like(acc_scr)

        lng = lng_ref[...]
        lnb = lnb_ref[...]

        def sub_tile(s, carry):
            sub = nsub - 1 - s
            rows = pl.ds(pl.multiple_of(sub * tm, tm), tm)
            first = jnp.logical_and(i == 0, s == 0)

            @pl.when(first)
            def _():
                ds_scr[:, pl.ds(tm, CONV_HALO), :] = jnp.zeros((SUBLANES, CONV_HALO, e), F32)

            @pl.when(jnp.logical_not(first))
            def _():
                ds_scr[:, pl.ds(tm, CONV_HALO), :] = ds_scr[:, pl.ds(0, CONV_HALO), :]

            xhat = xr_ref[rows, pl.ds(0, e)]
            rstd = xr_ref[rows, pl.ds(e, 1)]
            ul = xhat * lng + lnb
            z = p_ref[rows, pl.ds(2 * e, e)]
            sg_u = sg_ref[rows, pl.ds(0, e)]
            sg_z = sg_ref[rows, pl.ds(e, e)]
            s_u = ul * sg_u
            s_z = z * sg_z
            v = (s_u * s_z).astype(BF16)
            dy = dho_ref[rows, :].astype(BF16)
            dv = _dot_nt(dy, wo_ref[...])
            dwo_scr[...] += _dot_tn(v, dy)

            dul = dv * s_z * _dsilu(ul, sg_u)
            dp_ref[rows, pl.ds(2 * e, e)] = (dv * s_u * _dsilu(z, sg_z)).astype(BF16)
            acc_scr[pl.ds((vec0 + 1) * SUBLANES, SUBLANES), :] += _rows8(dul * xhat)
            acc_scr[pl.ds((vec0 + 2) * SUBLANES, SUBLANES), :] += _rows8(dul)
            dxh = dul * lng
            duc = rstd * (dxh - jnp.mean(dxh, axis=-1, keepdims=True)
                          - xhat * jnp.mean(dxh * xhat, axis=-1, keepdims=True))
            acc_scr[pl.ds(vec0 * SUBLANES, SUBLANES), :] += _rows8(duc)
            ds_scr[0, pl.ds(0, tm), :] = duc
            for r in range(1, SUBLANES):
                ds_scr[r, pl.ds(0, tm), :] = ds_scr[0, pl.ds(r, tm), :]

            def c_conv(base):
                prow = pl.ds(pl.multiple_of(sub * tm + base, rc), rc)
                a = p_ref[prow, pl.ds(0, e)]
                b = p_ref[prow, pl.ds(e, e)]
                sb = _sigmoid(b)
                u = a * sb
                du = jnp.zeros((rc, e), F32)
                for o in range(CONV_TAPS):
                    q, r = divmod(o, SUBLANES)
                    k = CONV_TAPS - 1 - o
                    sh = ds_scr[r, pl.ds(pl.multiple_of(base + SUBLANES * q, SUBLANES), rc), :]
                    du = du + dw_ref[pl.ds(k, 1), :] * sh
                    acc_scr[pl.ds(k * SUBLANES, SUBLANES), :] += _rows8(u * sh)
                dp_ref[prow, pl.ds(0, e)] = (du * sb).astype(BF16)
                dp_ref[prow, pl.ds(e, e)] = (du * u * (1.0 - sb)).astype(BF16)
            _chunks(tm // rc, rc, c_conv)
            return carry

        lax.fori_loop(0, nsub, sub_tile, 0)

        @pl.when(i == nt - 1)
        def _():
            dwo_ref[...] = dwo_scr[...].astype(BF16)
            slot_sum = lambda k: jnp.sum(acc_scr[pl.ds(k * SUBLANES, SUBLANES), :], axis=0, keepdims=True)
            for k in range(CONV_TAPS):
                ddw_ref[pl.ds(k, 1), :] = slot_sum(k)
            dvec_ref[...] = jnp.zeros_like(dvec_ref)
            for k in range(3):
                dvec_ref[pl.ds(k, 1), :] = slot_sum(vec0 + k)

    rev = lambda i: (nt - 1 - i, 0)
    vec = _const((None, 1, e), (layer, 0, 0))
    (dp, dwo, ddw, dvec), xouts = _launch(
        body, name, nt,
        [pl.BlockSpec((bt, d), rev), pl.BlockSpec((bt, 3 * e), rev), pl.BlockSpec((bt, e + LANES), rev),
         pl.BlockSpec((bt, 2 * e), rev),
         _const((None, CONV_TAPS, e), (layer, 0, 0)), vec, vec, _const((e, d), (0, 0))],
        [pl.BlockSpec((bt, 3 * e), rev), _const((e, d), (0, 0)), _const((CONV_TAPS, e), (0, 0)),
         _const((SUBLANES, e), (0, 0))],
        [jax.ShapeDtypeStruct((t, 3 * e), BF16), jax.ShapeDtypeStruct((e, d), BF16),
         jax.ShapeDtypeStruct((CONV_TAPS, e), F32), jax.ShapeDtypeStruct((SUBLANES, e), F32)],
        [pltpu.VMEM((SUBLANES, tm + CONV_HALO, e), F32), pltpu.VMEM(((vec0 + 3) * SUBLANES, e), F32),
         pltpu.VMEM((e, d), F32)],
        (dho, p, *saved, dw, vecs[1], vecs[2], w_out), ex)
    return dp, dwo, ddw, dvec, xouts


def _inv_count(tile, tm, w):
    tpos = tile * tm + lax.broadcasted_iota(jnp.int32, (tm, 1), 0)
    return 1.0 / jnp.minimum(tpos + 1, w).astype(F32)


def _pool_d_group(ue_scr, tile, tm, gc, g):
    w = POOL_WINDOWS[g]
    win = ue_scr[:, pl.ds(g * gc, gc)]
    s = win
    sh = 1
    while sh < w:
        s = s + pltpu.roll(s, sh, axis=0)
        sh *= 2
    return s[POOL_HALO:, :] * _inv_count(tile, tm, w) - win[POOL_HALO:, :]


def _pool_d(ue_scr, tile, tm, gc):
    return [_pool_d_group(ue_scr, tile, tm, gc, g) for g in range(len(POOL_WINDOWS))]


def _final_rows(ho, tg_ref, fg_ref, dh_ref, acc_scr, lacc_scr):
    d = ho.shape[-1]
    r = lax.rsqrt(jnp.mean(ho * ho, axis=-1, keepdims=True) + RMS_EPS)
    nrm = ho * r
    err = nrm * fg_ref[...] - tg_ref[...]
    lacc_scr[...] += _rows8(err * err)
    dy = err * (1.0 / d)
    acc_scr[...] += _rows8(dy * nrm)
    dq = dy * fg_ref[...]
    dh_ref[...] = r * (dq - nrm * jnp.mean(dq * nrm, axis=-1, keepdims=True))


def _pool_fwd(h, norm_g, nlayer, w_in, w_grp, vecs, layer, w_out, name, ex=None, final=None):
    t, d = h.shape
    e = w_out.shape[0]
    ng = len(POOL_WINDOWS)
    gc = e // ng
    tm = _tile(t, 512)
    nt = t // tm
    cw = 2 * e // (2 * ng)

    def layer_rows(i, hn_ref, hc_ref, g_ref, wi_ref, wg_ref, bg_ref, sc_ref, wo_ref, p_ref,
                   ue_scr, y_scr, pbuf, z_scr, hn_scr):
        tile = jnp.maximum(i - 1, 0)

        @pl.when(i == 0)
        def _():
            pbuf[...] = jnp.zeros_like(pbuf)

        @pl.when(i <= 1)
        def _():
            ue_scr[pl.ds(0, POOL_HALO), :] = jnp.zeros((POOL_HALO, e), F32)

        @pl.when(i > 1)
        def _():
            ue_scr[pl.ds(0, POOL_HALO), :] = ue_scr[pl.ds(tm, POOL_HALO), :]

        ue_scr[pl.ds(POOL_HALO, tm), :] = pbuf[:, pl.ds(0, e)]
        z_scr[...] = pbuf[:, pl.ds(e, e)]

        x = hn_ref[...]
        r = lax.rsqrt(jnp.mean(x * x, axis=-1, keepdims=True) + RMS_EPS)
        hn_scr[...] = (x * r * g_ref[...]).astype(BF16)

        def project(c):
            part = _dot(hn_scr[...], wi_ref[:, pl.ds(c * cw, cw)])
            pbuf[:, pl.ds(c * cw, cw)] = part
            p_ref[:, pl.ds(c * cw, cw)] = part

        for g in range(ng):
            project(2 * g)
            cols = pl.ds(g * gc, gc)
            dg = _pool_d_group(ue_scr, tile, tm, gc, g)
            z = z_scr[:, cols]
            y1 = (_dot(dg.astype(BF16), wg_ref[g]) + bg_ref[:, cols]) * sc_ref[:, cols]
            y_scr[:, cols] = (y1 * (z * _sigmoid(z))).astype(BF16)
            project(2 * g + 1)

        return hc_ref[...] + _dot(y_scr[...], wo_ref[...])

    nxt = lambda i: (jnp.minimum(i, nt - 1), 0)
    cur = lambda i: (jnp.maximum(i - 1, 0), 0)
    vec = _const((None, 1, e), (layer, 0, 0))
    in_specs = [pl.BlockSpec((tm, d), nxt), pl.BlockSpec((tm, d), cur), _const((None, 1, d), (nlayer, 0, 0)),
                _const((d, 2 * e), (0, 0)), _const((ng, gc, gc), (0, 0, 0)), vec, vec, _const((e, d), (0, 0))]
    scratch = [pltpu.VMEM((tm + POOL_HALO, e), F32), pltpu.VMEM((tm, e), BF16), pltpu.VMEM((tm, 2 * e), F32),
               pltpu.VMEM((tm, e), F32), pltpu.VMEM((tm, d), BF16)]
    args = (h, h, norm_g, w_in, w_grp, *vecs, w_out)

    if final is None:
        def body(hn_ref, hc_ref, g_ref, wi_ref, wg_ref, bg_ref, sc_ref, wo_ref, ho_ref, p_ref, *scr):
            ho_ref[...] = layer_rows(pl.program_id(0), hn_ref, hc_ref, g_ref, wi_ref, wg_ref, bg_ref, sc_ref, wo_ref,
                                     p_ref, *scr)

        (ho, p), xouts = _launch(
            body, name, nt + 1, in_specs, [pl.BlockSpec((tm, d), cur), pl.BlockSpec((tm, 2 * e), nxt)],
            [jax.ShapeDtypeStruct((t, d), F32), jax.ShapeDtypeStruct((t, 2 * e), F32)], scratch, args, ex)
        return ho, p, xouts

    target, final_g = final

    def body(hn_ref, hc_ref, g_ref, wi_ref, wg_ref, bg_ref, sc_ref, wo_ref, tg_ref, fg_ref,
             dh_ref, p_ref, dfg_ref, loss_ref, ue_scr, y_scr, pbuf, z_scr, hn_scr, acc_scr, lacc_scr):
        i = pl.program_id(0)

        @pl.when(i <= 1)
        def _():
            acc_scr[...] = jnp.zeros_like(acc_scr)
            lacc_scr[...] = jnp.zeros_like(lacc_scr)

        ho = layer_rows(i, hn_ref, hc_ref, g_ref, wi_ref, wg_ref, bg_ref, sc_ref, wo_ref, p_ref,
                        ue_scr, y_scr, pbuf, z_scr, hn_scr)
        _final_rows(ho, tg_ref, fg_ref, dh_ref, acc_scr, lacc_scr)

        @pl.when(i == nt)
        def _():
            dfg_ref[...] = jnp.zeros_like(dfg_ref)
            dfg_ref[pl.ds(0, 1), :] = jnp.sum(acc_scr[...], axis=0, keepdims=True)
            loss_ref[...] = jnp.broadcast_to(jnp.sum(lacc_scr[...]) * (0.5 / d), loss_ref.shape)

    (dh, p, dfg, loss), xouts = _launch(
        body, name, nt + 1, in_specs + [pl.BlockSpec((tm, d), cur), _const((1, d), (0, 0))],
        [pl.BlockSpec((tm, d), cur), pl.BlockSpec((tm, 2 * e), nxt), _const((SUBLANES, d), (0, 0)),
         _const((SUBLANES, LANES), (0, 0))],
        [jax.ShapeDtypeStruct((t, d), F32), jax.ShapeDtypeStruct((t, 2 * e), F32),
         jax.ShapeDtypeStruct((SUBLANES, d), F32), jax.ShapeDtypeStruct((SUBLANES, LANES), F32)],
        scratch + [pltpu.VMEM((SUBLANES, d), F32), pltpu.VMEM((SUBLANES, d), F32)],
        args + (target, final_g), ex)
    return dh, p, dfg, loss, xouts


def _pool_bwd(dho, p, w_grp, vecs, layer, w_out, name, ex=None):
    t, d = dho.shape
    e = w_out.shape[0]
    ng = len(POOL_WINDOWS)
    gc = e // ng
    tm = _tile(t, 512)
    nt = t // tm
    hb = tm // POOL_HALO

    def body(dho_ref, p_ref, ph_ref, wg_ref, bg_ref, sc_ref, wo_ref, dp_ref, dwo_ref, dwg_ref, dvec_ref,
             ue_scr, ee_scr, acc_scr, dwo_scr, dwg_scr):
        i = pl.program_id(0)
        tile = nt - 1 - i

        @pl.when(i == 0)
        def _():
            dwo_scr[...] = jnp.zeros_like(dwo_scr)
            dwg_scr[...] = jnp.zeros_like(dwg_scr)
            acc_scr[...] = jnp.zeros_like(acc_scr)
            ee_scr[pl.ds(tm, POOL_HALO), :] = jnp.zeros((POOL_HALO, e), F32)

        @pl.when(i > 0)
        def _():
            ee_scr[pl.ds(tm, POOL_HALO), :] = ee_scr[pl.ds(0, POOL_HALO), :]

        @pl.when(tile == 0)
        def _():
            ue_scr[pl.ds(0, POOL_HALO), :] = jnp.zeros((POOL_HALO, e), F32)

        @pl.when(tile > 0)
        def _():
            ue_scr[pl.ds(0, POOL_HALO), :] = ph_ref[:, pl.ds(0, e)]

        ue_scr[pl.ds(POOL_HALO, tm), :] = p_ref[:, pl.ds(0, e)]

        bg = bg_ref[...]
        sc = sc_ref[...]
        ds = [dg.astype(BF16) for dg in _pool_d(ue_scr, tile, tm, gc)]
        ob = jnp.concatenate([_dot(ds[g], wg_ref[g]) for g in range(ng)], axis=1) + bg
        z = p_ref[:, pl.ds(e, e)]
        sg_z = _sigmoid(z)
        s_z = z * sg_z
        y1 = ob * sc
        dy = dho_ref[...].astype(BF16)
        dy2 = _dot_nt(dy, wo_ref[...])
        dwo_scr[...] += _dot_tn((y1 * s_z).astype(BF16), dy)
        dy1 = dy2 * s_z
        dp_ref[:, pl.ds(e, e)] = (dy2 * y1 * _dsilu(z, sg_z)).astype(BF16)
        acc_scr[pl.ds(SUBLANES, SUBLANES), :] += _rows8(dy1 * ob)
        do = dy1 * sc
        acc_scr[pl.ds(0, SUBLANES), :] += _rows8(do)

        n = tm + POOL_HALO
        for g, w in enumerate(POOL_WINDOWS):
            cols = pl.ds(g * gc, gc)
            do_g = do[:, g * gc:(g + 1) * gc].astype(BF16)
            dwg_scr[g] += _dot_tn(ds[g], do_g)
            dd = _dot_nt(do_g, wg_ref[g])
            ee_scr[pl.ds(0, tm), cols] = dd * _inv_count(tile, tm, w)
            s = ee_scr[:, cols]
            sh = 1
            while sh < w:
                s = s + pltpu.roll(s, n - sh, axis=0)
                sh *= 2
            dp_ref[:, cols] = (s[:tm, :] - dd).astype(BF16)

        @pl.when(i == nt - 1)
        def _():
            dwo_ref[...] = dwo_scr[...].astype(BF16)
            dwg_ref[...] = dwg_scr[...].astype(BF16)
            dvec_ref[...] = jnp.zeros_like(dvec_ref)
            for k in range(2):
                dvec_ref[pl.ds(k, 1), :] = jnp.sum(acc_scr[pl.ds(k * SUBLANES, SUBLANES), :], axis=0, keepdims=True)

    rev = lambda i: (nt - 1 - i, 0)
    vec = _const((None, 1, e), (layer, 0, 0))
    (dp, dwo, dwg, dvec), xouts = _launch(
        body, name, nt,
        [pl.BlockSpec((tm, d), rev), pl.BlockSpec((tm, 2 * e), rev),
         pl.BlockSpec((POOL_HALO, 2 * e), lambda i: (jnp.maximum((nt - 1 - i) * hb - 1, 0), 0)),
         _const((ng, gc, gc), (0, 0, 0)), vec, vec, _const((e, d), (0, 0))],
        [pl.BlockSpec((tm, 2 * e), rev), _const((e, d), (0, 0)), _const((ng, gc, gc), (0, 0, 0)),
         _const((SUBLANES, e), (0, 0))],
        [jax.ShapeDtypeStruct((t, 2 * e), BF16), jax.ShapeDtypeStruct((e, d), BF16),
         jax.ShapeDtypeStruct((ng, gc, gc), BF16), jax.ShapeDtypeStruct((SUBLANES, e), F32)],
        [pltpu.VMEM((tm + POOL_HALO, e), F32), pltpu.VMEM((tm + POOL_HALO, e), F32),
         pltpu.VMEM((2 * SUBLANES, e), F32), pltpu.VMEM((e, d), F32), pltpu.VMEM((ng, gc, gc), F32)],
        (dho, p, p, w_grp, *vecs, w_out), ex)
    return dp, dwo, dwg, dvec, xouts


def _in_bwd(dp, h, dho, norm_g, layer, w_in, name, ex=None):
    t, d = h.shape
    n = w_in.shape[-1]
    tm = _tile(t, 512)
    nt = t // tm

    def body(dp_ref, h_ref, dho_ref, g_ref, w_ref, dh_ref, dw_ref, dg_ref, acc_scr, dw_scr):
        i = pl.program_id(0)

        @pl.when(i == 0)
        def _():
            dw_scr[...] = jnp.zeros_like(dw_scr)
            acc_scr[...] = jnp.zeros_like(acc_scr)

        x = h_ref[...]
        r = lax.rsqrt(jnp.mean(x * x, axis=-1, keepdims=True) + RMS_EPS)
        nrm = x * r
        dp = dp_ref[...]
        dhn = _dot_nt(dp, w_ref[...])
        dw_scr[...] += _dot_tn((nrm * g_ref[...]).astype(BF16), dp)
        acc_scr[...] += _rows8(dhn * nrm)
        dq = dhn * g_ref[...]
        dh_ref[...] = dho_ref[...] + r * (dq - nrm * jnp.mean(dq * nrm, axis=-1, keepdims=True))

        @pl.when(i == nt - 1)
        def _():
            dw_ref[...] = dw_scr[...].astype(BF16)
            dg_ref[...] = jnp.zeros_like(dg_ref)
            dg_ref[pl.ds(0, 1), :] = jnp.sum(acc_scr[...], axis=0, keepdims=True)

    (dh, dw, dg), xouts = _launch(
        body, name, nt,
        [pl.BlockSpec((tm, n), lambda i: (i, 0)), pl.BlockSpec((tm, d), lambda i: (i, 0)),
         pl.BlockSpec((tm, d), lambda i: (i, 0)), _const((None, 1, d), (layer, 0, 0)), _const((d, n), (0, 0))],
        [pl.BlockSpec((tm, d), lambda i: (i, 0)), _const((d, n), (0, 0)), _const((SUBLANES, d), (0, 0))],
        [jax.ShapeDtypeStruct((t, d), F32), jax.ShapeDtypeStruct((d, n), BF16),
         jax.ShapeDtypeStruct((SUBLANES, d), F32)],
        [pltpu.VMEM((SUBLANES, d), F32), pltpu.VMEM((d, n), F32)],
        (dp, h, dho, norm_g, w_in), ex)
    return dh, dw, dg, xouts


def _adam_update(g, w, m, v):
    c1 = 1.0 / (1.0 - ADAM_B1 ** ADAM_STEP)
    c2 = 1.0 / (1.0 - ADAM_B2 ** ADAM_STEP)
    nm = ADAM_B1 * m + (1.0 - ADAM_B1) * g
    nv = ADAM_B2 * v + (1.0 - ADAM_B2) * (g * g)
    return -ADAM_LR * ((nm * c1) / (jnp.sqrt(nv * c2) + ADAM_EPS) + ADAM_WD * w), nm, nv


def _adamw_small(params, stacks, loss_stack, name):
    ns, npar = len(stacks), len(params)

    def body(*refs):
        st = refs[:ns]
        pr = refs[ns:ns + 3 * npar]
        ls_ref = refs[ns + 3 * npar]
        outs = refs[ns + 3 * npar + 1:ns + 7 * npar + 1]
        loss_ref = refs[ns + 7 * npar + 1]
        for q, (w, _, _, pieces) in enumerate(params):
            w_ref, m_ref, v_ref = pr[3 * q:3 * q + 3]
            g_ref, d_ref, nm_ref, nv_ref = outs[4 * q:4 * q + 4]
            for s, row, slab in pieces:
                if w.ndim == 3:
                    take = lambda k: st[s][k]
                    at = slab
                else:
                    take = lambda k: st[s][k, pl.ds(row, 1), :]
                    at = (pl.ds(slab, 1), slice(None))
                g = take(0)
                for k in range(1, N_DEV):
                    g = g + take(k)
                g_ref[at] = g
                d_ref[at], nm_ref[at], nv_ref[at] = _adam_update(g, w_ref[at], m_ref[at], v_ref[at])
        tot = ls_ref[0]
        for k in range(1, N_DEV):
            tot = tot + ls_ref[k]
        loss_ref[...] = tot

    flat = [a for (w, m, v, _) in params for a in (w, m, v)]
    out_shape = [jax.ShapeDtypeStruct(w.shape, F32) for (w, _, _, _) in params for _ in range(4)]
    whole = pl.BlockSpec(memory_space=pltpu.VMEM)
    outs = pl.pallas_call(
        body, name=name, in_specs=[whole] * (ns + 3 * npar + 1), out_specs=[whole] * (4 * npar + 1),
        out_shape=out_shape + [jax.ShapeDtypeStruct(loss_stack.shape[1:], F32)],
    )(*stacks, *flat, loss_stack)
    return [outs[4 * q:4 * q + 4] for q in range(npar)], outs[-1]


def _adamw(stacks, w, m, v, name):
    nl = len(stacks)
    shp = w.shape
    c = shp[-1]
    r = 1
    for s in shp[1:-1]:
        r *= s
    tr = r
    for cand in (512, 256, 128, 64, 32, 16):
        if r % cand == 0 and r > cand:
            tr = cand
            break
    nrb = r // tr

    def body(*refs):
        s_refs = refs[:nl]
        w_ref, m_ref, v_ref, g_ref, d_ref, nm_ref, nv_ref = refs[nl:]
        layer = pl.program_id(0)
        for l in range(nl):
            @pl.when(layer == l)
            def _(l=l):
                g = s_refs[l][0].astype(F32)
                for k in range(1, N_DEV):
                    g = g + s_refs[l][k].astype(F32)
                g_ref[...] = g
                d_ref[...], nm_ref[...], nv_ref[...] = _adam_update(g, w_ref[...], m_ref[...], v_ref[...])

    def stack_spec(l):
        return pl.BlockSpec((N_DEV, tr, c),
                            lambda j, i: (0, jnp.where(j == l, i, jnp.where(j < l, 0, nrb - 1)), 0))

    spec = pl.BlockSpec((None, tr, c), lambda j, i: (j, i, 0))
    outs = pl.pallas_call(
        body, name=name, grid=(nl, nrb),
        in_specs=[stack_spec(l) for l in range(nl)] + [spec, spec, spec],
        out_specs=[spec] * 4,
        out_shape=[jax.ShapeDtypeStruct((nl, r, c), F32)] * 4,
        compiler_params=_params(2),
    )(*[s.reshape(N_DEV, r, c) for s in stacks], w.reshape(nl, r, c), m.reshape(nl, r, c), v.reshape(nl, r, c))
    return [o.reshape(shp) for o in outs]


def kernel(x, norm_g, final_g, conv_w_in, conv_dw, conv_dw_b, conv_ln_g, conv_ln_b, conv_w_out, pool_w_in, pool_w_grp, pool_b_grp, pool_scale, pool_w_out, loss_target, m_norm_g, m_final_g, m_conv_w_in, m_conv_dw, m_conv_dw_b, m_conv_ln_g, m_conv_ln_b, m_conv_w_out, m_pool_w_in, m_pool_w_grp, m_pool_b_grp, m_pool_scale, m_pool_w_out, v_norm_g, v_final_g, v_conv_w_in, v_conv_dw, v_conv_dw_b, v_conv_ln_g, v_conv_ln_b, v_conv_w_out, v_pool_w_in, v_pool_w_grp, v_pool_b_grp, v_pool_scale, v_pool_w_out):
    h0 = x[0]
    target = loss_target[0]
    ng3 = norm_g[:, None, :]
    row3 = lambda a: a[:, None, :]
    conv_vecs = (row3(conv_dw_b), row3(conv_ln_g), row3(conv_ln_b))
    gather = lambda arrays, axes: _Exchange("gather", arrays, axes)
    scatter = lambda arrays, axes: _Exchange("scatter", arrays, axes)

    cwi, cwo, pwi = conv_w_in.astype(BF16), conv_w_out.astype(BF16), pool_w_in.astype(BF16)
    pwg, pwo = pool_w_grp.astype(BF16), pool_w_out.astype(BF16)

    (cw_in0,) = _gather_via_sibling_call(gather([(cwi, 0)], [1]), "gather_first")
    p0, (cw_out0, dw_full, bg_full, sc_full, pw_in0) = _in_fwd(
        h0, ng3, 0, cw_in0, "conv_in_fwd_0",
        gather([(cwo, 0), conv_dw, pool_b_grp, pool_scale, (pwi, 0)], [0, 2, 1, 1, 1]))
    pool_vecs = (row3(bg_full), row3(sc_full))
    h1, uc0, (pw_grp0, pw_out0, cw_in1, cw_out1, pw_in1, pw_grp1, pw_out1) = _conv_fwd(
        p0, h0, dw_full, conv_vecs, 0, cw_out0, "conv_mix_fwd_0",
        gather([(pwg, 0), (pwo, 0), (cwi, 1), (cwo, 1), (pwi, 1), (pwg, 1), (pwo, 1)], [1, 0, 1, 0, 1, 1, 0]))
    h2, p1, _ = _pool_fwd(h1, ng3, 1, pw_in0, pw_grp0, pool_vecs, 0, pw_out0, "pool_fwd_0")
    p2, _ = _in_fwd(h2, ng3, 2, cw_in1, "conv_in_fwd_1")
    h3, uc2, _ = _conv_fwd(p2, h2, dw_full, conv_vecs, 1, cw_out1, "conv_mix_fwd_1")
    dh, p3, d_final_g, loss_part, _ = _pool_fwd(h3, ng3, 3, pw_in1, pw_grp1, pool_vecs, 1, pw_out1, "pool_fwd_1",
                                               final=(target, final_g[None, :]))

    dp, g_pwo1, g_pwg1, dpv1, _ = _pool_bwd(dh, p3, pw_grp1, pool_vecs, 1, pw_out1, "pool_mix_bwd_1")
    dh, g_pwi1, dg3, _ = _in_bwd(dp, h3, dh, ng3, 3, pw_in1, "pool_in_bwd_1")
    dp, g_cwo1, ddw1, dcv1, (s_pwo1, s_pwg1, s_pwi1) = _conv_bwd(
        dh, p2, uc2, dw_full, conv_vecs, 1, cw_out1, "conv_mix_bwd_1", scatter([g_pwo1, g_pwg1, g_pwi1], [0, 1, 1]))
    dh, g_cwi1, dg2, _ = _in_bwd(dp, h2, dh, ng3, 2, cw_in1, "conv_in_bwd_1")
    dp, g_pwo0, g_pwg0, dpv0, (s_cwo1, s_cwi1) = _pool_bwd(dh, p1, pw_grp0, pool_vecs, 0, pw_out0, "pool_mix_bwd_0",
                                                           scatter([g_cwo1, g_cwi1], [0, 1]))
    dh, g_pwi0, dg1, _ = _in_bwd(dp, h1, dh, ng3, 1, pw_in0, "pool_in_bwd_0")
    dp, g_cwo0, ddw0, dcv0, (s_pwo0, s_pwg0, s_pwi0, s_ddw1, s_dcv1, s_dpv0, s_dpv1, s_dg1, s_dg2, s_dg3, s_dfg,
                             s_loss) = _conv_bwd(
        dh, p0, uc0, dw_full, conv_vecs, 0, cw_out0, "conv_mix_bwd_0",
        scatter([g_pwo0, g_pwg0, g_pwi0, ddw1, dcv1, dpv0, dpv1, dg1, dg2, dg3, d_final_g, loss_part],
                [0, 1, 1, 1, None, 1, 1, None, None, None, None, None]))
    dh, g_cwi0, dg0, _ = _in_bwd(dp, h0, dh, ng3, 0, cw_in0, "conv_in_bwd_0")
    grad_x = dh[None]
    s_cwo0, s_cwi0, s_ddw0, s_dcv0, s_dg0 = _exchange_call(
        scatter([g_cwo0, g_cwi0, ddw0, dcv0, dg0], [0, 1, 1, None, None]), "scatter_last")

    res = {}
    res["conv_w_in"] = _adamw([s_cwi0, s_cwi1], conv_w_in, m_conv_w_in, v_conv_w_in, "adamw_conv_w_in")
    res["conv_w_out"] = _adamw([s_cwo0, s_cwo1], conv_w_out, m_conv_w_out, v_conv_w_out, "adamw_conv_w_out")
    res["pool_w_in"] = _adamw([s_pwi0, s_pwi1], pool_w_in, m_pool_w_in, v_pool_w_in, "adamw_pool_w_in")
    res["pool_w_grp"] = _adamw([s_pwg0, s_pwg1], pool_w_grp, m_pool_w_grp, v_pool_w_grp, "adamw_pool_w_grp")
    res["pool_w_out"] = _adamw([s_pwo0, s_pwo1], pool_w_out, m_pool_w_out, v_pool_w_out, "adamw_pool_w_out")
    stacks = [s_ddw0, s_ddw1, s_dpv0, s_dpv1, s_dg0, s_dg1, s_dg2, s_dg3, s_dfg, s_dcv0, s_dcv1]
    small = [
        ("conv_dw", conv_dw, m_conv_dw, v_conv_dw, [(0, None, 0), (1, None, 1)]),
        ("pool_b_grp", pool_b_grp, m_pool_b_grp, v_pool_b_grp, [(2, 0, 0), (3, 0, 1)]),
        ("pool_scale", pool_scale, m_pool_scale, v_pool_scale, [(2, 1, 0), (3, 1, 1)]),
        ("norm_g", norm_g, m_norm_g, v_norm_g, [(4, 0, 0), (5, 0, 1), (6, 0, 2), (7, 0, 3)]),
        ("final_g", final_g[None, :], m_final_g[None, :], v_final_g[None, :], [(8, 0, 0)]),
        ("conv_dw_b", conv_dw_b, m_conv_dw_b, v_conv_dw_b, [(9, 0, 0), (10, 0, 1)]),
        ("conv_ln_g", conv_ln_g, m_conv_ln_g, v_conv_ln_g, [(9, 1, 0), (10, 1, 1)]),
        ("conv_ln_b", conv_ln_b, m_conv_ln_b, v_conv_ln_b, [(9, 2, 0), (10, 2, 1)]),
    ]
    small_res, loss_block = _adamw_small([s[1:] for s in small], stacks, s_loss, "adamw_small")
    for (name, *_), r in zip(small, small_res):
        res[name] = [a[0] for a in r] if name == "final_g" else r
    loss = loss_block[0, 0]

    names = ["norm_g", "final_g", "conv_w_in", "conv_dw", "conv_dw_b", "conv_ln_g", "conv_ln_b", "conv_w_out",
             "pool_w_in", "pool_w_grp", "pool_b_grp", "pool_scale", "pool_w_out"]
    return (loss, grad_x) + tuple(res[n][q] for q in range(4) for n in names)
```

```python
import jax
import jax.numpy as jnp
from jax import lax
from jax.experimental import pallas as pl
from jax.experimental.pallas import tpu as pltpu

F32 = jnp.float32
BF16 = jnp.bfloat16

RMS_EPS = 1e-6
LN_EPS = 1e-5
CONV_TAPS = 31
CONV_HALO = 32
POOL_WINDOWS = (2, 4, 8, 16)
POOL_HALO = 16
SUBLANES = 8
LANES = 128
N_DEV = 8
V7X_VMEM_LIMIT = 56 * 1024 * 1024

ADAM_LR = 0.001
ADAM_B1 = 0.9
ADAM_B2 = 0.999
ADAM_EPS = 1e-08
ADAM_WD = 0.01
ADAM_STEP = 10

MESH = pl.DeviceIdType.MESH
ANY = pl.BlockSpec(memory_space=pl.ANY)


def _dot(a, b):
    return lax.dot_general(a, b, (((1,), (0,)), ((), ())), preferred_element_type=F32)


def _dot_nt(a, b):
    return lax.dot_general(a, b, (((1,), (1,)), ((), ())), preferred_element_type=F32)


def _dot_tn(a, b):
    return lax.dot_general(a, b, (((0,), (0,)), ((), ())), preferred_element_type=F32)


def _sigmoid(x):
    return jax.nn.sigmoid(x)


def _dsilu(x, s):
    return s * (1.0 + x * (1.0 - s))


def _rows8(x):
    r, c = x.shape
    return jnp.sum(x.reshape(r // SUBLANES, SUBLANES, c), axis=0)


def _tile(t, pref):
    return pref if t >= 2 * pref else t // 2


def _const(shape, index):
    return pl.BlockSpec(shape, lambda *_: index, pipeline_mode=pl.Buffered(1))


def _params(grid_rank=1):
    return pltpu.CompilerParams(dimension_semantics=("arbitrary",) * grid_rank, vmem_limit_bytes=V7X_VMEM_LIMIT)


def _chunks(n, rc, fn):
    def step(c, carry):
        fn(pl.multiple_of(c * rc, rc))
        return carry
    lax.fori_loop(0, n, step, 0)


def _mesh_position():
    return lax.axis_index("x"), lax.axis_index("y"), lax.axis_index("c")


def _peer(j):
    x, y, c = _mesh_position()
    px = 1 - x if j & 4 else x
    py = 1 - y if j & 2 else y
    pc = 1 - c if j & 1 else c
    return (px, py, pc), 4 * px + 2 * py + pc


def _block(ref, axis, index, size):
    idx = [slice(None)] * len(ref.shape)
    idx[axis] = pl.ds(pl.multiple_of(index * size, size), size)
    return ref.at[tuple(idx)]


class _Exchange:
    def __init__(self, kind, arrays, axes):
        self.kind, self.axes = kind, list(axes)
        self.arrays = [a[0] if isinstance(a, tuple) else a for a in arrays]
        self.layers = [a[1] if isinstance(a, tuple) else None for a in arrays]
        self.n = len(self.arrays)
        self.blk, self.out_shapes = [], []
        for a, layer, ax in zip(self.arrays, self.layers, self.axes):
            s = list(a.shape if layer is None else a.shape[1:])
            if kind == "gather":
                self.blk.append(s[ax])
                s[ax] *= N_DEV
                self.out_shapes.append(jax.ShapeDtypeStruct(tuple(s), a.dtype))
            else:
                if ax is not None:
                    s[ax] //= N_DEV
                    self.blk.append(s[ax])
                else:
                    self.blk.append(None)
                self.out_shapes.append(jax.ShapeDtypeStruct((N_DEV,) + tuple(s), a.dtype))

    def sem_shapes(self):
        return [pltpu.SemaphoreType.DMA((N_DEV - 1, self.n)), pltpu.SemaphoreType.DMA((N_DEV - 1, self.n)),
                pltpu.SemaphoreType.DMA((self.n,))]

    def _src(self, ins, k, owner):
        ref = ins[k] if self.layers[k] is None else ins[k].at[self.layers[k]]
        if self.kind == "gather" or self.axes[k] is None:
            return ref
        return _block(ref, self.axes[k], owner, self.blk[k])

    def _dst(self, outs, k, sender):
        if self.kind == "gather":
            return _block(outs[k], self.axes[k], sender, self.blk[k])
        return outs[k].at[sender]

    def _copies(self, ins, outs, sems, arriving):
        send, recv, loc = sems
        x, y, c = _mesh_position()
        me = 4 * x + 2 * y + c
        if not arriving:
            local = [pltpu.make_async_copy(self._src(ins, k, me), self._dst(outs, k, me), loc.at[k])
                     for k in range(self.n)]
        else:
            local = []
        remote = []
        for j in range(1, N_DEV):
            peer, peer_id = _peer(j)
            for k in range(self.n):
                owner, sender = (me, peer_id) if arriving else (peer_id, me)
                remote.append(pltpu.make_async_remote_copy(
                    src_ref=self._src(ins, k, owner), dst_ref=self._dst(outs, k, sender),
                    send_sem=send.at[j - 1, k], recv_sem=recv.at[j - 1, k], device_id=peer, device_id_type=MESH))
        return local, remote

    def start(self, ins, outs, sems):
        local, sends = self._copies(ins, outs, sems, arriving=False)
        for cp in local + sends:
            cp.start()

    def finish(self, ins, outs, sems):
        for cp in self._copies(ins, outs, sems, arriving=True)[1]:
            cp.wait_recv()
        local, sends = self._copies(ins, outs, sems, arriving=False)
        for cp in sends:
            cp.wait_send()
        for cp in local:
            cp.wait()


def _exchange_call(ex, name):
    def body(*refs):
        ins, outs, sems = refs[:ex.n], refs[ex.n:2 * ex.n], refs[2 * ex.n:]
        ex.start(ins, outs, sems)
        ex.finish(ins, outs, sems)

    return pl.pallas_call(body, name=name, in_specs=[ANY] * ex.n, out_specs=[ANY] * ex.n,
                          out_shape=ex.out_shapes, scratch_shapes=ex.sem_shapes())(*ex.arrays)


def _gather_via_sibling_call(ex, name):
    n = ex.n

    def body(*refs):
        ins, outs = refs[:n], refs[n:2 * n]
        send, recv, loc = refs[2 * n:]
        x, y, c = _mesh_position()
        ident = lambda px, py, pc: 4 * px + 2 * py + pc
        me, sibling = ident(x, y, c), (x, y, 1 - c)
        chips = [(1 - x, y), (x, 1 - y), (1 - x, 1 - y)]

        def copy(row, k, block, to, src=None):
            place = ex._dst(outs, k, block)
            return pltpu.make_async_remote_copy(
                src_ref=place if src is None else src, dst_ref=place,
                send_sem=send.at[row, k], recv_sem=recv.at[row, k], device_id=to, device_id_type=MESH)

        local = [pltpu.make_async_copy(ex._src(ins, k, me), ex._dst(outs, k, me), loc.at[k]) for k in range(n)]
        first = []
        for k in range(n):
            mine = ex._src(ins, k, me)
            first.append(copy(0, k, me, sibling, src=mine))
            first += [copy(1 + j, k, me, (*chip, c), src=mine) for j, chip in enumerate(chips)]
        for cp in local + first:
            cp.start()
        passed = []
        for j, chip in enumerate(chips):
            for k in range(n):
                copy(1 + j, k, ident(*chip, c), sibling).wait_recv()
                passed.append(copy(4 + j, k, ident(*chip, c), sibling))
                passed[-1].start()
        for k in range(n):
            copy(0, k, ident(x, y, 1 - c), sibling).wait_recv()
            for j, chip in enumerate(chips):
                copy(4 + j, k, ident(*chip, 1 - c), sibling).wait_recv()
        for cp in first + passed:
            cp.wait_send()
        for cp in local:
            cp.wait()

    return pl.pallas_call(body, name=name, in_specs=[ANY] * n, out_specs=[ANY] * n, out_shape=ex.out_shapes,
                          scratch_shapes=ex.sem_shapes())(*ex.arrays)


def _launch(body, name, nt, in_specs, out_specs, out_shape, scratch_shapes, args, ex=None):
    if ex is None:
        outs = pl.pallas_call(body, name=name, grid=(nt,), in_specs=in_specs, out_specs=out_specs,
                              out_shape=out_shape, scratch_shapes=scratch_shapes, compiler_params=_params())(*args)
        return list(outs), []
    n_in, n_out, n_scr = len(in_specs), len(out_specs), len(scratch_shapes)

    def riding(*refs):
        a, xa = refs[:n_in], refs[n_in:n_in + ex.n]
        o = refs[n_in + ex.n:n_in + ex.n + n_out]
        xo = refs[n_in + ex.n + n_out:n_in + 2 * ex.n + n_out]
        s = refs[n_in + 2 * ex.n + n_out:n_in + 2 * ex.n + n_out + n_scr]
        sems = refs[n_in + 2 * ex.n + n_out + n_scr:]
        i = pl.program_id(0)

        @pl.when(i == 0)
        def _():
            ex.start(xa, xo, sems)

        body(*a, *o, *s)

        @pl.when(i == nt - 1)
        def _():
            ex.finish(xa, xo, sems)

    outs = pl.pallas_call(
        riding, name=name, grid=(nt,),
        in_specs=list(in_specs) + [ANY] * ex.n, out_specs=list(out_specs) + [ANY] * ex.n,
        out_shape=list(out_shape) + ex.out_shapes, scratch_shapes=list(scratch_shapes) + ex.sem_shapes(),
        compiler_params=_params())(*args, *ex.arrays)
    return list(outs[:n_out]), list(outs[n_out:])


def _in_fwd(h, norm_g, layer, w_in, name, ex=None):
    t, d = h.shape
    n = w_in.shape[-1]
    tm = _tile(t, 512)

    def body(h_ref, g_ref, w_ref, p_ref):
        x = h_ref[...]
        r = lax.rsqrt(jnp.mean(x * x, axis=-1, keepdims=True) + RMS_EPS)
        p_ref[...] = _dot((x * r * g_ref[...]).astype(BF16), w_ref[...])

    (p,), xouts = _launch(
        body, name, t // tm,
        [pl.BlockSpec((tm, d), lambda i: (i, 0)), _const((None, 1, d), (layer, 0, 0)), _const((d, n), (0, 0))],
        [pl.BlockSpec((tm, n), lambda i: (i, 0))],
        [jax.ShapeDtypeStruct((t, n), F32)], [], (h, norm_g, w_in), ex)
    return p, xouts


def _layernorm_rows(uc, lng, lnb):
    mu = jnp.mean(uc, axis=-1, keepdims=True)
    xc = uc - mu
    rstd = lax.rsqrt(jnp.mean(xc * xc, axis=-1, keepdims=True) + LN_EPS)
    xhat = xc * rstd
    return xhat, rstd, xhat * lng + lnb


def _conv_fwd(p, h, dw, vecs, layer, w_out, name, ex=None):
    t, d = h.shape
    e = w_out.shape[0]
    tm = _tile(t, 256)
    rc = _tile(tm, 128)

    def body(p_ref, h_ref, dw_ref, dwb_ref, lng_ref, lnb_ref, wo_ref, ho_ref, xr_ref, sg_ref, ub_ref, us_scr, uc_ref):
        i = pl.program_id(0)

        @pl.when(i == 0)
        def _():
            us_scr[:, pl.ds(0, CONV_HALO), :] = jnp.zeros((SUBLANES, CONV_HALO, e), F32)

        @pl.when(i > 0)
        def _():
            us_scr[:, pl.ds(0, CONV_HALO), :] = us_scr[:, pl.ds(tm, CONV_HALO), :]

        sb = _sigmoid(p_ref[:, pl.ds(e, e)])
        u = p_ref[:, pl.ds(0, e)] * sb
        ub_ref[:, pl.ds(0, e)] = u
        ub_ref[:, pl.ds(e, e)] = sb
        us_scr[0, pl.ds(CONV_HALO, tm), :] = u
        for r in range(1, SUBLANES):
            us_scr[r, pl.ds(CONV_HALO, tm), :] = us_scr[0, pl.ds(CONV_HALO - r, tm), :]

        def c_conv(base):
            for lt in range(e // LANES):
                cols = pl.ds(lt * LANES, LANES)
                acc = jnp.broadcast_to(dwb_ref[:, cols], (rc, LANES))
                for r in range(SUBLANES):
                    nq = (CONV_TAPS - 1 - r) // SUBLANES + 1
                    lo = SUBLANES * (nq - 1)
                    win = us_scr[r, pl.ds(pl.multiple_of(CONV_HALO + base - lo, SUBLANES), rc + lo), cols]
                    for q in range(nq):
                        k = CONV_TAPS - 1 - (SUBLANES * q + r)
                        at = lo - SUBLANES * q
                        acc = acc + dw_ref[pl.ds(k, 1), cols] * win[at:at + rc, :]
                uc_ref[pl.ds(base, rc), cols] = acc
        _chunks(tm // rc, rc, c_conv)

        xhat, rstd, ul = _layernorm_rows(uc_ref[...], lng_ref[...], lnb_ref[...])
        z = p_ref[:, pl.ds(2 * e, e)]
        sg_u = _sigmoid(ul)
        sg_z = _sigmoid(z)
        xr_ref[:, pl.ds(0, e)] = xhat
        xr_ref[:, pl.ds(e, LANES)] = jnp.broadcast_to(rstd, (tm, LANES))
        sg_ref[:, pl.ds(0, e)] = sg_u
        sg_ref[:, pl.ds(e, e)] = sg_z
        v = ((ul * sg_u) * (z * sg_z)).astype(BF16)
        ho_ref[...] = h_ref[...] + _dot(v, wo_ref[...])

    vec = _const((None, 1, e), (layer, 0, 0))
    row = lambda i: (i, 0)
    outs, xouts = _launch(
        body, name, t // tm,
        [pl.BlockSpec((tm, 3 * e), row), pl.BlockSpec((tm, d), row),
         _const((None, CONV_TAPS, e), (layer, 0, 0)), vec, vec, vec, _const((e, d), (0, 0))],
        [pl.BlockSpec((tm, d), row), pl.BlockSpec((tm, e + LANES), row), pl.BlockSpec((tm, 2 * e), row),
         pl.BlockSpec((tm, 2 * e), row)],
        [jax.ShapeDtypeStruct((t, d), F32), jax.ShapeDtypeStruct((t, e + LANES), F32),
         jax.ShapeDtypeStruct((t, 2 * e), F32), jax.ShapeDtypeStruct((t, 2 * e), F32)],
        [pltpu.VMEM((SUBLANES, tm + CONV_HALO, e), F32), pltpu.VMEM((tm, e), F32)], (p, h, dw, *vecs, w_out), ex)
    return outs[0], tuple(outs[1:]), xouts


def _conv_bwd(dho, p, saved, dw, vecs, layer, w_out, name, ex=None):
    t, d = dho.shape
    e = w_out.shape[0]
    tm = _tile(t, 256)
    nt = t // tm
    rc = 16
    vec0 = CONV_TAPS + 1

    def body(dho_ref, z_ref, xr_ref, sg_ref, ub_ref, dw_ref, lng_ref, lnb_ref, wo_ref,
             dp_ref, dwo_ref, ddw_ref, dvec_ref, ds_scr, acc_scr, dwo_scr):
        i = pl.program_id(0)

        @pl.when(i == 0)
        def _():
            dwo_scr[...] = jnp.zeros_like(dwo_scr)
            acc_scr[...] = jnp.zeros_like(acc_scr)
            ds_scr[:, pl.ds(tm, CONV_HALO), :] = jnp.zeros((SUBLANES, CONV_HALO, e), F32)

        @pl.when(i > 0)
        def _():
            ds_scr[:, pl.ds(tm, CONV_HALO), :] = ds_scr[:, pl.ds(0, CONV_HALO), :]

        lng = lng_ref[...]
        lnb = lnb_ref[...]

        xhat = xr_ref[:, pl.ds(0, e)]
        rstd = xr_ref[:, pl.ds(e, 1)]
        ul = xhat * lng + lnb
        z = z_ref[...]
        sg_u = sg_ref[:, pl.ds(0, e)]
        sg_z = sg_ref[:, pl.ds(e, e)]
        s_u = ul * sg_u
        s_z = z * sg_z
        v = (s_u * s_z).astype(BF16)
        dy = dho_ref[...].astype(BF16)
        dv = _dot_nt(dy, wo_ref[...])
        dwo_scr[...] += _dot_tn(v, dy)

        dul = dv * s_z * _dsilu(ul, sg_u)
        dp_ref[:, pl.ds(2 * e, e)] = (dv * s_u * _dsilu(z, sg_z)).astype(BF16)
        acc_scr[pl.ds((vec0 + 1) * SUBLANES, SUBLANES), :] += _rows8(dul * xhat)
        acc_scr[pl.ds((vec0 + 2) * SUBLANES, SUBLANES), :] += _rows8(dul)
        dxh = dul * lng
        duc = rstd * (dxh - jnp.mean(dxh, axis=-1, keepdims=True)
                      - xhat * jnp.mean(dxh * xhat, axis=-1, keepdims=True))
        acc_scr[pl.ds(vec0 * SUBLANES, SUBLANES), :] += _rows8(duc)
        ds_scr[0, pl.ds(0, tm), :] = duc
        for r in range(1, SUBLANES):
            ds_scr[r, pl.ds(0, tm), :] = ds_scr[0, pl.ds(r, tm), :]

        def c_conv(base):
            rows = pl.ds(base, rc)
            u = ub_ref[rows, pl.ds(0, e)]
            sb = ub_ref[rows, pl.ds(e, e)]
            du = jnp.zeros((rc, e), F32)
            for o in range(CONV_TAPS):
                q, r = divmod(o, SUBLANES)
                k = CONV_TAPS - 1 - o
                sh = ds_scr[r, pl.ds(pl.multiple_of(base + SUBLANES * q, SUBLANES), rc), :]
                du = du + dw_ref[pl.ds(k, 1), :] * sh
                acc_scr[pl.ds(k * SUBLANES, SUBLANES), :] += _rows8(u * sh)
            dp_ref[rows, pl.ds(0, e)] = (du * sb).astype(BF16)
            dp_ref[rows, pl.ds(e, e)] = (du * u * (1.0 - sb)).astype(BF16)
        _chunks(tm // rc, rc, c_conv)

        @pl.when(i == nt - 1)
        def _():
            dwo_ref[...] = dwo_scr[...].astype(BF16)
            slot_sum = lambda k: jnp.sum(acc_scr[pl.ds(k * SUBLANES, SUBLANES), :], axis=0, keepdims=True)
            for k in range(CONV_TAPS):
                ddw_ref[pl.ds(k, 1), :] = slot_sum(k)
            dvec_ref[...] = jnp.zeros_like(dvec_ref)
            for k in range(3):
                dvec_ref[pl.ds(k, 1), :] = slot_sum(vec0 + k)

    rev = lambda i: (nt - 1 - i, 0)
    vec = _const((None, 1, e), (layer, 0, 0))
    (dp, dwo, ddw, dvec), xouts = _launch(
        body, name, nt,
        [pl.BlockSpec((tm, d), rev), pl.BlockSpec((tm, e), lambda i: (nt - 1 - i, 2)),
         pl.BlockSpec((tm, e + LANES), rev), pl.BlockSpec((tm, 2 * e), rev), pl.BlockSpec((tm, 2 * e), rev),
         _const((None, CONV_TAPS, e), (layer, 0, 0)), vec, vec, _const((e, d), (0, 0))],
        [pl.BlockSpec((tm, 3 * e), rev), _const((e, d), (0, 0)), _const((CONV_TAPS, e), (0, 0)),
         _const((SUBLANES, e), (0, 0))],
        [jax.ShapeDtypeStruct((t, 3 * e), BF16), jax.ShapeDtypeStruct((e, d), BF16),
         jax.ShapeDtypeStruct((CONV_TAPS, e), F32), jax.ShapeDtypeStruct((SUBLANES, e), F32)],
        [pltpu.VMEM((SUBLANES, tm + CONV_HALO, e), F32), pltpu.VMEM(((vec0 + 3) * SUBLANES, e), F32),
         pltpu.VMEM((e, d), F32)],
        (dho, p, *saved, dw, vecs[1], vecs[2], w_out), ex)
    return dp, dwo, ddw, dvec, xouts


def _inv_count(tile, tm, w):
    tpos = tile * tm + lax.broadcasted_iota(jnp.int32, (tm, 1), 0)
    return 1.0 / jnp.minimum(tpos + 1, w).astype(F32)


def _pool_d_group(ue_scr, tile, tm, gc, g):
    w = POOL_WINDOWS[g]
    win = ue_scr[:, pl.ds(g * gc, gc)]
    s = win
    sh = 1
    while sh < w:
        s = s + pltpu.roll(s, sh, axis=0)
        sh *= 2
    return s[POOL_HALO:, :] * _inv_count(tile, tm, w) - win[POOL_HALO:, :]


def _pool_d(ue_scr, tile, tm, gc):
    return [_pool_d_group(ue_scr, tile, tm, gc, g) for g in range(len(POOL_WINDOWS))]


def _final_rows(ho, tg_ref, fg_ref, dh_ref, acc_scr, lacc_scr):
    d = ho.shape[-1]
    r = lax.rsqrt(jnp.mean(ho * ho, axis=-1, keepdims=True) + RMS_EPS)
    nrm = ho * r
    err = nrm * fg_ref[...] - tg_ref[...]
    lacc_scr[...] += _rows8(err * err)
    dy = err * (1.0 / d)
    acc_scr[...] += _rows8(dy * nrm)
    dq = dy * fg_ref[...]
    dh_ref[...] = r * (dq - nrm * jnp.mean(dq * nrm, axis=-1, keepdims=True))


def _pool_fwd(h, norm_g, nlayer, w_in, w_grp, vecs, layer, w_out, name, ex=None, final=None):
    t, d = h.shape
    e = w_out.shape[0]
    ng = len(POOL_WINDOWS)
    gc = e // ng
    tm = _tile(t, 512)
    nt = t // tm
    cw = 2 * e // (2 * ng)

    def layer_rows(i, hn_ref, hc_ref, g_ref, wi_ref, wg_ref, bg_ref, sc_ref, wo_ref, p_ref,
                   ue_scr, y_scr, pbuf, z_scr, hn_scr):
        tile = jnp.maximum(i - 1, 0)

        @pl.when(i == 0)
        def _():
            pbuf[...] = jnp.zeros_like(pbuf)

        @pl.when(i <= 1)
        def _():
            ue_scr[pl.ds(0, POOL_HALO), :] = jnp.zeros((POOL_HALO, e), F32)

        @pl.when(i > 1)
        def _():
            ue_scr[pl.ds(0, POOL_HALO), :] = ue_scr[pl.ds(tm, POOL_HALO), :]

        ue_scr[pl.ds(POOL_HALO, tm), :] = pbuf[:, pl.ds(0, e)]
        z_scr[...] = pbuf[:, pl.ds(e, e)]

        x = hn_ref[...]
        r = lax.rsqrt(jnp.mean(x * x, axis=-1, keepdims=True) + RMS_EPS)
        hn_scr[...] = (x * r * g_ref[...]).astype(BF16)

        def project(c):
            part = _dot(hn_scr[...], wi_ref[:, pl.ds(c * cw, cw)])
            pbuf[:, pl.ds(c * cw, cw)] = part
            p_ref[:, pl.ds(c * cw, cw)] = part

        for g in range(ng):
            project(2 * g)
            cols = pl.ds(g * gc, gc)
            dg = _pool_d_group(ue_scr, tile, tm, gc, g)
            z = z_scr[:, cols]
            y1 = (_dot(dg.astype(BF16), wg_ref[g]) + bg_ref[:, cols]) * sc_ref[:, cols]
            y_scr[:, cols] = (y1 * (z * _sigmoid(z))).astype(BF16)
            project(2 * g + 1)

        return hc_ref[...] + _dot(y_scr[...], wo_ref[...])

    nxt = lambda i: (jnp.minimum(i, nt - 1), 0)
    cur = lambda i: (jnp.maximum(i - 1, 0), 0)
    vec = _const((None, 1, e), (layer, 0, 0))
    in_specs = [pl.BlockSpec((tm, d), nxt), pl.BlockSpec((tm, d), cur), _const((None, 1, d), (nlayer, 0, 0)),
                _const((d, 2 * e), (0, 0)), _const((ng, gc, gc), (0, 0, 0)), vec, vec, _const((e, d), (0, 0))]
    scratch = [pltpu.VMEM((tm + POOL_HALO, e), F32), pltpu.VMEM((tm, e), BF16), pltpu.VMEM((tm, 2 * e), F32),
               pltpu.VMEM((tm, e), F32), pltpu.VMEM((tm, d), BF16)]
    args = (h, h, norm_g, w_in, w_grp, *vecs, w_out)

    if final is None:
        def body(hn_ref, hc_ref, g_ref, wi_ref, wg_ref, bg_ref, sc_ref, wo_ref, ho_ref, p_ref, *scr):
            ho_ref[...] = layer_rows(pl.program_id(0), hn_ref, hc_ref, g_ref, wi_ref, wg_ref, bg_ref, sc_ref, wo_ref,
                                     p_ref, *scr)

        (ho, p), xouts = _launch(
            body, name, nt + 1, in_specs, [pl.BlockSpec((tm, d), cur), pl.BlockSpec((tm, 2 * e), nxt)],
            [jax.ShapeDtypeStruct((t, d), F32), jax.ShapeDtypeStruct((t, 2 * e), F32)], scratch, args, ex)
        return ho, p, xouts

    target, final_g = final

    def body(hn_ref, hc_ref, g_ref, wi_ref, wg_ref, bg_ref, sc_ref, wo_ref, tg_ref, fg_ref,
             dh_ref, p_ref, dfg_ref, loss_ref, ue_scr, y_scr, pbuf, z_scr, hn_scr, acc_scr, lacc_scr):
        i = pl.program_id(0)

        @pl.when(i <= 1)
        def _():
            acc_scr[...] = jnp.zeros_like(acc_scr)
            lacc_scr[...] = jnp.zeros_like(lacc_scr)

        ho = layer_rows(i, hn_ref, hc_ref, g_ref, wi_ref, wg_ref, bg_ref, sc_ref, wo_ref, p_ref,
                        ue_scr, y_scr, pbuf, z_scr, hn_scr)
        _final_rows(ho, tg_ref, fg_ref, dh_ref, acc_scr, lacc_scr)

        @pl.when(i == nt)
        def _():
            dfg_ref[...] = jnp.zeros_like(dfg_ref)
            dfg_ref[pl.ds(0, 1), :] = jnp.sum(acc_scr[...], axis=0, keepdims=True)
            loss_ref[...] = jnp.broadcast_to(jnp.sum(lacc_scr[...]) * (0.5 / d), loss_ref.shape)

    (dh, p, dfg, loss), xouts = _launch(
        body, name, nt + 1, in_specs + [pl.BlockSpec((tm, d), cur), _const((1, d), (0, 0))],
        [pl.BlockSpec((tm, d), cur), pl.BlockSpec((tm, 2 * e), nxt), _const((SUBLANES, d), (0, 0)),
         _const((SUBLANES, LANES), (0, 0))],
        [jax.ShapeDtypeStruct((t, d), F32), jax.ShapeDtypeStruct((t, 2 * e), F32),
         jax.ShapeDtypeStruct((SUBLANES, d), F32), jax.ShapeDtypeStruct((SUBLANES, LANES), F32)],
        scratch + [pltpu.VMEM((SUBLANES, d), F32), pltpu.VMEM((SUBLANES, d), F32)],
        args + (target, final_g), ex)
    return dh, p, dfg, loss, xouts


def _pool_bwd(dho, p, w_grp, vecs, layer, w_out, name, ex=None):
    t, d = dho.shape
    e = w_out.shape[0]
    ng = len(POOL_WINDOWS)
    gc = e // ng
    tm = _tile(t, 512)
    nt = t // tm
    hb = tm // POOL_HALO

    def body(dho_ref, p_ref, ph_ref, wg_ref, bg_ref, sc_ref, wo_ref, dp_ref, dwo_ref, dwg_ref, dvec_ref,
             ue_scr, ee_scr, acc_scr, dwo_scr, dwg_scr):
        i = pl.program_id(0)
        tile = nt - 1 - i

        @pl.when(i == 0)
        def _():
            dwo_scr[...] = jnp.zeros_like(dwo_scr)
            dwg_scr[...] = jnp.zeros_like(dwg_scr)
            acc_scr[...] = jnp.zeros_like(acc_scr)
            ee_scr[pl.ds(tm, POOL_HALO), :] = jnp.zeros((POOL_HALO, e), F32)

        @pl.when(i > 0)
        def _():
            ee_scr[pl.ds(tm, POOL_HALO), :] = ee_scr[pl.ds(0, POOL_HALO), :]

        @pl.when(tile == 0)
        def _():
            ue_scr[pl.ds(0, POOL_HALO), :] = jnp.zeros((POOL_HALO, e), F32)

        @pl.when(tile > 0)
        def _():
            ue_scr[pl.ds(0, POOL_HALO), :] = ph_ref[:, pl.ds(0, e)]

        ue_scr[pl.ds(POOL_HALO, tm), :] = p_ref[:, pl.ds(0, e)]

        bg = bg_ref[...]
        sc = sc_ref[...]
        ds = [dg.astype(BF16) for dg in _pool_d(ue_scr, tile, tm, gc)]
        ob = jnp.concatenate([_dot(ds[g], wg_ref[g]) for g in range(ng)], axis=1) + bg
        z = p_ref[:, pl.ds(e, e)]
        sg_z = _sigmoid(z)
        s_z = z * sg_z
        y1 = ob * sc
        dy = dho_ref[...].astype(BF16)
        dy2 = _dot_nt(dy, wo_ref[...])
        dwo_scr[...] += _dot_tn((y1 * s_z).astype(BF16), dy)
        dy1 = dy2 * s_z
        dp_ref[:, pl.ds(e, e)] = (dy2 * y1 * _dsilu(z, sg_z)).astype(BF16)
        acc_scr[pl.ds(SUBLANES, SUBLANES), :] += _rows8(dy1 * ob)
        do = dy1 * sc
        acc_scr[pl.ds(0, SUBLANES), :] += _rows8(do)

        n = tm + POOL_HALO
        for g, w in enumerate(POOL_WINDOWS):
            cols = pl.ds(g * gc, gc)
            do_g = do[:, g * gc:(g + 1) * gc].astype(BF16)
            dwg_scr[g] += _dot_tn(ds[g], do_g)
            dd = _dot_nt(do_g, wg_ref[g])
            ee_scr[pl.ds(0, tm), cols] = dd * _inv_count(tile, tm, w)
            s = ee_scr[:, cols]
            sh = 1
            while sh < w:
                s = s + pltpu.roll(s, n - sh, axis=0)
                sh *= 2
            dp_ref[:, cols] = (s[:tm, :] - dd).astype(BF16)

        @pl.when(i == nt - 1)
        def _():
            dwo_ref[...] = dwo_scr[...].astype(BF16)
            dwg_ref[...] = dwg_scr[...].astype(BF16)
            dvec_ref[...] = jnp.zeros_like(dvec_ref)
            for k in range(2):
                dvec_ref[pl.ds(k, 1), :] = jnp.sum(acc_scr[pl.ds(k * SUBLANES, SUBLANES), :], axis=0, keepdims=True)

    rev = lambda i: (nt - 1 - i, 0)
    vec = _const((None, 1, e), (layer, 0, 0))
    (dp, dwo, dwg, dvec), xouts = _launch(
        body, name, nt,
        [pl.BlockSpec((tm, d), rev), pl.BlockSpec((tm, 2 * e), rev),
         pl.BlockSpec((POOL_HALO, 2 * e), lambda i: (jnp.maximum((nt - 1 - i) * hb - 1, 0), 0)),
         _const((ng, gc, gc), (0, 0, 0)), vec, vec, _const((e, d), (0, 0))],
        [pl.BlockSpec((tm, 2 * e), rev), _const((e, d), (0, 0)), _const((ng, gc, gc), (0, 0, 0)),
         _const((SUBLANES, e), (0, 0))],
        [jax.ShapeDtypeStruct((t, 2 * e), BF16), jax.ShapeDtypeStruct((e, d), BF16),
         jax.ShapeDtypeStruct((ng, gc, gc), BF16), jax.ShapeDtypeStruct((SUBLANES, e), F32)],
        [pltpu.VMEM((tm + POOL_HALO, e), F32), pltpu.VMEM((tm + POOL_HALO, e), F32),
         pltpu.VMEM((2 * SUBLANES, e), F32), pltpu.VMEM((e, d), F32), pltpu.VMEM((ng, gc, gc), F32)],
        (dho, p, p, w_grp, *vecs, w_out), ex)
    return dp, dwo, dwg, dvec, xouts


def _in_bwd(dp, h, dho, norm_g, layer, w_in, name, ex=None):
    t, d = h.shape
    n = w_in.shape[-1]
    tm = _tile(t, 512)
    nt = t // tm

    def body(dp_ref, h_ref, dho_ref, g_ref, w_ref, dh_ref, dw_ref, dg_ref, acc_scr, dw_scr):
        i = pl.program_id(0)

        @pl.when(i == 0)
        def _():
            dw_scr[...] = jnp.zeros_like(dw_scr)
            acc_scr[...] = jnp.zeros_like(acc_scr)

        x = h_ref[...]
        r = lax.rsqrt(jnp.mean(x * x, axis=-1, keepdims=True) + RMS_EPS)
        nrm = x * r
        dp = dp_ref[...]
        dhn = _dot_nt(dp, w_ref[...])
        dw_scr[...] += _dot_tn((nrm * g_ref[...]).astype(BF16), dp)
        acc_scr[...] += _rows8(dhn * nrm)
        dq = dhn * g_ref[...]
        dh_ref[...] = dho_ref[...] + r * (dq - nrm * jnp.mean(dq * nrm, axis=-1, keepdims=True))

        @pl.when(i == nt - 1)
        def _():
            dw_ref[...] = dw_scr[...].astype(BF16)
            dg_ref[...] = jnp.zeros_like(dg_ref)
            dg_ref[pl.ds(0, 1), :] = jnp.sum(acc_scr[...], axis=0, keepdims=True)

    (dh, dw, dg), xouts = _launch(
        body, name, nt,
        [pl.BlockSpec((tm, n), lambda i: (i, 0)), pl.BlockSpec((tm, d), lambda i: (i, 0)),
         pl.BlockSpec((tm, d), lambda i: (i, 0)), _const((None, 1, d), (layer, 0, 0)), _const((d, n), (0, 0))],
        [pl.BlockSpec((tm, d), lambda i: (i, 0)), _const((d, n), (0, 0)), _const((SUBLANES, d), (0, 0))],
        [jax.ShapeDtypeStruct((t, d), F32), jax.ShapeDtypeStruct((d, n), BF16),
         jax.ShapeDtypeStruct((SUBLANES, d), F32)],
        [pltpu.VMEM((SUBLANES, d), F32), pltpu.VMEM((d, n), F32)],
        (dp, h, dho, norm_g, w_in), ex)
    return dh, dw, dg, xouts


def _adam_update(g, w, m, v):
    c1 = 1.0 / (1.0 - ADAM_B1 ** ADAM_STEP)
    c2 = 1.0 / (1.0 - ADAM_B2 ** ADAM_STEP)
    nm = ADAM_B1 * m + (1.0 - ADAM_B1) * g
    nv = ADAM_B2 * v + (1.0 - ADAM_B2) * (g * g)
    return -ADAM_LR * ((nm * c1) / (jnp.sqrt(nv * c2) + ADAM_EPS) + ADAM_WD * w), nm, nv


def _adamw_small(params, stacks, loss_stack, name):
    ns, npar = len(stacks), len(params)

    def body(*refs):
        st = refs[:ns]
        pr = refs[ns:ns + 3 * npar]
        ls_ref = refs[ns + 3 * npar]
        outs = refs[ns + 3 * npar + 1:ns + 7 * npar + 1]
        loss_ref = refs[ns + 7 * npar + 1]
        for q, (w, _, _, pieces) in enumerate(params):
            w_ref, m_ref, v_ref = pr[3 * q:3 * q + 3]
            g_ref, d_ref, nm_ref, nv_ref = outs[4 * q:4 * q + 4]
            for s, row, slab in pieces:
                if w.ndim == 3:
                    take = lambda k: st[s][k]
                    at = slab
                else:
                    take = lambda k: st[s][k, pl.ds(row, 1), :]
                    at = (pl.ds(slab, 1), slice(None))
                g = take(0)
                for k in range(1, N_DEV):
                    g = g + take(k)
                g_ref[at] = g
                d_ref[at], nm_ref[at], nv_ref[at] = _adam_update(g, w_ref[at], m_ref[at], v_ref[at])
        tot = ls_ref[0]
        for k in range(1, N_DEV):
            tot = tot + ls_ref[k]
        loss_ref[...] = tot

    flat = [a for (w, m, v, _) in params for a in (w, m, v)]
    out_shape = [jax.ShapeDtypeStruct(w.shape, F32) for (w, _, _, _) in params for _ in range(4)]
    whole = pl.BlockSpec(memory_space=pltpu.VMEM)
    outs = pl.pallas_call(
        body, name=name, in_specs=[whole] * (ns + 3 * npar + 1), out_specs=[whole] * (4 * npar + 1),
        out_shape=out_shape + [jax.ShapeDtypeStruct(loss_stack.shape[1:], F32)],
    )(*stacks, *flat, loss_stack)
    return [outs[4 * q:4 * q + 4] for q in range(npar)], outs[-1]


def _adamw(stacks, w, m, v, name):
    nl = len(stacks)
    shp = w.shape
    c = shp[-1]
    r = 1
    for s in shp[1:-1]:
        r *= s
    tr = r
    for cand in (512, 256, 128, 64, 32, 16):
        if r % cand == 0 and r > cand:
            tr = cand
            break
    nrb = r // tr

    def body(*refs):
        s_refs = refs[:nl]
        w_ref, m_ref, v_ref, g_ref, d_ref, nm_ref, nv_ref = refs[nl:]
        layer = pl.program_id(0)
        for l in range(nl):
            @pl.when(layer == l)
            def _(l=l):
                g = s_refs[l][0].astype(F32)
                for k in range(1, N_DEV):
                    g = g + s_refs[l][k].astype(F32)
                g_ref[...] = g
                d_ref[...], nm_ref[...], nv_ref[...] = _adam_update(g, w_ref[...], m_ref[...], v_ref[...])

    def stack_spec(l):
        return pl.BlockSpec((N_DEV, tr, c),
                            lambda j, i: (0, jnp.where(j == l, i, jnp.where(j < l, 0, nrb - 1)), 0))

    spec = pl.BlockSpec((None, tr, c), lambda j, i: (j, i, 0))
    outs = pl.pallas_call(
        body, name=name, grid=(nl, nrb),
        in_specs=[stack_spec(l) for l in range(nl)] + [spec, spec, spec],
        out_specs=[spec] * 4,
        out_shape=[jax.ShapeDtypeStruct((nl, r, c), F32)] * 4,
        compiler_params=_params(2),
    )(*[s.reshape(N_DEV, r, c) for s in stacks], w.reshape(nl, r, c), m.reshape(nl, r, c), v.reshape(nl, r, c))
    return [o.reshape(shp) for o in outs]


def kernel(x, norm_g, final_g, conv_w_in, conv_dw, conv_dw_b, conv_ln_g, conv_ln_b, conv_w_out, pool_w_in, pool_w_grp, pool_b_grp, pool_scale, pool_w_out, loss_target, m_norm_g, m_final_g, m_conv_w_in, m_conv_dw, m_conv_dw_b, m_conv_ln_g, m_conv_ln_b, m_conv_w_out, m_pool_w_in, m_pool_w_grp, m_pool_b_grp, m_pool_scale, m_pool_w_out, v_norm_g, v_final_g, v_conv_w_in, v_conv_dw, v_conv_dw_b, v_conv_ln_g, v_conv_ln_b, v_conv_w_out, v_pool_w_in, v_pool_w_grp, v_pool_b_grp, v_pool_scale, v_pool_w_out):
    h0 = x[0]
    target = loss_target[0]
    ng3 = norm_g[:, None, :]
    row3 = lambda a: a[:, None, :]
    conv_vecs = (row3(conv_dw_b), row3(conv_ln_g), row3(conv_ln_b))
    gather = lambda arrays, axes: _Exchange("gather", arrays, axes)
    scatter = lambda arrays, axes: _Exchange("scatter", arrays, axes)

    cwi, cwo, pwi = conv_w_in.astype(BF16), conv_w_out.astype(BF16), pool_w_in.astype(BF16)
    pwg, pwo = pool_w_grp.astype(BF16), pool_w_out.astype(BF16)

    (cw_in0,) = _gather_via_sibling_call(gather([(cwi, 0)], [1]), "gather_first")
    p0, (cw_out0, dw_full, bg_full, sc_full, pw_in0) = _in_fwd(
        h0, ng3, 0, cw_in0, "conv_in_fwd_0",
        gather([(cwo, 0), conv_dw, pool_b_grp, pool_scale, (pwi, 0)], [0, 2, 1, 1, 1]))
    pool_vecs = (row3(bg_full), row3(sc_full))
    h1, uc0, (pw_grp0, pw_out0, cw_in1, cw_out1, pw_in1, pw_grp1, pw_out1) = _conv_fwd(
        p0, h0, dw_full, conv_vecs, 0, cw_out0, "conv_mix_fwd_0",
        gather([(pwg, 0), (pwo, 0), (cwi, 1), (cwo, 1), (pwi, 1), (pwg, 1), (pwo, 1)], [1, 0, 1, 0, 1, 1, 0]))
    h2, p1, _ = _pool_fwd(h1, ng3, 1, pw_in0, pw_grp0, pool_vecs, 0, pw_out0, "pool_fwd_0")
    p2, _ = _in_fwd(h2, ng3, 2, cw_in1, "conv_in_fwd_1")
    h3, uc2, _ = _conv_fwd(p2, h2, dw_full, conv_vecs, 1, cw_out1, "conv_mix_fwd_1")
    dh, p3, d_final_g, loss_part, _ = _pool_fwd(h3, ng3, 3, pw_in1, pw_grp1, pool_vecs, 1, pw_out1, "pool_fwd_1",
                                               final=(target, final_g[None, :]))

    dp, g_pwo1, g_pwg1, dpv1, _ = _pool_bwd(dh, p3, pw_grp1, pool_vecs, 1, pw_out1, "pool_mix_bwd_1")
    dh, g_pwi1, dg3, _ = _in_bwd(dp, h3, dh, ng3, 3, pw_in1, "pool_in_bwd_1")
    dp, g_cwo1, ddw1, dcv1, (s_pwo1, s_pwg1, s_pwi1) = _conv_bwd(
        dh, p2, uc2, dw_full, conv_vecs, 1, cw_out1, "conv_mix_bwd_1", scatter([g_pwo1, g_pwg1, g_pwi1], [0, 1, 1]))
    dh, g_cwi1, dg2, _ = _in_bwd(dp, h2, dh, ng3, 2, cw_in1, "conv_in_bwd_1")
    dp, g_pwo0, g_pwg0, dpv0, (s_cwo1, s_cwi1) = _pool_bwd(dh, p1, pw_grp0, pool_vecs, 0, pw_out0, "pool_mix_bwd_0",
                                                           scatter([g_cwo1, g_cwi1], [0, 1]))
    dh, g_pwi0, dg1, _ = _in_bwd(dp, h1, dh, ng3, 1, pw_in0, "pool_in_bwd_0")
    dp, g_cwo0, ddw0, dcv0, (s_pwo0, s_pwg0, s_pwi0, s_ddw1, s_dcv1, s_dpv0, s_dpv1, s_dg1, s_dg2, s_dg3, s_dfg,
                             s_loss) = _conv_bwd(
        dh, p0, uc0, dw_full, conv_vecs, 0, cw_out0, "conv_mix_bwd_0",
        scatter([g_pwo0, g_pwg0, g_pwi0, ddw1, dcv1, dpv0, dpv1, dg1, dg2, dg3, d_final_g, loss_part],
                [0, 1, 1, 1, None, 1, 1, None, None, None, None, None]))
    dh, g_cwi0, dg0, _ = _in_bwd(dp, h0, dh, ng3, 0, cw_in0, "conv_in_bwd_0")
    grad_x = dh[None]
    s_cwo0, s_cwi0, s_ddw0, s_dcv0, s_dg0 = _exchange_call(
        scatter([g_cwo0, g_cwi0, ddw0, dcv0, dg0], [0, 1, 1, None, None]), "scatter_last")

    res = {}
    res["conv_w_in"] = _adamw([s_cwi0, s_cwi1], conv_w_in, m_conv_w_in, v_conv_w_in, "adamw_conv_w_in")
    res["conv_w_out"] = _adamw([s_cwo0, s_cwo1], conv_w_out, m_conv_w_out, v_conv_w_out, "adamw_conv_w_out")
    res["pool_w_in"] = _adamw([s_pwi0, s_pwi1], pool_w_in, m_pool_w_in, v_pool_w_in, "adamw_pool_w_in")
    res["pool_w_grp"] = _adamw([s_pwg0, s_pwg1], pool_w_grp, m_pool_w_grp, v_pool_w_grp, "adamw_pool_w_grp")
    res["pool_w_out"] = _adamw([s_pwo0, s_pwo1], pool_w_out, m_pool_w_out, v_pool_w_out, "adamw_pool_w_out")
    stacks = [s_ddw0, s_ddw1, s_dpv0, s_dpv1, s_dg0, s_dg1, s_dg2, s_dg3, s_dfg, s_dcv0, s_dcv1]
    small = [
        ("conv_dw", conv_dw, m_conv_dw, v_conv_dw, [(0, None, 0), (1, None, 1)]),
        ("pool_b_grp", pool_b_grp, m_pool_b_grp, v_pool_b_grp, [(2, 0, 0), (3, 0, 1)]),
        ("pool_scale", pool_scale, m_pool_scale, v_pool_scale, [(2, 1, 0), (3, 1, 1)]),
        ("norm_g", norm_g, m_norm_g, v_norm_g, [(4, 0, 0), (5, 0, 1), (6, 0, 2), (7, 0, 3)]),
        ("final_g", final_g[None, :], m_final_g[None, :], v_final_g[None, :], [(8, 0, 0)]),
        ("conv_dw_b", conv_dw_b, m_conv_dw_b, v_conv_dw_b, [(9, 0, 0), (10, 0, 1)]),
        ("conv_ln_g", conv_ln_g, m_conv_ln_g, v_conv_ln_g, [(9, 1, 0), (10, 1, 1)]),
        ("conv_ln_b", conv_ln_b, m_conv_ln_b, v_conv_ln_b, [(9, 2, 0), (10, 2, 1)]),
    ]
    small_res, loss_block = _adamw_small([s[1:] for s in small], stacks, s_loss, "adamw_small")
    for (name, *_), r in zip(small, small_res):
        res[name] = [a[0] for a in r] if name == "final_g" else r
    loss = loss_block[0, 0]

    names = ["norm_g", "final_g", "conv_w_in", "conv_dw", "conv_dw_b", "conv_ln_g", "conv_ln_b", "conv_w_out",
             "pool_w_in", "pool_w_grp", "pool_b_grp", "pool_scale", "pool_w_out"]
    return (loss, grad_x) + tuple(res[n][q] for q in range(4) for n in names)
```

```python
import jax
import jax.numpy as jnp
from jax import lax
from jax.experimental import pallas as pl
from jax.experimental.pallas import tpu as pltpu

F32 = jnp.float32
BF16 = jnp.bfloat16

RMS_EPS = 1e-6
LN_EPS = 1e-5
CONV_TAPS = 31
CONV_HALO = 32
POOL_WINDOWS = (2, 4, 8, 16)
POOL_HALO = 16
SUBLANES = 8
LANES = 128
N_DEV = 8
V7X_VMEM_LIMIT = 56 * 1024 * 1024

ADAM_LR = 0.001
ADAM_B1 = 0.9
ADAM_B2 = 0.999
ADAM_EPS = 1e-08
ADAM_WD = 0.01
ADAM_STEP = 10

MESH = pl.DeviceIdType.MESH
ANY = pl.BlockSpec(memory_space=pl.ANY)


def _dot(a, b):
    return lax.dot_general(a, b, (((1,), (0,)), ((), ())), preferred_element_type=F32)


def _dot_nt(a, b):
    return lax.dot_general(a, b, (((1,), (1,)), ((), ())), preferred_element_type=F32)


def _dot_tn(a, b):
    return lax.dot_general(a, b, (((0,), (0,)), ((), ())), preferred_element_type=F32)


def _sigmoid(x):
    return jax.nn.sigmoid(x)


def _dsilu(x, s):
    return s * (1.0 + x * (1.0 - s))


def _rows8(x):
    r, c = x.shape
    return jnp.sum(x.reshape(r // SUBLANES, SUBLANES, c), axis=0)


def _tile(t, pref):
    return pref if t >= 2 * pref else t // 2


def _const(shape, index):
    return pl.BlockSpec(shape, lambda *_: index, pipeline_mode=pl.Buffered(1))


def _params(grid_rank=1):
    return pltpu.CompilerParams(dimension_semantics=("arbitrary",) * grid_rank, vmem_limit_bytes=V7X_VMEM_LIMIT)


def _chunks(n, rc, fn):
    def step(c, carry):
        fn(pl.multiple_of(c * rc, rc))
        return carry
    lax.fori_loop(0, n, step, 0)


def _mesh_position():
    return lax.axis_index("x"), lax.axis_index("y"), lax.axis_index("c")


def _peer(j):
    x, y, c = _mesh_position()
    px = 1 - x if j & 4 else x
    py = 1 - y if j & 2 else y
    pc = 1 - c if j & 1 else c
    return (px, py, pc), 4 * px + 2 * py + pc


def _block(ref, axis, index, size):
    idx = [slice(None)] * len(ref.shape)
    idx[axis] = pl.ds(pl.multiple_of(index * size, size), size)
    return ref.at[tuple(idx)]


class _Exchange:
    def __init__(self, kind, arrays, axes):
        self.kind, self.axes = kind, list(axes)
        self.arrays = [a[0] if isinstance(a, tuple) else a for a in arrays]
        self.layers = [a[1] if isinstance(a, tuple) else None for a in arrays]
        self.n = len(self.arrays)
        self.blk, self.out_shapes = [], []
        for a, layer, ax in zip(self.arrays, self.layers, self.axes):
            s = list(a.shape if layer is None else a.shape[1:])
            if kind == "gather":
                self.blk.append(s[ax])
                s[ax] *= N_DEV
                self.out_shapes.append(jax.ShapeDtypeStruct(tuple(s), a.dtype))
            else:
                if ax is not None:
                    s[ax] //= N_DEV
                    self.blk.append(s[ax])
                else:
                    self.blk.append(None)
                self.out_shapes.append(jax.ShapeDtypeStruct((N_DEV,) + tuple(s), a.dtype))

    def sem_shapes(self):
        return [pltpu.SemaphoreType.DMA((N_DEV - 1, self.n)), pltpu.SemaphoreType.DMA((N_DEV - 1, self.n)),
                pltpu.SemaphoreType.DMA((self.n,))]

    def _src(self, ins, k, owner):
        ref = ins[k] if self.layers[k] is None else ins[k].at[self.layers[k]]
        if self.kind == "gather" or self.axes[k] is None:
            return ref
        return _block(ref, self.axes[k], owner, self.blk[k])

    def _dst(self, outs, k, sender):
        if self.kind == "gather":
            return _block(outs[k], self.axes[k], sender, self.blk[k])
        return outs[k].at[sender]

    def _copies(self, ins, outs, sems, arriving):
        send, recv, loc = sems
        x, y, c = _mesh_position()
        me = 4 * x + 2 * y + c
        if not arriving:
            local = [pltpu.make_async_copy(self._src(ins, k, me), self._dst(outs, k, me), loc.at[k])
                     for k in range(self.n)]
        else:
            local = []
        remote = []
        for j in range(1, N_DEV):
            peer, peer_id = _peer(j)
            for k in range(self.n):
                owner, sender = (me, peer_id) if arriving else (peer_id, me)
                remote.append(pltpu.make_async_remote_copy(
                    src_ref=self._src(ins, k, owner), dst_ref=self._dst(outs, k, sender),
                    send_sem=send.at[j - 1, k], recv_sem=recv.at[j - 1, k], device_id=peer, device_id_type=MESH))
        return local, remote

    def start(self, ins, outs, sems):
        local, sends = self._copies(ins, outs, sems, arriving=False)
        for cp in local + sends:
            cp.start()

    def finish(self, ins, outs, sems):
        for cp in self._copies(ins, outs, sems, arriving=True)[1]:
            cp.wait_recv()
        local, sends = self._copies(ins, outs, sems, arriving=False)
        for cp in sends:
            cp.wait_send()
        for cp in local:
            cp.wait()


def _exchange_call(ex, name):
    def body(*refs):
        ins, outs, sems = refs[:ex.n], refs[ex.n:2 * ex.n], refs[2 * ex.n:]
        ex.start(ins, outs, sems)
        ex.finish(ins, outs, sems)

    return pl.pallas_call(body, name=name, in_specs=[ANY] * ex.n, out_specs=[ANY] * ex.n,
                          out_shape=ex.out_shapes, scratch_shapes=ex.sem_shapes())(*ex.arrays)


def _gather_via_sibling_call(ex, name):
    n = ex.n

    def body(*refs):
        ins, outs = refs[:n], refs[n:2 * n]
        send, recv, loc = refs[2 * n:]
        x, y, c = _mesh_position()
        ident = lambda px, py, pc: 4 * px + 2 * py + pc
        me, sibling = ident(x, y, c), (x, y, 1 - c)
        chips = [(1 - x, y), (x, 1 - y), (1 - x, 1 - y)]

        def copy(row, k, block, to, src=None):
            place = ex._dst(outs, k, block)
            return pltpu.make_async_remote_copy(
                src_ref=place if src is None else src, dst_ref=place,
                send_sem=send.at[row, k], recv_sem=recv.at[row, k], device_id=to, device_id_type=MESH)

        local = [pltpu.make_async_copy(ex._src(ins, k, me), ex._dst(outs, k, me), loc.at[k]) for k in range(n)]
        first = []
        for k in range(n):
            mine = ex._src(ins, k, me)
            first.append(copy(0, k, me, sibling, src=mine))
            first += [copy(1 + j, k, me, (*chip, c), src=mine) for j, chip in enumerate(chips)]
        for cp in local + first:
            cp.start()
        passed = []
        for j, chip in enumerate(chips):
            for k in range(n):
                copy(1 + j, k, ident(*chip, c), sibling).wait_recv()
                passed.append(copy(4 + j, k, ident(*chip, c), sibling))
                passed[-1].start()
        for k in range(n):
            copy(0, k, ident(x, y, 1 - c), sibling).wait_recv()
            for j, chip in enumerate(chips):
                copy(4 + j, k, ident(*chip, 1 - c), sibling).wait_recv()
        for cp in first + passed:
            cp.wait_send()
        for cp in local:
            cp.wait()

    return pl.pallas_call(body, name=name, in_specs=[ANY] * n, out_specs=[ANY] * n, out_shape=ex.out_shapes,
                          scratch_shapes=ex.sem_shapes())(*ex.arrays)


def _launch(body, name, nt, in_specs, out_specs, out_shape, scratch_shapes, args, ex=None):
    if ex is None:
        outs = pl.pallas_call(body, name=name, grid=(nt,), in_specs=in_specs, out_specs=out_specs,
                              out_shape=out_shape, scratch_shapes=scratch_shapes, compiler_params=_params())(*args)
        return list(outs), []
    n_in, n_out, n_scr = len(in_specs), len(out_specs), len(scratch_shapes)

    def riding(*refs):
        a, xa = refs[:n_in], refs[n_in:n_in + ex.n]
        o = refs[n_in + ex.n:n_in + ex.n + n_out]
        xo = refs[n_in + ex.n + n_out:n_in + 2 * ex.n + n_out]
        s = refs[n_in + 2 * ex.n + n_out:n_in + 2 * ex.n + n_out + n_scr]
        sems = refs[n_in + 2 * ex.n + n_out + n_scr:]
        i = pl.program_id(0)

        @pl.when(i == 0)
        def _():
            ex.start(xa, xo, sems)

        body(*a, *o, *s)

        @pl.when(i == nt - 1)
        def _():
            ex.finish(xa, xo, sems)

    outs = pl.pallas_call(
        riding, name=name, grid=(nt,),
        in_specs=list(in_specs) + [ANY] * ex.n, out_specs=list(out_specs) + [ANY] * ex.n,
        out_shape=list(out_shape) + ex.out_shapes, scratch_shapes=list(scratch_shapes) + ex.sem_shapes(),
        compiler_params=_params())(*args, *ex.arrays)
    return list(outs[:n_out]), list(outs[n_out:])


def _in_fwd(h, norm_g, layer, w_in, name, ex=None):
    t, d = h.shape
    n = w_in.shape[-1]
    tm = _tile(t, 512)

    def body(h_ref, g_ref, w_ref, p_ref):
        x = h_ref[...]
        r = lax.rsqrt(jnp.mean(x * x, axis=-1, keepdims=True) + RMS_EPS)
        p_ref[...] = _dot((x * r * g_ref[...]).astype(BF16), w_ref[...])

    (p,), xouts = _launch(
        body, name, t // tm,
        [pl.BlockSpec((tm, d), lambda i: (i, 0)), _const((None, 1, d), (layer, 0, 0)), _const((d, n), (0, 0))],
        [pl.BlockSpec((tm, n), lambda i: (i, 0))],
        [jax.ShapeDtypeStruct((t, n), F32)], [], (h, norm_g, w_in), ex)
    return p, xouts


def _layernorm_rows(uc, lng, lnb):
    mu = jnp.mean(uc, axis=-1, keepdims=True)
    xc = uc - mu
    rstd = lax.rsqrt(jnp.mean(xc * xc, axis=-1, keepdims=True) + LN_EPS)
    xhat = xc * rstd
    return xhat, rstd, xhat * lng + lnb


def _conv_fwd(p, h, dw, vecs, layer, w_out, name, ex=None):
    t, d = h.shape
    e = w_out.shape[0]
    tm = _tile(t, 256)
    rc = _tile(tm, 128)

    def body(p_ref, h_ref, dw_ref, dwb_ref, lng_ref, lnb_ref, wo_ref, ho_ref, xr_ref, sg_ref, ub_ref, us_scr, uc_ref):
        i = pl.program_id(0)

        @pl.when(i == 0)
        def _():
            us_scr[:, pl.ds(0, CONV_HALO), :] = jnp.zeros((SUBLANES, CONV_HALO, e), F32)

        @pl.when(i > 0)
        def _():
            us_scr[:, pl.ds(0, CONV_HALO), :] = us_scr[:, pl.ds(tm, CONV_HALO), :]

        sb = _sigmoid(p_ref[:, pl.ds(e, e)])
        u = p_ref[:, pl.ds(0, e)] * sb
        ub_ref[:, pl.ds(0, e)] = u
        ub_ref[:, pl.ds(e, e)] = sb
        us_scr[0, pl.ds(CONV_HALO, tm), :] = u
        for r in range(1, SUBLANES):
            us_scr[r, pl.ds(CONV_HALO, tm), :] = us_scr[0, pl.ds(CONV_HALO - r, tm), :]

        def c_conv(base):
            for lt in range(e // LANES):
                cols = pl.ds(lt * LANES, LANES)
                acc = jnp.broadcast_to(dwb_ref[:, cols], (rc, LANES))
                for r in range(SUBLANES):
                    nq = (CONV_TAPS - 1 - r) // SUBLANES + 1
                    lo = SUBLANES * (nq - 1)
                    win = us_scr[r, pl.ds(pl.multiple_of(CONV_HALO + base - lo, SUBLANES), rc + lo), cols]
                    for q in range(nq):
                        k = CONV_TAPS - 1 - (SUBLANES * q + r)
                        at = lo - SUBLANES * q
                        acc = acc + dw_ref[pl.ds(k, 1), cols] * win[at:at + rc, :]
                uc_ref[pl.ds(base, rc), cols] = acc
        _chunks(tm // rc, rc, c_conv)

        xhat, rstd, ul = _layernorm_rows(uc_ref[...], lng_ref[...], lnb_ref[...])
        z = p_ref[:, pl.ds(2 * e, e)]
        sg_u = _sigmoid(ul)
        sg_z = _sigmoid(z)
        xr_ref[:, pl.ds(0, e)] = xhat
        xr_ref[:, pl.ds(e, LANES)] = jnp.broadcast_to(rstd, (tm, LANES))
        sg_ref[:, pl.ds(0, e)] = sg_u
        sg_ref[:, pl.ds(e, e)] = sg_z
        v = ((ul * sg_u) * (z * sg_z)).astype(BF16)
        ho_ref[...] = h_ref[...] + _dot(v, wo_ref[...])

    vec = _const((None, 1, e), (layer, 0, 0))
    row = lambda i: (i, 0)
    outs, xouts = _launch(
        body, name, t // tm,
        [pl.BlockSpec((tm, 3 * e), row), pl.BlockSpec((tm, d), row),
         _const((None, CONV_TAPS, e), (layer, 0, 0)), vec, vec, vec, _const((e, d), (0, 0))],
        [pl.BlockSpec((tm, d), row), pl.BlockSpec((tm, e + LANES), row), pl.BlockSpec((tm, 2 * e), row),
         pl.BlockSpec((tm, 2 * e), row)],
        [jax.ShapeDtypeStruct((t, d), F32), jax.ShapeDtypeStruct((t, e + LANES), F32),
         jax.ShapeDtypeStruct((t, 2 * e), F32), jax.ShapeDtypeStruct((t, 2 * e), F32)],
        [pltpu.VMEM((SUBLANES, tm + CONV_HALO, e), F32), pltpu.VMEM((tm, e), F32)], (p, h, dw, *vecs, w_out), ex)
    return outs[0], tuple(outs[1:]), xouts


def _conv_bwd(dho, p, saved, dw, vecs, layer, w_out, name, ex=None):
    t, d = dho.shape
    e = w_out.shape[0]
    tm = _tile(t, 256)
    nt = t // tm
    rc = 16
    vec0 = CONV_TAPS + 1

    def body(dho_ref, z_ref, xr_ref, sg_ref, ub_ref, dw_ref, lng_ref, lnb_ref, wo_ref,
             dp_ref, dwo_ref, ddw_ref, dvec_ref, ds_scr, acc_scr, dwo_scr):
        i = pl.program_id(0)

        @pl.when(i == 0)
        def _():
            dwo_scr[...] = jnp.zeros_like(dwo_scr)
            acc_scr[...] = jnp.zeros_like(acc_scr)
            ds_scr[:, pl.ds(tm, CONV_HALO), :] = jnp.zeros((SUBLANES, CONV_HALO, e), F32)

        @pl.when(i > 0)
        def _():
            ds_scr[:, pl.ds(tm, CONV_HALO), :] = ds_scr[:, pl.ds(0, CONV_HALO), :]

        lng = lng_ref[...]
        lnb = lnb_ref[...]

        xhat = xr_ref[:, pl.ds(0, e)]
        rstd = xr_ref[:, pl.ds(e, 1)]
        ul = xhat * lng + lnb
        z = z_ref[...]
        sg_u = sg_ref[:, pl.ds(0, e)]
        sg_z = sg_ref[:, pl.ds(e, e)]
        s_u = ul * sg_u
        s_z = z * sg_z
        v = (s_u * s_z).astype(BF16)
        dy = dho_ref[...].astype(BF16)
        dv = _dot_nt(dy, wo_ref[...])
        dwo_scr[...] += _dot_tn(v, dy)

        dul = dv * s_z * _dsilu(ul, sg_u)
        dp_ref[:, pl.ds(2 * e, e)] = (dv * s_u * _dsilu(z, sg_z)).astype(BF16)
        acc_scr[pl.ds((vec0 + 1) * SUBLANES, SUBLANES), :] += _rows8(dul * xhat)
        acc_scr[pl.ds((vec0 + 2) * SUBLANES, SUBLANES), :] += _rows8(dul)
        dxh = dul * lng
        duc = rstd * (dxh - jnp.mean(dxh, axis=-1, keepdims=True)
                      - xhat * jnp.mean(dxh * xhat, axis=-1, keepdims=True))
        acc_scr[pl.ds(vec0 * SUBLANES, SUBLANES), :] += _rows8(duc)
        ds_scr[0, pl.ds(0, tm), :] = duc
        for r in range(1, SUBLANES):
            ds_scr[r, pl.ds(0, tm), :] = ds_scr[0, pl.ds(r, tm), :]

        def c_conv(base):
            rows = pl.ds(base, rc)
            u = ub_ref[rows, pl.ds(0, e)]
            sb = ub_ref[rows, pl.ds(e, e)]
            du = jnp.zeros((rc, e), F32)
            for o in range(CONV_TAPS):
                q, r = divmod(o, SUBLANES)
                k = CONV_TAPS - 1 - o
                sh = ds_scr[r, pl.ds(pl.multiple_of(base + SUBLANES * q, SUBLANES), rc), :]
                du = du + dw_ref[pl.ds(k, 1), :] * sh
                acc_scr[pl.ds(k * SUBLANES, SUBLANES), :] += _rows8(u * sh)
            dp_ref[rows, pl.ds(0, e)] = (du * sb).astype(BF16)
            dp_ref[rows, pl.ds(e, e)] = (du * u * (1.0 - sb)).astype(BF16)
        _chunks(tm // rc, rc, c_conv)

        @pl.when(i == nt - 1)
        def _():
            dwo_ref[...] = dwo_scr[...].astype(BF16)
            slot_sum = lambda k: jnp.sum(acc_scr[pl.ds(k * SUBLANES, SUBLANES), :], axis=0, keepdims=True)
            for k in range(CONV_TAPS):
                ddw_ref[pl.ds(k, 1), :] = slot_sum(k)
            dvec_ref[...] = jnp.zeros_like(dvec_ref)
            for k in range(3):
                dvec_ref[pl.ds(k, 1), :] = slot_sum(vec0 + k)

    rev = lambda i: (nt - 1 - i, 0)
    vec = _const((None, 1, e), (layer, 0, 0))
    (dp, dwo, ddw, dvec), xouts = _launch(
        body, name, nt,
        [pl.BlockSpec((tm, d), rev), pl.BlockSpec((tm, e), lambda i: (nt - 1 - i, 2)),
         pl.BlockSpec((tm, e + LANES), rev), pl.BlockSpec((tm, 2 * e), rev), pl.BlockSpec((tm, 2 * e), rev),
         _const((None, CONV_TAPS, e), (layer, 0, 0)), vec, vec, _const((e, d), (0, 0))],
        [pl.BlockSpec((tm, 3 * e), rev), _const((e, d), (0, 0)), _const((CONV_TAPS, e), (0, 0)),
         _const((SUBLANES, e), (0, 0))],
        [jax.ShapeDtypeStruct((t, 3 * e), BF16), jax.ShapeDtypeStruct((e, d), BF16),
         jax.ShapeDtypeStruct((CONV_TAPS, e), F32), jax.ShapeDtypeStruct((SUBLANES, e), F32)],
        [pltpu.VMEM((SUBLANES, tm + CONV_HALO, e), F32), pltpu.VMEM(((vec0 + 3) * SUBLANES, e), F32),
         pltpu.VMEM((e, d), F32)],
        (dho, p, *saved, dw, vecs[1], vecs[2], w_out), ex)
    return dp, dwo, ddw, dvec, xouts


def _inv_count(tile, tm, w):
    tpos = tile * tm + lax.broadcasted_iota(jnp.int32, (tm, 1), 0)
    return 1.0 / jnp.minimum(tpos + 1, w).astype(F32)


def _pool_d_group(ue_scr, tile, tm, gc, g):
    w = POOL_WINDOWS[g]
    win = ue_scr[:, pl.ds(g * gc, gc)]
    s = win
    sh = 1
    while sh < w:
        s = s + pltpu.roll(s, sh, axis=0)
        sh *= 2
    return s[POOL_HALO:, :] * _inv_count(tile, tm, w) - win[POOL_HALO:, :]


def _final_rows(ho, tg_ref, fg_ref, dh_ref, acc_scr, lacc_scr):
    d = ho.shape[-1]
    r = lax.rsqrt(jnp.mean(ho * ho, axis=-1, keepdims=True) + RMS_EPS)
    nrm = ho * r
    err = nrm * fg_ref[...] - tg_ref[...]
    lacc_scr[...] += _rows8(err * err)
    dy = err * (1.0 / d)
    acc_scr[...] += _rows8(dy * nrm)
    dq = dy * fg_ref[...]
    dh_ref[...] = r * (dq - nrm * jnp.mean(dq * nrm, axis=-1, keepdims=True))


def _pool_fwd(h, norm_g, nlayer, w_in, w_grp, vecs, layer, w_out, name, ex=None, final=None):
    t, d = h.shape
    e = w_out.shape[0]
    ng = len(POOL_WINDOWS)
    gc = e // ng
    tm = _tile(t, 512)
    nt = t // tm
    cw = 2 * e // (2 * ng)

    def layer_rows(i, hn_ref, hc_ref, g_ref, wi_ref, wg_ref, bg_ref, sc_ref, wo_ref, p_ref, d_ref,
                   ue_scr, y_scr, pbuf, z_scr, hn_scr):
        tile = jnp.maximum(i - 1, 0)

        @pl.when(i == 0)
        def _():
            pbuf[...] = jnp.zeros_like(pbuf)

        @pl.when(i <= 1)
        def _():
            ue_scr[pl.ds(0, POOL_HALO), :] = jnp.zeros((POOL_HALO, e), F32)

        @pl.when(i > 1)
        def _():
            ue_scr[pl.ds(0, POOL_HALO), :] = ue_scr[pl.ds(tm, POOL_HALO), :]

        ue_scr[pl.ds(POOL_HALO, tm), :] = pbuf[:, pl.ds(0, e)]
        z_scr[...] = pbuf[:, pl.ds(e, e)]

        x = hn_ref[...]
        r = lax.rsqrt(jnp.mean(x * x, axis=-1, keepdims=True) + RMS_EPS)
        hn_scr[...] = (x * r * g_ref[...]).astype(BF16)

        def project(c):
            part = _dot(hn_scr[...], wi_ref[:, pl.ds(c * cw, cw)])
            pbuf[:, pl.ds(c * cw, cw)] = part
            p_ref[:, pl.ds(c * cw, cw)] = part

        for g in range(ng):
            project(2 * g)
            cols = pl.ds(g * gc, gc)
            dg = _pool_d_group(ue_scr, tile, tm, gc, g).astype(BF16)
            d_ref[:, cols] = dg
            z = z_scr[:, cols]
            y1 = (_dot(dg, wg_ref[g]) + bg_ref[:, cols]) * sc_ref[:, cols]
            y_scr[:, cols] = (y1 * (z * _sigmoid(z))).astype(BF16)
            project(2 * g + 1)

        return hc_ref[...] + _dot(y_scr[...], wo_ref[...])

    nxt = lambda i: (jnp.minimum(i, nt - 1), 0)
    cur = lambda i: (jnp.maximum(i - 1, 0), 0)
    vec = _const((None, 1, e), (layer, 0, 0))
    in_specs = [pl.BlockSpec((tm, d), nxt), pl.BlockSpec((tm, d), cur), _const((None, 1, d), (nlayer, 0, 0)),
                _const((d, 2 * e), (0, 0)), _const((ng, gc, gc), (0, 0, 0)), vec, vec, _const((e, d), (0, 0))]
    scratch = [pltpu.VMEM((tm + POOL_HALO, e), F32), pltpu.VMEM((tm, e), BF16), pltpu.VMEM((tm, 2 * e), F32),
               pltpu.VMEM((tm, e), F32), pltpu.VMEM((tm, d), BF16)]
    args = (h, h, norm_g, w_in, w_grp, *vecs, w_out)

    if final is None:
        def body(hn_ref, hc_ref, g_ref, wi_ref, wg_ref, bg_ref, sc_ref, wo_ref, ho_ref, p_ref, d_ref, *scr):
            ho_ref[...] = layer_rows(pl.program_id(0), hn_ref, hc_ref, g_ref, wi_ref, wg_ref, bg_ref, sc_ref, wo_ref,
                                     p_ref, d_ref, *scr)

        (ho, p, dsaved), xouts = _launch(
            body, name, nt + 1, in_specs,
            [pl.BlockSpec((tm, d), cur), pl.BlockSpec((tm, 2 * e), nxt), pl.BlockSpec((tm, e), cur)],
            [jax.ShapeDtypeStruct((t, d), F32), jax.ShapeDtypeStruct((t, 2 * e), F32),
             jax.ShapeDtypeStruct((t, e), BF16)], scratch, args, ex)
        return ho, (p, dsaved), xouts

    target, final_g = final

    def body(hn_ref, hc_ref, g_ref, wi_ref, wg_ref, bg_ref, sc_ref, wo_ref, tg_ref, fg_ref,
             dh_ref, p_ref, d_ref, dfg_ref, loss_ref, ue_scr, y_scr, pbuf, z_scr, hn_scr, acc_scr, lacc_scr):
        i = pl.program_id(0)

        @pl.when(i <= 1)
        def _():
            acc_scr[...] = jnp.zeros_like(acc_scr)
            lacc_scr[...] = jnp.zeros_like(lacc_scr)

        ho = layer_rows(i, hn_ref, hc_ref, g_ref, wi_ref, wg_ref, bg_ref, sc_ref, wo_ref, p_ref, d_ref,
                        ue_scr, y_scr, pbuf, z_scr, hn_scr)
        _final_rows(ho, tg_ref, fg_ref, dh_ref, acc_scr, lacc_scr)

        @pl.when(i == nt)
        def _():
            dfg_ref[...] = jnp.zeros_like(dfg_ref)
            dfg_ref[pl.ds(0, 1), :] = jnp.sum(acc_scr[...], axis=0, keepdims=True)
            loss_ref[...] = jnp.broadcast_to(jnp.sum(lacc_scr[...]) * (0.5 / d), loss_ref.shape)

    (dh, p, dsaved, dfg, loss), xouts = _launch(
        body, name, nt + 1, in_specs + [pl.BlockSpec((tm, d), cur), _const((1, d), (0, 0))],
        [pl.BlockSpec((tm, d), cur), pl.BlockSpec((tm, 2 * e), nxt), pl.BlockSpec((tm, e), cur),
         _const((SUBLANES, d), (0, 0)), _const((SUBLANES, LANES), (0, 0))],
        [jax.ShapeDtypeStruct((t, d), F32), jax.ShapeDtypeStruct((t, 2 * e), F32), jax.ShapeDtypeStruct((t, e), BF16),
         jax.ShapeDtypeStruct((SUBLANES, d), F32), jax.ShapeDtypeStruct((SUBLANES, LANES), F32)],
        scratch + [pltpu.VMEM((SUBLANES, d), F32), pltpu.VMEM((SUBLANES, d), F32)],
        args + (target, final_g), ex)
    return dh, (p, dsaved), dfg, loss, xouts


def _pool_bwd(dho, saved, w_grp, vecs, layer, w_out, name, ex=None):
    t, d = dho.shape
    e = w_out.shape[0]
    ng = len(POOL_WINDOWS)
    gc = e // ng
    tm = _tile(t, 512)
    nt = t // tm

    def body(dho_ref, z_ref, d_ref, wg_ref, bg_ref, sc_ref, wo_ref, dp_ref, dwo_ref, dwg_ref, dvec_ref,
             ee_scr, acc_scr, dwo_scr, dwg_scr):
        i = pl.program_id(0)
        tile = nt - 1 - i

        @pl.when(i == 0)
        def _():
            dwo_scr[...] = jnp.zeros_like(dwo_scr)
            dwg_scr[...] = jnp.zeros_like(dwg_scr)
            acc_scr[...] = jnp.zeros_like(acc_scr)
            ee_scr[pl.ds(tm, POOL_HALO), :] = jnp.zeros((POOL_HALO, e), F32)

        @pl.when(i > 0)
        def _():
            ee_scr[pl.ds(tm, POOL_HALO), :] = ee_scr[pl.ds(0, POOL_HALO), :]

        bg = bg_ref[...]
        sc = sc_ref[...]
        ds = [d_ref[:, pl.ds(g * gc, gc)] for g in range(ng)]
        ob = jnp.concatenate([_dot(ds[g], wg_ref[g]) for g in range(ng)], axis=1) + bg
        z = z_ref[...]
        sg_z = _sigmoid(z)
        s_z = z * sg_z
        y1 = ob * sc
        dy = dho_ref[...].astype(BF16)
        dy2 = _dot_nt(dy, wo_ref[...])
        dwo_scr[...] += _dot_tn((y1 * s_z).astype(BF16), dy)
        dy1 = dy2 * s_z
        dp_ref[:, pl.ds(e, e)] = (dy2 * y1 * _dsilu(z, sg_z)).astype(BF16)
        acc_scr[pl.ds(SUBLANES, SUBLANES), :] += _rows8(dy1 * ob)
        do = dy1 * sc
        acc_scr[pl.ds(0, SUBLANES), :] += _rows8(do)

        n = tm + POOL_HALO
        for g, w in enumerate(POOL_WINDOWS):
            cols = pl.ds(g * gc, gc)
            do_g = do[:, g * gc:(g + 1) * gc].astype(BF16)
            dwg_scr[g] += _dot_tn(ds[g], do_g)
            dd = _dot_nt(do_g, wg_ref[g])
            ee_scr[pl.ds(0, tm), cols] = dd * _inv_count(tile, tm, w)
            s = ee_scr[:, cols]
            sh = 1
            while sh < w:
                s = s + pltpu.roll(s, n - sh, axis=0)
                sh *= 2
            dp_ref[:, cols] = (s[:tm, :] - dd).astype(BF16)

        @pl.when(i == nt - 1)
        def _():
            dwo_ref[...] = dwo_scr[...].astype(BF16)
            dwg_ref[...] = dwg_scr[...].astype(BF16)
            dvec_ref[...] = jnp.zeros_like(dvec_ref)
            for k in range(2):
                dvec_ref[pl.ds(k, 1), :] = jnp.sum(acc_scr[pl.ds(k * SUBLANES, SUBLANES), :], axis=0, keepdims=True)

    rev = lambda i: (nt - 1 - i, 0)
    vec = _const((None, 1, e), (layer, 0, 0))
    (dp, dwo, dwg, dvec), xouts = _launch(
        body, name, nt,
        [pl.BlockSpec((tm, d), rev), pl.BlockSpec((tm, e), lambda i: (nt - 1 - i, 1)), pl.BlockSpec((tm, e), rev),
         _const((ng, gc, gc), (0, 0, 0)), vec, vec, _const((e, d), (0, 0))],
        [pl.BlockSpec((tm, 2 * e), rev), _const((e, d), (0, 0)), _const((ng, gc, gc), (0, 0, 0)),
         _const((SUBLANES, e), (0, 0))],
        [jax.ShapeDtypeStruct((t, 2 * e), BF16), jax.ShapeDtypeStruct((e, d), BF16),
         jax.ShapeDtypeStruct((ng, gc, gc), BF16), jax.ShapeDtypeStruct((SUBLANES, e), F32)],
        [pltpu.VMEM((tm + POOL_HALO, e), F32),
         pltpu.VMEM((2 * SUBLANES, e), F32), pltpu.VMEM((e, d), F32), pltpu.VMEM((ng, gc, gc), F32)],
        (dho, *saved, w_grp, *vecs, w_out), ex)
    return dp, dwo, dwg, dvec, xouts


def _in_bwd(dp, h, dho, norm_g, layer, w_in, name, ex=None):
    t, d = h.shape
    n = w_in.shape[-1]
    tm = _tile(t, 512)
    nt = t // tm

    def body(dp_ref, h_ref, dho_ref, g_ref, w_ref, dh_ref, dw_ref, dg_ref, acc_scr, dw_scr):
        i = pl.program_id(0)

        @pl.when(i == 0)
        def _():
            dw_scr[...] = jnp.zeros_like(dw_scr)
            acc_scr[...] = jnp.zeros_like(acc_scr)

        x = h_ref[...]
        r = lax.rsqrt(jnp.mean(x * x, axis=-1, keepdims=True) + RMS_EPS)
        nrm = x * r
        dp = dp_ref[...]
        dhn = _dot_nt(dp, w_ref[...])
        dw_scr[...] += _dot_tn((nrm * g_ref[...]).astype(BF16), dp)
        acc_scr[...] += _rows8(dhn * nrm)
        dq = dhn * g_ref[...]
        dh_ref[...] = dho_ref[...] + r * (dq - nrm * jnp.mean(dq * nrm, axis=-1, keepdims=True))

        @pl.when(i == nt - 1)
        def _():
            dw_ref[...] = dw_scr[...].astype(BF16)
            dg_ref[...] = jnp.zeros_like(dg_ref)
            dg_ref[pl.ds(0, 1), :] = jnp.sum(acc_scr[...], axis=0, keepdims=True)

    (dh, dw, dg), xouts = _launch(
        body, name, nt,
        [pl.BlockSpec((tm, n), lambda i: (i, 0)), pl.BlockSpec((tm, d), lambda i: (i, 0)),
         pl.BlockSpec((tm, d), lambda i: (i, 0)), _const((None, 1, d), (layer, 0, 0)), _const((d, n), (0, 0))],
        [pl.BlockSpec((tm, d), lambda i: (i, 0)), _const((d, n), (0, 0)), _const((SUBLANES, d), (0, 0))],
        [jax.ShapeDtypeStruct((t, d), F32), jax.ShapeDtypeStruct((d, n), BF16),
         jax.ShapeDtypeStruct((SUBLANES, d), F32)],
        [pltpu.VMEM((SUBLANES, d), F32), pltpu.VMEM((d, n), F32)],
        (dp, h, dho, norm_g, w_in), ex)
    return dh, dw, dg, xouts


def _adam_update(g, w, m, v):
    c1 = 1.0 / (1.0 - ADAM_B1 ** ADAM_STEP)
    c2 = 1.0 / (1.0 - ADAM_B2 ** ADAM_STEP)
    nm = ADAM_B1 * m + (1.0 - ADAM_B1) * g
    nv = ADAM_B2 * v + (1.0 - ADAM_B2) * (g * g)
    return -ADAM_LR * ((nm * c1) / (jnp.sqrt(nv * c2) + ADAM_EPS) + ADAM_WD * w), nm, nv


def _adamw_small(params, stacks, loss_stack, name):
    ns, npar = len(stacks), len(params)

    def body(*refs):
        st = refs[:ns]
        pr = refs[ns:ns + 3 * npar]
        ls_ref = refs[ns + 3 * npar]
        outs = refs[ns + 3 * npar + 1:ns + 7 * npar + 1]
        loss_ref = refs[ns + 7 * npar + 1]
        for q, (w, _, _, pieces) in enumerate(params):
            w_ref, m_ref, v_ref = pr[3 * q:3 * q + 3]
            g_ref, d_ref, nm_ref, nv_ref = outs[4 * q:4 * q + 4]
            for s, row, slab in pieces:
                if w.ndim == 3:
                    take = lambda k: st[s][k]
                    at = slab
                else:
                    take = lambda k: st[s][k, pl.ds(row, 1), :]
                    at = (pl.ds(slab, 1), slice(None))
                g = take(0)
                for k in range(1, N_DEV):
                    g = g + take(k)
                g_ref[at] = g
                d_ref[at], nm_ref[at], nv_ref[at] = _adam_update(g, w_ref[at], m_ref[at], v_ref[at])
        tot = ls_ref[0]
        for k in range(1, N_DEV):
            tot = tot + ls_ref[k]
        loss_ref[...] = tot

    flat = [a for (w, m, v, _) in params for a in (w, m, v)]
    out_shape = [jax.ShapeDtypeStruct(w.shape, F32) for (w, _, _, _) in params for _ in range(4)]
    whole = pl.BlockSpec(memory_space=pltpu.VMEM)
    outs = pl.pallas_call(
        body, name=name, in_specs=[whole] * (ns + 3 * npar + 1), out_specs=[whole] * (4 * npar + 1),
        out_shape=out_shape + [jax.ShapeDtypeStruct(loss_stack.shape[1:], F32)],
    )(*stacks, *flat, loss_stack)
    return [outs[4 * q:4 * q + 4] for q in range(npar)], outs[-1]


def _adamw(stacks, w, m, v, name):
    nl = len(stacks)
    shp = w.shape
    c = shp[-1]
    r = 1
    for s in shp[1:-1]:
        r *= s
    tr = r
    for cand in (512, 256, 128, 64, 32, 16):
        if r % cand == 0 and r > cand:
            tr = cand
            break
    nrb = r // tr

    def body(*refs):
        s_refs = refs[:nl]
        w_ref, m_ref, v_ref, g_ref, d_ref, nm_ref, nv_ref = refs[nl:]
        layer = pl.program_id(0)
        for l in range(nl):
            @pl.when(layer == l)
            def _(l=l):
                g = s_refs[l][0].astype(F32)
                for k in range(1, N_DEV):
                    g = g + s_refs[l][k].astype(F32)
                g_ref[...] = g
                d_ref[...], nm_ref[...], nv_ref[...] = _adam_update(g, w_ref[...], m_ref[...], v_ref[...])

    def stack_spec(l):
        return pl.BlockSpec((N_DEV, tr, c),
                            lambda j, i: (0, jnp.where(j == l, i, jnp.where(j < l, 0, nrb - 1)), 0))

    spec = pl.BlockSpec((None, tr, c), lambda j, i: (j, i, 0))
    outs = pl.pallas_call(
        body, name=name, grid=(nl, nrb),
        in_specs=[stack_spec(l) for l in range(nl)] + [spec, spec, spec],
        out_specs=[spec] * 4,
        out_shape=[jax.ShapeDtypeStruct((nl, r, c), F32)] * 4,
        compiler_params=_params(2),
    )(*[s.reshape(N_DEV, r, c) for s in stacks], w.reshape(nl, r, c), m.reshape(nl, r, c), v.reshape(nl, r, c))
    return [o.reshape(shp) for o in outs]


def kernel(x, norm_g, final_g, conv_w_in, conv_dw, conv_dw_b, conv_ln_g, conv_ln_b, conv_w_out, pool_w_in, pool_w_grp, pool_b_grp, pool_scale, pool_w_out, loss_target, m_norm_g, m_final_g, m_conv_w_in, m_conv_dw, m_conv_dw_b, m_conv_ln_g, m_conv_ln_b, m_conv_w_out, m_pool_w_in, m_pool_w_grp, m_pool_b_grp, m_pool_scale, m_pool_w_out, v_norm_g, v_final_g, v_conv_w_in, v_conv_dw, v_conv_dw_b, v_conv_ln_g, v_conv_ln_b, v_conv_w_out, v_pool_w_in, v_pool_w_grp, v_pool_b_grp, v_pool_scale, v_pool_w_out):
    h0 = x[0]
    target = loss_target[0]
    ng3 = norm_g[:, None, :]
    row3 = lambda a: a[:, None, :]
    conv_vecs = (row3(conv_dw_b), row3(conv_ln_g), row3(conv_ln_b))
    gather = lambda arrays, axes: _Exchange("gather", arrays, axes)
    scatter = lambda arrays, axes: _Exchange("scatter", arrays, axes)

    cwi, cwo, pwi = conv_w_in.astype(BF16), conv_w_out.astype(BF16), pool_w_in.astype(BF16)
    pwg, pwo = pool_w_grp.astype(BF16), pool_w_out.astype(BF16)

    (cw_in0,) = _gather_via_sibling_call(gather([(cwi, 0)], [1]), "gather_first")
    p0, (cw_out0, dw_full, bg_full, sc_full, pw_in0) = _in_fwd(
        h0, ng3, 0, cw_in0, "conv_in_fwd_0",
        gather([(cwo, 0), conv_dw, pool_b_grp, pool_scale, (pwi, 0)], [0, 2, 1, 1, 1]))
    pool_vecs = (row3(bg_full), row3(sc_full))
    h1, uc0, (pw_grp0, pw_out0, cw_in1, cw_out1, pw_in1, pw_grp1, pw_out1) = _conv_fwd(
        p0, h0, dw_full, conv_vecs, 0, cw_out0, "conv_mix_fwd_0",
        gather([(pwg, 0), (pwo, 0), (cwi, 1), (cwo, 1), (pwi, 1), (pwg, 1), (pwo, 1)], [1, 0, 1, 0, 1, 1, 0]))
    h2, p1, _ = _pool_fwd(h1, ng3, 1, pw_in0, pw_grp0, pool_vecs, 0, pw_out0, "pool_fwd_0")
    p2, _ = _in_fwd(h2, ng3, 2, cw_in1, "conv_in_fwd_1")
    h3, uc2, _ = _conv_fwd(p2, h2, dw_full, conv_vecs, 1, cw_out1, "conv_mix_fwd_1")
    dh, p3, d_final_g, loss_part, _ = _pool_fwd(h3, ng3, 3, pw_in1, pw_grp1, pool_vecs, 1, pw_out1, "pool_fwd_1",
                                               final=(target, final_g[None, :]))

    dp, g_pwo1, g_pwg1, dpv1, _ = _pool_bwd(dh, p3, pw_grp1, pool_vecs, 1, pw_out1, "pool_mix_bwd_1")
    dh, g_pwi1, dg3, _ = _in_bwd(dp, h3, dh, ng3, 3, pw_in1, "pool_in_bwd_1")
    dp, g_cwo1, ddw1, dcv1, (s_pwo1, s_pwg1, s_pwi1) = _conv_bwd(
        dh, p2, uc2, dw_full, conv_vecs, 1, cw_out1, "conv_mix_bwd_1", scatter([g_pwo1, g_pwg1, g_pwi1], [0, 1, 1]))
    dh, g_cwi1, dg2, _ = _in_bwd(dp, h2, dh, ng3, 2, cw_in1, "conv_in_bwd_1")
    dp, g_pwo0, g_pwg0, dpv0, (s_cwo1, s_cwi1) = _pool_bwd(dh, p1, pw_grp0, pool_vecs, 0, pw_out0, "pool_mix_bwd_0",
                                                           scatter([g_cwo1, g_cwi1], [0, 1]))
    dh, g_pwi0, dg1, _ = _in_bwd(dp, h1, dh, ng3, 1, pw_in0, "pool_in_bwd_0")
    dp, g_cwo0, ddw0, dcv0, (s_pwo0, s_pwg0, s_pwi0, s_ddw1, s_dcv1, s_dpv0, s_dpv1, s_dg1, s_dg2, s_dg3, s_dfg,
                             s_loss) = _conv_bwd(
        dh, p0, uc0, dw_full, conv_vecs, 0, cw_out0, "conv_mix_bwd_0",
        scatter([g_pwo0, g_pwg0, g_pwi0, ddw1, dcv1, dpv0, dpv1, dg1, dg2, dg3, d_final_g, loss_part],
                [0, 1, 1, 1, None, 1, 1, None, None, None, None, None]))
    dh, g_cwi0, dg0, _ = _in_bwd(dp, h0, dh, ng3, 0, cw_in0, "conv_in_bwd_0")
    grad_x = dh[None]
    s_cwo0, s_cwi0, s_ddw0, s_dcv0, s_dg0 = _exchange_call(
        scatter([g_cwo0, g_cwi0, ddw0, dcv0, dg0], [0, 1, 1, None, None]), "scatter_last")

    res = {}
    res["conv_w_in"] = _adamw([s_cwi0, s_cwi1], conv_w_in, m_conv_w_in, v_conv_w_in, "adamw_conv_w_in")
    res["conv_w_out"] = _adamw([s_cwo0, s_cwo1], conv_w_out, m_conv_w_out, v_conv_w_out, "adamw_conv_w_out")
    res["pool_w_in"] = _adamw([s_pwi0, s_pwi1], pool_w_in, m_pool_w_in, v_pool_w_in, "adamw_pool_w_in")
    res["pool_w_grp"] = _adamw([s_pwg0, s_pwg1], pool_w_grp, m_pool_w_grp, v_pool_w_grp, "adamw_pool_w_grp")
    res["pool_w_out"] = _adamw([s_pwo0, s_pwo1], pool_w_out, m_pool_w_out, v_pool_w_out, "adamw_pool_w_out")
    stacks = [s_ddw0, s_ddw1, s_dpv0, s_dpv1, s_dg0, s_dg1, s_dg2, s_dg3, s_dfg, s_dcv0, s_dcv1]
    small = [
        ("conv_dw", conv_dw, m_conv_dw, v_conv_dw, [(0, None, 0), (1, None, 1)]),
        ("pool_b_grp", pool_b_grp, m_pool_b_grp, v_pool_b_grp, [(2, 0, 0), (3, 0, 1)]),
        ("pool_scale", pool_scale, m_pool_scale, v_pool_scale, [(2, 1, 0), (3, 1, 1)]),
        ("norm_g", norm_g, m_norm_g, v_norm_g, [(4, 0, 0), (5, 0, 1), (6, 0, 2), (7, 0, 3)]),
        ("final_g", final_g[None, :], m_final_g[None, :], v_final_g[None, :], [(8, 0, 0)]),
        ("conv_dw_b", conv_dw_b, m_conv_dw_b, v_conv_dw_b, [(9, 0, 0), (10, 0, 1)]),
        ("conv_ln_g", conv_ln_g, m_conv_ln_g, v_conv_ln_g, [(9, 1, 0), (10, 1, 1)]),
        ("conv_ln_b", conv_ln_b, m_conv_ln_b, v_conv_ln_b, [(9, 2, 0), (10, 2, 1)]),
    ]
    small_res, loss_block = _adamw_small([s[1:] for s in small], stacks, s_loss, "adamw_small")
    for (name, *_), r in zip(small, small_res):
        res[name] = [a[0] for a in r] if name == "final_g" else r
    loss = loss_block[0, 0]

    names = ["norm_g", "final_g", "conv_w_in", "conv_dw", "conv_dw_b", "conv_ln_g", "conv_ln_b", "conv_w_out",
             "pool_w_in", "pool_w_grp", "pool_b_grp", "pool_scale", "pool_w_out"]
    return (loss, grad_x) + tuple(res[n][q] for q in range(4) for n in names)
```

```python
import jax
import jax.numpy as jnp
from jax import lax
from jax.experimental import pallas as pl
from jax.experimental.pallas import tpu as pltpu

F32 = jnp.float32
BF16 = jnp.bfloat16

RMS_EPS = 1e-6
LN_EPS = 1e-5
CONV_TAPS = 31
CONV_HALO = 32
POOL_WINDOWS = (2, 4, 8, 16)
POOL_HALO = 16
SUBLANES = 8
LANES = 128
N_DEV = 8
V7X_VMEM_LIMIT = 56 * 1024 * 1024

ADAM_LR = 0.001
ADAM_B1 = 0.9
ADAM_B2 = 0.999
ADAM_EPS = 1e-08
ADAM_WD = 0.01
ADAM_STEP = 10

MESH = pl.DeviceIdType.MESH
ANY = pl.BlockSpec(memory_space=pl.ANY)


def _dot(a, b):
    return lax.dot_general(a, b, (((1,), (0,)), ((), ())), preferred_element_type=F32)


def _dot_nt(a, b):
    return lax.dot_general(a, b, (((1,), (1,)), ((), ())), preferred_element_type=F32)


def _dot_tn(a, b):
    return lax.dot_general(a, b, (((0,), (0,)), ((), ())), preferred_element_type=F32)


def _sigmoid(x):
    return jax.nn.sigmoid(x)


def _dsilu(x, s):
    return s * (1.0 + x * (1.0 - s))


def _rows8(x):
    r, c = x.shape
    return jnp.sum(x.reshape(r // SUBLANES, SUBLANES, c), axis=0)


def _tile(t, pref):
    return pref if t >= 2 * pref else t // 2


def _const(shape, index):
    return pl.BlockSpec(shape, lambda *_: index, pipeline_mode=pl.Buffered(1))


def _params(grid_rank=1):
    return pltpu.CompilerParams(dimension_semantics=("arbitrary",) * grid_rank, vmem_limit_bytes=V7X_VMEM_LIMIT)


def _chunks(n, rc, fn):
    def step(c, carry):
        fn(pl.multiple_of(c * rc, rc))
        return carry
    lax.fori_loop(0, n, step, 0)


def _mesh_position():
    return lax.axis_index("x"), lax.axis_index("y"), lax.axis_index("c")


def _peer(j):
    x, y, c = _mesh_position()
    px = 1 - x if j & 4 else x
    py = 1 - y if j & 2 else y
    pc = 1 - c if j & 1 else c
    return (px, py, pc), 4 * px + 2 * py + pc


def _block(ref, axis, index, size):
    idx = [slice(None)] * len(ref.shape)
    idx[axis] = pl.ds(pl.multiple_of(index * size, size), size)
    return ref.at[tuple(idx)]


class _Exchange:
    def __init__(self, kind, arrays, axes):
        self.kind, self.axes = kind, list(axes)
        self.arrays = [a[0] if isinstance(a, tuple) else a for a in arrays]
        self.layers = [a[1] if isinstance(a, tuple) else None for a in arrays]
        self.n = len(self.arrays)
        self.blk, self.out_shapes = [], []
        for a, layer, ax in zip(self.arrays, self.layers, self.axes):
            s = list(a.shape if layer is None else a.shape[1:])
            if kind == "gather":
                self.blk.append(s[ax])
                s[ax] *= N_DEV
                self.out_shapes.append(jax.ShapeDtypeStruct(tuple(s), a.dtype))
            else:
                if ax is not None:
                    s[ax] //= N_DEV
                    self.blk.append(s[ax])
                else:
                    self.blk.append(None)
                self.out_shapes.append(jax.ShapeDtypeStruct((N_DEV,) + tuple(s), a.dtype))

    def sem_shapes(self):
        return [pltpu.SemaphoreType.DMA((N_DEV - 1, self.n)), pltpu.SemaphoreType.DMA((N_DEV - 1, self.n)),
                pltpu.SemaphoreType.DMA((self.n,))]

    def _src(self, ins, k, owner):
        ref = ins[k] if self.layers[k] is None else ins[k].at[self.layers[k]]
        if self.kind == "gather" or self.axes[k] is None:
            return ref
        return _block(ref, self.axes[k], owner, self.blk[k])

    def _dst(self, outs, k, sender):
        if self.kind == "gather":
            return _block(outs[k], self.axes[k], sender, self.blk[k])
        return outs[k].at[sender]

    def _copies(self, ins, outs, sems, arriving):
        send, recv, loc = sems
        x, y, c = _mesh_position()
        me = 4 * x + 2 * y + c
        if not arriving:
            local = [pltpu.make_async_copy(self._src(ins, k, me), self._dst(outs, k, me), loc.at[k])
                     for k in range(self.n)]
        else:
            local = []
        remote = []
        for j in range(1, N_DEV):
            peer, peer_id = _peer(j)
            for k in range(self.n):
                owner, sender = (me, peer_id) if arriving else (peer_id, me)
                remote.append(pltpu.make_async_remote_copy(
                    src_ref=self._src(ins, k, owner), dst_ref=self._dst(outs, k, sender),
                    send_sem=send.at[j - 1, k], recv_sem=recv.at[j - 1, k], device_id=peer, device_id_type=MESH))
        return local, remote

    def start(self, ins, outs, sems):
        local, sends = self._copies(ins, outs, sems, arriving=False)
        for cp in local + sends:
            cp.start()

    def finish(self, ins, outs, sems):
        for cp in self._copies(ins, outs, sems, arriving=True)[1]:
            cp.wait_recv()
        local, sends = self._copies(ins, outs, sems, arriving=False)
        for cp in sends:
            cp.wait_send()
        for cp in local:
            cp.wait()


def _exchange_call(ex, name):
    def body(*refs):
        ins, outs, sems = refs[:ex.n], refs[ex.n:2 * ex.n], refs[2 * ex.n:]
        ex.start(ins, outs, sems)
        ex.finish(ins, outs, sems)

    return pl.pallas_call(body, name=name, in_specs=[ANY] * ex.n, out_specs=[ANY] * ex.n,
                          out_shape=ex.out_shapes, scratch_shapes=ex.sem_shapes())(*ex.arrays)


def _gather_via_sibling_call(ex, name):
    n = ex.n

    def body(*refs):
        ins, outs = refs[:n], refs[n:2 * n]
        send, recv, loc = refs[2 * n:]
        x, y, c = _mesh_position()
        ident = lambda px, py, pc: 4 * px + 2 * py + pc
        me, sibling = ident(x, y, c), (x, y, 1 - c)
        chips = [(1 - x, y), (x, 1 - y), (1 - x, 1 - y)]

        def copy(row, k, block, to, src=None):
            place = ex._dst(outs, k, block)
            return pltpu.make_async_remote_copy(
                src_ref=place if src is None else src, dst_ref=place,
                send_sem=send.at[row, k], recv_sem=recv.at[row, k], device_id=to, device_id_type=MESH)

        local = [pltpu.make_async_copy(ex._src(ins, k, me), ex._dst(outs, k, me), loc.at[k]) for k in range(n)]
        first = []
        for k in range(n):
            mine = ex._src(ins, k, me)
            first.append(copy(0, k, me, sibling, src=mine))
            first += [copy(1 + j, k, me, (*chip, c), src=mine) for j, chip in enumerate(chips)]
        for cp in local + first:
            cp.start()
        passed = []
        for j, chip in enumerate(chips):
            for k in range(n):
                copy(1 + j, k, ident(*chip, c), sibling).wait_recv()
                passed.append(copy(4 + j, k, ident(*chip, c), sibling))
                passed[-1].start()
        for k in range(n):
            copy(0, k, ident(x, y, 1 - c), sibling).wait_recv()
            for j, chip in enumerate(chips):
                copy(4 + j, k, ident(*chip, 1 - c), sibling).wait_recv()
        for cp in first + passed:
            cp.wait_send()
        for cp in local:
            cp.wait()

    return pl.pallas_call(body, name=name, in_specs=[ANY] * n, out_specs=[ANY] * n, out_shape=ex.out_shapes,
                          scratch_shapes=ex.sem_shapes())(*ex.arrays)


def _launch(body, name, nt, in_specs, out_specs, out_shape, scratch_shapes, args, ex=None):
    if ex is None:
        outs = pl.pallas_call(body, name=name, grid=(nt,), in_specs=in_specs, out_specs=out_specs,
                              out_shape=out_shape, scratch_shapes=scratch_shapes, compiler_params=_params())(*args)
        return list(outs), []
    n_in, n_out, n_scr = len(in_specs), len(out_specs), len(scratch_shapes)

    def riding(*refs):
        a, xa = refs[:n_in], refs[n_in:n_in + ex.n]
        o = refs[n_in + ex.n:n_in + ex.n + n_out]
        xo = refs[n_in + ex.n + n_out:n_in + 2 * ex.n + n_out]
        s = refs[n_in + 2 * ex.n + n_out:n_in + 2 * ex.n + n_out + n_scr]
        sems = refs[n_in + 2 * ex.n + n_out + n_scr:]
        i = pl.program_id(0)

        @pl.when(i == 0)
        def _():
            ex.start(xa, xo, sems)

        body(*a, *o, *s)

        @pl.when(i == nt - 1)
        def _():
            ex.finish(xa, xo, sems)

    outs = pl.pallas_call(
        riding, name=name, grid=(nt,),
        in_specs=list(in_specs) + [ANY] * ex.n, out_specs=list(out_specs) + [ANY] * ex.n,
        out_shape=list(out_shape) + ex.out_shapes, scratch_shapes=list(scratch_shapes) + ex.sem_shapes(),
        compiler_params=_params())(*args, *ex.arrays)
    return list(outs[:n_out]), list(outs[n_out:])


def _in_fwd(h, norm_g, layer, w_in, name, ex=None):
    t, d = h.shape
    n = w_in.shape[-1]
    tm = _tile(t, 512)

    def body(h_ref, g_ref, w_ref, p_ref):
        x = h_ref[...]
        r = lax.rsqrt(jnp.mean(x * x, axis=-1, keepdims=True) + RMS_EPS)
        p_ref[...] = _dot((x * r * g_ref[...]).astype(BF16), w_ref[...])

    (p,), xouts = _launch(
        body, name, t // tm,
        [pl.BlockSpec((tm, d), lambda i: (i, 0)), _const((None, 1, d), (layer, 0, 0)), _const((d, n), (0, 0))],
        [pl.BlockSpec((tm, n), lambda i: (i, 0))],
        [jax.ShapeDtypeStruct((t, n), F32)], [], (h, norm_g, w_in), ex)
    return p, xouts


def _layernorm_rows(uc, lng, lnb):
    mu = jnp.mean(uc, axis=-1, keepdims=True)
    xc = uc - mu
    rstd = lax.rsqrt(jnp.mean(xc * xc, axis=-1, keepdims=True) + LN_EPS)
    xhat = xc * rstd
    return xhat, rstd, xhat * lng + lnb


def _conv_fwd(p, h, dw, vecs, layer, w_out, name, ex=None):
    t, d = h.shape
    e = w_out.shape[0]
    tm = _tile(t, 256)
    rc = _tile(tm, 128)

    def body(p_ref, h_ref, dw_ref, dwb_ref, lng_ref, lnb_ref, wo_ref, ho_ref, xr_ref, sg_ref, ub_ref, us_scr, uc_ref):
        i = pl.program_id(0)

        @pl.when(i == 0)
        def _():
            us_scr[:, pl.ds(0, CONV_HALO), :] = jnp.zeros((SUBLANES, CONV_HALO, e), F32)

        @pl.when(i > 0)
        def _():
            us_scr[:, pl.ds(0, CONV_HALO), :] = us_scr[:, pl.ds(tm, CONV_HALO), :]

        sb = _sigmoid(p_ref[:, pl.ds(e, e)])
        u = p_ref[:, pl.ds(0, e)] * sb
        ub_ref[:, pl.ds(0, e)] = u
        ub_ref[:, pl.ds(e, e)] = sb
        us_scr[0, pl.ds(CONV_HALO, tm), :] = u
        for r in range(1, SUBLANES):
            us_scr[r, pl.ds(CONV_HALO, tm), :] = us_scr[0, pl.ds(CONV_HALO - r, tm), :]

        def c_conv(base):
            for lt in range(e // LANES):
                cols = pl.ds(lt * LANES, LANES)
                acc = jnp.broadcast_to(dwb_ref[:, cols], (rc, LANES))
                for r in range(SUBLANES):
                    nq = (CONV_TAPS - 1 - r) // SUBLANES + 1
                    lo = SUBLANES * (nq - 1)
                    win = us_scr[r, pl.ds(pl.multiple_of(CONV_HALO + base - lo, SUBLANES), rc + lo), cols]
                    for q in range(nq):
                        k = CONV_TAPS - 1 - (SUBLANES * q + r)
                        at = lo - SUBLANES * q
                        acc = acc + dw_ref[pl.ds(k, 1), cols] * win[at:at + rc, :]
                uc_ref[pl.ds(base, rc), cols] = acc
        _chunks(tm // rc, rc, c_conv)

        xhat, rstd, ul = _layernorm_rows(uc_ref[...], lng_ref[...], lnb_ref[...])
        z = p_ref[:, pl.ds(2 * e, e)]
        sg_u = _sigmoid(ul)
        sg_z = _sigmoid(z)
        xr_ref[:, pl.ds(0, e)] = xhat
        xr_ref[:, pl.ds(e, LANES)] = jnp.broadcast_to(rstd, (tm, LANES))
        sg_ref[:, pl.ds(0, e)] = sg_u
        sg_ref[:, pl.ds(e, e)] = sg_z
        v = ((ul * sg_u) * (z * sg_z)).astype(BF16)
        ho_ref[...] = h_ref[...] + _dot(v, wo_ref[...])

    vec = _const((None, 1, e), (layer, 0, 0))
    row = lambda i: (i, 0)
    outs, xouts = _launch(
        body, name, t // tm,
        [pl.BlockSpec((tm, 3 * e), row), pl.BlockSpec((tm, d), row),
         _const((None, CONV_TAPS, e), (layer, 0, 0)), vec, vec, vec, _const((e, d), (0, 0))],
        [pl.BlockSpec((tm, d), row), pl.BlockSpec((tm, e + LANES), row), pl.BlockSpec((tm, 2 * e), row),
         pl.BlockSpec((tm, 2 * e), row)],
        [jax.ShapeDtypeStruct((t, d), F32), jax.ShapeDtypeStruct((t, e + LANES), F32),
         jax.ShapeDtypeStruct((t, 2 * e), F32), jax.ShapeDtypeStruct((t, 2 * e), F32)],
        [pltpu.VMEM((SUBLANES, tm + CONV_HALO, e), F32), pltpu.VMEM((tm, e), F32)], (p, h, dw, *vecs, w_out), ex)
    return outs[0], tuple(outs[1:]), xouts


def _conv_bwd(dho, p, saved, dw, vecs, layer, w_out, name, ex=None):
    t, d = dho.shape
    e = w_out.shape[0]
    tm = _tile(t, 256)
    nt = t // tm
    rc = 16
    vec0 = CONV_TAPS + 1

    def body(dho_ref, z_ref, xr_ref, sg_ref, ub_ref, dw_ref, lng_ref, lnb_ref, wo_ref,
             dp_ref, dwo_ref, ddw_ref, dvec_ref, ds_scr, acc_scr, dwo_scr):
        i = pl.program_id(0)

        @pl.when(i == 0)
        def _():
            dwo_scr[...] = jnp.zeros_like(dwo_scr)
            acc_scr[...] = jnp.zeros_like(acc_scr)
            ds_scr[:, pl.ds(tm, CONV_HALO), :] = jnp.zeros((SUBLANES, CONV_HALO, e), F32)

        @pl.when(i > 0)
        def _():
            ds_scr[:, pl.ds(tm, CONV_HALO), :] = ds_scr[:, pl.ds(0, CONV_HALO), :]

        lng = lng_ref[...]
        lnb = lnb_ref[...]

        xhat = xr_ref[:, pl.ds(0, e)]
        rstd = xr_ref[:, pl.ds(e, 1)]
        ul = xhat * lng + lnb
        z = z_ref[...]
        sg_u = sg_ref[:, pl.ds(0, e)]
        sg_z = sg_ref[:, pl.ds(e, e)]
        s_u = ul * sg_u
        s_z = z * sg_z
        v = (s_u * s_z).astype(BF16)
        dy = dho_ref[...].astype(BF16)
        dv = _dot_nt(dy, wo_ref[...])
        dwo_scr[...] += _dot_tn(v, dy)

        dul = dv * s_z * _dsilu(ul, sg_u)
        dp_ref[:, pl.ds(2 * e, e)] = (dv * s_u * _dsilu(z, sg_z)).astype(BF16)
        acc_scr[pl.ds((vec0 + 1) * SUBLANES, SUBLANES), :] += _rows8(dul * xhat)
        acc_scr[pl.ds((vec0 + 2) * SUBLANES, SUBLANES), :] += _rows8(dul)
        dxh = dul * lng
        duc = rstd * (dxh - jnp.mean(dxh, axis=-1, keepdims=True)
                      - xhat * jnp.mean(dxh * xhat, axis=-1, keepdims=True))
        acc_scr[pl.ds(vec0 * SUBLANES, SUBLANES), :] += _rows8(duc)
        ds_scr[0, pl.ds(0, tm), :] = duc
        for r in range(1, SUBLANES):
            ds_scr[r, pl.ds(0, tm), :] = ds_scr[0, pl.ds(r, tm), :]

        def c_conv(base):
            rows = pl.ds(base, rc)
            u = ub_ref[rows, pl.ds(0, e)]
            sb = ub_ref[rows, pl.ds(e, e)]
            du = jnp.zeros((rc, e), F32)
            for o in range(CONV_TAPS):
                q, r = divmod(o, SUBLANES)
                k = CONV_TAPS - 1 - o
                sh = ds_scr[r, pl.ds(pl.multiple_of(base + SUBLANES * q, SUBLANES), rc), :]
                du = du + dw_ref[pl.ds(k, 1), :] * sh
                acc_scr[pl.ds(k * SUBLANES, SUBLANES), :] += _rows8(u * sh)
            dp_ref[rows, pl.ds(0, e)] = (du * sb).astype(BF16)
            dp_ref[rows, pl.ds(e, e)] = (du * u * (1.0 - sb)).astype(BF16)
        _chunks(tm // rc, rc, c_conv)

        @pl.when(i == nt - 1)
        def _():
            dwo_ref[...] = dwo_scr[...].astype(BF16)
            slot_sum = lambda k: jnp.sum(acc_scr[pl.ds(k * SUBLANES, SUBLANES), :], axis=0, keepdims=True)
            for k in range(CONV_TAPS):
                ddw_ref[pl.ds(k, 1), :] = slot_sum(k)
            dvec_ref[...] = jnp.zeros_like(dvec_ref)
            for k in range(3):
                dvec_ref[pl.ds(k, 1), :] = slot_sum(vec0 + k)

    rev = lambda i: (nt - 1 - i, 0)
    vec = _const((None, 1, e), (layer, 0, 0))
    (dp, dwo, ddw, dvec), xouts = _launch(
        body, name, nt,
        [pl.BlockSpec((tm, d), rev), pl.BlockSpec((tm, e), lambda i: (nt - 1 - i, 2)),
         pl.BlockSpec((tm, e + LANES), rev), pl.BlockSpec((tm, 2 * e), rev), pl.BlockSpec((tm, 2 * e), rev),
         _const((None, CONV_TAPS, e), (layer, 0, 0)), vec, vec, _const((e, d), (0, 0))],
        [pl.BlockSpec((tm, 3 * e), rev), _const((e, d), (0, 0)), _const((CONV_TAPS, e), (0, 0)),
         _const((SUBLANES, e), (0, 0))],
        [jax.ShapeDtypeStruct((t, 3 * e), BF16), jax.ShapeDtypeStruct((e, d), BF16),
         jax.ShapeDtypeStruct((CONV_TAPS, e), F32), jax.ShapeDtypeStruct((SUBLANES, e), F32)],
        [pltpu.VMEM((SUBLANES, tm + CONV_HALO, e), F32), pltpu.VMEM(((vec0 + 3) * SUBLANES, e), F32),
         pltpu.VMEM((e, d), F32)],
        (dho, p, *saved, dw, vecs[1], vecs[2], w_out), ex)
    return dp, dwo, ddw, dvec, xouts


def _inv_count(tile, tm, w):
    tpos = tile * tm + lax.broadcasted_iota(jnp.int32, (tm, 1), 0)
    return 1.0 / jnp.minimum(tpos + 1, w).astype(F32)


def _pool_d_group(ue_scr, tile, tm, gc, g):
    w = POOL_WINDOWS[g]
    win = ue_scr[:, pl.ds(g * gc, gc)]
    s = win
    sh = 1
    while sh < w:
        s = s + pltpu.roll(s, sh, axis=0)
        sh *= 2
    return s[POOL_HALO:, :] * _inv_count(tile, tm, w) - win[POOL_HALO:, :]


def _final_rows(ho, tg_ref, fg_ref, dh_ref, acc_scr, lacc_scr):
    d = ho.shape[-1]
    r = lax.rsqrt(jnp.mean(ho * ho, axis=-1, keepdims=True) + RMS_EPS)
    nrm = ho * r
    err = nrm * fg_ref[...] - tg_ref[...]
    lacc_scr[...] += _rows8(err * err)
    dy = err * (1.0 / d)
    acc_scr[...] += _rows8(dy * nrm)
    dq = dy * fg_ref[...]
    dh_ref[...] = r * (dq - nrm * jnp.mean(dq * nrm, axis=-1, keepdims=True))


def _pool_fwd(h, norm_g, nlayer, w_in, w_grp, vecs, layer, w_out, name, ex=None, final=None):
    t, d = h.shape
    e = w_out.shape[0]
    ng = len(POOL_WINDOWS)
    gc = e // ng
    tm = _tile(t, 512)
    nt = t // tm
    cw = 2 * e // (2 * ng)

    def layer_rows(i, hn_ref, hc_ref, g_ref, wi_ref, wg_ref, bg_ref, sc_ref, wo_ref, p_ref, d_ref,
                   ue_scr, y_scr, pbuf, z_scr, hn_scr):
        tile = jnp.maximum(i - 1, 0)

        @pl.when(i == 0)
        def _():
            pbuf[...] = jnp.zeros_like(pbuf)

        @pl.when(i <= 1)
        def _():
            ue_scr[pl.ds(0, POOL_HALO), :] = jnp.zeros((POOL_HALO, e), F32)

        @pl.when(i > 1)
        def _():
            ue_scr[pl.ds(0, POOL_HALO), :] = ue_scr[pl.ds(tm, POOL_HALO), :]

        ue_scr[pl.ds(POOL_HALO, tm), :] = pbuf[:, pl.ds(0, e)]
        z_scr[...] = pbuf[:, pl.ds(e, e)]

        x = hn_ref[...]
        r = lax.rsqrt(jnp.mean(x * x, axis=-1, keepdims=True) + RMS_EPS)
        hn_scr[...] = (x * r * g_ref[...]).astype(BF16)

        def project(c):
            part = _dot(hn_scr[...], wi_ref[:, pl.ds(c * cw, cw)])
            pbuf[:, pl.ds(c * cw, cw)] = part
            p_ref[:, pl.ds(c * cw, cw)] = part

        for g in range(ng):
            project(2 * g)
            cols = pl.ds(g * gc, gc)
            dg = _pool_d_group(ue_scr, tile, tm, gc, g).astype(BF16)
            d_ref[:, cols] = dg
            z = z_scr[:, cols]
            y1 = (_dot(dg, wg_ref[g]) + bg_ref[:, cols]) * sc_ref[:, cols]
            y_scr[:, cols] = (y1 * (z * _sigmoid(z))).astype(BF16)
            project(2 * g + 1)

        return hc_ref[...] + _dot(y_scr[...], wo_ref[...])

    nxt = lambda i: (jnp.minimum(i, nt - 1), 0)
    cur = lambda i: (jnp.maximum(i - 1, 0), 0)
    vec = _const((None, 1, e), (layer, 0, 0))
    in_specs = [pl.BlockSpec((tm, d), nxt), pl.BlockSpec((tm, d), cur), _const((None, 1, d), (nlayer, 0, 0)),
                _const((d, 2 * e), (0, 0)), _const((ng, gc, gc), (0, 0, 0)), vec, vec, _const((e, d), (0, 0))]
    scratch = [pltpu.VMEM((tm + POOL_HALO, e), F32), pltpu.VMEM((tm, e), BF16), pltpu.VMEM((tm, 2 * e), F32),
               pltpu.VMEM((tm, e), F32), pltpu.VMEM((tm, d), BF16)]
    args = (h, h, norm_g, w_in, w_grp, *vecs, w_out)

    if final is None:
        def body(hn_ref, hc_ref, g_ref, wi_ref, wg_ref, bg_ref, sc_ref, wo_ref, ho_ref, p_ref, d_ref, *scr):
            ho_ref[...] = layer_rows(pl.program_id(0), hn_ref, hc_ref, g_ref, wi_ref, wg_ref, bg_ref, sc_ref, wo_ref,
                                     p_ref, d_ref, *scr)

        (ho, p, dsaved), xouts = _launch(
            body, name, nt + 1, in_specs,
            [pl.BlockSpec((tm, d), cur), pl.BlockSpec((tm, 2 * e), nxt), pl.BlockSpec((tm, e), cur)],
            [jax.ShapeDtypeStruct((t, d), F32), jax.ShapeDtypeStruct((t, 2 * e), F32),
             jax.ShapeDtypeStruct((t, e), BF16)], scratch, args, ex)
        return ho, (p, dsaved), xouts

    target, final_g = final

    def body(hn_ref, hc_ref, g_ref, wi_ref, wg_ref, bg_ref, sc_ref, wo_ref, tg_ref, fg_ref,
             dh_ref, p_ref, d_ref, dfg_ref, loss_ref, ue_scr, y_scr, pbuf, z_scr, hn_scr, acc_scr, lacc_scr):
        i = pl.program_id(0)

        @pl.when(i <= 1)
        def _():
            acc_scr[...] = jnp.zeros_like(acc_scr)
            lacc_scr[...] = jnp.zeros_like(lacc_scr)

        ho = layer_rows(i, hn_ref, hc_ref, g_ref, wi_ref, wg_ref, bg_ref, sc_ref, wo_ref, p_ref, d_ref,
                        ue_scr, y_scr, pbuf, z_scr, hn_scr)
        _final_rows(ho, tg_ref, fg_ref, dh_ref, acc_scr, lacc_scr)

        @pl.when(i == nt)
        def _():
            dfg_ref[...] = jnp.zeros_like(dfg_ref)
            dfg_ref[pl.ds(0, 1), :] = jnp.sum(acc_scr[...], axis=0, keepdims=True)
            loss_ref[...] = jnp.broadcast_to(jnp.sum(lacc_scr[...]) * (0.5 / d), loss_ref.shape)

    (dh, p, dsaved, dfg, loss), xouts = _launch(
        body, name, nt + 1, in_specs + [pl.BlockSpec((tm, d), cur), _const((1, d), (0, 0))],
        [pl.BlockSpec((tm, d), cur), pl.BlockSpec((tm, 2 * e), nxt), pl.BlockSpec((tm, e), cur),
         _const((SUBLANES, d), (0, 0)), _const((SUBLANES, LANES), (0, 0))],
        [jax.ShapeDtypeStruct((t, d), F32), jax.ShapeDtypeStruct((t, 2 * e), F32), jax.ShapeDtypeStruct((t, e), BF16),
         jax.ShapeDtypeStruct((SUBLANES, d), F32), jax.ShapeDtypeStruct((SUBLANES, LANES), F32)],
        scratch + [pltpu.VMEM((SUBLANES, d), F32), pltpu.VMEM((SUBLANES, d), F32)],
        args + (target, final_g), ex)
    return dh, (p, dsaved), dfg, loss, xouts


def _pool_bwd(dho, saved, w_grp, vecs, layer, w_out, name, ex=None):
    t, d = dho.shape
    e = w_out.shape[0]
    ng = len(POOL_WINDOWS)
    gc = e // ng
    tm = _tile(t, 512)
    nt = t // tm

    def body(dho_ref, z_ref, d_ref, wg_ref, bg_ref, sc_ref, wo_ref, dp_ref, dwo_ref, dwg_ref, dvec_ref,
             ee_scr, acc_scr, dwo_scr, dwg_scr):
        i = pl.program_id(0)
        tile = nt - 1 - i

        @pl.when(i == 0)
        def _():
            dwo_scr[...] = jnp.zeros_like(dwo_scr)
            dwg_scr[...] = jnp.zeros_like(dwg_scr)
            acc_scr[...] = jnp.zeros_like(acc_scr)
            ee_scr[pl.ds(tm, POOL_HALO), :] = jnp.zeros((POOL_HALO, e), F32)

        @pl.when(i > 0)
        def _():
            ee_scr[pl.ds(tm, POOL_HALO), :] = ee_scr[pl.ds(0, POOL_HALO), :]

        bg = bg_ref[...]
        sc = sc_ref[...]
        ds = [d_ref[:, pl.ds(g * gc, gc)] for g in range(ng)]
        ob = jnp.concatenate([_dot(ds[g], wg_ref[g]) for g in range(ng)], axis=1) + bg
        z = z_ref[...]
        sg_z = _sigmoid(z)
        s_z = z * sg_z
        y1 = ob * sc
        dy = dho_ref[...].astype(BF16)
        dy2 = _dot_nt(dy, wo_ref[...])
        dwo_scr[...] += _dot_tn((y1 * s_z).astype(BF16), dy)
        dy1 = dy2 * s_z
        dp_ref[:, pl.ds(e, e)] = (dy2 * y1 * _dsilu(z, sg_z)).astype(BF16)
        acc_scr[pl.ds(SUBLANES, SUBLANES), :] += _rows8(dy1 * ob)
        do = dy1 * sc
        acc_scr[pl.ds(0, SUBLANES), :] += _rows8(do)

        n = tm + POOL_HALO
        for g, w in enumerate(POOL_WINDOWS):
            cols = pl.ds(g * gc, gc)
            do_g = do[:, g * gc:(g + 1) * gc].astype(BF16)
            dwg_scr[g] += _dot_tn(ds[g], do_g)
            dd = _dot_nt(do_g, wg_ref[g])
            ee_scr[pl.ds(0, tm), cols] = dd * _inv_count(tile, tm, w)
            s = ee_scr[:, cols]
            sh = 1
            while sh < w:
                s = s + pltpu.roll(s, n - sh, axis=0)
                sh *= 2
            dp_ref[:, cols] = (s[:tm, :] - dd).astype(BF16)

        @pl.when(i == nt - 1)
        def _():
            dwo_ref[...] = dwo_scr[...].astype(BF16)
            dwg_ref[...] = dwg_scr[...].astype(BF16)
            dvec_ref[...] = jnp.zeros_like(dvec_ref)
            for k in range(2):
                dvec_ref[pl.ds(k, 1), :] = jnp.sum(acc_scr[pl.ds(k * SUBLANES, SUBLANES), :], axis=0, keepdims=True)

    rev = lambda i: (nt - 1 - i, 0)
    vec = _const((None, 1, e), (layer, 0, 0))
    (dp, dwo, dwg, dvec), xouts = _launch(
        body, name, nt,
        [pl.BlockSpec((tm, d), rev), pl.BlockSpec((tm, e), lambda i: (nt - 1 - i, 1)), pl.BlockSpec((tm, e), rev),
         _const((ng, gc, gc), (0, 0, 0)), vec, vec, _const((e, d), (0, 0))],
        [pl.BlockSpec((tm, 2 * e), rev), _const((e, d), (0, 0)), _const((ng, gc, gc), (0, 0, 0)),
         _const((SUBLANES, e), (0, 0))],
        [jax.ShapeDtypeStruct((t, 2 * e), BF16), jax.ShapeDtypeStruct((e, d), BF16),
         jax.ShapeDtypeStruct((ng, gc, gc), BF16), jax.ShapeDtypeStruct((SUBLANES, e), F32)],
        [pltpu.VMEM((tm + POOL_HALO, e), F32),
         pltpu.VMEM((2 * SUBLANES, e), F32), pltpu.VMEM((e, d), F32), pltpu.VMEM((ng, gc, gc), F32)],
        (dho, *saved, w_grp, *vecs, w_out), ex)
    return dp, dwo, dwg, dvec, xouts


def _in_bwd(dp, h, dho, norm_g, layer, w_in, name, ex=None):
    t, d = h.shape
    n = w_in.shape[-1]
    tm = _tile(t, 512)
    nt = t // tm

    def body(dp_ref, h_ref, dho_ref, g_ref, w_ref, dh_ref, dw_ref, dg_ref, acc_scr, dw_scr):
        i = pl.program_id(0)

        @pl.when(i == 0)
        def _():
            dw_scr[...] = jnp.zeros_like(dw_scr)
            acc_scr[...] = jnp.zeros_like(acc_scr)

        x = h_ref[...]
        r = lax.rsqrt(jnp.mean(x * x, axis=-1, keepdims=True) + RMS_EPS)
        nrm = x * r
        dp = dp_ref[...]
        dhn = _dot_nt(dp, w_ref[...])
        dw_scr[...] += _dot_tn((nrm * g_ref[...]).astype(BF16), dp)
        acc_scr[...] += _rows8(dhn * nrm)
        dq = dhn * g_ref[...]
        dh_ref[...] = dho_ref[...] + r * (dq - nrm * jnp.mean(dq * nrm, axis=-1, keepdims=True))

        @pl.when(i == nt - 1)
        def _():
            dw_ref[...] = dw_scr[...].astype(BF16)
            dg_ref[...] = jnp.zeros_like(dg_ref)
            dg_ref[pl.ds(0, 1), :] = jnp.sum(acc_scr[...], axis=0, keepdims=True)

    (dh, dw, dg), xouts = _launch(
        body, name, nt,
        [pl.BlockSpec((tm, n), lambda i: (i, 0)), pl.BlockSpec((tm, d), lambda i: (i, 0)),
         pl.BlockSpec((tm, d), lambda i: (i, 0)), _const((None, 1, d), (layer, 0, 0)), _const((d, n), (0, 0))],
        [pl.BlockSpec((tm, d), lambda i: (i, 0)), _const((d, n), (0, 0)), _const((SUBLANES, d), (0, 0))],
        [jax.ShapeDtypeStruct((t, d), F32), jax.ShapeDtypeStruct((d, n), BF16),
         jax.ShapeDtypeStruct((SUBLANES, d), F32)],
        [pltpu.VMEM((SUBLANES, d), F32), pltpu.VMEM((d, n), F32)],
        (dp, h, dho, norm_g, w_in), ex)
    return dh, dw, dg, xouts


def _in_bwd_dw(dp, h, norm_g, layer, name, ex=None):
    t, d = h.shape
    n = dp.shape[-1]
    tm = _tile(t, 512)
    nt = t // tm

    def body(dp_ref, h_ref, g_ref, dw_ref, dw_scr):
        i = pl.program_id(0)

        @pl.when(i == 0)
        def _():
            dw_scr[...] = jnp.zeros_like(dw_scr)

        x = h_ref[...]
        r = lax.rsqrt(jnp.mean(x * x, axis=-1, keepdims=True) + RMS_EPS)
        dw_scr[...] += _dot_tn((x * r * g_ref[...]).astype(BF16), dp_ref[...])

        @pl.when(i == nt - 1)
        def _():
            dw_ref[...] = dw_scr[...].astype(BF16)

    (dw,), xouts = _launch(
        body, name, nt,
        [pl.BlockSpec((tm, n), lambda i: (i, 0)), pl.BlockSpec((tm, d), lambda i: (i, 0)),
         _const((None, 1, d), (layer, 0, 0))],
        [_const((d, n), (0, 0))], [jax.ShapeDtypeStruct((d, n), BF16)], [pltpu.VMEM((d, n), F32)],
        (dp, h, norm_g), ex)
    return dw, xouts


def _in_bwd_dh(dp, h, dho, norm_g, layer, w_in, name, ex=None):
    t, d = h.shape
    n = w_in.shape[-1]
    tm = _tile(t, 512)
    nt = t // tm

    def body(dp_ref, h_ref, dho_ref, g_ref, w_ref, dh_ref, dg_ref, acc_scr):
        i = pl.program_id(0)

        @pl.when(i == 0)
        def _():
            acc_scr[...] = jnp.zeros_like(acc_scr)

        x = h_ref[...]
        r = lax.rsqrt(jnp.mean(x * x, axis=-1, keepdims=True) + RMS_EPS)
        nrm = x * r
        dhn = _dot_nt(dp_ref[...], w_ref[...])
        acc_scr[...] += _rows8(dhn * nrm)
        dq = dhn * g_ref[...]
        dh_ref[...] = dho_ref[...] + r * (dq - nrm * jnp.mean(dq * nrm, axis=-1, keepdims=True))

        @pl.when(i == nt - 1)
        def _():
            dg_ref[...] = jnp.zeros_like(dg_ref)
            dg_ref[pl.ds(0, 1), :] = jnp.sum(acc_scr[...], axis=0, keepdims=True)

    (dh, dg), xouts = _launch(
        body, name, nt,
        [pl.BlockSpec((tm, n), lambda i: (i, 0)), pl.BlockSpec((tm, d), lambda i: (i, 0)),
         pl.BlockSpec((tm, d), lambda i: (i, 0)), _const((None, 1, d), (layer, 0, 0)), _const((d, n), (0, 0))],
        [pl.BlockSpec((tm, d), lambda i: (i, 0)), _const((SUBLANES, d), (0, 0))],
        [jax.ShapeDtypeStruct((t, d), F32), jax.ShapeDtypeStruct((SUBLANES, d), F32)],
        [pltpu.VMEM((SUBLANES, d), F32)],
        (dp, h, dho, norm_g, w_in), ex)
    return dh, dg, xouts


def _adam_update(g, w, m, v):
    c1 = 1.0 / (1.0 - ADAM_B1 ** ADAM_STEP)
    c2 = 1.0 / (1.0 - ADAM_B2 ** ADAM_STEP)
    nm = ADAM_B1 * m + (1.0 - ADAM_B1) * g
    nv = ADAM_B2 * v + (1.0 - ADAM_B2) * (g * g)
    return -ADAM_LR * ((nm * c1) / (jnp.sqrt(nv * c2) + ADAM_EPS) + ADAM_WD * w), nm, nv


def _adamw_small(params, stacks, loss_stack, name):
    ns, npar = len(stacks), len(params)

    def body(*refs):
        st = refs[:ns]
        pr = refs[ns:ns + 3 * npar]
        ls_ref = refs[ns + 3 * npar]
        outs = refs[ns + 3 * npar + 1:ns + 7 * npar + 1]
        loss_ref = refs[ns + 7 * npar + 1]
        for q, (w, _, _, pieces) in enumerate(params):
            w_ref, m_ref, v_ref = pr[3 * q:3 * q + 3]
            g_ref, d_ref, nm_ref, nv_ref = outs[4 * q:4 * q + 4]
            for s, row, slab in pieces:
                if w.ndim == 3:
                    take = lambda k: st[s][k]
                    at = slab
                else:
                    take = lambda k: st[s][k, pl.ds(row, 1), :]
                    at = (pl.ds(slab, 1), slice(None))
                g = take(0)
                for k in range(1, N_DEV):
                    g = g + take(k)
                g_ref[at] = g
                d_ref[at], nm_ref[at], nv_ref[at] = _adam_update(g, w_ref[at], m_ref[at], v_ref[at])
        tot = ls_ref[0]
        for k in range(1, N_DEV):
            tot = tot + ls_ref[k]
        loss_ref[...] = tot

    flat = [a for (w, m, v, _) in params for a in (w, m, v)]
    out_shape = [jax.ShapeDtypeStruct(w.shape, F32) for (w, _, _, _) in params for _ in range(4)]
    whole = pl.BlockSpec(memory_space=pltpu.VMEM)
    outs = pl.pallas_call(
        body, name=name, in_specs=[whole] * (ns + 3 * npar + 1), out_specs=[whole] * (4 * npar + 1),
        out_shape=out_shape + [jax.ShapeDtypeStruct(loss_stack.shape[1:], F32)],
    )(*stacks, *flat, loss_stack)
    return [outs[4 * q:4 * q + 4] for q in range(npar)], outs[-1]


def _adamw(stacks, w, m, v, name):
    nl = len(stacks)
    shp = w.shape
    c = shp[-1]
    r = 1
    for s in shp[1:-1]:
        r *= s
    tr = r
    for cand in (512, 256, 128, 64, 32, 16):
        if r % cand == 0 and r > cand:
            tr = cand
            break
    nrb = r // tr

    def body(*refs):
        s_refs = refs[:nl]
        w_ref, m_ref, v_ref, g_ref, d_ref, nm_ref, nv_ref = refs[nl:]
        layer = pl.program_id(0)
        for l in range(nl):
            @pl.when(layer == l)
            def _(l=l):
                g = s_refs[l][0].astype(F32)
                for k in range(1, N_DEV):
                    g = g + s_refs[l][k].astype(F32)
                g_ref[...] = g
                d_ref[...], nm_ref[...], nv_ref[...] = _adam_update(g, w_ref[...], m_ref[...], v_ref[...])

    def stack_spec(l):
        return pl.BlockSpec((N_DEV, tr, c),
                            lambda j, i: (0, jnp.where(j == l, i, jnp.where(j < l, 0, nrb - 1)), 0))

    spec = pl.BlockSpec((None, tr, c), lambda j, i: (j, i, 0))
    outs = pl.pallas_call(
        body, name=name, grid=(nl, nrb),
        in_specs=[stack_spec(l) for l in range(nl)] + [spec, spec, spec],
        out_specs=[spec] * 4,
        out_shape=[jax.ShapeDtypeStruct((nl, r, c), F32)] * 4,
        compiler_params=_params(2),
    )(*[s.reshape(N_DEV, r, c) for s in stacks], w.reshape(nl, r, c), m.reshape(nl, r, c), v.reshape(nl, r, c))
    return [o.reshape(shp) for o in outs]


def kernel(x, norm_g, final_g, conv_w_in, conv_dw, conv_dw_b, conv_ln_g, conv_ln_b, conv_w_out, pool_w_in, pool_w_grp, pool_b_grp, pool_scale, pool_w_out, loss_target, m_norm_g, m_final_g, m_conv_w_in, m_conv_dw, m_conv_dw_b, m_conv_ln_g, m_conv_ln_b, m_conv_w_out, m_pool_w_in, m_pool_w_grp, m_pool_b_grp, m_pool_scale, m_pool_w_out, v_norm_g, v_final_g, v_conv_w_in, v_conv_dw, v_conv_dw_b, v_conv_ln_g, v_conv_ln_b, v_conv_w_out, v_pool_w_in, v_pool_w_grp, v_pool_b_grp, v_pool_scale, v_pool_w_out):
    h0 = x[0]
    target = loss_target[0]
    ng3 = norm_g[:, None, :]
    row3 = lambda a: a[:, None, :]
    conv_vecs = (row3(conv_dw_b), row3(conv_ln_g), row3(conv_ln_b))
    gather = lambda arrays, axes: _Exchange("gather", arrays, axes)
    scatter = lambda arrays, axes: _Exchange("scatter", arrays, axes)

    cwi, cwo, pwi = conv_w_in.astype(BF16), conv_w_out.astype(BF16), pool_w_in.astype(BF16)
    pwg, pwo = pool_w_grp.astype(BF16), pool_w_out.astype(BF16)

    (cw_in0,) = _gather_via_sibling_call(gather([(cwi, 0)], [1]), "gather_first")
    p0, (cw_out0, dw_full, bg_full, sc_full, pw_in0) = _in_fwd(
        h0, ng3, 0, cw_in0, "conv_in_fwd_0",
        gather([(cwo, 0), conv_dw, pool_b_grp, pool_scale, (pwi, 0)], [0, 2, 1, 1, 1]))
    pool_vecs = (row3(bg_full), row3(sc_full))
    h1, uc0, (pw_grp0, pw_out0, cw_in1, cw_out1, pw_in1, pw_grp1, pw_out1) = _conv_fwd(
        p0, h0, dw_full, conv_vecs, 0, cw_out0, "conv_mix_fwd_0",
        gather([(pwg, 0), (pwo, 0), (cwi, 1), (cwo, 1), (pwi, 1), (pwg, 1), (pwo, 1)], [1, 0, 1, 0, 1, 1, 0]))
    h2, p1, _ = _pool_fwd(h1, ng3, 1, pw_in0, pw_grp0, pool_vecs, 0, pw_out0, "pool_fwd_0")
    p2, _ = _in_fwd(h2, ng3, 2, cw_in1, "conv_in_fwd_1")
    h3, uc2, _ = _conv_fwd(p2, h2, dw_full, conv_vecs, 1, cw_out1, "conv_mix_fwd_1")
    dh, p3, d_final_g, loss_part, _ = _pool_fwd(h3, ng3, 3, pw_in1, pw_grp1, pool_vecs, 1, pw_out1, "pool_fwd_1",
                                               final=(target, final_g[None, :]))

    dp, g_pwo1, g_pwg1, dpv1, _ = _pool_bwd(dh, p3, pw_grp1, pool_vecs, 1, pw_out1, "pool_mix_bwd_1")
    dh, g_pwi1, dg3, _ = _in_bwd(dp, h3, dh, ng3, 3, pw_in1, "pool_in_bwd_1")
    dp, g_cwo1, ddw1, dcv1, (s_pwo1, s_pwg1, s_pwi1) = _conv_bwd(
        dh, p2, uc2, dw_full, conv_vecs, 1, cw_out1, "conv_mix_bwd_1", scatter([g_pwo1, g_pwg1, g_pwi1], [0, 1, 1]))
    dh, g_cwi1, dg2, _ = _in_bwd(dp, h2, dh, ng3, 2, cw_in1, "conv_in_bwd_1")
    dp, g_pwo0, g_pwg0, dpv0, (s_cwo1, s_cwi1) = _pool_bwd(dh, p1, pw_grp0, pool_vecs, 0, pw_out0, "pool_mix_bwd_0",
                                                           scatter([g_cwo1, g_cwi1], [0, 1]))
    dh, g_pwi0, dg1, _ = _in_bwd(dp, h1, dh, ng3, 1, pw_in0, "pool_in_bwd_0")
    dp, g_cwo0, ddw0, dcv0, (s_pwo0, s_pwg0, s_pwi0, s_ddw1, s_dcv1, s_dpv0, s_dpv1, s_dg1, s_dg2, s_dg3, s_dfg,
                             s_loss) = _conv_bwd(
        dh, p0, uc0, dw_full, conv_vecs, 0, cw_out0, "conv_mix_bwd_0",
        scatter([g_pwo0, g_pwg0, g_pwi0, ddw1, dcv1, dpv0, dpv1, dg1, dg2, dg3, d_final_g, loss_part],
                [0, 1, 1, 1, None, 1, 1, None, None, None, None, None]))
    g_cwi0, (s_cwo0, s_ddw0, s_dcv0) = _in_bwd_dw(dp, h0, ng3, 0, "conv_in_bwd_dw_0",
                                                  scatter([g_cwo0, ddw0, dcv0], [0, 1, None]))
    dh, dg0, (s_cwi0,) = _in_bwd_dh(dp, h0, dh, ng3, 0, cw_in0, "conv_in_bwd_dh_0", scatter([g_cwi0], [1]))
    grad_x = dh[None]
    (s_dg0,) = _exchange_call(scatter([dg0], [None]), "scatter_last")

    res = {}
    res["conv_w_in"] = _adamw([s_cwi0, s_cwi1], conv_w_in, m_conv_w_in, v_conv_w_in, "adamw_conv_w_in")
    res["conv_w_out"] = _adamw([s_cwo0, s_cwo1], conv_w_out, m_conv_w_out, v_conv_w_out, "adamw_conv_w_out")
    res["pool_w_in"] = _adamw([s_pwi0, s_pwi1], pool_w_in, m_pool_w_in, v_pool_w_in, "adamw_pool_w_in")
    res["pool_w_grp"] = _adamw([s_pwg0, s_pwg1], pool_w_grp, m_pool_w_grp, v_pool_w_grp, "adamw_pool_w_grp")
    res["pool_w_out"] = _adamw([s_pwo0, s_pwo1], pool_w_out, m_pool_w_out, v_pool_w_out, "adamw_pool_w_out")
    stacks = [s_ddw0, s_ddw1, s_dpv0, s_dpv1, s_dg0, s_dg1, s_dg2, s_dg3, s_dfg, s_dcv0, s_dcv1]
    small = [
        ("conv_dw", conv_dw, m_conv_dw, v_conv_dw, [(0, None, 0), (1, None, 1)]),
        ("pool_b_grp", pool_b_grp, m_pool_b_grp, v_pool_b_grp, [(2, 0, 0), (3, 0, 1)]),
        ("pool_scale", pool_scale, m_pool_scale, v_pool_scale, [(2, 1, 0), (3, 1, 1)]),
        ("norm_g", norm_g, m_norm_g, v_norm_g, [(4, 0, 0), (5, 0, 1), (6, 0, 2), (7, 0, 3)]),
        ("final_g", final_g[None, :], m_final_g[None, :], v_final_g[None, :], [(8, 0, 0)]),
        ("conv_dw_b", conv_dw_b, m_conv_dw_b, v_conv_dw_b, [(9, 0, 0), (10, 0, 1)]),
        ("conv_ln_g", conv_ln_g, m_conv_ln_g, v_conv_ln_g, [(9, 1, 0), (10, 1, 1)]),
        ("conv_ln_b", conv_ln_b, m_conv_ln_b, v_conv_ln_b, [(9, 2, 0), (10, 2, 1)]),
    ]
    small_res, loss_block = _adamw_small([s[1:] for s in small], stacks, s_loss, "adamw_small")
    for (name, *_), r in zip(small, small_res):
        res[name] = [a[0] for a in r] if name == "final_g" else r
    loss = loss_block[0, 0]

    names = ["norm_g", "final_g", "conv_w_in", "conv_dw", "conv_dw_b", "conv_ln_g", "conv_ln_b", "conv_w_out",
             "pool_w_in", "pool_w_grp", "pool_b_grp", "pool_scale", "pool_w_out"]
    return (loss, grad_x) + tuple(res[n][q] for q in range(4) for n in names)
```

```python
import jax
import jax.numpy as jnp
from jax import lax
from jax.experimental import pallas as pl
from jax.experimental.pallas import tpu as pltpu

F32 = jnp.float32
BF16 = jnp.bfloat16

RMS_EPS = 1e-6
LN_EPS = 1e-5
CONV_TAPS = 31
CONV_HALO = 32
POOL_WINDOWS = (2, 4, 8, 16)
POOL_HALO = 16
SUBLANES = 8
LANES = 128
N_DEV = 8
V7X_VMEM_LIMIT = 56 * 1024 * 1024

ADAM_LR = 0.001
ADAM_B1 = 0.9
ADAM_B2 = 0.999
ADAM_EPS = 1e-08
ADAM_WD = 0.01
ADAM_STEP = 10

MESH = pl.DeviceIdType.MESH
ANY = pl.BlockSpec(memory_space=pl.ANY)


def _dot(a, b):
    return lax.dot_general(a, b, (((1,), (0,)), ((), ())), preferred_element_type=F32)


def _dot_nt(a, b):
    return lax.dot_general(a, b, (((1,), (1,)), ((), ())), preferred_element_type=F32)


def _dot_tn(a, b):
    return lax.dot_general(a, b, (((0,), (0,)), ((), ())), preferred_element_type=F32)


def _sigmoid(x):
    return jax.nn.sigmoid(x)


def _dsilu(x, s):
    return s * (1.0 + x * (1.0 - s))


def _rows8(x):
    r, c = x.shape
    return jnp.sum(x.reshape(r // SUBLANES, SUBLANES, c), axis=0)


def _tile(t, pref):
    return pref if t >= 2 * pref else t // 2


def _const(shape, index):
    return pl.BlockSpec(shape, lambda *_: index, pipeline_mode=pl.Buffered(1))


def _params(grid_rank=1):
    return pltpu.CompilerParams(dimension_semantics=("arbitrary",) * grid_rank, vmem_limit_bytes=V7X_VMEM_LIMIT)


def _chunks(n, rc, fn):
    def step(c, carry):
        fn(pl.multiple_of(c * rc, rc))
        return carry
    lax.fori_loop(0, n, step, 0)


def _mesh_position():
    return lax.axis_index("x"), lax.axis_index("y"), lax.axis_index("c")


def _peer(j):
    x, y, c = _mesh_position()
    px = 1 - x if j & 4 else x
    py = 1 - y if j & 2 else y
    pc = 1 - c if j & 1 else c
    return (px, py, pc), 4 * px + 2 * py + pc


def _block(ref, axis, index, size):
    idx = [slice(None)] * len(ref.shape)
    idx[axis] = pl.ds(pl.multiple_of(index * size, size), size)
    return ref.at[tuple(idx)]


class _Exchange:
    def __init__(self, kind, arrays, axes):
        self.kind, self.axes = kind, list(axes)
        self.arrays = [a[0] if isinstance(a, tuple) else a for a in arrays]
        self.layers = [a[1] if isinstance(a, tuple) else None for a in arrays]
        self.n = len(self.arrays)
        self.blk, self.out_shapes = [], []
        for a, layer, ax in zip(self.arrays, self.layers, self.axes):
            s = list(a.shape if layer is None else a.shape[1:])
            if kind == "gather":
                self.blk.append(s[ax])
                s[ax] *= N_DEV
                self.out_shapes.append(jax.ShapeDtypeStruct(tuple(s), a.dtype))
            else:
                if ax is not None:
                    s[ax] //= N_DEV
                    self.blk.append(s[ax])
                else:
                    self.blk.append(None)
                self.out_shapes.append(jax.ShapeDtypeStruct((N_DEV,) + tuple(s), a.dtype))

    def sem_shapes(self):
        return [pltpu.SemaphoreType.DMA((N_DEV - 1, self.n)), pltpu.SemaphoreType.DMA((N_DEV - 1, self.n)),
                pltpu.SemaphoreType.DMA((self.n,))]

    def _src(self, ins, k, owner):
        ref = ins[k] if self.layers[k] is None else ins[k].at[self.layers[k]]
        if self.kind == "gather" or self.axes[k] is None:
            return ref
        return _block(ref, self.axes[k], owner, self.blk[k])

    def _dst(self, outs, k, sender):
        if self.kind == "gather":
            return _block(outs[k], self.axes[k], sender, self.blk[k])
        return outs[k].at[sender]

    def _copies(self, ins, outs, sems, arriving):
        send, recv, loc = sems
        x, y, c = _mesh_position()
        me = 4 * x + 2 * y + c
        if not arriving:
            local = [pltpu.make_async_copy(self._src(ins, k, me), self._dst(outs, k, me), loc.at[k])
                     for k in range(self.n)]
        else:
            local = []
        remote = []
        for j in range(1, N_DEV):
            peer, peer_id = _peer(j)
            for k in range(self.n):
                owner, sender = (me, peer_id) if arriving else (peer_id, me)
                remote.append(pltpu.make_async_remote_copy(
                    src_ref=self._src(ins, k, owner), dst_ref=self._dst(outs, k, sender),
                    send_sem=send.at[j - 1, k], recv_sem=recv.at[j - 1, k], device_id=peer, device_id_type=MESH))
        return local, remote

    def start(self, ins, outs, sems):
        local, sends = self._copies(ins, outs, sems, arriving=False)
        for cp in local + sends:
            cp.start()

    def finish(self, ins, outs, sems):
        for cp in self._copies(ins, outs, sems, arriving=True)[1]:
            cp.wait_recv()
        local, sends = self._copies(ins, outs, sems, arriving=False)
        for cp in sends:
            cp.wait_send()
        for cp in local:
            cp.wait()


def _exchange_call(ex, name):
    def body(*refs):
        ins, outs, sems = refs[:ex.n], refs[ex.n:2 * ex.n], refs[2 * ex.n:]
        ex.start(ins, outs, sems)
        ex.finish(ins, outs, sems)

    return pl.pallas_call(body, name=name, in_specs=[ANY] * ex.n, out_specs=[ANY] * ex.n,
                          out_shape=ex.out_shapes, scratch_shapes=ex.sem_shapes())(*ex.arrays)


def _gather_via_sibling_call(ex, name):
    n = ex.n

    def body(*refs):
        ins, outs = refs[:n], refs[n:2 * n]
        send, recv, loc = refs[2 * n:]
        x, y, c = _mesh_position()
        ident = lambda px, py, pc: 4 * px + 2 * py + pc
        me, sibling = ident(x, y, c), (x, y, 1 - c)
        chips = [(1 - x, y), (x, 1 - y), (1 - x, 1 - y)]

        def copy(row, k, block, to, src=None):
            place = ex._dst(outs, k, block)
            return pltpu.make_async_remote_copy(
                src_ref=place if src is None else src, dst_ref=place,
                send_sem=send.at[row, k], recv_sem=recv.at[row, k], device_id=to, device_id_type=MESH)

        local = [pltpu.make_async_copy(ex._src(ins, k, me), ex._dst(outs, k, me), loc.at[k]) for k in range(n)]
        first = []
        for k in range(n):
            mine = ex._src(ins, k, me)
            first.append(copy(0, k, me, sibling, src=mine))
            first += [copy(1 + j, k, me, (*chip, c), src=mine) for j, chip in enumerate(chips)]
        for cp in local + first:
            cp.start()
        passed = []
        for j, chip in enumerate(chips):
            for k in range(n):
                copy(1 + j, k, ident(*chip, c), sibling).wait_recv()
                passed.append(copy(4 + j, k, ident(*chip, c), sibling))
                passed[-1].start()
        for k in range(n):
            copy(0, k, ident(x, y, 1 - c), sibling).wait_recv()
            for j, chip in enumerate(chips):
                copy(4 + j, k, ident(*chip, 1 - c), sibling).wait_recv()
        for cp in first + passed:
            cp.wait_send()
        for cp in local:
            cp.wait()

    return pl.pallas_call(body, name=name, in_specs=[ANY] * n, out_specs=[ANY] * n, out_shape=ex.out_shapes,
                          scratch_shapes=ex.sem_shapes())(*ex.arrays)


def _launch(body, name, nt, in_specs, out_specs, out_shape, scratch_shapes, args, ex=None, aliases=None):
    if ex is None:
        outs = pl.pallas_call(body, name=name, grid=(nt,), in_specs=in_specs, out_specs=out_specs,
                              out_shape=out_shape, scratch_shapes=scratch_shapes, compiler_params=_params(),
                              input_output_aliases=aliases or {})(*args)
        return list(outs), []
    assert aliases is None
    n_in, n_out, n_scr = len(in_specs), len(out_specs), len(scratch_shapes)

    def riding(*refs):
        a, xa = refs[:n_in], refs[n_in:n_in + ex.n]
        o = refs[n_in + ex.n:n_in + ex.n + n_out]
        xo = refs[n_in + ex.n + n_out:n_in + 2 * ex.n + n_out]
        s = refs[n_in + 2 * ex.n + n_out:n_in + 2 * ex.n + n_out + n_scr]
        sems = refs[n_in + 2 * ex.n + n_out + n_scr:]
        i = pl.program_id(0)

        @pl.when(i == 0)
        def _():
            ex.start(xa, xo, sems)

        body(*a, *o, *s)

        @pl.when(i == nt - 1)
        def _():
            ex.finish(xa, xo, sems)

    outs = pl.pallas_call(
        riding, name=name, grid=(nt,),
        in_specs=list(in_specs) + [ANY] * ex.n, out_specs=list(out_specs) + [ANY] * ex.n,
        out_shape=list(out_shape) + ex.out_shapes, scratch_shapes=list(scratch_shapes) + ex.sem_shapes(),
        compiler_params=_params())(*args, *ex.arrays)
    return list(outs[:n_out]), list(outs[n_out:])


def _in_fwd(h, norm_g, layer, w_in, name, ex=None):
    t, d = h.shape
    n = w_in.shape[-1]
    tm = _tile(t, 512)

    def body(h_ref, g_ref, w_ref, p_ref):
        x = h_ref[...]
        r = lax.rsqrt(jnp.mean(x * x, axis=-1, keepdims=True) + RMS_EPS)
        p_ref[...] = _dot((x * r * g_ref[...]).astype(BF16), w_ref[...])

    (p,), xouts = _launch(
        body, name, t // tm,
        [pl.BlockSpec((tm, d), lambda i: (i, 0)), _const((None, 1, d), (layer, 0, 0)), _const((d, n), (0, 0))],
        [pl.BlockSpec((tm, n), lambda i: (i, 0))],
        [jax.ShapeDtypeStruct((t, n), F32)], [], (h, norm_g, w_in), ex)
    return p, xouts


def _layernorm_rows(uc, lng, lnb):
    mu = jnp.mean(uc, axis=-1, keepdims=True)
    xc = uc - mu
    rstd = lax.rsqrt(jnp.mean(xc * xc, axis=-1, keepdims=True) + LN_EPS)
    xhat = xc * rstd
    return xhat, rstd, xhat * lng + lnb


def _conv_fwd(p, h, dw, vecs, layer, w_out, name, ex=None):
    t, d = h.shape
    e = w_out.shape[0]
    tm = _tile(t, 256)
    rc = _tile(tm, 128)

    def body(p_ref, h_ref, dw_ref, dwb_ref, lng_ref, lnb_ref, wo_ref, ho_ref, xr_ref, sg_ref, ub_ref, us_scr, uc_ref):
        i = pl.program_id(0)

        @pl.when(i == 0)
        def _():
            us_scr[:, pl.ds(0, CONV_HALO), :] = jnp.zeros((SUBLANES, CONV_HALO, e), F32)

        @pl.when(i > 0)
        def _():
            us_scr[:, pl.ds(0, CONV_HALO), :] = us_scr[:, pl.ds(tm, CONV_HALO), :]

        sb = _sigmoid(p_ref[:, pl.ds(e, e)])
        u = p_ref[:, pl.ds(0, e)] * sb
        ub_ref[:, pl.ds(0, e)] = u
        ub_ref[:, pl.ds(e, e)] = sb
        us_scr[0, pl.ds(CONV_HALO, tm), :] = u
        for r in range(1, SUBLANES):
            us_scr[r, pl.ds(CONV_HALO, tm), :] = us_scr[0, pl.ds(CONV_HALO - r, tm), :]

        def c_conv(base):
            for lt in range(e // LANES):
                cols = pl.ds(lt * LANES, LANES)
                acc = jnp.broadcast_to(dwb_ref[:, cols], (rc, LANES))
                for r in range(SUBLANES):
                    nq = (CONV_TAPS - 1 - r) // SUBLANES + 1
                    lo = SUBLANES * (nq - 1)
                    win = us_scr[r, pl.ds(pl.multiple_of(CONV_HALO + base - lo, SUBLANES), rc + lo), cols]
                    for q in range(nq):
                        k = CONV_TAPS - 1 - (SUBLANES * q + r)
                        at = lo - SUBLANES * q
                        acc = acc + dw_ref[pl.ds(k, 1), cols] * win[at:at + rc, :]
                uc_ref[pl.ds(base, rc), cols] = acc
        _chunks(tm // rc, rc, c_conv)

        xhat, rstd, ul = _layernorm_rows(uc_ref[...], lng_ref[...], lnb_ref[...])
        z = p_ref[:, pl.ds(2 * e, e)]
        sg_u = _sigmoid(ul)
        sg_z = _sigmoid(z)
        xr_ref[:, pl.ds(0, e)] = xhat
        xr_ref[:, pl.ds(e, LANES)] = jnp.broadcast_to(rstd, (tm, LANES))
        sg_ref[:, pl.ds(0, e)] = sg_u
        sg_ref[:, pl.ds(e, e)] = sg_z
        v = ((ul * sg_u) * (z * sg_z)).astype(BF16)
        ho_ref[...] = h_ref[...] + _dot(v, wo_ref[...])

    vec = _const((None, 1, e), (layer, 0, 0))
    row = lambda i: (i, 0)
    outs, xouts = _launch(
        body, name, t // tm,
        [pl.BlockSpec((tm, 3 * e), row), pl.BlockSpec((tm, d), row),
         _const((None, CONV_TAPS, e), (layer, 0, 0)), vec, vec, vec, _const((e, d), (0, 0))],
        [pl.BlockSpec((tm, d), row), pl.BlockSpec((tm, e + LANES), row), pl.BlockSpec((tm, 2 * e), row),
         pl.BlockSpec((tm, 2 * e), row)],
        [jax.ShapeDtypeStruct((t, d), F32), jax.ShapeDtypeStruct((t, e + LANES), F32),
         jax.ShapeDtypeStruct((t, 2 * e), F32), jax.ShapeDtypeStruct((t, 2 * e), F32)],
        [pltpu.VMEM((SUBLANES, tm + CONV_HALO, e), F32), pltpu.VMEM((tm, e), F32)], (p, h, dw, *vecs, w_out), ex)
    return outs[0], tuple(outs[1:]), xouts


def _conv_bwd(dho, p, saved, dw, vecs, layer, w_out, name, ex=None):
    t, d = dho.shape
    e = w_out.shape[0]
    tm = _tile(t, 256)
    nt = t // tm
    rc = 16
    vec0 = CONV_TAPS + 1

    def body(dho_ref, z_ref, xr_ref, sg_ref, ub_ref, dw_ref, lng_ref, lnb_ref, wo_ref,
             dp_ref, dwo_ref, ddw_ref, dvec_ref, ds_scr, acc_scr, dwo_scr):
        i = pl.program_id(0)

        @pl.when(i == 0)
        def _():
            dwo_scr[...] = jnp.zeros_like(dwo_scr)
            acc_scr[...] = jnp.zeros_like(acc_scr)
            ds_scr[:, pl.ds(tm, CONV_HALO), :] = jnp.zeros((SUBLANES, CONV_HALO, e), F32)

        @pl.when(i > 0)
        def _():
            ds_scr[:, pl.ds(tm, CONV_HALO), :] = ds_scr[:, pl.ds(0, CONV_HALO), :]

        lng = lng_ref[...]
        lnb = lnb_ref[...]

        xhat = xr_ref[:, pl.ds(0, e)]
        rstd = xr_ref[:, pl.ds(e, 1)]
        ul = xhat * lng + lnb
        z = z_ref[...]
        sg_u = sg_ref[:, pl.ds(0, e)]
        sg_z = sg_ref[:, pl.ds(e, e)]
        s_u = ul * sg_u
        s_z = z * sg_z
        v = (s_u * s_z).astype(BF16)
        dy = dho_ref[...].astype(BF16)
        dv = _dot_nt(dy, wo_ref[...])
        dwo_scr[...] += _dot_tn(v, dy)

        dul = dv * s_z * _dsilu(ul, sg_u)
        dp_ref[:, pl.ds(2 * e, e)] = (dv * s_u * _dsilu(z, sg_z)).astype(BF16)
        acc_scr[pl.ds((vec0 + 1) * SUBLANES, SUBLANES), :] += _rows8(dul * xhat)
        acc_scr[pl.ds((vec0 + 2) * SUBLANES, SUBLANES), :] += _rows8(dul)
        dxh = dul * lng
        duc = rstd * (dxh - jnp.mean(dxh, axis=-1, keepdims=True)
                      - xhat * jnp.mean(dxh * xhat, axis=-1, keepdims=True))
        acc_scr[pl.ds(vec0 * SUBLANES, SUBLANES), :] += _rows8(duc)
        ds_scr[0, pl.ds(0, tm), :] = duc
        for r in range(1, SUBLANES):
            ds_scr[r, pl.ds(0, tm), :] = ds_scr[0, pl.ds(r, tm), :]

        def c_conv(base):
            rows = pl.ds(base, rc)
            u = ub_ref[rows, pl.ds(0, e)]
            sb = ub_ref[rows, pl.ds(e, e)]
            du = jnp.zeros((rc, e), F32)
            for o in range(CONV_TAPS):
                q, r = divmod(o, SUBLANES)
                k = CONV_TAPS - 1 - o
                sh = ds_scr[r, pl.ds(pl.multiple_of(base + SUBLANES * q, SUBLANES), rc), :]
                du = du + dw_ref[pl.ds(k, 1), :] * sh
                acc_scr[pl.ds(k * SUBLANES, SUBLANES), :] += _rows8(u * sh)
            dp_ref[rows, pl.ds(0, e)] = (du * sb).astype(BF16)
            dp_ref[rows, pl.ds(e, e)] = (du * u * (1.0 - sb)).astype(BF16)
        _chunks(tm // rc, rc, c_conv)

        @pl.when(i == nt - 1)
        def _():
            dwo_ref[...] = dwo_scr[...].astype(BF16)
            slot_sum = lambda k: jnp.sum(acc_scr[pl.ds(k * SUBLANES, SUBLANES), :], axis=0, keepdims=True)
            for k in range(CONV_TAPS):
                ddw_ref[pl.ds(k, 1), :] = slot_sum(k)
            dvec_ref[...] = jnp.zeros_like(dvec_ref)
            for k in range(3):
                dvec_ref[pl.ds(k, 1), :] = slot_sum(vec0 + k)

    rev = lambda i: (nt - 1 - i, 0)
    vec = _const((None, 1, e), (layer, 0, 0))
    (dp, dwo, ddw, dvec), xouts = _launch(
        body, name, nt,
        [pl.BlockSpec((tm, d), rev), pl.BlockSpec((tm, e), lambda i: (nt - 1 - i, 2)),
         pl.BlockSpec((tm, e + LANES), rev), pl.BlockSpec((tm, 2 * e), rev), pl.BlockSpec((tm, 2 * e), rev),
         _const((None, CONV_TAPS, e), (layer, 0, 0)), vec, vec, _const((e, d), (0, 0))],
        [pl.BlockSpec((tm, 3 * e), rev), _const((e, d), (0, 0)), _const((CONV_TAPS, e), (0, 0)),
         _const((SUBLANES, e), (0, 0))],
        [jax.ShapeDtypeStruct((t, 3 * e), BF16), jax.ShapeDtypeStruct((e, d), BF16),
         jax.ShapeDtypeStruct((CONV_TAPS, e), F32), jax.ShapeDtypeStruct((SUBLANES, e), F32)],
        [pltpu.VMEM((SUBLANES, tm + CONV_HALO, e), F32), pltpu.VMEM(((vec0 + 3) * SUBLANES, e), F32),
         pltpu.VMEM((e, d), F32)],
        (dho, p, *saved, dw, vecs[1], vecs[2], w_out), ex)
    return dp, dwo, ddw, dvec, xouts


def _inv_count(tile, tm, w):
    tpos = tile * tm + lax.broadcasted_iota(jnp.int32, (tm, 1), 0)
    return 1.0 / jnp.minimum(tpos + 1, w).astype(F32)


def _pool_d_group(ue_scr, tile, tm, gc, g):
    w = POOL_WINDOWS[g]
    win = ue_scr[:, pl.ds(g * gc, gc)]
    s = win
    sh = 1
    while sh < w:
        s = s + pltpu.roll(s, sh, axis=0)
        sh *= 2
    return s[POOL_HALO:, :] * _inv_count(tile, tm, w) - win[POOL_HALO:, :]


def _final_rows(ho, tg_ref, fg_ref, dh_ref, acc_scr, lacc_scr):
    d = ho.shape[-1]
    r = lax.rsqrt(jnp.mean(ho * ho, axis=-1, keepdims=True) + RMS_EPS)
    nrm = ho * r
    err = nrm * fg_ref[...] - tg_ref[...]
    lacc_scr[...] += _rows8(err * err)
    dy = err * (1.0 / d)
    acc_scr[...] += _rows8(dy * nrm)
    dq = dy * fg_ref[...]
    dh_ref[...] = r * (dq - nrm * jnp.mean(dq * nrm, axis=-1, keepdims=True))


def _pool_fwd(h, norm_g, nlayer, w_in, w_grp, vecs, layer, w_out, name, ex=None, final=None):
    t, d = h.shape
    e = w_out.shape[0]
    ng = len(POOL_WINDOWS)
    gc = e // ng
    tm = _tile(t, 512)
    nt = t // tm
    cw = 2 * e // (2 * ng)

    def layer_rows(i, hn_ref, hc_ref, g_ref, wi_ref, wg_ref, bg_ref, sc_ref, wo_ref, p_ref, d_ref,
                   ue_scr, y_scr, pbuf, z_scr, hn_scr):
        tile = jnp.maximum(i - 1, 0)

        @pl.when(i == 0)
        def _():
            pbuf[...] = jnp.zeros_like(pbuf)

        @pl.when(i <= 1)
        def _():
            ue_scr[pl.ds(0, POOL_HALO), :] = jnp.zeros((POOL_HALO, e), F32)

        @pl.when(i > 1)
        def _():
            ue_scr[pl.ds(0, POOL_HALO), :] = ue_scr[pl.ds(tm, POOL_HALO), :]

        ue_scr[pl.ds(POOL_HALO, tm), :] = pbuf[:, pl.ds(0, e)]
        z_scr[...] = pbuf[:, pl.ds(e, e)]

        x = hn_ref[...]
        r = lax.rsqrt(jnp.mean(x * x, axis=-1, keepdims=True) + RMS_EPS)
        hn_scr[...] = (x * r * g_ref[...]).astype(BF16)

        def project(c):
            part = _dot(hn_scr[...], wi_ref[:, pl.ds(c * cw, cw)])
            pbuf[:, pl.ds(c * cw, cw)] = part
            p_ref[:, pl.ds(c * cw, cw)] = part

        for g in range(ng):
            project(2 * g)
            cols = pl.ds(g * gc, gc)
            dg = _pool_d_group(ue_scr, tile, tm, gc, g).astype(BF16)
            d_ref[:, cols] = dg
            z = z_scr[:, cols]
            y1 = (_dot(dg, wg_ref[g]) + bg_ref[:, cols]) * sc_ref[:, cols]
            y_scr[:, cols] = (y1 * (z * _sigmoid(z))).astype(BF16)
            project(2 * g + 1)

        return hc_ref[...] + _dot(y_scr[...], wo_ref[...])

    nxt = lambda i: (jnp.minimum(i, nt - 1), 0)
    cur = lambda i: (jnp.maximum(i - 1, 0), 0)
    vec = _const((None, 1, e), (layer, 0, 0))
    in_specs = [pl.BlockSpec((tm, d), nxt), pl.BlockSpec((tm, d), cur), _const((None, 1, d), (nlayer, 0, 0)),
                _const((d, 2 * e), (0, 0)), _const((ng, gc, gc), (0, 0, 0)), vec, vec, _const((e, d), (0, 0))]
    scratch = [pltpu.VMEM((tm + POOL_HALO, e), F32), pltpu.VMEM((tm, e), BF16), pltpu.VMEM((tm, 2 * e), F32),
               pltpu.VMEM((tm, e), F32), pltpu.VMEM((tm, d), BF16)]
    args = (h, h, norm_g, w_in, w_grp, *vecs, w_out)

    if final is None:
        def body(hn_ref, hc_ref, g_ref, wi_ref, wg_ref, bg_ref, sc_ref, wo_ref, ho_ref, p_ref, d_ref, *scr):
            ho_ref[...] = layer_rows(pl.program_id(0), hn_ref, hc_ref, g_ref, wi_ref, wg_ref, bg_ref, sc_ref, wo_ref,
                                     p_ref, d_ref, *scr)

        (ho, p, dsaved), xouts = _launch(
            body, name, nt + 1, in_specs,
            [pl.BlockSpec((tm, d), cur), pl.BlockSpec((tm, 2 * e), nxt), pl.BlockSpec((tm, e), cur)],
            [jax.ShapeDtypeStruct((t, d), F32), jax.ShapeDtypeStruct((t, 2 * e), F32),
             jax.ShapeDtypeStruct((t, e), BF16)], scratch, args, ex)
        return ho, (p, dsaved), xouts

    target, final_g = final

    def body(hn_ref, hc_ref, g_ref, wi_ref, wg_ref, bg_ref, sc_ref, wo_ref, tg_ref, fg_ref,
             dh_ref, p_ref, d_ref, dfg_ref, loss_ref, ue_scr, y_scr, pbuf, z_scr, hn_scr, acc_scr, lacc_scr):
        i = pl.program_id(0)

        @pl.when(i <= 1)
        def _():
            acc_scr[...] = jnp.zeros_like(acc_scr)
            lacc_scr[...] = jnp.zeros_like(lacc_scr)

        ho = layer_rows(i, hn_ref, hc_ref, g_ref, wi_ref, wg_ref, bg_ref, sc_ref, wo_ref, p_ref, d_ref,
                        ue_scr, y_scr, pbuf, z_scr, hn_scr)
        _final_rows(ho, tg_ref, fg_ref, dh_ref, acc_scr, lacc_scr)

        @pl.when(i == nt)
        def _():
            dfg_ref[...] = jnp.zeros_like(dfg_ref)
            dfg_ref[pl.ds(0, 1), :] = jnp.sum(acc_scr[...], axis=0, keepdims=True)
            loss_ref[...] = jnp.broadcast_to(jnp.sum(lacc_scr[...]) * (0.5 / d), loss_ref.shape)

    (dh, p, dsaved, dfg, loss), xouts = _launch(
        body, name, nt + 1, in_specs + [pl.BlockSpec((tm, d), cur), _const((1, d), (0, 0))],
        [pl.BlockSpec((tm, d), cur), pl.BlockSpec((tm, 2 * e), nxt), pl.BlockSpec((tm, e), cur),
         _const((SUBLANES, d), (0, 0)), _const((SUBLANES, LANES), (0, 0))],
        [jax.ShapeDtypeStruct((t, d), F32), jax.ShapeDtypeStruct((t, 2 * e), F32), jax.ShapeDtypeStruct((t, e), BF16),
         jax.ShapeDtypeStruct((SUBLANES, d), F32), jax.ShapeDtypeStruct((SUBLANES, LANES), F32)],
        scratch + [pltpu.VMEM((SUBLANES, d), F32), pltpu.VMEM((SUBLANES, d), F32)],
        args + (target, final_g), ex)
    return dh, (p, dsaved), dfg, loss, xouts


def _pool_bwd(dho, saved, w_grp, vecs, layer, w_out, name, ex=None):
    t, d = dho.shape
    e = w_out.shape[0]
    ng = len(POOL_WINDOWS)
    gc = e // ng
    tm = _tile(t, 512)
    nt = t // tm

    def body(dho_ref, z_ref, d_ref, wg_ref, bg_ref, sc_ref, wo_ref, dp_ref, dwo_ref, dwg_ref, dvec_ref,
             ee_scr, acc_scr, dwo_scr, dwg_scr):
        i = pl.program_id(0)
        tile = nt - 1 - i

        @pl.when(i == 0)
        def _():
            dwo_scr[...] = jnp.zeros_like(dwo_scr)
            dwg_scr[...] = jnp.zeros_like(dwg_scr)
            acc_scr[...] = jnp.zeros_like(acc_scr)
            ee_scr[pl.ds(tm, POOL_HALO), :] = jnp.zeros((POOL_HALO, e), F32)

        @pl.when(i > 0)
        def _():
            ee_scr[pl.ds(tm, POOL_HALO), :] = ee_scr[pl.ds(0, POOL_HALO), :]

        bg = bg_ref[...]
        sc = sc_ref[...]
        ds = [d_ref[:, pl.ds(g * gc, gc)] for g in range(ng)]
        ob = jnp.concatenate([_dot(ds[g], wg_ref[g]) for g in range(ng)], axis=1) + bg
        z = z_ref[...]
        sg_z = _sigmoid(z)
        s_z = z * sg_z
        y1 = ob * sc
        dy = dho_ref[...].astype(BF16)
        dy2 = _dot_nt(dy, wo_ref[...])
        dwo_scr[...] += _dot_tn((y1 * s_z).astype(BF16), dy)
        dy1 = dy2 * s_z
        dp_ref[:, pl.ds(e, e)] = (dy2 * y1 * _dsilu(z, sg_z)).astype(BF16)
        acc_scr[pl.ds(SUBLANES, SUBLANES), :] += _rows8(dy1 * ob)
        do = dy1 * sc
        acc_scr[pl.ds(0, SUBLANES), :] += _rows8(do)

        n = tm + POOL_HALO
        for g, w in enumerate(POOL_WINDOWS):
            cols = pl.ds(g * gc, gc)
            do_g = do[:, g * gc:(g + 1) * gc].astype(BF16)
            dwg_scr[g] += _dot_tn(ds[g], do_g)
            dd = _dot_nt(do_g, wg_ref[g])
            ee_scr[pl.ds(0, tm), cols] = dd * _inv_count(tile, tm, w)
            s = ee_scr[:, cols]
            sh = 1
            while sh < w:
                s = s + pltpu.roll(s, n - sh, axis=0)
                sh *= 2
            dp_ref[:, cols] = (s[:tm, :] - dd).astype(BF16)

        @pl.when(i == nt - 1)
        def _():
            dwo_ref[...] = dwo_scr[...].astype(BF16)
            dwg_ref[...] = dwg_scr[...].astype(BF16)
            dvec_ref[...] = jnp.zeros_like(dvec_ref)
            for k in range(2):
                dvec_ref[pl.ds(k, 1), :] = jnp.sum(acc_scr[pl.ds(k * SUBLANES, SUBLANES), :], axis=0, keepdims=True)

    rev = lambda i: (nt - 1 - i, 0)
    vec = _const((None, 1, e), (layer, 0, 0))
    (dp, dwo, dwg, dvec), xouts = _launch(
        body, name, nt,
        [pl.BlockSpec((tm, d), rev), pl.BlockSpec((tm, e), lambda i: (nt - 1 - i, 1)), pl.BlockSpec((tm, e), rev),
         _const((ng, gc, gc), (0, 0, 0)), vec, vec, _const((e, d), (0, 0))],
        [pl.BlockSpec((tm, 2 * e), rev), _const((e, d), (0, 0)), _const((ng, gc, gc), (0, 0, 0)),
         _const((SUBLANES, e), (0, 0))],
        [jax.ShapeDtypeStruct((t, 2 * e), BF16), jax.ShapeDtypeStruct((e, d), BF16),
         jax.ShapeDtypeStruct((ng, gc, gc), BF16), jax.ShapeDtypeStruct((SUBLANES, e), F32)],
        [pltpu.VMEM((tm + POOL_HALO, e), F32),
         pltpu.VMEM((2 * SUBLANES, e), F32), pltpu.VMEM((e, d), F32), pltpu.VMEM((ng, gc, gc), F32)],
        (dho, *saved, w_grp, *vecs, w_out), ex)
    return dp, dwo, dwg, dvec, xouts


def _in_bwd(dp, h, dho, norm_g, layer, w_in, name, ex=None):
    t, d = h.shape
    n = w_in.shape[-1]
    tm = _tile(t, 512)
    nt = t // tm

    def body(dp_ref, h_ref, dho_ref, g_ref, w_ref, dh_ref, dw_ref, dg_ref, acc_scr, dw_scr):
        i = pl.program_id(0)

        @pl.when(i == 0)
        def _():
            dw_scr[...] = jnp.zeros_like(dw_scr)
            acc_scr[...] = jnp.zeros_like(acc_scr)

        x = h_ref[...]
        r = lax.rsqrt(jnp.mean(x * x, axis=-1, keepdims=True) + RMS_EPS)
        nrm = x * r
        dp = dp_ref[...]
        dhn = _dot_nt(dp, w_ref[...])
        dw_scr[...] += _dot_tn((nrm * g_ref[...]).astype(BF16), dp)
        acc_scr[...] += _rows8(dhn * nrm)
        dq = dhn * g_ref[...]
        dh_ref[...] = dho_ref[...] + r * (dq - nrm * jnp.mean(dq * nrm, axis=-1, keepdims=True))

        @pl.when(i == nt - 1)
        def _():
            dw_ref[...] = dw_scr[...].astype(BF16)
            dg_ref[...] = jnp.zeros_like(dg_ref)
            dg_ref[pl.ds(0, 1), :] = jnp.sum(acc_scr[...], axis=0, keepdims=True)

    (dh, dw, dg), xouts = _launch(
        body, name, nt,
        [pl.BlockSpec((tm, n), lambda i: (i, 0)), pl.BlockSpec((tm, d), lambda i: (i, 0)),
         pl.BlockSpec((tm, d), lambda i: (i, 0)), _const((None, 1, d), (layer, 0, 0)), _const((d, n), (0, 0))],
        [pl.BlockSpec((tm, d), lambda i: (i, 0)), _const((d, n), (0, 0)), _const((SUBLANES, d), (0, 0))],
        [jax.ShapeDtypeStruct((t, d), F32), jax.ShapeDtypeStruct((d, n), BF16),
         jax.ShapeDtypeStruct((SUBLANES, d), F32)],
        [pltpu.VMEM((SUBLANES, d), F32), pltpu.VMEM((d, n), F32)],
        (dp, h, dho, norm_g, w_in), ex)
    return dh, dw, dg, xouts


def _in_bwd_dw(dp, h, norm_g, layer, name, ex=None):
    t, d = h.shape
    n = dp.shape[-1]
    tm = _tile(t, 512)
    nt = t // tm

    def body(dp_ref, h_ref, g_ref, dw_ref, dw_scr):
        i = pl.program_id(0)

        @pl.when(i == 0)
        def _():
            dw_scr[...] = jnp.zeros_like(dw_scr)

        x = h_ref[...]
        r = lax.rsqrt(jnp.mean(x * x, axis=-1, keepdims=True) + RMS_EPS)
        dw_scr[...] += _dot_tn((x * r * g_ref[...]).astype(BF16), dp_ref[...])

        @pl.when(i == nt - 1)
        def _():
            dw_ref[...] = dw_scr[...].astype(BF16)

    (dw,), xouts = _launch(
        body, name, nt,
        [pl.BlockSpec((tm, n), lambda i: (i, 0)), pl.BlockSpec((tm, d), lambda i: (i, 0)),
         _const((None, 1, d), (layer, 0, 0))],
        [_const((d, n), (0, 0))], [jax.ShapeDtypeStruct((d, n), BF16)], [pltpu.VMEM((d, n), F32)],
        (dp, h, norm_g), ex)
    return dw, xouts


def _in_bwd_dh(dp, h, dho, norm_g, layer, w_in, name, ex=None, tiles=None, into=None):
    t, d = h.shape
    n = w_in.shape[-1]
    tm = _tile(t, 512)
    first, nt = tiles if tiles is not None else (0, t // tm)
    row = lambda i: (i + first, 0)

    def body(dp_ref, h_ref, dho_ref, g_ref, w_ref, *rest):
        dh_ref, dg_ref, acc_scr = rest[-3:]
        i = pl.program_id(0)

        @pl.when(i == 0)
        def _():
            acc_scr[...] = jnp.zeros_like(acc_scr)

        x = h_ref[...]
        r = lax.rsqrt(jnp.mean(x * x, axis=-1, keepdims=True) + RMS_EPS)
        nrm = x * r
        dhn = _dot_nt(dp_ref[...], w_ref[...])
        acc_scr[...] += _rows8(dhn * nrm)
        dq = dhn * g_ref[...]
        dh_ref[...] = dho_ref[...] + r * (dq - nrm * jnp.mean(dq * nrm, axis=-1, keepdims=True))

        @pl.when(i == nt - 1)
        def _():
            dg_ref[...] = jnp.zeros_like(dg_ref)
            dg_ref[pl.ds(0, 1), :] = jnp.sum(acc_scr[...], axis=0, keepdims=True)

    in_specs = [pl.BlockSpec((tm, n), row), pl.BlockSpec((tm, d), row), pl.BlockSpec((tm, d), row),
                _const((None, 1, d), (layer, 0, 0)), _const((d, n), (0, 0))]
    args = (dp, h, dho, norm_g, w_in)
    if into is not None:
        in_specs, args = in_specs + [ANY], args + (into,)
    (dh, dg), xouts = _launch(
        body, name, nt, in_specs,
        [pl.BlockSpec((tm, d), row), _const((SUBLANES, d), (0, 0))],
        [jax.ShapeDtypeStruct((t, d), F32), jax.ShapeDtypeStruct((SUBLANES, d), F32)],
        [pltpu.VMEM((SUBLANES, d), F32)], args, ex, aliases=None if into is None else {len(args) - 1: 0})
    return dh, dg, xouts


def _adam_update(g, w, m, v):
    c1 = 1.0 / (1.0 - ADAM_B1 ** ADAM_STEP)
    c2 = 1.0 / (1.0 - ADAM_B2 ** ADAM_STEP)
    nm = ADAM_B1 * m + (1.0 - ADAM_B1) * g
    nv = ADAM_B2 * v + (1.0 - ADAM_B2) * (g * g)
    return -ADAM_LR * ((nm * c1) / (jnp.sqrt(nv * c2) + ADAM_EPS) + ADAM_WD * w), nm, nv


def _adamw_small(params, stacks, loss_stack, name):
    ns, npar = len(stacks), len(params)

    def body(*refs):
        st = refs[:ns]
        pr = refs[ns:ns + 3 * npar]
        ls_ref = refs[ns + 3 * npar]
        outs = refs[ns + 3 * npar + 1:ns + 7 * npar + 1]
        loss_ref = refs[ns + 7 * npar + 1]
        for q, (w, _, _, pieces) in enumerate(params):
            w_ref, m_ref, v_ref = pr[3 * q:3 * q + 3]
            g_ref, d_ref, nm_ref, nv_ref = outs[4 * q:4 * q + 4]
            for s, row, slab in pieces:
                if w.ndim == 3:
                    take = lambda k: st[s][k]
                    at = slab
                else:
                    take = lambda k: st[s][k, pl.ds(row, 1), :]
                    at = (pl.ds(slab, 1), slice(None))
                g = take(0)
                for k in range(1, N_DEV):
                    g = g + take(k)
                g_ref[at] = g
                d_ref[at], nm_ref[at], nv_ref[at] = _adam_update(g, w_ref[at], m_ref[at], v_ref[at])
        tot = ls_ref[0]
        for k in range(1, N_DEV):
            tot = tot + ls_ref[k]
        loss_ref[...] = tot

    flat = [a for (w, m, v, _) in params for a in (w, m, v)]
    out_shape = [jax.ShapeDtypeStruct(w.shape, F32) for (w, _, _, _) in params for _ in range(4)]
    whole = pl.BlockSpec(memory_space=pltpu.VMEM)
    outs = pl.pallas_call(
        body, name=name, in_specs=[whole] * (ns + 3 * npar + 1), out_specs=[whole] * (4 * npar + 1),
        out_shape=out_shape + [jax.ShapeDtypeStruct(loss_stack.shape[1:], F32)],
    )(*stacks, *flat, loss_stack)
    return [outs[4 * q:4 * q + 4] for q in range(npar)], outs[-1]


def _adamw(stacks, w, m, v, name):
    nl = len(stacks)
    shp = w.shape
    c = shp[-1]
    r = 1
    for s in shp[1:-1]:
        r *= s
    tr = r
    for cand in (512, 256, 128, 64, 32, 16):
        if r % cand == 0 and r > cand:
            tr = cand
            break
    nrb = r // tr

    def body(*refs):
        s_refs = refs[:nl]
        w_ref, m_ref, v_ref, g_ref, d_ref, nm_ref, nv_ref = refs[nl:]
        layer = pl.program_id(0)
        for l in range(nl):
            @pl.when(layer == l)
            def _(l=l):
                g = s_refs[l][0].astype(F32)
                for k in range(1, N_DEV):
                    g = g + s_refs[l][k].astype(F32)
                g_ref[...] = g
                d_ref[...], nm_ref[...], nv_ref[...] = _adam_update(g, w_ref[...], m_ref[...], v_ref[...])

    def stack_spec(l):
        return pl.BlockSpec((N_DEV, tr, c),
                            lambda j, i: (0, jnp.where(j == l, i, jnp.where(j < l, 0, nrb - 1)), 0))

    spec = pl.BlockSpec((None, tr, c), lambda j, i: (j, i, 0))
    outs = pl.pallas_call(
        body, name=name, grid=(nl, nrb),
        in_specs=[stack_spec(l) for l in range(nl)] + [spec, spec, spec],
        out_specs=[spec] * 4,
        out_shape=[jax.ShapeDtypeStruct((nl, r, c), F32)] * 4,
        compiler_params=_params(2),
    )(*[s.reshape(N_DEV, r, c) for s in stacks], w.reshape(nl, r, c), m.reshape(nl, r, c), v.reshape(nl, r, c))
    return [o.reshape(shp) for o in outs]


def kernel(x, norm_g, final_g, conv_w_in, conv_dw, conv_dw_b, conv_ln_g, conv_ln_b, conv_w_out, pool_w_in, pool_w_grp, pool_b_grp, pool_scale, pool_w_out, loss_target, m_norm_g, m_final_g, m_conv_w_in, m_conv_dw, m_conv_dw_b, m_conv_ln_g, m_conv_ln_b, m_conv_w_out, m_pool_w_in, m_pool_w_grp, m_pool_b_grp, m_pool_scale, m_pool_w_out, v_norm_g, v_final_g, v_conv_w_in, v_conv_dw, v_conv_dw_b, v_conv_ln_g, v_conv_ln_b, v_conv_w_out, v_pool_w_in, v_pool_w_grp, v_pool_b_grp, v_pool_scale, v_pool_w_out):
    h0 = x[0]
    target = loss_target[0]
    ng3 = norm_g[:, None, :]
    row3 = lambda a: a[:, None, :]
    conv_vecs = (row3(conv_dw_b), row3(conv_ln_g), row3(conv_ln_b))
    gather = lambda arrays, axes: _Exchange("gather", arrays, axes)
    scatter = lambda arrays, axes: _Exchange("scatter", arrays, axes)

    cwi, cwo, pwi = conv_w_in.astype(BF16), conv_w_out.astype(BF16), pool_w_in.astype(BF16)
    pwg, pwo = pool_w_grp.astype(BF16), pool_w_out.astype(BF16)

    (cw_in0,) = _gather_via_sibling_call(gather([(cwi, 0)], [1]), "gather_first")
    p0, (cw_out0, dw_full, bg_full, sc_full, pw_in0) = _in_fwd(
        h0, ng3, 0, cw_in0, "conv_in_fwd_0",
        gather([(cwo, 0), conv_dw, pool_b_grp, pool_scale, (pwi, 0)], [0, 2, 1, 1, 1]))
    pool_vecs = (row3(bg_full), row3(sc_full))
    h1, uc0, (pw_grp0, pw_out0, cw_in1, cw_out1, pw_in1, pw_grp1, pw_out1) = _conv_fwd(
        p0, h0, dw_full, conv_vecs, 0, cw_out0, "conv_mix_fwd_0",
        gather([(pwg, 0), (pwo, 0), (cwi, 1), (cwo, 1), (pwi, 1), (pwg, 1), (pwo, 1)], [1, 0, 1, 0, 1, 1, 0]))
    h2, p1, _ = _pool_fwd(h1, ng3, 1, pw_in0, pw_grp0, pool_vecs, 0, pw_out0, "pool_fwd_0")
    p2, _ = _in_fwd(h2, ng3, 2, cw_in1, "conv_in_fwd_1")
    h3, uc2, _ = _conv_fwd(p2, h2, dw_full, conv_vecs, 1, cw_out1, "conv_mix_fwd_1")
    dh, p3, d_final_g, loss_part, _ = _pool_fwd(h3, ng3, 3, pw_in1, pw_grp1, pool_vecs, 1, pw_out1, "pool_fwd_1",
                                               final=(target, final_g[None, :]))

    dp, g_pwo1, g_pwg1, dpv1, _ = _pool_bwd(dh, p3, pw_grp1, pool_vecs, 1, pw_out1, "pool_mix_bwd_1")
    dh, g_pwi1, dg3, _ = _in_bwd(dp, h3, dh, ng3, 3, pw_in1, "pool_in_bwd_1")
    dp, g_cwo1, ddw1, dcv1, (s_pwo1, s_pwg1, s_pwi1) = _conv_bwd(
        dh, p2, uc2, dw_full, conv_vecs, 1, cw_out1, "conv_mix_bwd_1", scatter([g_pwo1, g_pwg1, g_pwi1], [0, 1, 1]))
    dh, g_cwi1, dg2, _ = _in_bwd(dp, h2, dh, ng3, 2, cw_in1, "conv_in_bwd_1")
    dp, g_pwo0, g_pwg0, dpv0, (s_cwo1, s_cwi1) = _pool_bwd(dh, p1, pw_grp0, pool_vecs, 0, pw_out0, "pool_mix_bwd_0",
                                                           scatter([g_cwo1, g_cwi1], [0, 1]))
    dh, g_pwi0, dg1, _ = _in_bwd(dp, h1, dh, ng3, 1, pw_in0, "pool_in_bwd_0")
    dp, g_cwo0, ddw0, dcv0, (s_pwo0, s_pwg0, s_pwi0, s_ddw1, s_dcv1, s_dpv0, s_dpv1, s_dg1, s_dg2, s_dg3, s_dfg,
                             s_loss) = _conv_bwd(
        dh, p0, uc0, dw_full, conv_vecs, 0, cw_out0, "conv_mix_bwd_0",
        scatter([g_pwo0, g_pwg0, g_pwi0, ddw1, dcv1, dpv0, dpv1, dg1, dg2, dg3, d_final_g, loss_part],
                [0, 1, 1, 1, None, 1, 1, None, None, None, None, None]))
    g_cwi0, (s_cwo0, s_ddw0, s_dcv0) = _in_bwd_dw(dp, h0, ng3, 0, "conv_in_bwd_dw_0",
                                                  scatter([g_cwo0, ddw0, dcv0], [0, 1, None]))
    half = h0.shape[0] // _tile(h0.shape[0], 512) // 2
    dh_a, dg0_a, (s_cwi0,) = _in_bwd_dh(dp, h0, dh, ng3, 0, cw_in0, "conv_in_bwd_dh_0a", scatter([g_cwi0], [1]),
                                        tiles=(0, half))
    dh, dg0_b, _ = _in_bwd_dh(dp, h0, dh, ng3, 0, cw_in0, "conv_in_bwd_dh_0b", tiles=(half, half), into=dh_a)
    dg0 = dg0_a + dg0_b
    grad_x = dh[None]
    (s_dg0,) = _exchange_call(scatter([dg0], [None]), "scatter_last")

    res = {}
    res["conv_w_in"] = _adamw([s_cwi0, s_cwi1], conv_w_in, m_conv_w_in, v_conv_w_in, "adamw_conv_w_in")
    res["conv_w_out"] = _adamw([s_cwo0, s_cwo1], conv_w_out, m_conv_w_out, v_conv_w_out, "adamw_conv_w_out")
    res["pool_w_in"] = _adamw([s_pwi0, s_pwi1], pool_w_in, m_pool_w_in, v_pool_w_in, "adamw_pool_w_in")
    res["pool_w_grp"] = _adamw([s_pwg0, s_pwg1], pool_w_grp, m_pool_w_grp, v_pool_w_grp, "adamw_pool_w_grp")
    res["pool_w_out"] = _adamw([s_pwo0, s_pwo1], pool_w_out, m_pool_w_out, v_pool_w_out, "adamw_pool_w_out")
    stacks = [s_ddw0, s_ddw1, s_dpv0, s_dpv1, s_dg0, s_dg1, s_dg2, s_dg3, s_dfg, s_dcv0, s_dcv1]
    small = [
        ("conv_dw", conv_dw, m_conv_dw, v_conv_dw, [(0, None, 0), (1, None, 1)]),
        ("pool_b_grp", pool_b_grp, m_pool_b_grp, v_pool_b_grp, [(2, 0, 0), (3, 0, 1)]),
        ("pool_scale", pool_scale, m_pool_scale, v_pool_scale, [(2, 1, 0), (3, 1, 1)]),
        ("norm_g", norm_g, m_norm_g, v_norm_g, [(4, 0, 0), (5, 0, 1), (6, 0, 2), (7, 0, 3)]),
        ("final_g", final_g[None, :], m_final_g[None, :], v_final_g[None, :], [(8, 0, 0)]),
        ("conv_dw_b", conv_dw_b, m_conv_dw_b, v_conv_dw_b, [(9, 0, 0), (10, 0, 1)]),
        ("conv_ln_g", conv_ln_g, m_conv_ln_g, v_conv_ln_g, [(9, 1, 0), (10, 1, 1)]),
        ("conv_ln_b", conv_ln_b, m_conv_ln_b, v_conv_ln_b, [(9, 2, 0), (10, 2, 1)]),
    ]
    small_res, loss_block = _adamw_small([s[1:] for s in small], stacks, s_loss, "adamw_small")
    for (name, *_), r in zip(small, small_res):
        res[name] = [a[0] for a in r] if name == "final_g" else r
    loss = loss_block[0, 0]

    names = ["norm_g", "final_g", "conv_w_in", "conv_dw", "conv_dw_b", "conv_ln_g", "conv_ln_b", "conv_w_out",
             "pool_w_in", "pool_w_grp", "pool_b_grp", "pool_scale", "pool_w_out"]
    return (loss, grad_x) + tuple(res[n][q] for q in range(4) for n in names)
```

```python
import jax
import jax.numpy as jnp
from jax import lax
from jax.experimental import pallas as pl
from jax.experimental.pallas import tpu as pltpu

F32 = jnp.float32
BF16 = jnp.bfloat16

RMS_EPS = 1e-6
LN_EPS = 1e-5
CONV_TAPS = 31
CONV_HALO = 32
POOL_WINDOWS = (2, 4, 8, 16)
POOL_HALO = 16
SUBLANES = 8
LANES = 128
N_DEV = 8
V7X_VMEM_LIMIT = 56 * 1024 * 1024

ADAM_LR = 0.001
ADAM_B1 = 0.9
ADAM_B2 = 0.999
ADAM_EPS = 1e-08
ADAM_WD = 0.01
ADAM_STEP = 10

MESH = pl.DeviceIdType.MESH
ANY = pl.BlockSpec(memory_space=pl.ANY)


def _dot(a, b):
    return lax.dot_general(a, b, (((1,), (0,)), ((), ())), preferred_element_type=F32)


def _dot_nt(a, b):
    return lax.dot_general(a, b, (((1,), (1,)), ((), ())), preferred_element_type=F32)


def _dot_tn(a, b):
    return lax.dot_general(a, b, (((0,), (0,)), ((), ())), preferred_element_type=F32)


def _sigmoid(x):
    return jax.nn.sigmoid(x)


def _dsilu(x, s):
    return s * (1.0 + x * (1.0 - s))


def _rows8(x):
    r, c = x.shape
    return jnp.sum(x.reshape(r // SUBLANES, SUBLANES, c), axis=0)


def _tile(t, pref):
    return pref if t >= 2 * pref else t // 2


def _const(shape, index):
    return pl.BlockSpec(shape, lambda *_: index, pipeline_mode=pl.Buffered(1))


def _params(grid_rank=1):
    return pltpu.CompilerParams(dimension_semantics=("arbitrary",) * grid_rank, vmem_limit_bytes=V7X_VMEM_LIMIT)


def _chunks(n, rc, fn):
    def step(c, carry):
        fn(pl.multiple_of(c * rc, rc))
        return carry
    lax.fori_loop(0, n, step, 0)


def _mesh_position():
    return lax.axis_index("x"), lax.axis_index("y"), lax.axis_index("c")


def _peer(j):
    x, y, c = _mesh_position()
    px = 1 - x if j & 4 else x
    py = 1 - y if j & 2 else y
    pc = 1 - c if j & 1 else c
    return (px, py, pc), 4 * px + 2 * py + pc


def _block(ref, axis, index, size):
    idx = [slice(None)] * len(ref.shape)
    idx[axis] = pl.ds(pl.multiple_of(index * size, size), size)
    return ref.at[tuple(idx)]


class _Exchange:
    def __init__(self, kind, arrays, axes):
        self.kind, self.axes = kind, list(axes)
        self.arrays = [a[0] if isinstance(a, tuple) else a for a in arrays]
        self.layers = [a[1] if isinstance(a, tuple) else None for a in arrays]
        self.n = len(self.arrays)
        self.blk, self.out_shapes = [], []
        for a, layer, ax in zip(self.arrays, self.layers, self.axes):
            s = list(a.shape if layer is None else a.shape[1:])
            if kind == "gather":
                self.blk.append(s[ax])
                s[ax] *= N_DEV
                self.out_shapes.append(jax.ShapeDtypeStruct(tuple(s), a.dtype))
            else:
                if ax is not None:
                    s[ax] //= N_DEV
                    self.blk.append(s[ax])
                else:
                    self.blk.append(None)
                self.out_shapes.append(jax.ShapeDtypeStruct((N_DEV,) + tuple(s), a.dtype))

    def sem_shapes(self):
        return [pltpu.SemaphoreType.DMA((N_DEV - 1, self.n)), pltpu.SemaphoreType.DMA((N_DEV - 1, self.n)),
                pltpu.SemaphoreType.DMA((self.n,))]

    def _src(self, ins, k, owner):
        ref = ins[k] if self.layers[k] is None else ins[k].at[self.layers[k]]
        if self.kind == "gather" or self.axes[k] is None:
            return ref
        return _block(ref, self.axes[k], owner, self.blk[k])

    def _dst(self, outs, k, sender):
        if self.kind == "gather":
            return _block(outs[k], self.axes[k], sender, self.blk[k])
        return outs[k].at[sender]

    def _copies(self, ins, outs, sems, arriving):
        send, recv, loc = sems
        x, y, c = _mesh_position()
        me = 4 * x + 2 * y + c
        if not arriving:
            local = [pltpu.make_async_copy(self._src(ins, k, me), self._dst(outs, k, me), loc.at[k])
                     for k in range(self.n)]
        else:
            local = []
        remote = []
        for j in range(1, N_DEV):
            peer, peer_id = _peer(j)
            for k in range(self.n):
                owner, sender = (me, peer_id) if arriving else (peer_id, me)
                remote.append(pltpu.make_async_remote_copy(
                    src_ref=self._src(ins, k, owner), dst_ref=self._dst(outs, k, sender),
                    send_sem=send.at[j - 1, k], recv_sem=recv.at[j - 1, k], device_id=peer, device_id_type=MESH))
        return local, remote

    def start(self, ins, outs, sems):
        local, sends = self._copies(ins, outs, sems, arriving=False)
        for cp in local + sends:
            cp.start()

    def finish(self, ins, outs, sems):
        for cp in self._copies(ins, outs, sems, arriving=True)[1]:
            cp.wait_recv()
        local, sends = self._copies(ins, outs, sems, arriving=False)
        for cp in sends:
            cp.wait_send()
        for cp in local:
            cp.wait()


def _exchange_call(ex, name):
    def body(*refs):
        ins, outs, sems = refs[:ex.n], refs[ex.n:2 * ex.n], refs[2 * ex.n:]
        ex.start(ins, outs, sems)
        ex.finish(ins, outs, sems)

    return pl.pallas_call(body, name=name, in_specs=[ANY] * ex.n, out_specs=[ANY] * ex.n,
                          out_shape=ex.out_shapes, scratch_shapes=ex.sem_shapes())(*ex.arrays)


def _gather_via_sibling_call(ex, name):
    n = ex.n

    def body(*refs):
        ins, outs = refs[:n], refs[n:2 * n]
        send, recv, loc = refs[2 * n:]
        x, y, c = _mesh_position()
        ident = lambda px, py, pc: 4 * px + 2 * py + pc
        me, sibling = ident(x, y, c), (x, y, 1 - c)
        chips = [(1 - x, y), (x, 1 - y), (1 - x, 1 - y)]

        def copy(row, k, block, to, src=None):
            place = ex._dst(outs, k, block)
            return pltpu.make_async_remote_copy(
                src_ref=place if src is None else src, dst_ref=place,
                send_sem=send.at[row, k], recv_sem=recv.at[row, k], device_id=to, device_id_type=MESH)

        local = [pltpu.make_async_copy(ex._src(ins, k, me), ex._dst(outs, k, me), loc.at[k]) for k in range(n)]
        first = []
        for k in range(n):
            mine = ex._src(ins, k, me)
            first.append(copy(0, k, me, sibling, src=mine))
            first += [copy(1 + j, k, me, (*chip, c), src=mine) for j, chip in enumerate(chips)]
        for cp in local + first:
            cp.start()
        passed = []
        for j, chip in enumerate(chips):
            for k in range(n):
                copy(1 + j, k, ident(*chip, c), sibling).wait_recv()
                passed.append(copy(4 + j, k, ident(*chip, c), sibling))
                passed[-1].start()
        for k in range(n):
            copy(0, k, ident(x, y, 1 - c), sibling).wait_recv()
            for j, chip in enumerate(chips):
                copy(4 + j, k, ident(*chip, 1 - c), sibling).wait_recv()
        for cp in first + passed:
            cp.wait_send()
        for cp in local:
            cp.wait()

    return pl.pallas_call(body, name=name, in_specs=[ANY] * n, out_specs=[ANY] * n, out_shape=ex.out_shapes,
                          scratch_shapes=ex.sem_shapes())(*ex.arrays)


def _launch(body, name, nt, in_specs, out_specs, out_shape, scratch_shapes, args, ex=None):
    if ex is None:
        outs = pl.pallas_call(body, name=name, grid=(nt,), in_specs=in_specs, out_specs=out_specs,
                              out_shape=out_shape, scratch_shapes=scratch_shapes, compiler_params=_params())(*args)
        return list(outs), []
    n_in, n_out, n_scr = len(in_specs), len(out_specs), len(scratch_shapes)

    def riding(*refs):
        a, xa = refs[:n_in], refs[n_in:n_in + ex.n]
        o = refs[n_in + ex.n:n_in + ex.n + n_out]
        xo = refs[n_in + ex.n + n_out:n_in + 2 * ex.n + n_out]
        s = refs[n_in + 2 * ex.n + n_out:n_in + 2 * ex.n + n_out + n_scr]
        sems = refs[n_in + 2 * ex.n + n_out + n_scr:]
        i = pl.program_id(0)

        @pl.when(i == 0)
        def _():
            ex.start(xa, xo, sems)

        body(*a, *o, *s)

        @pl.when(i == nt - 1)
        def _():
            ex.finish(xa, xo, sems)

    outs = pl.pallas_call(
        riding, name=name, grid=(nt,),
        in_specs=list(in_specs) + [ANY] * ex.n, out_specs=list(out_specs) + [ANY] * ex.n,
        out_shape=list(out_shape) + ex.out_shapes, scratch_shapes=list(scratch_shapes) + ex.sem_shapes(),
        compiler_params=_params())(*args, *ex.arrays)
    return list(outs[:n_out]), list(outs[n_out:])


def _in_fwd(h, norm_g, layer, w_in, name, ex=None):
    t, d = h.shape
    n = w_in.shape[-1]
    tm = _tile(t, 512)

    def body(h_ref, g_ref, w_ref, p_ref):
        x = h_ref[...]
        r = lax.rsqrt(jnp.mean(x * x, axis=-1, keepdims=True) + RMS_EPS)
        p_ref[...] = _dot((x * r * g_ref[...]).astype(BF16), w_ref[...])

    (p,), xouts = _launch(
        body, name, t // tm,
        [pl.BlockSpec((tm, d), lambda i: (i, 0)), _const((None, 1, d), (layer, 0, 0)), _const((d, n), (0, 0))],
        [pl.BlockSpec((tm, n), lambda i: (i, 0))],
        [jax.ShapeDtypeStruct((t, n), F32)], [], (h, norm_g, w_in), ex)
    return p, xouts


def _layernorm_rows(uc, lng, lnb):
    mu = jnp.mean(uc, axis=-1, keepdims=True)
    xc = uc - mu
    rstd = lax.rsqrt(jnp.mean(xc * xc, axis=-1, keepdims=True) + LN_EPS)
    xhat = xc * rstd
    return xhat, rstd, xhat * lng + lnb


def _conv_fwd(p, h, dw, vecs, layer, w_out, name, ex=None):
    t, d = h.shape
    e = w_out.shape[0]
    tm = _tile(t, 256)
    rc = _tile(tm, 128)

    def body(p_ref, h_ref, dw_ref, dwb_ref, lng_ref, lnb_ref, wo_ref, ho_ref, xr_ref, sg_ref, ub_ref, us_scr, uc_ref):
        i = pl.program_id(0)

        @pl.when(i == 0)
        def _():
            us_scr[:, pl.ds(0, CONV_HALO), :] = jnp.zeros((SUBLANES, CONV_HALO, e), F32)

        @pl.when(i > 0)
        def _():
            us_scr[:, pl.ds(0, CONV_HALO), :] = us_scr[:, pl.ds(tm, CONV_HALO), :]

        sb = _sigmoid(p_ref[:, pl.ds(e, e)])
        u = p_ref[:, pl.ds(0, e)] * sb
        ub_ref[:, pl.ds(0, e)] = u
        ub_ref[:, pl.ds(e, e)] = sb
        us_scr[0, pl.ds(CONV_HALO, tm), :] = u
        for r in range(1, SUBLANES):
            us_scr[r, pl.ds(CONV_HALO, tm), :] = us_scr[0, pl.ds(CONV_HALO - r, tm), :]

        def c_conv(base):
            for lt in range(e // LANES):
                cols = pl.ds(lt * LANES, LANES)
                acc = jnp.broadcast_to(dwb_ref[:, cols], (rc, LANES))
                for r in range(SUBLANES):
                    nq = (CONV_TAPS - 1 - r) // SUBLANES + 1
                    lo = SUBLANES * (nq - 1)
                    win = us_scr[r, pl.ds(pl.multiple_of(CONV_HALO + base - lo, SUBLANES), rc + lo), cols]
                    for q in range(nq):
                        k = CONV_TAPS - 1 - (SUBLANES * q + r)
                        at = lo - SUBLANES * q
                        acc = acc + dw_ref[pl.ds(k, 1), cols] * win[at:at + rc, :]
                uc_ref[pl.ds(base, rc), cols] = acc
        _chunks(tm // rc, rc, c_conv)

        xhat, rstd, ul = _layernorm_rows(uc_ref[...], lng_ref[...], lnb_ref[...])
        z = p_ref[:, pl.ds(2 * e, e)]
        sg_u = _sigmoid(ul)
        sg_z = _sigmoid(z)
        xr_ref[:, pl.ds(0, e)] = xhat
        xr_ref[:, pl.ds(e, LANES)] = jnp.broadcast_to(rstd, (tm, LANES))
        sg_ref[:, pl.ds(0, e)] = sg_u
        sg_ref[:, pl.ds(e, e)] = sg_z
        v = ((ul * sg_u) * (z * sg_z)).astype(BF16)
        ho_ref[...] = h_ref[...] + _dot(v, wo_ref[...])

    vec = _const((None, 1, e), (layer, 0, 0))
    row = lambda i: (i, 0)
    outs, xouts = _launch(
        body, name, t // tm,
        [pl.BlockSpec((tm, 3 * e), row), pl.BlockSpec((tm, d), row),
         _const((None, CONV_TAPS, e), (layer, 0, 0)), vec, vec, vec, _const((e, d), (0, 0))],
        [pl.BlockSpec((tm, d), row), pl.BlockSpec((tm, e + LANES), row), pl.BlockSpec((tm, 2 * e), row),
         pl.BlockSpec((tm, 2 * e), row)],
        [jax.ShapeDtypeStruct((t, d), F32), jax.ShapeDtypeStruct((t, e + LANES), F32),
         jax.ShapeDtypeStruct((t, 2 * e), F32), jax.ShapeDtypeStruct((t, 2 * e), F32)],
        [pltpu.VMEM((SUBLANES, tm + CONV_HALO, e), F32), pltpu.VMEM((tm, e), F32)], (p, h, dw, *vecs, w_out), ex)
    return outs[0], tuple(outs[1:]), xouts


def _conv_bwd(dho, p, saved, dw, vecs, layer, w_out, name, ex=None):
    t, d = dho.shape
    e = w_out.shape[0]
    tm = _tile(t, 256)
    nt = t // tm
    rc = 16
    vec0 = CONV_TAPS + 1

    def body(dho_ref, z_ref, xr_ref, sg_ref, ub_ref, dw_ref, lng_ref, lnb_ref, wo_ref,
             dp_ref, dwo_ref, ddw_ref, dvec_ref, ds_scr, acc_scr, dwo_scr):
        i = pl.program_id(0)

        @pl.when(i == 0)
        def _():
            dwo_scr[...] = jnp.zeros_like(dwo_scr)
            acc_scr[...] = jnp.zeros_like(acc_scr)
            ds_scr[:, pl.ds(tm, CONV_HALO), :] = jnp.zeros((SUBLANES, CONV_HALO, e), F32)

        @pl.when(i > 0)
        def _():
            ds_scr[:, pl.ds(tm, CONV_HALO), :] = ds_scr[:, pl.ds(0, CONV_HALO), :]

        lng = lng_ref[...]
        lnb = lnb_ref[...]

        xhat = xr_ref[:, pl.ds(0, e)]
        rstd = xr_ref[:, pl.ds(e, 1)]
        ul = xhat * lng + lnb
        z = z_ref[...]
        sg_u = sg_ref[:, pl.ds(0, e)]
        sg_z = sg_ref[:, pl.ds(e, e)]
        s_u = ul * sg_u
        s_z = z * sg_z
        v = (s_u * s_z).astype(BF16)
        dy = dho_ref[...].astype(BF16)
        dv = _dot_nt(dy, wo_ref[...])
        dwo_scr[...] += _dot_tn(v, dy)

        dul = dv * s_z * _dsilu(ul, sg_u)
        dp_ref[:, pl.ds(2 * e, e)] = (dv * s_u * _dsilu(z, sg_z)).astype(BF16)
        acc_scr[pl.ds((vec0 + 1) * SUBLANES, SUBLANES), :] += _rows8(dul * xhat)
        acc_scr[pl.ds((vec0 + 2) * SUBLANES, SUBLANES), :] += _rows8(dul)
        dxh = dul * lng
        duc = rstd * (dxh - jnp.mean(dxh, axis=-1, keepdims=True)
                      - xhat * jnp.mean(dxh * xhat, axis=-1, keepdims=True))
        acc_scr[pl.ds(vec0 * SUBLANES, SUBLANES), :] += _rows8(duc)
        ds_scr[0, pl.ds(0, tm), :] = duc
        for r in range(1, SUBLANES):
            ds_scr[r, pl.ds(0, tm), :] = ds_scr[0, pl.ds(r, tm), :]

        def c_conv(base):
            rows = pl.ds(base, rc)
            u = ub_ref[rows, pl.ds(0, e)]
            sb = ub_ref[rows, pl.ds(e, e)]
            du = jnp.zeros((rc, e), F32)
            for o in range(CONV_TAPS):
                q, r = divmod(o, SUBLANES)
                k = CONV_TAPS - 1 - o
                sh = ds_scr[r, pl.ds(pl.multiple_of(base + SUBLANES * q, SUBLANES), rc), :]
                du = du + dw_ref[pl.ds(k, 1), :] * sh
                acc_scr[pl.ds(k * SUBLANES, SUBLANES), :] += _rows8(u * sh)
            dp_ref[rows, pl.ds(0, e)] = (du * sb).astype(BF16)
            dp_ref[rows, pl.ds(e, e)] = (du * u * (1.0 - sb)).astype(BF16)
        _chunks(tm // rc, rc, c_conv)

        @pl.when(i == nt - 1)
        def _():
            dwo_ref[...] = dwo_scr[...].astype(BF16)
            slot_sum = lambda k: jnp.sum(acc_scr[pl.ds(k * SUBLANES, SUBLANES), :], axis=0, keepdims=True)
            for k in range(CONV_TAPS):
                ddw_ref[pl.ds(k, 1), :] = slot_sum(k)
            dvec_ref[...] = jnp.zeros_like(dvec_ref)
            for k in range(3):
                dvec_ref[pl.ds(k, 1), :] = slot_sum(vec0 + k)

    rev = lambda i: (nt - 1 - i, 0)
    vec = _const((None, 1, e), (layer, 0, 0))
    (dp, dwo, ddw, dvec), xouts = _launch(
        body, name, nt,
        [pl.BlockSpec((tm, d), rev), pl.BlockSpec((tm, e), lambda i: (nt - 1 - i, 2)),
         pl.BlockSpec((tm, e + LANES), rev), pl.BlockSpec((tm, 2 * e), rev), pl.BlockSpec((tm, 2 * e), rev),
         _const((None, CONV_TAPS, e), (layer, 0, 0)), vec, vec, _const((e, d), (0, 0))],
        [pl.BlockSpec((tm, 3 * e), rev), _const((e, d), (0, 0)), _const((CONV_TAPS, e), (0, 0)),
         _const((SUBLANES, e), (0, 0))],
        [jax.ShapeDtypeStruct((t, 3 * e), BF16), jax.ShapeDtypeStruct((e, d), BF16),
         jax.ShapeDtypeStruct((CONV_TAPS, e), F32), jax.ShapeDtypeStruct((SUBLANES, e), F32)],
        [pltpu.VMEM((SUBLANES, tm + CONV_HALO, e), F32), pltpu.VMEM(((vec0 + 3) * SUBLANES, e), F32),
         pltpu.VMEM((e, d), F32)],
        (dho, p, *saved, dw, vecs[1], vecs[2], w_out), ex)
    return dp, dwo, ddw, dvec, xouts


def _inv_count(tile, tm, w):
    tpos = tile * tm + lax.broadcasted_iota(jnp.int32, (tm, 1), 0)
    return 1.0 / jnp.minimum(tpos + 1, w).astype(F32)


def _pool_d_group(ue_scr, tile, tm, gc, g):
    w = POOL_WINDOWS[g]
    win = ue_scr[:, pl.ds(g * gc, gc)]
    s = win
    sh = 1
    while sh < w:
        s = s + pltpu.roll(s, sh, axis=0)
        sh *= 2
    return s[POOL_HALO:, :] * _inv_count(tile, tm, w) - win[POOL_HALO:, :]


def _final_rows(ho, tg_ref, fg_ref, dh_ref, acc_scr, lacc_scr):
    d = ho.shape[-1]
    r = lax.rsqrt(jnp.mean(ho * ho, axis=-1, keepdims=True) + RMS_EPS)
    nrm = ho * r
    err = nrm * fg_ref[...] - tg_ref[...]
    lacc_scr[...] += _rows8(err * err)
    dy = err * (1.0 / d)
    acc_scr[...] += _rows8(dy * nrm)
    dq = dy * fg_ref[...]
    dh_ref[...] = r * (dq - nrm * jnp.mean(dq * nrm, axis=-1, keepdims=True))


def _pool_fwd(h, norm_g, nlayer, w_in, w_grp, vecs, layer, w_out, name, ex=None, final=None):
    t, d = h.shape
    e = w_out.shape[0]
    ng = len(POOL_WINDOWS)
    gc = e // ng
    tm = _tile(t, 512)
    nt = t // tm
    cw = 2 * e // (2 * ng)

    def layer_rows(i, hn_ref, hc_ref, g_ref, wi_ref, wg_ref, bg_ref, sc_ref, wo_ref, p_ref, d_ref,
                   ue_scr, y_scr, pbuf, z_scr, hn_scr):
        tile = jnp.maximum(i - 1, 0)

        @pl.when(i == 0)
        def _():
            pbuf[...] = jnp.zeros_like(pbuf)

        @pl.when(i <= 1)
        def _():
            ue_scr[pl.ds(0, POOL_HALO), :] = jnp.zeros((POOL_HALO, e), F32)

        @pl.when(i > 1)
        def _():
            ue_scr[pl.ds(0, POOL_HALO), :] = ue_scr[pl.ds(tm, POOL_HALO), :]

        ue_scr[pl.ds(POOL_HALO, tm), :] = pbuf[:, pl.ds(0, e)]
        z_scr[...] = pbuf[:, pl.ds(e, e)]

        x = hn_ref[...]
        r = lax.rsqrt(jnp.mean(x * x, axis=-1, keepdims=True) + RMS_EPS)
        hn_scr[...] = (x * r * g_ref[...]).astype(BF16)

        def project(c):
            part = _dot(hn_scr[...], wi_ref[:, pl.ds(c * cw, cw)])
            pbuf[:, pl.ds(c * cw, cw)] = part
            p_ref[:, pl.ds(c * cw, cw)] = part

        for g in range(ng):
            project(2 * g)
            cols = pl.ds(g * gc, gc)
            dg = _pool_d_group(ue_scr, tile, tm, gc, g).astype(BF16)
            d_ref[:, cols] = dg
            z = z_scr[:, cols]
            y1 = (_dot(dg, wg_ref[g]) + bg_ref[:, cols]) * sc_ref[:, cols]
            y_scr[:, cols] = (y1 * (z * _sigmoid(z))).astype(BF16)
            project(2 * g + 1)

        return hc_ref[...] + _dot(y_scr[...], wo_ref[...])

    nxt = lambda i: (jnp.minimum(i, nt - 1), 0)
    cur = lambda i: (jnp.maximum(i - 1, 0), 0)
    vec = _const((None, 1, e), (layer, 0, 0))
    in_specs = [pl.BlockSpec((tm, d), nxt), pl.BlockSpec((tm, d), cur), _const((None, 1, d), (nlayer, 0, 0)),
                _const((d, 2 * e), (0, 0)), _const((ng, gc, gc), (0, 0, 0)), vec, vec, _const((e, d), (0, 0))]
    scratch = [pltpu.VMEM((tm + POOL_HALO, e), F32), pltpu.VMEM((tm, e), BF16), pltpu.VMEM((tm, 2 * e), F32),
               pltpu.VMEM((tm, e), F32), pltpu.VMEM((tm, d), BF16)]
    args = (h, h, norm_g, w_in, w_grp, *vecs, w_out)

    if final is None:
        def body(hn_ref, hc_ref, g_ref, wi_ref, wg_ref, bg_ref, sc_ref, wo_ref, ho_ref, p_ref, d_ref, *scr):
            ho_ref[...] = layer_rows(pl.program_id(0), hn_ref, hc_ref, g_ref, wi_ref, wg_ref, bg_ref, sc_ref, wo_ref,
                                     p_ref, d_ref, *scr)

        (ho, p, dsaved), xouts = _launch(
            body, name, nt + 1, in_specs,
            [pl.BlockSpec((tm, d), cur), pl.BlockSpec((tm, 2 * e), nxt), pl.BlockSpec((tm, e), cur)],
            [jax.ShapeDtypeStruct((t, d), F32), jax.ShapeDtypeStruct((t, 2 * e), F32),
             jax.ShapeDtypeStruct((t, e), BF16)], scratch, args, ex)
        return ho, (p, dsaved), xouts

    target, final_g = final

    def body(hn_ref, hc_ref, g_ref, wi_ref, wg_ref, bg_ref, sc_ref, wo_ref, tg_ref, fg_ref,
             dh_ref, p_ref, d_ref, dfg_ref, loss_ref, ue_scr, y_scr, pbuf, z_scr, hn_scr, acc_scr, lacc_scr):
        i = pl.program_id(0)

        @pl.when(i <= 1)
        def _():
            acc_scr[...] = jnp.zeros_like(acc_scr)
            lacc_scr[...] = jnp.zeros_like(lacc_scr)

        ho = layer_rows(i, hn_ref, hc_ref, g_ref, wi_ref, wg_ref, bg_ref, sc_ref, wo_ref, p_ref, d_ref,
                        ue_scr, y_scr, pbuf, z_scr, hn_scr)
        _final_rows(ho, tg_ref, fg_ref, dh_ref, acc_scr, lacc_scr)

        @pl.when(i == nt)
        def _():
            dfg_ref[...] = jnp.zeros_like(dfg_ref)
            dfg_ref[pl.ds(0, 1), :] = jnp.sum(acc_scr[...], axis=0, keepdims=True)
            loss_ref[...] = jnp.broadcast_to(jnp.sum(lacc_scr[...]) * (0.5 / d), loss_ref.shape)

    (dh, p, dsaved, dfg, loss), xouts = _launch(
        body, name, nt + 1, in_specs + [pl.BlockSpec((tm, d), cur), _const((1, d), (0, 0))],
        [pl.BlockSpec((tm, d), cur), pl.BlockSpec((tm, 2 * e), nxt), pl.BlockSpec((tm, e), cur),
         _const((SUBLANES, d), (0, 0)), _const((SUBLANES, LANES), (0, 0))],
        [jax.ShapeDtypeStruct((t, d), F32), jax.ShapeDtypeStruct((t, 2 * e), F32), jax.ShapeDtypeStruct((t, e), BF16),
         jax.ShapeDtypeStruct((SUBLANES, d), F32), jax.ShapeDtypeStruct((SUBLANES, LANES), F32)],
        scratch + [pltpu.VMEM((SUBLANES, d), F32), pltpu.VMEM((SUBLANES, d), F32)],
        args + (target, final_g), ex)
    return dh, (p, dsaved), dfg, loss, xouts


def _pool_bwd(dho, saved, w_grp, vecs, layer, w_out, name, ex=None):
    t, d = dho.shape
    e = w_out.shape[0]
    ng = len(POOL_WINDOWS)
    gc = e // ng
    tm = _tile(t, 512)
    nt = t // tm

    def body(dho_ref, z_ref, d_ref, wg_ref, bg_ref, sc_ref, wo_ref, dp_ref, dwo_ref, dwg_ref, dvec_ref,
             ee_scr, acc_scr, dwo_scr, dwg_scr):
        i = pl.program_id(0)
        tile = nt - 1 - i

        @pl.when(i == 0)
        def _():
            dwo_scr[...] = jnp.zeros_like(dwo_scr)
            dwg_scr[...] = jnp.zeros_like(dwg_scr)
            acc_scr[...] = jnp.zeros_like(acc_scr)
            ee_scr[pl.ds(tm, POOL_HALO), :] = jnp.zeros((POOL_HALO, e), F32)

        @pl.when(i > 0)
        def _():
            ee_scr[pl.ds(tm, POOL_HALO), :] = ee_scr[pl.ds(0, POOL_HALO), :]

        bg = bg_ref[...]
        sc = sc_ref[...]
        ds = [d_ref[:, pl.ds(g * gc, gc)] for g in range(ng)]
        ob = jnp.concatenate([_dot(ds[g], wg_ref[g]) for g in range(ng)], axis=1) + bg
        z = z_ref[...]
        sg_z = _sigmoid(z)
        s_z = z * sg_z
        y1 = ob * sc
        dy = dho_ref[...].astype(BF16)
        dy2 = _dot_nt(dy, wo_ref[...])
        dwo_scr[...] += _dot_tn((y1 * s_z).astype(BF16), dy)
        dy1 = dy2 * s_z
        dp_ref[:, pl.ds(e, e)] = (dy2 * y1 * _dsilu(z, sg_z)).astype(BF16)
        acc_scr[pl.ds(SUBLANES, SUBLANES), :] += _rows8(dy1 * ob)
        do = dy1 * sc
        acc_scr[pl.ds(0, SUBLANES), :] += _rows8(do)

        n = tm + POOL_HALO
        for g, w in enumerate(POOL_WINDOWS):
            cols = pl.ds(g * gc, gc)
            do_g = do[:, g * gc:(g + 1) * gc].astype(BF16)
            dwg_scr[g] += _dot_tn(ds[g], do_g)
            dd = _dot_nt(do_g, wg_ref[g])
            ee_scr[pl.ds(0, tm), cols] = dd * _inv_count(tile, tm, w)
            s = ee_scr[:, cols]
            sh = 1
            while sh < w:
                s = s + pltpu.roll(s, n - sh, axis=0)
                sh *= 2
            dp_ref[:, cols] = (s[:tm, :] - dd).astype(BF16)

        @pl.when(i == nt - 1)
        def _():
            dwo_ref[...] = dwo_scr[...].astype(BF16)
            dwg_ref[...] = dwg_scr[...].astype(BF16)
            dvec_ref[...] = jnp.zeros_like(dvec_ref)
            for k in range(2):
                dvec_ref[pl.ds(k, 1), :] = jnp.sum(acc_scr[pl.ds(k * SUBLANES, SUBLANES), :], axis=0, keepdims=True)

    rev = lambda i: (nt - 1 - i, 0)
    vec = _const((None, 1, e), (layer, 0, 0))
    (dp, dwo, dwg, dvec), xouts = _launch(
        body, name, nt,
        [pl.BlockSpec((tm, d), rev), pl.BlockSpec((tm, e), lambda i: (nt - 1 - i, 1)), pl.BlockSpec((tm, e), rev),
         _const((ng, gc, gc), (0, 0, 0)), vec, vec, _const((e, d), (0, 0))],
        [pl.BlockSpec((tm, 2 * e), rev), _const((e, d), (0, 0)), _const((ng, gc, gc), (0, 0, 0)),
         _const((SUBLANES, e), (0, 0))],
        [jax.ShapeDtypeStruct((t, 2 * e), BF16), jax.ShapeDtypeStruct((e, d), BF16),
         jax.ShapeDtypeStruct((ng, gc, gc), BF16), jax.ShapeDtypeStruct((SUBLANES, e), F32)],
        [pltpu.VMEM((tm + POOL_HALO, e), F32),
         pltpu.VMEM((2 * SUBLANES, e), F32), pltpu.VMEM((e, d), F32), pltpu.VMEM((ng, gc, gc), F32)],
        (dho, *saved, w_grp, *vecs, w_out), ex)
    return dp, dwo, dwg, dvec, xouts


def _in_bwd(dp, h, dho, norm_g, layer, w_in, name, ex=None):
    t, d = h.shape
    n = w_in.shape[-1]
    tm = _tile(t, 512)
    nt = t // tm

    def body(dp_ref, h_ref, dho_ref, g_ref, w_ref, dh_ref, dw_ref, dg_ref, acc_scr, dw_scr):
        i = pl.program_id(0)

        @pl.when(i == 0)
        def _():
            dw_scr[...] = jnp.zeros_like(dw_scr)
            acc_scr[...] = jnp.zeros_like(acc_scr)

        x = h_ref[...]
        r = lax.rsqrt(jnp.mean(x * x, axis=-1, keepdims=True) + RMS_EPS)
        nrm = x * r
        dp = dp_ref[...]
        dhn = _dot_nt(dp, w_ref[...])
        dw_scr[...] += _dot_tn((nrm * g_ref[...]).astype(BF16), dp)
        acc_scr[...] += _rows8(dhn * nrm)
        dq = dhn * g_ref[...]
        dh_ref[...] = dho_ref[...] + r * (dq - nrm * jnp.mean(dq * nrm, axis=-1, keepdims=True))

        @pl.when(i == nt - 1)
        def _():
            dw_ref[...] = dw_scr[...].astype(BF16)
            dg_ref[...] = jnp.zeros_like(dg_ref)
            dg_ref[pl.ds(0, 1), :] = jnp.sum(acc_scr[...], axis=0, keepdims=True)

    (dh, dw, dg), xouts = _launch(
        body, name, nt,
        [pl.BlockSpec((tm, n), lambda i: (i, 0)), pl.BlockSpec((tm, d), lambda i: (i, 0)),
         pl.BlockSpec((tm, d), lambda i: (i, 0)), _const((None, 1, d), (layer, 0, 0)), _const((d, n), (0, 0))],
        [pl.BlockSpec((tm, d), lambda i: (i, 0)), _const((d, n), (0, 0)), _const((SUBLANES, d), (0, 0))],
        [jax.ShapeDtypeStruct((t, d), F32), jax.ShapeDtypeStruct((d, n), BF16),
         jax.ShapeDtypeStruct((SUBLANES, d), F32)],
        [pltpu.VMEM((SUBLANES, d), F32), pltpu.VMEM((d, n), F32)],
        (dp, h, dho, norm_g, w_in), ex)
    return dh, dw, dg, xouts


def _in_bwd_dw(dp, h, norm_g, layer, name, ex=None):
    t, d = h.shape
    n = dp.shape[-1]
    tm = _tile(t, 1024)
    nt = t // tm

    def body(dp_ref, h_ref, g_ref, dw_ref, dw_scr):
        i = pl.program_id(0)

        @pl.when(i == 0)
        def _():
            dw_scr[...] = jnp.zeros_like(dw_scr)

        x = h_ref[...]
        r = lax.rsqrt(jnp.mean(x * x, axis=-1, keepdims=True) + RMS_EPS)
        dw_scr[...] += _dot_tn((x * r * g_ref[...]).astype(BF16), dp_ref[...])

        @pl.when(i == nt - 1)
        def _():
            dw_ref[...] = dw_scr[...].astype(BF16)

    (dw,), xouts = _launch(
        body, name, nt,
        [pl.BlockSpec((tm, n), lambda i: (i, 0)), pl.BlockSpec((tm, d), lambda i: (i, 0)),
         _const((None, 1, d), (layer, 0, 0))],
        [_const((d, n), (0, 0))], [jax.ShapeDtypeStruct((d, n), BF16)], [pltpu.VMEM((d, n), F32)],
        (dp, h, norm_g), ex)
    return dw, xouts


def _in_bwd_dh(dp, h, dho, norm_g, layer, w_in, name, ex=None):
    t, d = h.shape
    n = w_in.shape[-1]
    tm = _tile(t, 512)
    nt = t // tm

    def body(dp_ref, h_ref, dho_ref, g_ref, w_ref, dh_ref, dg_ref, acc_scr):
        i = pl.program_id(0)

        @pl.when(i == 0)
        def _():
            acc_scr[...] = jnp.zeros_like(acc_scr)

        x = h_ref[...]
        r = lax.rsqrt(jnp.mean(x * x, axis=-1, keepdims=True) + RMS_EPS)
        nrm = x * r
        dhn = _dot_nt(dp_ref[...], w_ref[...])
        acc_scr[...] += _rows8(dhn * nrm)
        dq = dhn * g_ref[...]
        dh_ref[...] = dho_ref[...] + r * (dq - nrm * jnp.mean(dq * nrm, axis=-1, keepdims=True))

        @pl.when(i == nt - 1)
        def _():
            dg_ref[...] = jnp.zeros_like(dg_ref)
            dg_ref[pl.ds(0, 1), :] = jnp.sum(acc_scr[...], axis=0, keepdims=True)

    (dh, dg), xouts = _launch(
        body, name, nt,
        [pl.BlockSpec((tm, n), lambda i: (i, 0)), pl.BlockSpec((tm, d), lambda i: (i, 0)),
         pl.BlockSpec((tm, d), lambda i: (i, 0)), _const((None, 1, d), (layer, 0, 0)), _const((d, n), (0, 0))],
        [pl.BlockSpec((tm, d), lambda i: (i, 0)), _const((SUBLANES, d), (0, 0))],
        [jax.ShapeDtypeStruct((t, d), F32), jax.ShapeDtypeStruct((SUBLANES, d), F32)],
        [pltpu.VMEM((SUBLANES, d), F32)],
        (dp, h, dho, norm_g, w_in), ex)
    return dh, dg, xouts


def _adam_update(g, w, m, v):
    c1 = 1.0 / (1.0 - ADAM_B1 ** ADAM_STEP)
    c2 = 1.0 / (1.0 - ADAM_B2 ** ADAM_STEP)
    nm = ADAM_B1 * m + (1.0 - ADAM_B1) * g
    nv = ADAM_B2 * v + (1.0 - ADAM_B2) * (g * g)
    return -ADAM_LR * ((nm * c1) / (jnp.sqrt(nv * c2) + ADAM_EPS) + ADAM_WD * w), nm, nv


def _adamw_small(params, stacks, loss_stack, name):
    ns, npar = len(stacks), len(params)

    def body(*refs):
        st = refs[:ns]
        pr = refs[ns:ns + 3 * npar]
        ls_ref = refs[ns + 3 * npar]
        outs = refs[ns + 3 * npar + 1:ns + 7 * npar + 1]
        loss_ref = refs[ns + 7 * npar + 1]
        for q, (w, _, _, pieces) in enumerate(params):
            w_ref, m_ref, v_ref = pr[3 * q:3 * q + 3]
            g_ref, d_ref, nm_ref, nv_ref = outs[4 * q:4 * q + 4]
            for s, row, slab in pieces:
                if w.ndim == 3:
                    take = lambda k: st[s][k]
                    at = slab
                else:
                    take = lambda k: st[s][k, pl.ds(row, 1), :]
                    at = (pl.ds(slab, 1), slice(None))
                g = take(0)
                for k in range(1, N_DEV):
                    g = g + take(k)
                g_ref[at] = g
                d_ref[at], nm_ref[at], nv_ref[at] = _adam_update(g, w_ref[at], m_ref[at], v_ref[at])
        tot = ls_ref[0]
        for k in range(1, N_DEV):
            tot = tot + ls_ref[k]
        loss_ref[...] = tot

    flat = [a for (w, m, v, _) in params for a in (w, m, v)]
    out_shape = [jax.ShapeDtypeStruct(w.shape, F32) for (w, _, _, _) in params for _ in range(4)]
    whole = pl.BlockSpec(memory_space=pltpu.VMEM)
    outs = pl.pallas_call(
        body, name=name, in_specs=[whole] * (ns + 3 * npar + 1), out_specs=[whole] * (4 * npar + 1),
        out_shape=out_shape + [jax.ShapeDtypeStruct(loss_stack.shape[1:], F32)],
    )(*stacks, *flat, loss_stack)
    return [outs[4 * q:4 * q + 4] for q in range(npar)], outs[-1]


def _adamw(stacks, w, m, v, name):
    nl = len(stacks)
    shp = w.shape
    c = shp[-1]
    r = 1
    for s in shp[1:-1]:
        r *= s
    tr = r
    for cand in (512, 256, 128, 64, 32, 16):
        if r % cand == 0 and r > cand:
            tr = cand
            break
    nrb = r // tr

    def body(*refs):
        s_refs = refs[:nl]
        w_ref, m_ref, v_ref, g_ref, d_ref, nm_ref, nv_ref = refs[nl:]
        layer = pl.program_id(0)
        for l in range(nl):
            @pl.when(layer == l)
            def _(l=l):
                g = s_refs[l][0].astype(F32)
                for k in range(1, N_DEV):
                    g = g + s_refs[l][k].astype(F32)
                g_ref[...] = g
                d_ref[...], nm_ref[...], nv_ref[...] = _adam_update(g, w_ref[...], m_ref[...], v_ref[...])

    def stack_spec(l):
        return pl.BlockSpec((N_DEV, tr, c),
                            lambda j, i: (0, jnp.where(j == l, i, jnp.where(j < l, 0, nrb - 1)), 0))

    spec = pl.BlockSpec((None, tr, c), lambda j, i: (j, i, 0))
    outs = pl.pallas_call(
        body, name=name, grid=(nl, nrb),
        in_specs=[stack_spec(l) for l in range(nl)] + [spec, spec, spec],
        out_specs=[spec] * 4,
        out_shape=[jax.ShapeDtypeStruct((nl, r, c), F32)] * 4,
        compiler_params=_params(2),
    )(*[s.reshape(N_DEV, r, c) for s in stacks], w.reshape(nl, r, c), m.reshape(nl, r, c), v.reshape(nl, r, c))
    return [o.reshape(shp) for o in outs]


def kernel(x, norm_g, final_g, conv_w_in, conv_dw, conv_dw_b, conv_ln_g, conv_ln_b, conv_w_out, pool_w_in, pool_w_grp, pool_b_grp, pool_scale, pool_w_out, loss_target, m_norm_g, m_final_g, m_conv_w_in, m_conv_dw, m_conv_dw_b, m_conv_ln_g, m_conv_ln_b, m_conv_w_out, m_pool_w_in, m_pool_w_grp, m_pool_b_grp, m_pool_scale, m_pool_w_out, v_norm_g, v_final_g, v_conv_w_in, v_conv_dw, v_conv_dw_b, v_conv_ln_g, v_conv_ln_b, v_conv_w_out, v_pool_w_in, v_pool_w_grp, v_pool_b_grp, v_pool_scale, v_pool_w_out):
    h0 = x[0]
    target = loss_target[0]
    ng3 = norm_g[:, None, :]
    row3 = lambda a: a[:, None, :]
    conv_vecs = (row3(conv_dw_b), row3(conv_ln_g), row3(conv_ln_b))
    gather = lambda arrays, axes: _Exchange("gather", arrays, axes)
    scatter = lambda arrays, axes: _Exchange("scatter", arrays, axes)

    cwi, cwo, pwi = conv_w_in.astype(BF16), conv_w_out.astype(BF16), pool_w_in.astype(BF16)
    pwg, pwo = pool_w_grp.astype(BF16), pool_w_out.astype(BF16)

    (cw_in0,) = _gather_via_sibling_call(gather([(cwi, 0)], [1]), "gather_first")
    p0, (cw_out0, dw_full, bg_full, sc_full, pw_in0) = _in_fwd(
        h0, ng3, 0, cw_in0, "conv_in_fwd_0",
        gather([(cwo, 0), conv_dw, pool_b_grp, pool_scale, (pwi, 0)], [0, 2, 1, 1, 1]))
    pool_vecs = (row3(bg_full), row3(sc_full))
    h1, uc0, (pw_grp0, pw_out0, cw_in1, cw_out1, pw_in1, pw_grp1, pw_out1) = _conv_fwd(
        p0, h0, dw_full, conv_vecs, 0, cw_out0, "conv_mix_fwd_0",
        gather([(pwg, 0), (pwo, 0), (cwi, 1), (cwo, 1), (pwi, 1), (pwg, 1), (pwo, 1)], [1, 0, 1, 0, 1, 1, 0]))
    h2, p1, _ = _pool_fwd(h1, ng3, 1, pw_in0, pw_grp0, pool_vecs, 0, pw_out0, "pool_fwd_0")
    p2, _ = _in_fwd(h2, ng3, 2, cw_in1, "conv_in_fwd_1")
    h3, uc2, _ = _conv_fwd(p2, h2, dw_full, conv_vecs, 1, cw_out1, "conv_mix_fwd_1")
    dh, p3, d_final_g, loss_part, _ = _pool_fwd(h3, ng3, 3, pw_in1, pw_grp1, pool_vecs, 1, pw_out1, "pool_fwd_1",
                                               final=(target, final_g[None, :]))

    dp, g_pwo1, g_pwg1, dpv1, _ = _pool_bwd(dh, p3, pw_grp1, pool_vecs, 1, pw_out1, "pool_mix_bwd_1")
    dh, g_pwi1, dg3, _ = _in_bwd(dp, h3, dh, ng3, 3, pw_in1, "pool_in_bwd_1")
    dp, g_cwo1, ddw1, dcv1, (s_pwo1, s_pwg1, s_pwi1) = _conv_bwd(
        dh, p2, uc2, dw_full, conv_vecs, 1, cw_out1, "conv_mix_bwd_1", scatter([g_pwo1, g_pwg1, g_pwi1], [0, 1, 1]))
    dh, g_cwi1, dg2, _ = _in_bwd(dp, h2, dh, ng3, 2, cw_in1, "conv_in_bwd_1")
    dp, g_pwo0, g_pwg0, dpv0, (s_cwo1, s_cwi1) = _pool_bwd(dh, p1, pw_grp0, pool_vecs, 0, pw_out0, "pool_mix_bwd_0",
                                                           scatter([g_cwo1, g_cwi1], [0, 1]))
    dh, g_pwi0, dg1, _ = _in_bwd(dp, h1, dh, ng3, 1, pw_in0, "pool_in_bwd_0")
    dp, g_cwo0, ddw0, dcv0, (s_pwo0, s_pwg0, s_pwi0, s_ddw1, s_dcv1, s_dpv0, s_dpv1, s_dg1, s_dg2, s_dg3, s_dfg,
                             s_loss) = _conv_bwd(
        dh, p0, uc0, dw_full, conv_vecs, 0, cw_out0, "conv_mix_bwd_0",
        scatter([g_pwo0, g_pwg0, g_pwi0, ddw1, dcv1, dpv0, dpv1, dg1, dg2, dg3, d_final_g, loss_part],
                [0, 1, 1, 1, None, 1, 1, None, None, None, None, None]))
    g_cwi0, (s_cwo0, s_ddw0, s_dcv0) = _in_bwd_dw(dp, h0, ng3, 0, "conv_in_bwd_dw_0",
                                                  scatter([g_cwo0, ddw0, dcv0], [0, 1, None]))
    dh, dg0, (s_cwi0,) = _in_bwd_dh(dp, h0, dh, ng3, 0, cw_in0, "conv_in_bwd_dh_0", scatter([g_cwi0], [1]))
    grad_x = dh[None]
    (s_dg0,) = _exchange_call(scatter([dg0], [None]), "scatter_last")

    res = {}
    res["conv_w_in"] = _adamw([s_cwi0, s_cwi1], conv_w_in, m_conv_w_in, v_conv_w_in, "adamw_conv_w_in")
    res["conv_w_out"] = _adamw([s_cwo0, s_cwo1], conv_w_out, m_conv_w_out, v_conv_w_out, "adamw_conv_w_out")
    res["pool_w_in"] = _adamw([s_pwi0, s_pwi1], pool_w_in, m_pool_w_in, v_pool_w_in, "adamw_pool_w_in")
    res["pool_w_grp"] = _adamw([s_pwg0, s_pwg1], pool_w_grp, m_pool_w_grp, v_pool_w_grp, "adamw_pool_w_grp")
    res["pool_w_out"] = _adamw([s_pwo0, s_pwo1], pool_w_out, m_pool_w_out, v_pool_w_out, "adamw_pool_w_out")
    stacks = [s_ddw0, s_ddw1, s_dpv0, s_dpv1, s_dg0, s_dg1, s_dg2, s_dg3, s_dfg, s_dcv0, s_dcv1]
    small = [
        ("conv_dw", conv_dw, m_conv_dw, v_conv_dw, [(0, None, 0), (1, None, 1)]),
        ("pool_b_grp", pool_b_grp, m_pool_b_grp, v_pool_b_grp, [(2, 0, 0), (3, 0, 1)]),
        ("pool_scale", pool_scale, m_pool_scale, v_pool_scale, [(2, 1, 0), (3, 1, 1)]),
        ("norm_g", norm_g, m_norm_g, v_norm_g, [(4, 0, 0), (5, 0, 1), (6, 0, 2), (7, 0, 3)]),
        ("final_g", final_g[None, :], m_final_g[None, :], v_final_g[None, :], [(8, 0, 0)]),
        ("conv_dw_b", conv_dw_b, m_conv_dw_b, v_conv_dw_b, [(9, 0, 0), (10, 0, 1)]),
        ("conv_ln_g", conv_ln_g, m_conv_ln_g, v_conv_ln_g, [(9, 1, 0), (10, 1, 1)]),
        ("conv_ln_b", conv_ln_b, m_conv_ln_b, v_conv_ln_b, [(9, 2, 0), (10, 2, 1)]),
    ]
    small_res, loss_block = _adamw_small([s[1:] for s in small], stacks, s_loss, "adamw_small")
    for (name, *_), r in zip(small, small_res):
        res[name] = [a[0] for a in r] if name == "final_g" else r
    loss = loss_block[0, 0]

    names = ["norm_g", "final_g", "conv_w_in", "conv_dw", "conv_dw_b", "conv_ln_g", "conv_ln_b", "conv_w_out",
             "pool_w_in", "pool_w_grp", "pool_b_grp", "pool_scale", "pool_w_out"]
    return (loss, grad_x) + tuple(res[n][q] for q in range(4) for n in names)
```
